```python
import math
import jax
import jax.numpy as jnp
from jax import lax
import numpy as np

D_MODEL = 2048
BATCH = 16
SEQ = 256
DEPTH = 2
DEC_BATCH = 8
DEC_SEQ = 2048
PAST_LEN = 512

GRID_W = 64
HEAD_DIM = 128
N_HEADS = 8
N_KV_HEADS = 2
GROUP = N_HEADS // N_KV_HEADS
ATTN_WIDTH = N_HEADS * HEAD_DIM
KV_WIDTH = N_KV_HEADS * HEAD_DIM
WINDOW = 128
ATTN_BLOCK = 128
ATTN_SCALE = HEAD_DIM ** -0.5
ROPE_BASE = 10000.0
ROT_F = HEAD_DIM // 4
SSM_WIDTH = 512
SSM_CH = 16
SSM_GROUPS = SSM_WIDTH // SSM_CH
SSM_STATE = 64
CONV_WIDTH = 512
CONV_K = 3
N_BRANCHES = 3
IN_WIDTH = ATTN_WIDTH + 2 * KV_WIDTH + SSM_WIDTH + 3 * CONV_WIDTH
SPLIT_POINTS = (ATTN_WIDTH, ATTN_WIDTH + KV_WIDTH, ATTN_WIDTH + 2 * KV_WIDTH, ATTN_WIDTH + 2 * KV_WIDTH + SSM_WIDTH, ATTN_WIDTH + 2 * KV_WIDTH + SSM_WIDTH + CONV_WIDTH, ATTN_WIDTH + 2 * KV_WIDTH + SSM_WIDTH + 2 * CONV_WIDTH)
N_EXPERTS = 64
TOP_K = 6
N_EXPERT_GROUPS = 8
TOPK_GROUPS = 4
D_EXPERT = 512
D_SHARED = 512
ROUTED_SCALE = 2.5
MOE_BLOCK = 128
EPS = 1e-6
NEG_INF = -1e30

kernel_name = 'hybrid_flow_backbone_ctx_prefix_step'


def _rmsnorm(x, g):
    xf = x.astype(jnp.float32)
    y = xf * lax.rsqrt(jnp.mean(xf * xf, axis=-1, keepdims=True) + EPS) * g.astype(jnp.float32)
    return y.astype(x.dtype)


def _adaln(cvec, w_ada, b_ada):
    mod = (jax.nn.silu(cvec) @ w_ada + b_ada)[:, None, :]
    return jnp.split(mod, 6, axis=-1)


def _axial_rope_tables(L):
    rows = L // GRID_W
    t = jnp.arange(rows * GRID_W)
    row = (t // GRID_W).astype(jnp.float32)
    col = (t % GRID_W).astype(jnp.float32)
    inv = ROPE_BASE ** (-jnp.arange(ROT_F, dtype=jnp.float32) / ROT_F)
    ar = row[:, None] * inv
    ac = col[:, None] * inv
    return jnp.cos(ar), jnp.sin(ar), jnp.cos(ac), jnp.sin(ac)


def _rotate(z, cos, sin):
    z1, z2 = jnp.split(z, 2, axis=-1)
    return jnp.concatenate([z1 * cos - z2 * sin, z2 * cos + z1 * sin], axis=-1)


def _apply_axial_rope(x, tables):
    cr, sr, cc, sc = tables
    shp = (1, x.shape[1]) + (1,) * (x.ndim - 3) + (ROT_F,)
    xr, xc = jnp.split(x.astype(jnp.float32), 2, axis=-1)
    out = jnp.concatenate([_rotate(xr, cr.reshape(shp), sr.reshape(shp)), _rotate(xc, cc.reshape(shp), sc.reshape(shp))], axis=-1)
    return out.astype(x.dtype)


def _joint_attend(q, parts, sink):
    scores = []
    for k, v, mask in parts:
        s = jnp.einsum('bqhgd,bkhd->bhgqk', q, k).astype(jnp.float32) * ATTN_SCALE
        if mask is not None:
            s = jnp.where(mask, s, NEG_INF)
        scores.append(s)
    sink_l = sink.astype(jnp.float32)[None, :, :, None, None]
    m = sink_l
    for s in scores:
        m = jnp.maximum(m, jnp.max(s, axis=-1, keepdims=True))
    probs = [jnp.exp(s - m) for s in scores]
    den = jnp.exp(sink_l - m)
    for p in probs:
        den = den + jnp.sum(p, axis=-1, keepdims=True)
    out = None
    for (k, v, _), p in zip(parts, probs):
        o = jnp.einsum('bhgqk,bkhd->bqhgd', (p / den).astype(v.dtype), v)
        out = o if out is None else out + o
    return out


def _context_attention(q, k, v, sink):
    B, L = q.shape[:2]
    nb = L // ATTN_BLOCK
    qb = jnp.swapaxes(q.reshape(B, nb, ATTN_BLOCK, N_KV_HEADS, GROUP, HEAD_DIM), 0, 1)
    ob = lax.map(lambda qi: _joint_attend(qi, [(k, v, None)], sink), qb)
    return jnp.swapaxes(ob, 0, 1).reshape(B, L, ATTN_WIDTH)


def _latent_attention(q, k, v, k_ctx, v_ctx, sink):
    B, L = q.shape[:2]
    nb = L // ATTN_BLOCK
    pad = ((0, 0), (ATTN_BLOCK, ATTN_BLOCK), (0, 0), (0, 0))
    kp = jnp.pad(k, pad)
    vp = jnp.pad(v, pad)
    qb = jnp.swapaxes(q.reshape(B, nb, ATTN_BLOCK, N_KV_HEADS, GROUP, HEAD_DIM), 0, 1)
    span = 3 * ATTN_BLOCK

    def block(args):
        qi, i = args
        s0 = i * ATTN_BLOCK
        kb = lax.dynamic_slice_in_dim(kp, s0, span, axis=1)
        vb = lax.dynamic_slice_in_dim(vp, s0, span, axis=1)
        qpos = s0 + jnp.arange(ATTN_BLOCK)
        kpos = s0 - ATTN_BLOCK + jnp.arange(span)
        mask = (jnp.abs(qpos[:, None] - kpos[None, :]) <= WINDOW) & (kpos >= 0)[None, :] & (kpos < L)[None, :]
        return _joint_attend(qi, [(kb, vb, mask), (k_ctx, v_ctx, None)], sink)

    ob = lax.map(block, (qb, jnp.arange(nb)))
    return jnp.swapaxes(ob, 0, 1).reshape(B, L, ATTN_WIDTH)


def _scan_op(e1, e2):
    a1, b1 = e1
    a2, b2 = e2
    return a1 * a2, a2 * b1 + b2


def _s5_bidirectional(u, lam_re, lam_im, log_dt, b_re, b_im, c_re, c_im, d_skip, h0):
    f32 = jnp.float32
    B, L, _ = u.shape
    ug = u.astype(f32).reshape(B, L, SSM_GROUPS, SSM_CH)
    lam = lax.complex(lam_re.astype(f32), lam_im.astype(f32))
    dt = jnp.exp(log_dt.astype(f32))[..., None]
    lam_bar = jnp.exp(lam * dt)
    b_bar = ((lam_bar - 1.0) / lam)[..., None] * lax.complex(b_re.astype(f32), b_im.astype(f32))
    c_mat = lax.complex(c_re.astype(f32), c_im.astype(f32))
    y = d_skip.astype(f32).reshape(SSM_GROUPS, SSM_CH) * ug
    finals = []
    for d in range(2):
        ud = ug if d == 0 else jnp.flip(ug, axis=1)
        bu = jnp.einsum('gnp,blgp->blgn', b_bar[d], ud.astype(jnp.complex64))
        if h0 is not None:
            bu = bu.at[:, 0].add(lam_bar[d] * h0[:, d])
        a = jnp.broadcast_to(lam_bar[d], bu.shape)
        _, hs = lax.associative_scan(_scan_op, (a, bu), axis=1)
        finals.append(hs[:, -1])
        yd = jnp.real(jnp.einsum('gpn,blgn->blgp', c_mat[d], hs))
        y = y + (yd if d == 0 else jnp.flip(yd, axis=1))
    return y.reshape(B, L, SSM_WIDTH).astype(u.dtype), jnp.stack(finals, axis=1)


def _short_conv3(z, w):
    zp = jnp.pad(z, ((0, 0), (1, 1), (0, 0)))
    return w[0] * zp[:, :-2] + w[1] * zp[:, 1:-1] + w[2] * zp[:, 2:]


def _moe(x2d, w_router, b_router, w_e_gate, w_e_up, w_e_down, w_s_gate, w_s_up, w_s_down):
    f32 = jnp.float32
    T = x2d.shape[0]
    scores = jax.nn.sigmoid(x2d.astype(f32) @ w_router.astype(f32))
    biased = scores + b_router.astype(f32)
    per_group = N_EXPERTS // N_EXPERT_GROUPS
    group_score = jnp.sum(lax.top_k(biased.reshape(T, N_EXPERT_GROUPS, per_group), 2)[0], axis=-1)
    _, top_groups = lax.top_k(group_score, TOPK_GROUPS)
    group_mask = jnp.any(top_groups[:, :, None] == jnp.arange(N_EXPERT_GROUPS)[None, None, :], axis=1)
    expert_mask = jnp.repeat(group_mask, per_group, axis=1)
    _, idx = lax.top_k(jnp.where(expert_mask, biased, -jnp.inf), TOP_K)
    w_sel = jnp.take_along_axis(scores, idx, axis=1)
    w_sel = w_sel / jnp.sum(w_sel, axis=-1, keepdims=True) * ROUTED_SCALE
    tk = T * TOP_K
    flat_e = idx.reshape(tk)
    flat_tok = jnp.repeat(jnp.arange(T, dtype=jnp.int32), TOP_K)
    order = jnp.argsort(flat_e, stable=True)
    e_sorted = flat_e[order]
    counts = jnp.bincount(flat_e, length=N_EXPERTS)
    padded = (counts + MOE_BLOCK - 1) // MOE_BLOCK * MOE_BLOCK
    pad_end = jnp.cumsum(padded)
    pad_start = pad_end - padded
    start = jnp.cumsum(counts) - counts
    dest = pad_start[e_sorted] + jnp.arange(tk, dtype=jnp.int32) - start[e_sorted]
    n_blocks = -(-tk // MOE_BLOCK) + N_EXPERTS
    n_rows = n_blocks * MOE_BLOCK
    row_tok = jnp.zeros((n_rows,), jnp.int32).at[dest].set(flat_tok[order])
    row_w = jnp.zeros((n_rows,), f32).at[dest].set(w_sel.reshape(tk)[order])
    block_e = jnp.minimum(jnp.searchsorted(pad_end, jnp.arange(n_blocks, dtype=jnp.int32) * MOE_BLOCK, side='right'), N_EXPERTS - 1)

    def expert_block(args):
        xb, e = args
        hb = jax.nn.silu(xb @ w_e_gate[e]) * (xb @ w_e_up[e])
        return hb @ w_e_down[e]

    yb = lax.map(expert_block, (x2d[row_tok].reshape(n_blocks, MOE_BLOCK, -1), block_e))
    routed = jnp.zeros_like(x2d).at[row_tok].add(yb.reshape(n_rows, -1) * row_w[:, None].astype(x2d.dtype))
    shared = (jax.nn.silu(x2d @ w_s_gate) * (x2d @ w_s_up)) @ w_s_down
    return routed + shared


def _layer(x, mod, lp, ctx_k=None, ctx_v=None, h0=None):
    latent = ctx_k is not None
    B, L, _ = x.shape
    shift1, scale1, gate1, shift2, scale2, gate2 = mod
    h = _rmsnorm(x, lp['norm1_g']) * (1 + scale1) + shift1
    q, k, v, u_ssm, g_b, g_c, u_conv = jnp.split(h @ lp['w_in'], SPLIT_POINTS, axis=-1)
    q = q.reshape(B, L, N_KV_HEADS, GROUP, HEAD_DIM)
    k = k.reshape(B, L, N_KV_HEADS, HEAD_DIM)
    v = v.reshape(B, L, N_KV_HEADS, HEAD_DIM)
    sink = lp['attn_sink'].reshape(N_KV_HEADS, GROUP)
    if latent:
        tables = _axial_rope_tables(L)
        attn = _latent_attention(_apply_axial_rope(q, tables), _apply_axial_rope(k, tables), v, ctx_k, ctx_v, sink)
    else:
        attn = _context_attention(q, k, v, sink)
    ssm, h_final = _s5_bidirectional(u_ssm, lp['ssm_lam_re'], lp['ssm_lam_im'], lp['ssm_log_dt'], lp['ssm_b_re'], lp['ssm_b_im'], lp['ssm_c_re'], lp['ssm_c_im'], lp['ssm_d'], h0)
    ssm = jax.nn.gelu(ssm)
    ssm = ssm * jax.nn.sigmoid(ssm @ lp['ssm_w_glu'])
    conv = g_b * _short_conv3(g_c * u_conv, lp['conv_w'])
    gates = jax.nn.sigmoid(h @ lp['w_gates'] + lp['b_gates']).reshape(B, L, N_BRANCHES, D_MODEL)
    merged = (gates[:, :, 0] * (ssm @ lp['w_br_ssm'])
              + gates[:, :, 1] * (attn @ lp['w_br_attn'])
              + gates[:, :, 2] * (conv @ lp['w_br_conv']))
    x = x + gate1 * (merged @ lp['w_out'])
    h2 = _rmsnorm(x, lp['norm2_g']) * (1 + scale2) + shift2
    ffn = _moe(h2.reshape(B * L, D_MODEL), lp['w_router'], lp['b_router'], lp['w_e_gate'], lp['w_e_up'], lp['w_e_down'], lp['w_s_gate'], lp['w_s_up'], lp['w_s_down'])
    x = x + gate2 * ffn.reshape(B, L, D_MODEL)
    if latent:
        return x, None
    return x, (k, v, h_final)


def setup_inputs(seed: int = 0) -> dict:
    f32 = jnp.float32
    key = jax.random.key(seed)
    keys = iter(jax.random.split(key, 64))

    def nrm(shape, scale):
        return jax.random.normal(next(keys), shape, f32) * scale

    def gain(shape):
        return 1.0 + nrm(shape, 0.02)

    Dm = D_MODEL
    n_idx = jnp.arange(SSM_STATE, dtype=f32)
    ssm_shape = (DEPTH, 2, SSM_GROUPS, SSM_STATE)
    return {
        'x_prompt': nrm((BATCH, SEQ, Dm), 1.0),
        'x_sample': nrm((DEC_BATCH, DEC_SEQ, Dm), 1.0),
        'c': nrm((DEC_BATCH, Dm), 1.0),
        'cache_k': nrm((DEC_BATCH, DEPTH, PAST_LEN, N_KV_HEADS, HEAD_DIM), 1.0),
        'cache_v': nrm((DEC_BATCH, DEPTH, PAST_LEN, N_KV_HEADS, HEAD_DIM), 1.0),
        'state_ssm_re': nrm((DEC_BATCH, DEPTH, 2, SSM_GROUPS, SSM_STATE), 0.1),
        'state_ssm_im': nrm((DEC_BATCH, DEPTH, 2, SSM_GROUPS, SSM_STATE), 0.1),
        'c_ctx': nrm((Dm,), 1.0),
        'w_ada': nrm((DEPTH, Dm, 6 * Dm), 0.5 * Dm ** -0.5),
        'b_ada': nrm((DEPTH, 6 * Dm), 0.02),
        'norm1_g': gain((DEPTH, Dm)),
        'norm2_g': gain((DEPTH, Dm)),
        'w_in': nrm((DEPTH, Dm, IN_WIDTH), Dm ** -0.5),
        'w_gates': nrm((DEPTH, Dm, N_BRANCHES * Dm), Dm ** -0.5),
        'b_gates': nrm((DEPTH, N_BRANCHES * Dm), 0.02),
        'ssm_lam_re': -0.5 + nrm(ssm_shape, 0.01),
        'ssm_lam_im': jnp.pi * n_idx + nrm(ssm_shape, 0.01),
        'ssm_log_dt': jax.random.uniform(next(keys), (DEPTH, 2, SSM_GROUPS), f32, math.log(1e-3), math.log(1e-1)),
        'ssm_b_re': nrm((DEPTH, 2, SSM_GROUPS, SSM_STATE, SSM_CH), (2 * SSM_CH) ** -0.5),
        'ssm_b_im': nrm((DEPTH, 2, SSM_GROUPS, SSM_STATE, SSM_CH), (2 * SSM_CH) ** -0.5),
        'ssm_c_re': nrm((DEPTH, 2, SSM_GROUPS, SSM_CH, SSM_STATE), (2 * SSM_STATE) ** -0.5),
        'ssm_c_im': nrm((DEPTH, 2, SSM_GROUPS, SSM_CH, SSM_STATE), (2 * SSM_STATE) ** -0.5),
        'ssm_d': nrm((DEPTH, SSM_WIDTH), 1.0),
        'ssm_w_glu': nrm((DEPTH, SSM_WIDTH, SSM_WIDTH), SSM_WIDTH ** -0.5),
        'conv_w': nrm((DEPTH, CONV_K, CONV_WIDTH), CONV_K ** -0.5),
        'attn_sink': nrm((DEPTH, N_HEADS), 1.0),
        'w_br_ssm': nrm((DEPTH, SSM_WIDTH, Dm), SSM_WIDTH ** -0.5),
        'w_br_attn': nrm((DEPTH, ATTN_WIDTH, Dm), ATTN_WIDTH ** -0.5),
        'w_br_conv': nrm((DEPTH, CONV_WIDTH, Dm), CONV_WIDTH ** -0.5),
        'w_out': nrm((DEPTH, Dm, Dm), Dm ** -0.5),
        'w_router': nrm((DEPTH, Dm, N_EXPERTS), Dm ** -0.5),
        'b_router': nrm((DEPTH, N_EXPERTS), 0.01),
        'w_e_gate': nrm((DEPTH, N_EXPERTS, Dm, D_EXPERT), Dm ** -0.5),
        'w_e_up': nrm((DEPTH, N_EXPERTS, Dm, D_EXPERT), Dm ** -0.5),
        'w_e_down': nrm((DEPTH, N_EXPERTS, D_EXPERT, Dm), D_EXPERT ** -0.5),
        'w_s_gate': nrm((DEPTH, Dm, D_SHARED), Dm ** -0.5),
        'w_s_up': nrm((DEPTH, Dm, D_SHARED), Dm ** -0.5),
        'w_s_down': nrm((DEPTH, D_SHARED, Dm), D_SHARED ** -0.5),
        'final_g': gain((Dm,)),
    }


def reference(x_prompt, x_sample, c, cache_k, cache_v, state_ssm_re, state_ssm_im, c_ctx, w_ada, b_ada, norm1_g, norm2_g, w_in, w_gates, b_gates, ssm_lam_re, ssm_lam_im, ssm_log_dt, ssm_b_re, ssm_b_im, ssm_c_re, ssm_c_im, ssm_d, ssm_w_glu, conv_w, attn_sink, w_br_ssm, w_br_attn, w_br_conv, w_out, w_router, b_router, w_e_gate, w_e_up, w_e_down, w_s_gate, w_s_up, w_s_down, final_g):
    x_p = x_prompt
    x_s = x_sample
    ks, vs, s_re, s_im = [], [], [], []
    for l in range(DEPTH):
        lp = dict(norm1_g=norm1_g[l], norm2_g=norm2_g[l], w_in=w_in[l], w_gates=w_gates[l], b_gates=b_gates[l],
                  ssm_lam_re=ssm_lam_re[l], ssm_lam_im=ssm_lam_im[l], ssm_log_dt=ssm_log_dt[l],
                  ssm_b_re=ssm_b_re[l], ssm_b_im=ssm_b_im[l], ssm_c_re=ssm_c_re[l], ssm_c_im=ssm_c_im[l],
                  ssm_d=ssm_d[l], ssm_w_glu=ssm_w_glu[l], conv_w=conv_w[l], attn_sink=attn_sink[l],
                  w_br_ssm=w_br_ssm[l], w_br_attn=w_br_attn[l], w_br_conv=w_br_conv[l], w_out=w_out[l],
                  w_router=w_router[l], b_router=b_router[l], w_e_gate=w_e_gate[l], w_e_up=w_e_up[l],
                  w_e_down=w_e_down[l], w_s_gate=w_s_gate[l], w_s_up=w_s_up[l], w_s_down=w_s_down[l])
        x_p, (k_l, v_l, h_l) = _layer(x_p, _adaln(c_ctx[None, :], w_ada[l], b_ada[l]), lp)
        ks.append(k_l)
        vs.append(v_l)
        s_re.append(jnp.real(h_l))
        s_im.append(jnp.imag(h_l))
        h0 = lax.complex(state_ssm_re[:, l].astype(jnp.float32), state_ssm_im[:, l].astype(jnp.float32))
        x_s, _ = _layer(x_s, _adaln(c, w_ada[l], b_ada[l]), lp, cache_k[:, l], cache_v[:, l], h0)
    y_prompt = _rmsnorm(x_p, final_g)
    y_sample = _rmsnorm(x_s, final_g)
    new_cache_k = jnp.stack(ks, axis=1)
    new_cache_v = jnp.stack(vs, axis=1)
    new_state_ssm_re = jnp.stack(s_re, axis=1)
    new_state_ssm_im = jnp.stack(s_im, axis=1)
    return (y_prompt, y_sample, new_cache_k, new_cache_v, new_state_ssm_re, new_state_ssm_im)
```

```python
import functools
import math

import jax
import jax.numpy as jnp
from jax import lax
from jax.experimental import pallas as pl
from jax.experimental.pallas import tpu as pltpu

HEAD_DIM = 128
N_HEADS = 8
N_KV_HEADS = 2
GROUP = N_HEADS // N_KV_HEADS
ATTN_WIDTH = N_HEADS * HEAD_DIM
KV_WIDTH = N_KV_HEADS * HEAD_DIM
WINDOW = 128
ATTN_BLOCK = 128
ATTN_SCALE = HEAD_DIM ** -0.5
ROPE_BASE = 10000.0
ROT_F = HEAD_DIM // 4
GRID_W = 64
SSM_WIDTH = 512
SSM_CH = 16
SSM_GROUPS = SSM_WIDTH // SSM_CH
SSM_STATE = 64
SSM_CHUNK = 16
CONV_WIDTH = 512
N_BRANCHES = 3
IN_WIDTH = ATTN_WIDTH + 2 * KV_WIDTH + SSM_WIDTH + 3 * CONV_WIDTH
N_EXPERTS = 64
TOP_K = 6
N_EXPERT_GROUPS = 8
TOPK_GROUPS = 4
D_EXPERT = 512
ROUTED_SCALE = 2.5
EPS = 1e-6
NEG_INF = -1e30

COL_TILE = 512
MOD_ROWS = 16
EXPERT_ROWS = 256
VMEM_LIMIT_V7X = 56 * 1024 * 1024

F32 = jnp.float32
BF16 = jnp.bfloat16
I32 = jnp.int32
U32 = jnp.uint32


def _cparams(sem, vmem=VMEM_LIMIT_V7X):
    return pltpu.CompilerParams(dimension_semantics=sem, vmem_limit_bytes=vmem)


def _sigmoid(x):
    return 1.0 / (1.0 + jnp.exp(-x))


def _pack_bf16_pairs(v):
    n = v.shape[1] // 2
    hi = lax.bitcast_convert_type(v[:, :n].astype(BF16).astype(F32), U32)
    lo = lax.bitcast_convert_type(v[:, n:].astype(BF16).astype(F32), U32)
    return hi | (lo >> 16)


def _unpack_bf16_pairs(p):
    hi = lax.bitcast_convert_type(p & jnp.uint32(0xFFFF0000), F32)
    lo = lax.bitcast_convert_type(p << 16, F32)
    return hi, lo


def _adaln_kernel(c_ref, w_ref, b_ref, o_ref):
    c = c_ref[...]
    s = (c * _sigmoid(c)).astype(BF16)
    o_ref[0] = jnp.dot(s, w_ref[0].astype(BF16), preferred_element_type=F32) + b_ref[0]


def _adaln(cvec, w_ada, b_ada):
    depth, d, n6 = w_ada.shape
    tn = math.gcd(1024, n6)
    return pl.pallas_call(
        _adaln_kernel,
        grid=(depth, n6 // tn),
        in_specs=[
            pl.BlockSpec((MOD_ROWS, d), lambda l, n: (0, 0)),
            pl.BlockSpec((1, d, tn), lambda l, n: (l, 0, n)),
            pl.BlockSpec((1, 1, tn), lambda l, n: (l, 0, n)),
        ],
        out_specs=pl.BlockSpec((1, MOD_ROWS, tn), lambda l, n: (l, 0, n)),
        out_shape=jax.ShapeDtypeStruct((depth, MOD_ROWS, n6), F32),
        compiler_params=_cparams(("arbitrary", "arbitrary")),
    )(cvec, w_ada, b_ada.reshape(depth, 1, n6))


class _Layout:
    def __init__(self, n_ctx_seq, len_ctx, n_lat_seq, len_lat, d_model):
        self.bc, self.lc, self.bs, self.ls, self.d = n_ctx_seq, len_ctx, n_lat_seq, len_lat, d_model
        self.tc = n_ctx_seq * len_ctx
        self.ts = n_lat_seq * len_lat
        self.t = self.tc + self.ts

    def row_tile(self, want):
        tm = math.gcd(math.gcd(self.tc, self.ls), want)
        assert tm % 16 == 0
        return tm

    def mod_index(self, i, tm):
        nct, tps = self.tc // tm, self.ls // tm
        return jnp.where(i < nct, 0, 1 + (i - nct) // tps)

    def seq_pos(self, rows, i, tm):
        is_lat = i >= self.tc // tm
        return jnp.where(is_lat, (rows - self.tc) % self.ls, rows % self.lc), jnp.where(is_lat, self.ls, self.lc)


def _mod_spec(lay, tm, slot, d):
    return pl.BlockSpec((1, 1, d), lambda i, *_: (lay.mod_index(i, tm) * 6 + slot, 0, 0))


def _rope(z, cos, sin_signed, first_half):
    swapped = jnp.where(first_half, pltpu.roll(z, HEAD_DIM - ROT_F, 1), pltpu.roll(z, ROT_F, 1))
    return z * cos + swapped * sin_signed


def _inproj_kernel(x_ref, shift_ref, scale_ref, g_ref, w_ref, cos_ref, sin_ref, h_ref, y_ref, kv_ref):
    n = pl.program_id(1)
    n_q = ATTN_WIDTH // COL_TILE

    @pl.when(n == 0)
    def _():
        x = x_ref[...]
        ms = jnp.mean(x * x, axis=-1, keepdims=True)
        y = x * lax.rsqrt(ms + EPS) * g_ref[...]
        h_ref[...] = (y * (1.0 + scale_ref[0]) + shift_ref[0]).astype(BF16)

    acc = jnp.dot(h_ref[...], w_ref[...], preferred_element_type=F32)

    def rotated(n_heads):
        cos, sin = cos_ref[...], sin_ref[...]
        first_half = (lax.broadcasted_iota(I32, cos.shape, 1) % (2 * ROT_F)) < ROT_F
        parts = [_rope(acc[:, s * HEAD_DIM:(s + 1) * HEAD_DIM], cos, sin, first_half) for s in range(n_heads)]
        parts.append(acc[:, n_heads * HEAD_DIM:])
        return jnp.concatenate(parts, axis=1) if n_heads * HEAD_DIM < COL_TILE else jnp.concatenate(parts[:-1], axis=1)

    @pl.when(n < n_q)
    def _():
        y_ref[...] = rotated(COL_TILE // HEAD_DIM).astype(BF16)

    @pl.when(n == n_q)
    def _():
        kv_ref[...] = acc
        y_ref[...] = rotated(N_KV_HEADS).astype(BF16)

    @pl.when(n > n_q)
    def _():
        y_ref[...] = acc.astype(BF16)


def _inproj(lay, x, mod_l, g1, w_in16, rope_cos, rope_sin):
    d = lay.d
    tm = lay.row_tile(1024)
    nct, tps = lay.tc // tm, lay.ls // tm

    def rope_idx(i, n):
        return (jnp.where(i < nct, 0, 1 + (i - nct) % tps), 0)

    return pl.pallas_call(
        _inproj_kernel,
        grid=(lay.t // tm, IN_WIDTH // COL_TILE),
        in_specs=[
            pl.BlockSpec((tm, d), lambda i, n: (i, 0)),
            _mod_spec(lay, tm, 0, d),
            _mod_spec(lay, tm, 1, d),
            pl.BlockSpec((1, d), lambda i, n: (0, 0)),
            pl.BlockSpec((d, COL_TILE), lambda i, n: (0, n)),
            pl.BlockSpec((tm, HEAD_DIM), rope_idx),
            pl.BlockSpec((tm, HEAD_DIM), rope_idx),
        ],
        out_specs=[
            pl.BlockSpec((tm, d), lambda i, n: (i, 0)),
            pl.BlockSpec((tm, COL_TILE), lambda i, n: (i, n)),
            pl.BlockSpec((tm, 2 * KV_WIDTH), lambda i, n: (i, 0)),
        ],
        out_shape=[
            jax.ShapeDtypeStruct((lay.t, d), BF16),
            jax.ShapeDtypeStruct((lay.t, IN_WIDTH), BF16),
            jax.ShapeDtypeStruct((lay.t, 2 * KV_WIDTH), F32),
        ],
        compiler_params=_cparams(("arbitrary", "arbitrary")),
    )(x, mod_l, mod_l, g1.reshape(1, d), w_in16, rope_cos, rope_sin)


def _rope_tables(lay, tm):
    t = jnp.arange(lay.ls)
    row = (t // GRID_W).astype(F32)
    col = (t % GRID_W).astype(F32)
    inv = ROPE_BASE ** (-jnp.arange(ROT_F, dtype=F32) / ROT_F)
    ar, ac = row[:, None] * inv, col[:, None] * inv
    cos = jnp.concatenate([jnp.cos(ar), jnp.cos(ar), jnp.cos(ac), jnp.cos(ac)], axis=1)
    sin = jnp.concatenate([-jnp.sin(ar), jnp.sin(ar), -jnp.sin(ac), jnp.sin(ac)], axis=1)
    cos = jnp.concatenate([jnp.ones((tm, HEAD_DIM), F32), cos], axis=0)
    sin = jnp.concatenate([jnp.zeros((tm, HEAD_DIM), F32), sin], axis=0)
    return cos, sin


def _attend(q_ref, sink_ref, j, parts, o_ref):
    q = q_ref[...]
    nq = q.shape[0]
    q4 = jnp.concatenate([q[:, g * HEAD_DIM:(g + 1) * HEAD_DIM] for g in range(GROUP)], axis=0)
    sink = jnp.concatenate([jnp.full((nq, 1), sink_ref[j * GROUP + g], F32) for g in range(GROUP)], axis=0)
    scores = []
    m = sink
    for k, _, mask in parts:
        s = lax.dot_general(q4, k, (((1,), (1,)), ((), ())), preferred_element_type=F32) * ATTN_SCALE
        if mask is not None:
            s = jnp.where(mask, s, NEG_INF)
        scores.append(s)
        m = jnp.maximum(m, jnp.max(s, axis=-1, keepdims=True))
    den = jnp.exp(sink - m)
    out = jnp.zeros((GROUP * nq, HEAD_DIM), F32)
    for s, (_, v, _) in zip(scores, parts):
        p = jnp.exp(s - m)
        den = den + jnp.sum(p, axis=-1, keepdims=True)
        out = out + jnp.dot(p.astype(BF16), v, preferred_element_type=F32)
    out = out / den
    o_ref[...] = jnp.concatenate([out[g * nq:(g + 1) * nq] for g in range(GROUP)], axis=1).astype(o_ref.dtype)


def _attn_ctx_kernel(sink_ref, q_ref, k_ref, v_ref, o_ref):
    _attend(q_ref, sink_ref, pl.program_id(1), [(k_ref[...], v_ref[...], None)], o_ref)


def _attn_lat_kernel(sink_ref, q_ref, kp_ref, kc_ref, kn_ref, vp_ref, vc_ref, vn_ref, ck_ref, cv_ref, o_ref, *, seq_len):
    i = pl.program_id(2)
    kw = jnp.concatenate([kp_ref[...], kc_ref[...], kn_ref[...]], axis=0)
    vw = jnp.concatenate([vp_ref[...], vc_ref[...], vn_ref[...]], axis=0)
    shape = (GROUP * ATTN_BLOCK, 3 * ATTN_BLOCK)
    qoff = lax.broadcasted_iota(I32, shape, 0) % ATTN_BLOCK
    koff = lax.broadcasted_iota(I32, shape, 1) - ATTN_BLOCK
    kabs = koff + i * ATTN_BLOCK
    mask = (jnp.abs(qoff - koff) <= WINDOW) & (kabs >= 0) & (kabs < seq_len)
    parts = [(kw, vw, mask), (ck_ref[...].astype(BF16), cv_ref[...].astype(BF16), None)]
    _attend(q_ref, sink_ref, pl.program_id(1), parts, o_ref)


def _attention(lay, y16, sink, cache_k_l, cache_v_l):
    smem = pl.BlockSpec(memory_space=pltpu.SMEM)
    qw = GROUP * HEAD_DIM
    kcol, vcol = ATTN_WIDTH // HEAD_DIM, (ATTN_WIDTH + KV_WIDTH) // HEAD_DIM
    ctx = pl.pallas_call(
        _attn_ctx_kernel,
        grid=(lay.bc, N_KV_HEADS),
        in_specs=[
            smem,
            pl.BlockSpec((lay.lc, qw), lambda b, j: (b, j)),
            pl.BlockSpec((lay.lc, HEAD_DIM), lambda b, j: (b, kcol + j)),
            pl.BlockSpec((lay.lc, HEAD_DIM), lambda b, j: (b, vcol + j)),
        ],
        out_specs=pl.BlockSpec((lay.lc, qw), lambda b, j: (b, j)),
        out_shape=jax.ShapeDtypeStruct((lay.tc, ATTN_WIDTH), BF16),
        compiler_params=_cparams(("arbitrary", "arbitrary")),
    )(sink, y16, y16, y16)

    nblk = lay.ls // ATTN_BLOCK
    base = lay.tc // ATTN_BLOCK
    last = lay.t // ATTN_BLOCK - 1
    past = cache_k_l.shape[1]

    def rb(b, i):
        return base + b * nblk + i

    def kspec(col, delta):
        return pl.BlockSpec((ATTN_BLOCK, HEAD_DIM),
                            lambda b, j, i: (jnp.clip(rb(b, i) + delta, 0, last), col + j))

    cspec = pl.BlockSpec((None, past, HEAD_DIM), lambda b, j, i: (b, 0, j))
    lat = pl.pallas_call(
        functools.partial(_attn_lat_kernel, seq_len=lay.ls),
        grid=(lay.bs, N_KV_HEADS, nblk),
        in_specs=[
            smem,
            pl.BlockSpec((ATTN_BLOCK, qw), lambda b, j, i: (rb(b, i), j)),
            kspec(kcol, -1), kspec(kcol, 0), kspec(kcol, 1),
            kspec(vcol, -1), kspec(vcol, 0), kspec(vcol, 1),
            cspec, cspec,
        ],
        out_specs=pl.BlockSpec((ATTN_BLOCK, qw), lambda b, j, i: (b * nblk + i, j)),
        out_shape=jax.ShapeDtypeStruct((lay.ts, ATTN_WIDTH), BF16),
        compiler_params=_cparams(("arbitrary", "arbitrary", "arbitrary")),
    )(sink, y16, y16, y16, y16, y16, y16, y16,
      cache_k_l.reshape(lay.bs, past, KV_WIDTH), cache_v_l.reshape(lay.bs, past, KV_WIDTH))
    return jnp.concatenate([ctx, lat], axis=0)


def _s5_matrices(lam_re, lam_im, log_dt, b_re, b_im, c_re, c_im, d_skip, n_steps):
    q, p, g, n = SSM_CHUNK, SSM_CH, SSM_GROUPS, SSM_STATE
    lam = lax.complex(lam_re.astype(F32), lam_im.astype(F32))
    dt = jnp.exp(log_dt.astype(F32))[..., None]
    lam_dt = lam * dt
    lam_bar = jnp.exp(lam_dt)
    b_bar = ((lam_bar - 1.0) / lam)[..., None] * lax.complex(b_re.astype(F32), b_im.astype(F32))
    c_mat = lax.complex(c_re.astype(F32), c_im.astype(F32))
    steps = jnp.arange(q + 1, dtype=F32)
    pw = jnp.exp(lam_dt[:, None] * steps[None, :, None, None])
    kern = jnp.real(jnp.einsum('dgpn,dkgn,dgnr->dkgpr', c_mat, pw[:, :q], b_bar))
    tau_in = jnp.arange(q)[:, None]
    tau_out = jnp.arange(q)[None, :]
    lag_f = tau_out - tau_in
    lag_b = tau_in - tau_out
    kf = jnp.where((lag_f >= 0)[:, :, None, None, None], kern[0][jnp.clip(lag_f, 0, q - 1)], 0.0)
    kb = jnp.where((lag_b >= 0)[:, :, None, None, None], kern[1][jnp.clip(lag_b, 0, q - 1)], 0.0)
    m = (kf + kb).transpose(2, 0, 4, 1, 3)
    eye_q = jnp.eye(q, dtype=F32)[None, :, None, :, None]
    eye_p = jnp.eye(p, dtype=F32)[None, None, :, None, :]
    m = m + eye_q * eye_p * d_skip.astype(F32).reshape(g, 1, p, 1, 1)
    m = m.reshape(g, q * p, q * p)
    ws_f = pw[0, :q][::-1][:, :, :, None] * b_bar[0][None]
    ws_b = pw[1, :q][:, :, :, None] * b_bar[1][None]

    def cols(w):
        return w.transpose(1, 0, 3, 2).reshape(g, q * p, n)

    w1 = jnp.concatenate([m, jnp.real(cols(ws_f)), jnp.real(cols(ws_b)),
                          jnp.imag(cols(ws_f)), jnp.imag(cols(ws_b))], axis=2)
    cy_f = c_mat[0][None] * pw[0, 1:][:, :, None, :]
    cy_b = c_mat[1][None] * pw[1, 1:][::-1][:, :, None, :]

    def rows(w):
        return w.transpose(1, 3, 0, 2).reshape(g, n, q * p)

    wy = jnp.concatenate([jnp.real(rows(cy_f)), jnp.real(rows(cy_b)),
                          -jnp.imag(rows(cy_f)), -jnp.imag(rows(cy_b))], axis=1)
    hops = (q * 2.0 ** jnp.arange(8, dtype=F32))[None, :, None, None]
    a = jnp.exp(lam_dt[:, None] * hops)
    a = jnp.concatenate([a[0], a[1]], axis=-1).transpose(1, 0, 2)
    assert n_steps <= 8
    return w1.astype(BF16), wy.astype(BF16), jnp.real(a), jnp.imag(a)


def _s5_kernel(u_ref, w1_ref, wy_ref, are_ref, aim_ref, h0re_ref, h0im_ref, y_ref, fre_ref, fim_ref, *, nb, nc):
    r_tot = nb * nc
    w = SSM_CHUNK * SSM_CH
    n2 = 2 * SSM_STATE
    proj = jnp.dot(u_ref[0], w1_ref[0], preferred_element_type=F32)
    y_intra = proj[:, :w]
    d_re = proj[:, w:w + n2]
    d_im = proj[:, w + n2:]
    row = lax.broadcasted_iota(I32, (r_tot, n2), 0)
    chunk = row % nc
    seq = row // nc
    fwd = lax.broadcasted_iota(I32, (r_tot, n2), 1) < SSM_STATE

    def previous(x, dist):
        valid = (fwd & (chunk >= dist)) | (~fwd & (chunk < nc - dist))
        moved = jnp.where(fwd, pltpu.roll(x, dist, 0), pltpu.roll(x, r_tot - dist, 0))
        return jnp.where(valid, moved, 0.0)

    h0_re = jnp.zeros((r_tot, n2), F32)
    h0_im = jnp.zeros((r_tot, n2), F32)
    for b in range(nb):
        h0_re = jnp.where(seq == b, h0re_ref[0, b:b + 1, :], h0_re)
        h0_im = jnp.where(seq == b, h0im_ref[0, b:b + 1, :], h0_im)
    first = (fwd & (chunk == 0)) | (~fwd & (chunk == nc - 1))
    e_re = jnp.where(first, h0_re, previous(d_re, 1))
    e_im = jnp.where(first, h0_im, previous(d_im, 1))
    k = 0
    while (1 << k) < nc:
        a_re = are_ref[0, k:k + 1, :]
        a_im = aim_ref[0, k:k + 1, :]
        p_re = previous(e_re, 1 << k)
        p_im = previous(e_im, 1 << k)
        e_re, e_im = e_re + a_re * p_re - a_im * p_im, e_im + a_re * p_im + a_im * p_re
        k += 1
    e_cat = jnp.concatenate([e_re, e_im], axis=1).astype(BF16)
    y_ref[0] = y_intra + jnp.dot(e_cat, wy_ref[0], preferred_element_type=F32)
    a_re = are_ref[0, 0:1, :]
    a_im = aim_ref[0, 0:1, :]
    f_re = a_re * e_re - a_im * e_im + d_re
    f_im = a_re * e_im + a_im * e_re + d_im
    fwd_row = fwd[0:1, :]
    for b in range(nb):
        lo, hi = b * nc, b * nc + nc - 1
        fre_ref[0, b:b + 1, :] = jnp.where(fwd_row, f_re[hi:hi + 1, :], f_re[lo:lo + 1, :])
        fim_ref[0, b:b + 1, :] = jnp.where(fwd_row, f_im[hi:hi + 1, :], f_im[lo:lo + 1, :])


def _s5_scan(u, mats, h0_re, h0_im):
    nb, length, _ = u.shape
    g, q, p, n = SSM_GROUPS, SSM_CHUNK, SSM_CH, SSM_STATE
    nc = length // q
    assert nc & (nc - 1) == 0 and nc % 8 == 0
    w1, wy, a_re, a_im = mats
    ut = u.reshape(nb, nc, q, g, p).transpose(3, 0, 1, 2, 4).reshape(g, nb * nc, q * p)

    def lanes(h):
        return h.astype(F32).transpose(2, 0, 1, 3).reshape(g, nb, 2 * n)

    r_tot = nb * nc
    per_g = lambda *shape: pl.BlockSpec((1,) + shape, lambda i: (i,) + (0,) * len(shape))
    y, f_re, f_im = pl.pallas_call(
        functools.partial(_s5_kernel, nb=nb, nc=nc),
        grid=(g,),
        in_specs=[per_g(r_tot, q * p), per_g(q * p, q * p + 4 * n), per_g(4 * n, q * p),
                  per_g(8, 2 * n), per_g(8, 2 * n), per_g(nb, 2 * n), per_g(nb, 2 * n)],
        out_specs=[per_g(r_tot, q * p), per_g(nb, 2 * n), per_g(nb, 2 * n)],
        out_shape=[jax.ShapeDtypeStruct((g, r_tot, q * p), F32),
                   jax.ShapeDtypeStruct((g, nb, 2 * n), F32),
                   jax.ShapeDtypeStruct((g, nb, 2 * n), F32)],
        compiler_params=_cparams(("arbitrary",)),
    )(ut, w1, wy, a_re, a_im, lanes(h0_re), lanes(h0_im))
    y = y.reshape(g, nb, nc, q, p).transpose(1, 2, 3, 0, 4).reshape(nb, length, g * p)

    def unlanes(f):
        return f.reshape(g, nb, 2, n).transpose(1, 2, 0, 3)

    return y, unlanes(f_re), unlanes(f_im)


def _merge_kernel(h_ref, ys_ref, at_ref, gb_ref, gc_ref, uc_ref, gcp_ref, ucp_ref, gcn_ref, ucn_ref, cw_ref,
                  wglu_ref, wg0_ref, wg1_ref, wg2_ref, bg0_ref, bg1_ref, bg2_ref, ws_ref, wa_ref, wc_ref,
                  o_ref, ssm_scr, conv_scr, *, lay, tm, halo):
    i = pl.program_id(0)
    n = pl.program_id(1)

    @pl.when(n == 0)
    def _():
        y = ys_ref[...]
        ge = 0.5 * y * (1.0 + jnp.tanh(math.sqrt(2.0 / math.pi) * (y + 0.044715 * (y * y * y))))
        glu = jnp.dot(ge.astype(BF16), wglu_ref[...], preferred_element_type=F32)
        ssm_scr[...] = (ge * _sigmoid(glu)).astype(BF16)

        z = gc_ref[...].astype(F32) * uc_ref[...].astype(F32)
        z_before = gcp_ref[halo - 1:halo, :].astype(F32) * ucp_ref[halo - 1:halo, :].astype(F32)
        z_after = gcn_ref[0:1, :].astype(F32) * ucn_ref[0:1, :].astype(F32)
        local = lax.broadcasted_iota(I32, z.shape, 0)
        pos, seq_len = lay.seq_pos(local + i * tm, i, tm)
        z_prev = jnp.where(local == 0, z_before, pltpu.roll(z, 1, 0))
        z_prev = jnp.where(pos == 0, 0.0, z_prev)
        z_next = jnp.where(local == tm - 1, z_after, pltpu.roll(z, tm - 1, 0))
        z_next = jnp.where(pos == seq_len - 1, 0.0, z_next)
        conv = cw_ref[0:1, :] * z_prev + cw_ref[1:2, :] * z + cw_ref[2:3, :] * z_next
        conv_scr[...] = (gb_ref[...].astype(F32) * conv).astype(BF16)

    h = h_ref[...]
    acc = None
    for act, wg_ref, bg_ref, wb_ref in ((ssm_scr[...], wg0_ref, bg0_ref, ws_ref),
                                        (at_ref[...], wg1_ref, bg1_ref, wa_ref),
                                        (conv_scr[...], wg2_ref, bg2_ref, wc_ref)):
        gate = _sigmoid(jnp.dot(h, wg_ref[...], preferred_element_type=F32) + bg_ref[...])
        term = gate * jnp.dot(act, wb_ref[...], preferred_element_type=F32)
        acc = term if acc is None else acc + term
    o_ref[...] = acc.astype(BF16)


def _merge(lay, h16, y_ssm, attn, y16, conv_w, wglu16, wgates16, b_gates, wbs16, wba16, wbc16):
    d = lay.d
    tm = lay.row_tile(512)
    tn = min(COL_TILE, d)
    nd = d // tn
    halo = 16
    hb = tm // halo
    last_h = lay.t // halo - 1
    c0 = (ATTN_WIDTH + 2 * KV_WIDTH + SSM_WIDTH) // CONV_WIDTH
    row = lambda cb: pl.BlockSpec((tm, CONV_WIDTH), lambda i, n: (i, cb))
    before = lambda cb: pl.BlockSpec((halo, CONV_WIDTH), lambda i, n: (jnp.maximum(i * hb - 1, 0), cb))
    after = lambda cb: pl.BlockSpec((halo, CONV_WIDTH), lambda i, n: (jnp.minimum((i + 1) * hb, last_h), cb))
    gate_w = lambda br: pl.BlockSpec((d, tn), lambda i, n: (0, br * nd + n))
    gate_b = lambda br: pl.BlockSpec((1, tn), lambda i, n: (0, br * nd + n))
    return pl.pallas_call(
        functools.partial(_merge_kernel, lay=lay, tm=tm, halo=halo),
        grid=(lay.t // tm, nd),
        in_specs=[
            pl.BlockSpec((tm, d), lambda i, n: (i, 0)),
            pl.BlockSpec((tm, SSM_WIDTH), lambda i, n: (i, 0)),
            pl.BlockSpec((tm, ATTN_WIDTH), lambda i, n: (i, 0)),
            row(c0), row(c0 + 1), row(c0 + 2),
            before(c0 + 1), before(c0 + 2), after(c0 + 1), after(c0 + 2),
            pl.BlockSpec((3, CONV_WIDTH), lambda i, n: (0, 0)),
            pl.BlockSpec((SSM_WIDTH, SSM_WIDTH), lambda i, n: (0, 0)),
            gate_w(0), gate_w(1), gate_w(2), gate_b(0), gate_b(1), gate_b(2),
            pl.BlockSpec((SSM_WIDTH, tn), lambda i, n: (0, n)),
            pl.BlockSpec((ATTN_WIDTH, tn), lambda i, n: (0, n)),
            pl.BlockSpec((CONV_WIDTH, tn), lambda i, n: (0, n)),
        ],
        out_specs=pl.BlockSpec((tm, tn), lambda i, n: (i, n)),
        out_shape=jax.ShapeDtypeStruct((lay.t, d), BF16),
        scratch_shapes=[pltpu.VMEM((tm, SSM_WIDTH), BF16), pltpu.VMEM((tm, CONV_WIDTH), BF16)],
        compiler_params=_cparams(("arbitrary", "arbitrary")),
    )(h16, y_ssm, attn, y16, y16, y16, y16, y16, y16, y16, conv_w, wglu16,
      wgates16, wgates16, wgates16, b_gates, b_gates, b_gates, wbs16, wba16, wbc16)


def _outproj_kernel(m_ref, w_ref, x_ref, gate_ref, g2_ref, shift_ref, scale_ref, wr_ref, xo_ref, hp_ref, lg_ref):
    acc = jnp.dot(m_ref[...], w_ref[...], preferred_element_type=F32)
    xn = x_ref[...] + gate_ref[0] * acc
    xo_ref[...] = xn
    ms = jnp.mean(xn * xn, axis=-1, keepdims=True)
    h2 = xn * lax.rsqrt(ms + EPS) * g2_ref[...]
    h2 = h2 * (1.0 + scale_ref[0]) + shift_ref[0]
    hp_ref[...] = _pack_bf16_pairs(h2)
    lg_ref[...] = lax.dot_general(wr_ref[...], h2, (((1,), (1,)), ((), ())),
                                  precision=lax.Precision.HIGHEST, preferred_element_type=F32)


def _outproj(lay, merged, wout16, x, mod_l, g2, w_router_t):
    d = lay.d
    tm = lay.row_tile(256)
    return pl.pallas_call(
        _outproj_kernel,
        grid=(lay.t // tm,),
        in_specs=[
            pl.BlockSpec((tm, d), lambda i: (i, 0)),
            pl.BlockSpec((d, d), lambda i: (0, 0)),
            pl.BlockSpec((tm, d), lambda i: (i, 0)),
            _mod_spec(lay, tm, 2, d),
            pl.BlockSpec((1, d), lambda i: (0, 0)),
            _mod_spec(lay, tm, 3, d),
            _mod_spec(lay, tm, 4, d),
            pl.BlockSpec((N_EXPERTS, d), lambda i: (0, 0)),
        ],
        out_specs=[
            pl.BlockSpec((tm, d), lambda i: (i, 0)),
            pl.BlockSpec((tm, d // 2), lambda i: (i, 0)),
            pl.BlockSpec((N_EXPERTS, tm), lambda i: (0, i)),
        ],
        out_shape=[
            jax.ShapeDtypeStruct((lay.t, d), F32),
            jax.ShapeDtypeStruct((lay.t, d // 2), U32),
            jax.ShapeDtypeStruct((N_EXPERTS, lay.t), F32),
        ],
        compiler_params=_cparams(("arbitrary",)),
    )(merged, wout16, x, mod_l, g2.reshape(1, d), mod_l, mod_l, w_router_t)


def _route_kernel(lg_ref, br_ref, idx_ref, w_ref, pos_ref, cnt_ref, carry):
    step = pl.program_id(0)
    tt = lg_ref.shape[1]
    per_group = N_EXPERTS // N_EXPERT_GROUPS

    @pl.when(step == 0)
    def _():
        carry[...] = jnp.zeros_like(carry)

    scores = _sigmoid(lg_ref[...])
    biased = scores + br_ref[...]
    sub = lax.broadcasted_iota(I32, (per_group, tt), 0).astype(F32)
    blocks, group_score = [], []
    for g in range(N_EXPERT_GROUPS):
        blk = biased[g * per_group:(g + 1) * per_group, :]
        m1 = jnp.max(blk, axis=0, keepdims=True)
        i1 = jnp.min(jnp.where(blk == m1, sub, float(per_group)), axis=0, keepdims=True)
        m2 = jnp.max(jnp.where(sub == i1, -jnp.inf, blk), axis=0, keepdims=True)
        blocks.append(blk)
        group_score.append(m1 + m2)
    masked = []
    for g in range(N_EXPERT_GROUPS):
        beaten_by = jnp.zeros((1, tt), F32)
        for o in range(N_EXPERT_GROUPS):
            if o == g:
                continue
            wins = (group_score[o] > group_score[g]) | ((group_score[o] == group_score[g]) & (o < g))
            beaten_by = beaten_by + wins.astype(F32)
        masked.append(jnp.where(beaten_by < TOPK_GROUPS, blocks[g], -jnp.inf))
    masked = jnp.concatenate(masked, axis=0)
    eid = lax.broadcasted_iota(I32, (N_EXPERTS, tt), 0).astype(F32)
    chosen, weights = [], []
    onehot = jnp.zeros((N_EXPERTS, tt), F32)
    for _ in range(TOP_K):
        m = jnp.max(masked, axis=0, keepdims=True)
        e = jnp.min(jnp.where(masked == m, eid, float(N_EXPERTS)), axis=0, keepdims=True)
        hit = eid == e
        chosen.append(e)
        weights.append(jnp.sum(jnp.where(hit, scores, 0.0), axis=0, keepdims=True))
        onehot = onehot + hit.astype(F32)
        masked = jnp.where(hit, -jnp.inf, masked)
    total = weights[0]
    for wk in weights[1:]:
        total = total + wk
    earlier = (lax.broadcasted_iota(I32, (tt, tt), 0) < lax.broadcasted_iota(I32, (tt, tt), 1)).astype(BF16)
    rank = carry[...][:, 0:1] + jnp.dot(onehot.astype(BF16), earlier, preferred_element_type=F32)
    for k in range(TOP_K):
        idx_ref[k:k + 1, :] = chosen[k].astype(I32)
        w_ref[k:k + 1, :] = weights[k] / total * ROUTED_SCALE
        pos_ref[k:k + 1, :] = jnp.sum(jnp.where(eid == chosen[k], rank, 0.0), axis=0, keepdims=True).astype(I32)
    for k in range(TOP_K, 8):
        idx_ref[k:k + 1, :] = jnp.zeros((1, tt), I32)
        w_ref[k:k + 1, :] = jnp.zeros((1, tt), F32)
        pos_ref[k:k + 1, :] = jnp.zeros((1, tt), I32)
    carry[...] = carry[...] + jnp.sum(onehot, axis=1, keepdims=True)
    cnt_ref[...] = carry[...]


def _route(logits_t, b_router):
    t = logits_t.shape[1]
    tt = math.gcd(t, 512)
    tok = pl.BlockSpec((8, tt), lambda i: (0, i))
    return pl.pallas_call(
        _route_kernel,
        grid=(t // tt,),
        in_specs=[pl.BlockSpec((N_EXPERTS, tt), lambda i: (0, i)),
                  pl.BlockSpec((N_EXPERTS, 1), lambda i: (0, 0))],
        out_specs=[tok, tok, tok, pl.BlockSpec((N_EXPERTS, 128), lambda i: (0, 0))],
        out_shape=[jax.ShapeDtypeStruct((8, t), I32), jax.ShapeDtypeStruct((8, t), F32),
                   jax.ShapeDtypeStruct((8, t), I32), jax.ShapeDtypeStruct((N_EXPERTS, 128), F32)],
        scratch_shapes=[pltpu.VMEM((N_EXPERTS, 128), F32)],
        compiler_params=_cparams(("arbitrary",)),
    )(logits_t, b_router.astype(F32).reshape(N_EXPERTS, 1))


def _row_copy(src_ref, src_row, dst_ref, dst_row, sem):
    return pltpu.make_async_copy(src_ref.at[pl.ds(src_row, 1)], dst_ref.at[pl.ds(dst_row, 1)], sem)


def _dispatch_kernel(dest_ref, h_ref, zero_ref, xs_ref, sem, *, tt):
    del zero_ref

    def body(t, carry):
        for k in range(TOP_K):
            _row_copy(h_ref, t, xs_ref, dest_ref[0, 0, k * tt + t], sem).start()
        return carry

    lax.fori_loop(0, tt, body, 0)

    def drain(t, carry):
        for k in range(TOP_K):
            _row_copy(h_ref, t, xs_ref, dest_ref[0, 0, k * tt + t], sem).wait()
        return carry

    lax.fori_loop(0, tt, drain, 0)


def _dispatch(h2p, dest_tiles, n_rows, tt):
    t, half = h2p.shape
    zeros = jnp.zeros((n_rows, half), U32)
    return pl.pallas_call(
        functools.partial(_dispatch_kernel, tt=tt),
        grid=(t // tt,),
        in_specs=[pl.BlockSpec((1, 1, TOP_K * tt), lambda i: (i, 0, 0), memory_space=pltpu.SMEM),
                  pl.BlockSpec((tt, half), lambda i: (i, 0)),
                  pl.BlockSpec(memory_space=pl.ANY)],
        out_specs=pl.BlockSpec(memory_space=pl.ANY),
        out_shape=jax.ShapeDtypeStruct((n_rows, half), U32),
        scratch_shapes=[pltpu.SemaphoreType.DMA],
        input_output_aliases={2: 0},
        compiler_params=_cparams(("arbitrary",)),
    )(dest_tiles, h2p, zeros)


def _expert_kernel(be_ref, xs_ref, wg_ref, wu_ref, wd_ref, ys_ref, wg16, wu16, wd16):
    i = pl.program_id(0)
    changed = jnp.logical_or(i == 0, be_ref[i] != be_ref[jnp.maximum(i - 1, 0)])

    @pl.when(changed)
    def _():
        rows = 256
        d = wg16.shape[0]

        def cast_in(r, carry):
            sl = pl.ds(pl.multiple_of(r * rows, rows), rows)
            wg16[sl, :] = wg_ref[0, sl, :].astype(BF16)
            wu16[sl, :] = wu_ref[0, sl, :].astype(BF16)
            return carry

        lax.fori_loop(0, d // rows, cast_in, 0)

        def cast_down(r, carry):
            sl = pl.ds(pl.multiple_of(r * 128, 128), 128)
            wd16[sl, :] = wd_ref[0, sl, :].astype(BF16)
            return carry

        lax.fori_loop(0, D_EXPERT // 128, cast_down, 0)

    hi, lo = _unpack_bf16_pairs(xs_ref[...])
    x = jnp.concatenate([hi, lo], axis=1).astype(BF16)
    gate = jnp.dot(x, wg16[...], preferred_element_type=F32)
    up = jnp.dot(x, wu16[...], preferred_element_type=F32)
    act = (gate * _sigmoid(gate) * up).astype(BF16)
    ys_ref[...] = _pack_bf16_pairs(jnp.dot(act, wd16[...], preferred_element_type=F32))


def _experts(xs, block_e, w_gate, w_up, w_down):
    n_rows, half = xs.shape
    d = 2 * half
    br = EXPERT_ROWS
    grid_spec = pltpu.PrefetchScalarGridSpec(
        num_scalar_prefetch=1,
        grid=(n_rows // br,),
        in_specs=[
            pl.BlockSpec((br, half), lambda i, be: (i, 0)),
            pl.BlockSpec((1, d, D_EXPERT), lambda i, be: (be[i], 0, 0)),
            pl.BlockSpec((1, d, D_EXPERT), lambda i, be: (be[i], 0, 0)),
            pl.BlockSpec((1, D_EXPERT, d), lambda i, be: (be[i], 0, 0)),
        ],
        out_specs=pl.BlockSpec((br, half), lambda i, be: (i, 0)),
        scratch_shapes=[pltpu.VMEM((d, D_EXPERT), BF16), pltpu.VMEM((d, D_EXPERT), BF16),
                        pltpu.VMEM((D_EXPERT, d), BF16)],
    )
    return pl.pallas_call(
        _expert_kernel,
        grid_spec=grid_spec,
        out_shape=jax.ShapeDtypeStruct((n_rows, half), U32),
        compiler_params=_cparams(("arbitrary",)),
    )(block_e, xs, w_gate, w_up, w_down)


def _combine_kernel(dest_ref, ys_ref, wsel_ref, hp_ref, x_ref, gate_ref, wsg_ref, wsu_ref, wsd_ref, fg_ref,
                    o_ref, buf, sem, *, tt, final):
    def issue(t, carry):
        for k in range(TOP_K):
            _row_copy(ys_ref, dest_ref[0, 0, k * tt + t], buf.at[k], t, sem).start()
        return carry

    lax.fori_loop(0, tt, issue, 0)

    hi, lo = _unpack_bf16_pairs(hp_ref[...])
    h2 = jnp.concatenate([hi, lo], axis=1).astype(BF16)
    sg = jnp.dot(h2, wsg_ref[...], preferred_element_type=F32)
    su = jnp.dot(h2, wsu_ref[...], preferred_element_type=F32)
    shared = jnp.dot((sg * _sigmoid(sg) * su).astype(BF16), wsd_ref[...], preferred_element_type=F32)

    def drain(t, carry):
        for k in range(TOP_K):
            _row_copy(ys_ref, dest_ref[0, 0, k * tt + t], buf.at[k], t, sem).wait()
        return carry

    lax.fori_loop(0, tt, drain, 0)

    half = hp_ref.shape[1]
    r_hi = jnp.zeros((tt, half), F32)
    r_lo = jnp.zeros((tt, half), F32)
    for k in range(TOP_K):
        y_hi, y_lo = _unpack_bf16_pairs(buf[k])
        wk = wsel_ref[:, k:k + 1]
        r_hi = r_hi + wk * y_hi
        r_lo = r_lo + wk * y_lo
    routed = jnp.concatenate([r_hi, r_lo], axis=1)
    out = x_ref[...] + gate_ref[0] * (routed + shared)
    if final:
        ms = jnp.mean(out * out, axis=-1, keepdims=True)
        out = out * lax.rsqrt(ms + EPS) * fg_ref[...]
    o_ref[...] = out


def _combine(lay, ys, dest_tiles, wsel, h2p, x, mod_l, wsg16, wsu16, wsd16, final_g, tt, final):
    d = lay.d
    half = d // 2
    return pl.pallas_call(
        functools.partial(_combine_kernel, tt=tt, final=final),
        grid=(lay.t // tt,),
        in_specs=[
            pl.BlockSpec((1, 1, TOP_K * tt), lambda i: (i, 0, 0), memory_space=pltpu.SMEM),
            pl.BlockSpec(memory_space=pl.ANY),
            pl.BlockSpec((tt, 8), lambda i: (i, 0)),
            pl.BlockSpec((tt, half), lambda i: (i, 0)),
            pl.BlockSpec((tt, d), lambda i: (i, 0)),
            _mod_spec(lay, tt, 5, d),
            pl.BlockSpec((d, D_EXPERT), lambda i: (0, 0)),
            pl.BlockSpec((d, D_EXPERT), lambda i: (0, 0)),
            pl.BlockSpec((D_EXPERT, d), lambda i: (0, 0)),
            pl.BlockSpec((1, d), lambda i: (0, 0)),
        ],
        out_specs=pl.BlockSpec((tt, d), lambda i: (i, 0)),
        out_shape=jax.ShapeDtypeStruct((lay.t, d), F32),
        scratch_shapes=[pltpu.VMEM((TOP_K, tt, half), U32), pltpu.SemaphoreType.DMA],
        compiler_params=_cparams(("arbitrary",)),
    )(dest_tiles, ys, wsel, h2p, x, mod_l, wsg16, wsu16, wsd16, final_g.reshape(1, d))


def _moe(lay, x, h2p, logits_t, mod_l, b_router, w_e_gate, w_e_up, w_e_down, wsg16, wsu16, wsd16, final_g, final):
    t = lay.t
    idx_t, w_t, pos_t, counts = _route(logits_t, b_router)
    br = EXPERT_ROWS
    counts = counts[:, 0].astype(I32)
    padded = (counts + br - 1) // br * br
    pad_end = jnp.cumsum(padded)
    pad_start = pad_end - padded
    n_blocks = -(-(t * TOP_K + N_EXPERTS * (br - 1)) // br)
    n_rows = n_blocks * br
    dest = pad_start[idx_t[:TOP_K]] + pos_t[:TOP_K]
    block_e = jnp.minimum(jnp.searchsorted(pad_end, jnp.arange(n_blocks, dtype=I32) * br, side='right'),
                          N_EXPERTS - 1).astype(I32)
    tt = lay.row_tile(256)
    dest_tiles = dest.reshape(TOP_K, t // tt, tt).transpose(1, 0, 2).reshape(t // tt, 1, TOP_K * tt)
    xs = _dispatch(h2p, dest_tiles, n_rows, tt)
    ys = _experts(xs, block_e, w_e_gate, w_e_up, w_e_down)
    return _combine(lay, ys, dest_tiles, w_t.T, h2p, x, mod_l, wsg16, wsu16, wsd16, final_g, tt, final)


def kernel(x_prompt, x_sample, c, cache_k, cache_v, state_ssm_re, state_ssm_im, c_ctx, w_ada, b_ada, norm1_g, norm2_g, w_in, w_gates, b_gates, ssm_lam_re, ssm_lam_im, ssm_log_dt, ssm_b_re, ssm_b_im, ssm_c_re, ssm_c_im, ssm_d, ssm_w_glu, conv_w, attn_sink, w_br_ssm, w_br_attn, w_br_conv, w_out, w_router, b_router, w_e_gate, w_e_up, w_e_down, w_s_gate, w_s_up, w_s_down, final_g):
    bc, lc, d = x_prompt.shape
    bs, ls, _ = x_sample.shape
    depth = w_in.shape[0]
    lay = _Layout(bc, lc, bs, ls, d)
    assert 1 + bs <= MOD_ROWS

    x = jnp.concatenate([x_prompt.reshape(lay.tc, d), x_sample.reshape(lay.ts, d)], axis=0)
    cvec = jnp.zeros((MOD_ROWS, d), F32).at[0].set(c_ctx).at[1:1 + bs].set(c)
    mod = _adaln(cvec, w_ada, b_ada).reshape(depth, MOD_ROWS * 6, 1, d)
    rope_cos, rope_sin = _rope_tables(lay, lay.row_tile(1024))
    zeros_state = jnp.zeros((bc, 2, SSM_GROUPS, SSM_STATE), F32)

    ks, vs, s_re, s_im = [], [], [], []
    for l in range(depth):
        mod_l = mod[l]
        h16, y16, kv32 = _inproj(lay, x, mod_l, norm1_g[l], w_in[l].astype(BF16), rope_cos, rope_sin)
        ks.append(kv32[:lay.tc, :KV_WIDTH].reshape(bc, lc, N_KV_HEADS, HEAD_DIM))
        vs.append(kv32[:lay.tc, KV_WIDTH:].reshape(bc, lc, N_KV_HEADS, HEAD_DIM))

        attn = _attention(lay, y16, attn_sink[l].astype(F32), cache_k[:, l], cache_v[:, l])

        u0 = ATTN_WIDTH + 2 * KV_WIDTH
        u = y16[:, u0:u0 + SSM_WIDTH]
        mats = _s5_matrices(ssm_lam_re[l], ssm_lam_im[l], ssm_log_dt[l], ssm_b_re[l], ssm_b_im[l],
                            ssm_c_re[l], ssm_c_im[l], ssm_d[l], 8)
        y_c, f_re, f_im = _s5_scan(u[:lay.tc].reshape(bc, lc, SSM_WIDTH), mats, zeros_state, zeros_state)
        y_s, _, _ = _s5_scan(u[lay.tc:].reshape(bs, ls, SSM_WIDTH), mats, state_ssm_re[:, l], state_ssm_im[:, l])
        s_re.append(f_re)
        s_im.append(f_im)
        y_ssm = jnp.concatenate([y_c.reshape(lay.tc, SSM_WIDTH), y_s.reshape(lay.ts, SSM_WIDTH)], axis=0)

        merged = _merge(lay, h16, y_ssm, attn, y16, conv_w[l], ssm_w_glu[l].astype(BF16),
                        w_gates[l].astype(BF16), b_gates[l].reshape(1, -1), w_br_ssm[l].astype(BF16),
                        w_br_attn[l].astype(BF16), w_br_conv[l].astype(BF16))
        x, h2p, logits_t = _outproj(lay, merged, w_out[l].astype(BF16), x, mod_l, norm2_g[l],
                                    w_router[l].astype(F32).T)
        x = _moe(lay, x, h2p, logits_t, mod_l, b_router[l], w_e_gate[l], w_e_up[l], w_e_down[l],
                 w_s_gate[l].astype(BF16), w_s_up[l].astype(BF16), w_s_down[l].astype(BF16),
                 final_g, l == depth - 1)

    y_prompt = x[:lay.tc].reshape(bc, lc, d)
    y_sample = x[lay.tc:].reshape(bs, ls, d)
    return (y_prompt, y_sample, jnp.stack(ks, axis=1), jnp.stack(vs, axis=1),
            jnp.stack(s_re, axis=1), jnp.stack(s_im, axis=1))
```

```python
import functools
import math

import jax
import jax.numpy as jnp
from jax import lax
from jax.experimental import pallas as pl
from jax.experimental.pallas import tpu as pltpu

HEAD_DIM = 128
N_HEADS = 8
N_KV_HEADS = 2
GROUP = N_HEADS // N_KV_HEADS
ATTN_WIDTH = N_HEADS * HEAD_DIM
KV_WIDTH = N_KV_HEADS * HEAD_DIM
WINDOW = 128
ATTN_BLOCK = 128
ATTN_SCALE = HEAD_DIM ** -0.5
ROPE_BASE = 10000.0
ROT_F = HEAD_DIM // 4
GRID_W = 64
SSM_WIDTH = 512
SSM_CH = 16
SSM_GROUPS = SSM_WIDTH // SSM_CH
SSM_STATE = 64
SSM_CHUNK = 16
CONV_WIDTH = 512
N_BRANCHES = 3
IN_WIDTH = ATTN_WIDTH + 2 * KV_WIDTH + SSM_WIDTH + 3 * CONV_WIDTH
N_EXPERTS = 64
TOP_K = 6
N_EXPERT_GROUPS = 8
TOPK_GROUPS = 4
D_EXPERT = 512
ROUTED_SCALE = 2.5
EPS = 1e-6
NEG_INF = -1e30

COL_TILE = 512
MOD_ROWS = 16
EXPERT_ROWS = 256
VMEM_LIMIT_V7X = 56 * 1024 * 1024

F32 = jnp.float32
BF16 = jnp.bfloat16
I32 = jnp.int32
U32 = jnp.uint32


def _cparams(sem, vmem=VMEM_LIMIT_V7X):
    return pltpu.CompilerParams(dimension_semantics=sem, vmem_limit_bytes=vmem)


def _sigmoid(x):
    return 1.0 / (1.0 + jnp.exp(-x))


def _pack_bf16_pairs(v):
    n = v.shape[1] // 2
    hi = lax.bitcast_convert_type(v[:, :n].astype(BF16).astype(F32), U32)
    lo = lax.bitcast_convert_type(v[:, n:].astype(BF16).astype(F32), U32)
    return hi | (lo >> 16)


def _unpack_bf16_pairs(p):
    hi = lax.bitcast_convert_type(p & jnp.uint32(0xFFFF0000), F32)
    lo = lax.bitcast_convert_type(p << 16, F32)
    return hi, lo


def _adaln_kernel(c_ref, w_ref, b_ref, o_ref):
    c = c_ref[...]
    s = (c * _sigmoid(c)).astype(BF16)
    o_ref[0] = jnp.dot(s, w_ref[0].astype(BF16), preferred_element_type=F32) + b_ref[0]


def _adaln(cvec, w_ada, b_ada):
    depth, d, n6 = w_ada.shape
    tn = math.gcd(1024, n6)
    return pl.pallas_call(
        _adaln_kernel,
        grid=(depth, n6 // tn),
        in_specs=[
            pl.BlockSpec((MOD_ROWS, d), lambda l, n: (0, 0)),
            pl.BlockSpec((1, d, tn), lambda l, n: (l, 0, n)),
            pl.BlockSpec((1, 1, tn), lambda l, n: (l, 0, n)),
        ],
        out_specs=pl.BlockSpec((1, MOD_ROWS, tn), lambda l, n: (l, 0, n)),
        out_shape=jax.ShapeDtypeStruct((depth, MOD_ROWS, n6), F32),
        compiler_params=_cparams(("arbitrary", "arbitrary")),
    )(cvec, w_ada, b_ada.reshape(depth, 1, n6))


class _Layout:
    def __init__(self, n_ctx_seq, len_ctx, n_lat_seq, len_lat, d_model):
        self.bc, self.lc, self.bs, self.ls, self.d = n_ctx_seq, len_ctx, n_lat_seq, len_lat, d_model
        self.tc = n_ctx_seq * len_ctx
        self.ts = n_lat_seq * len_lat
        self.t = self.tc + self.ts

    def row_tile(self, want):
        tm = math.gcd(math.gcd(self.tc, self.ls), want)
        assert tm % 16 == 0
        return tm

    def mod_index(self, i, tm):
        nct, tps = self.tc // tm, self.ls // tm
        return jnp.where(i < nct, 0, 1 + (i - nct) // tps)

    def seq_pos(self, rows, i, tm):
        is_lat = i >= self.tc // tm
        return jnp.where(is_lat, (rows - self.tc) % self.ls, rows % self.lc), jnp.where(is_lat, self.ls, self.lc)


def _mod_spec(lay, tm, slot, d):
    return pl.BlockSpec((1, 1, d), lambda i, *_: (lay.mod_index(i, tm) * 6 + slot, 0, 0))


def _rope(z, cos, sin_signed, first_half):
    swapped = jnp.where(first_half, pltpu.roll(z, HEAD_DIM - ROT_F, 1), pltpu.roll(z, ROT_F, 1))
    return z * cos + swapped * sin_signed


def _inproj_kernel(x_ref, shift_ref, scale_ref, g_ref, w_ref, cos_ref, sin_ref, h_ref, y_ref, kv_ref):
    n = pl.program_id(1)
    n_q = ATTN_WIDTH // COL_TILE

    @pl.when(n == 0)
    def _():
        x = x_ref[...]
        ms = jnp.mean(x * x, axis=-1, keepdims=True)
        y = x * lax.rsqrt(ms + EPS) * g_ref[...]
        h_ref[...] = (y * (1.0 + scale_ref[0]) + shift_ref[0]).astype(BF16)

    acc = jnp.dot(h_ref[...], w_ref[...], preferred_element_type=F32)

    def rotated(n_heads):
        cos, sin = cos_ref[...], sin_ref[...]
        first_half = (lax.broadcasted_iota(I32, cos.shape, 1) % (2 * ROT_F)) < ROT_F
        parts = [_rope(acc[:, s * HEAD_DIM:(s + 1) * HEAD_DIM], cos, sin, first_half) for s in range(n_heads)]
        parts.append(acc[:, n_heads * HEAD_DIM:])
        return jnp.concatenate(parts, axis=1) if n_heads * HEAD_DIM < COL_TILE else jnp.concatenate(parts[:-1], axis=1)

    @pl.when(n < n_q)
    def _():
        y_ref[...] = rotated(COL_TILE // HEAD_DIM).astype(BF16)

    @pl.when(n == n_q)
    def _():
        kv_ref[...] = acc
        y_ref[...] = rotated(N_KV_HEADS).astype(BF16)

    @pl.when(n > n_q)
    def _():
        y_ref[...] = acc.astype(BF16)


def _inproj(lay, x, mod_l, g1, w_in16, rope_cos, rope_sin):
    d = lay.d
    tm = lay.row_tile(1024)
    nct, tps = lay.tc // tm, lay.ls // tm

    def rope_idx(i, n):
        return (jnp.where(i < nct, 0, 1 + (i - nct) % tps), 0)

    return pl.pallas_call(
        _inproj_kernel,
        grid=(lay.t // tm, IN_WIDTH // COL_TILE),
        in_specs=[
            pl.BlockSpec((tm, d), lambda i, n: (i, 0)),
            _mod_spec(lay, tm, 0, d),
            _mod_spec(lay, tm, 1, d),
            pl.BlockSpec((1, d), lambda i, n: (0, 0)),
            pl.BlockSpec((d, COL_TILE), lambda i, n: (0, n)),
            pl.BlockSpec((tm, HEAD_DIM), rope_idx),
            pl.BlockSpec((tm, HEAD_DIM), rope_idx),
        ],
        out_specs=[
            pl.BlockSpec((tm, d), lambda i, n: (i, 0)),
            pl.BlockSpec((tm, COL_TILE), lambda i, n: (i, n)),
            pl.BlockSpec((tm, 2 * KV_WIDTH), lambda i, n: (i, 0)),
        ],
        out_shape=[
            jax.ShapeDtypeStruct((lay.t, d), BF16),
            jax.ShapeDtypeStruct((lay.t, IN_WIDTH), BF16),
            jax.ShapeDtypeStruct((lay.t, 2 * KV_WIDTH), F32),
        ],
        compiler_params=_cparams(("arbitrary", "arbitrary")),
    )(x, mod_l, mod_l, g1.reshape(1, d), w_in16, rope_cos, rope_sin)


def _rope_tables(lay, tm):
    t = jnp.arange(lay.ls)
    row = (t // GRID_W).astype(F32)
    col = (t % GRID_W).astype(F32)
    inv = ROPE_BASE ** (-jnp.arange(ROT_F, dtype=F32) / ROT_F)
    ar, ac = row[:, None] * inv, col[:, None] * inv
    cos = jnp.concatenate([jnp.cos(ar), jnp.cos(ar), jnp.cos(ac), jnp.cos(ac)], axis=1)
    sin = jnp.concatenate([-jnp.sin(ar), jnp.sin(ar), -jnp.sin(ac), jnp.sin(ac)], axis=1)
    cos = jnp.concatenate([jnp.ones((tm, HEAD_DIM), F32), cos], axis=0)
    sin = jnp.concatenate([jnp.zeros((tm, HEAD_DIM), F32), sin], axis=0)
    return cos, sin


def _attend(q_ref, sink_ref, j, parts, o_ref):
    q = q_ref[...]
    nq = q.shape[0]
    q4 = jnp.concatenate([q[:, g * HEAD_DIM:(g + 1) * HEAD_DIM] for g in range(GROUP)], axis=0)
    sink = jnp.concatenate([jnp.full((nq, 1), sink_ref[j * GROUP + g], F32) for g in range(GROUP)], axis=0)
    scores = []
    m = sink
    for k, _, mask in parts:
        s = lax.dot_general(q4, k, (((1,), (1,)), ((), ())), preferred_element_type=F32) * ATTN_SCALE
        if mask is not None:
            s = jnp.where(mask, s, NEG_INF)
        scores.append(s)
        m = jnp.maximum(m, jnp.max(s, axis=-1, keepdims=True))
    den = jnp.exp(sink - m)
    out = jnp.zeros((GROUP * nq, HEAD_DIM), F32)
    for s, (_, v, _) in zip(scores, parts):
        p = jnp.exp(s - m)
        den = den + jnp.sum(p, axis=-1, keepdims=True)
        out = out + jnp.dot(p.astype(BF16), v, preferred_element_type=F32)
    out = out / den
    o_ref[...] = jnp.concatenate([out[g * nq:(g + 1) * nq] for g in range(GROUP)], axis=1).astype(o_ref.dtype)


def _attn_ctx_kernel(sink_ref, q_ref, k_ref, v_ref, o_ref):
    _attend(q_ref, sink_ref, pl.program_id(1), [(k_ref[...], v_ref[...], None)], o_ref)


def _attn_lat_kernel(sink_ref, q_ref, kp_ref, kc_ref, kn_ref, vp_ref, vc_ref, vn_ref, ck_ref, cv_ref, o_ref, *, seq_len):
    i = pl.program_id(2)
    kw = jnp.concatenate([kp_ref[...], kc_ref[...], kn_ref[...]], axis=0)
    vw = jnp.concatenate([vp_ref[...], vc_ref[...], vn_ref[...]], axis=0)
    shape = (GROUP * ATTN_BLOCK, 3 * ATTN_BLOCK)
    qoff = lax.broadcasted_iota(I32, shape, 0) % ATTN_BLOCK
    koff = lax.broadcasted_iota(I32, shape, 1) - ATTN_BLOCK
    kabs = koff + i * ATTN_BLOCK
    mask = (jnp.abs(qoff - koff) <= WINDOW) & (kabs >= 0) & (kabs < seq_len)
    parts = [(kw, vw, mask), (ck_ref[...].astype(BF16), cv_ref[...].astype(BF16), None)]
    _attend(q_ref, sink_ref, pl.program_id(1), parts, o_ref)


def _attention(lay, y16, sink, cache_k_l, cache_v_l):
    smem = pl.BlockSpec(memory_space=pltpu.SMEM)
    qw = GROUP * HEAD_DIM
    kcol, vcol = ATTN_WIDTH // HEAD_DIM, (ATTN_WIDTH + KV_WIDTH) // HEAD_DIM
    ctx = pl.pallas_call(
        _attn_ctx_kernel,
        grid=(lay.bc, N_KV_HEADS),
        in_specs=[
            smem,
            pl.BlockSpec((lay.lc, qw), lambda b, j: (b, j)),
            pl.BlockSpec((lay.lc, HEAD_DIM), lambda b, j: (b, kcol + j)),
            pl.BlockSpec((lay.lc, HEAD_DIM), lambda b, j: (b, vcol + j)),
        ],
        out_specs=pl.BlockSpec((lay.lc, qw), lambda b, j: (b, j)),
        out_shape=jax.ShapeDtypeStruct((lay.tc, ATTN_WIDTH), BF16),
        compiler_params=_cparams(("arbitrary", "arbitrary")),
    )(sink, y16, y16, y16)

    nblk = lay.ls // ATTN_BLOCK
    base = lay.tc // ATTN_BLOCK
    last = lay.t // ATTN_BLOCK - 1
    past = cache_k_l.shape[1]

    def rb(b, i):
        return base + b * nblk + i

    def kspec(col, delta):
        return pl.BlockSpec((ATTN_BLOCK, HEAD_DIM),
                            lambda b, j, i: (jnp.clip(rb(b, i) + delta, 0, last), col + j))

    cspec = pl.BlockSpec((None, past, HEAD_DIM), lambda b, j, i: (b, 0, j))
    lat = pl.pallas_call(
        functools.partial(_attn_lat_kernel, seq_len=lay.ls),
        grid=(lay.bs, N_KV_HEADS, nblk),
        in_specs=[
            smem,
            pl.BlockSpec((ATTN_BLOCK, qw), lambda b, j, i: (rb(b, i), j)),
            kspec(kcol, -1), kspec(kcol, 0), kspec(kcol, 1),
            kspec(vcol, -1), kspec(vcol, 0), kspec(vcol, 1),
            cspec, cspec,
        ],
        out_specs=pl.BlockSpec((ATTN_BLOCK, qw), lambda b, j, i: (b * nblk + i, j)),
        out_shape=jax.ShapeDtypeStruct((lay.ts, ATTN_WIDTH), BF16),
        compiler_params=_cparams(("arbitrary", "arbitrary", "arbitrary")),
    )(sink, y16, y16, y16, y16, y16, y16, y16,
      cache_k_l.reshape(lay.bs, past, KV_WIDTH), cache_v_l.reshape(lay.bs, past, KV_WIDTH))
    return jnp.concatenate([ctx, lat], axis=0)


def _s5_matrices(lam_re, lam_im, log_dt, b_re, b_im, c_re, c_im, d_skip, n_steps):
    q, p, g, n = SSM_CHUNK, SSM_CH, SSM_GROUPS, SSM_STATE
    lam = lax.complex(lam_re.astype(F32), lam_im.astype(F32))
    dt = jnp.exp(log_dt.astype(F32))[..., None]
    lam_dt = lam * dt
    lam_bar = jnp.exp(lam_dt)
    b_bar = ((lam_bar - 1.0) / lam)[..., None] * lax.complex(b_re.astype(F32), b_im.astype(F32))
    c_mat = lax.complex(c_re.astype(F32), c_im.astype(F32))
    steps = jnp.arange(q + 1, dtype=F32)
    pw = jnp.exp(lam_dt[:, None] * steps[None, :, None, None])
    kern = jnp.real(jnp.einsum('dgpn,dkgn,dgnr->dkgpr', c_mat, pw[:, :q], b_bar))
    tau_in = jnp.arange(q)[:, None]
    tau_out = jnp.arange(q)[None, :]
    lag_f = tau_out - tau_in
    lag_b = tau_in - tau_out
    kf = jnp.where((lag_f >= 0)[:, :, None, None, None], kern[0][jnp.clip(lag_f, 0, q - 1)], 0.0)
    kb = jnp.where((lag_b >= 0)[:, :, None, None, None], kern[1][jnp.clip(lag_b, 0, q - 1)], 0.0)
    m = (kf + kb).transpose(2, 0, 4, 1, 3)
    eye_q = jnp.eye(q, dtype=F32)[None, :, None, :, None]
    eye_p = jnp.eye(p, dtype=F32)[None, None, :, None, :]
    m = m + eye_q * eye_p * d_skip.astype(F32).reshape(g, 1, p, 1, 1)
    m = m.reshape(g, q * p, q * p)
    ws_f = pw[0, :q][::-1][:, :, :, None] * b_bar[0][None]
    ws_b = pw[1, :q][:, :, :, None] * b_bar[1][None]

    def cols(w):
        return w.transpose(1, 0, 3, 2).reshape(g, q * p, n)

    w1 = jnp.concatenate([m, jnp.real(cols(ws_f)), jnp.real(cols(ws_b)),
                          jnp.imag(cols(ws_f)), jnp.imag(cols(ws_b))], axis=2)
    cy_f = c_mat[0][None] * pw[0, 1:][:, :, None, :]
    cy_b = c_mat[1][None] * pw[1, 1:][::-1][:, :, None, :]

    def rows(w):
        return w.transpose(1, 3, 0, 2).reshape(g, n, q * p)

    wy = jnp.concatenate([jnp.real(rows(cy_f)), jnp.real(rows(cy_b)),
                          -jnp.imag(rows(cy_f)), -jnp.imag(rows(cy_b))], axis=1)
    hops = (q * 2.0 ** jnp.arange(8, dtype=F32))[None, :, None, None]
    a = jnp.exp(lam_dt[:, None] * hops)
    a = jnp.concatenate([a[0], a[1]], axis=-1).transpose(1, 0, 2)
    assert n_steps <= 8
    return w1.astype(BF16), wy.astype(BF16), jnp.real(a), jnp.imag(a)


def _s5_kernel(u_ref, w1_ref, wy_ref, are_ref, aim_ref, h0re_ref, h0im_ref, y_ref, fre_ref, fim_ref, *, nb, nc):
    r_tot = nb * nc
    w = SSM_CHUNK * SSM_CH
    n2 = 2 * SSM_STATE
    proj = jnp.dot(u_ref[0], w1_ref[0], preferred_element_type=F32)
    y_intra = proj[:, :w]
    d_re = proj[:, w:w + n2]
    d_im = proj[:, w + n2:]
    row = lax.broadcasted_iota(I32, (r_tot, n2), 0)
    chunk = row % nc
    seq = row // nc
    fwd = lax.broadcasted_iota(I32, (r_tot, n2), 1) < SSM_STATE

    def previous(x, dist):
        valid = (fwd & (chunk >= dist)) | (~fwd & (chunk < nc - dist))
        moved = jnp.where(fwd, pltpu.roll(x, dist, 0), pltpu.roll(x, r_tot - dist, 0))
        return jnp.where(valid, moved, 0.0)

    h0_re = jnp.zeros((r_tot, n2), F32)
    h0_im = jnp.zeros((r_tot, n2), F32)
    for b in range(nb):
        h0_re = jnp.where(seq == b, h0re_ref[0, b:b + 1, :], h0_re)
        h0_im = jnp.where(seq == b, h0im_ref[0, b:b + 1, :], h0_im)
    first = (fwd & (chunk == 0)) | (~fwd & (chunk == nc - 1))
    e_re = jnp.where(first, h0_re, previous(d_re, 1))
    e_im = jnp.where(first, h0_im, previous(d_im, 1))
    k = 0
    while (1 << k) < nc:
        a_re = are_ref[0, k:k + 1, :]
        a_im = aim_ref[0, k:k + 1, :]
        p_re = previous(e_re, 1 << k)
        p_im = previous(e_im, 1 << k)
        e_re, e_im = e_re + a_re * p_re - a_im * p_im, e_im + a_re * p_im + a_im * p_re
        k += 1
    e_cat = jnp.concatenate([e_re, e_im], axis=1).astype(BF16)
    y_ref[0] = y_intra + jnp.dot(e_cat, wy_ref[0], preferred_element_type=F32)
    a_re = are_ref[0, 0:1, :]
    a_im = aim_ref[0, 0:1, :]
    f_re = a_re * e_re - a_im * e_im + d_re
    f_im = a_re * e_im + a_im * e_re + d_im
    fwd_row = fwd[0:1, :]
    for b in range(nb):
        lo, hi = b * nc, b * nc + nc - 1
        fre_ref[0, b:b + 1, :] = jnp.where(fwd_row, f_re[hi:hi + 1, :], f_re[lo:lo + 1, :])
        fim_ref[0, b:b + 1, :] = jnp.where(fwd_row, f_im[hi:hi + 1, :], f_im[lo:lo + 1, :])


def _s5_scan(u, mats, h0_re, h0_im):
    nb, length, _ = u.shape
    g, q, p, n = SSM_GROUPS, SSM_CHUNK, SSM_CH, SSM_STATE
    nc = length // q
    assert nc & (nc - 1) == 0 and nc % 8 == 0
    w1, wy, a_re, a_im = mats
    ut = u.reshape(nb, nc, q, g, p).transpose(3, 0, 1, 2, 4).reshape(g, nb * nc, q * p)

    def lanes(h):
        return h.astype(F32).transpose(2, 0, 1, 3).reshape(g, nb, 2 * n)

    r_tot = nb * nc
    per_g = lambda *shape: pl.BlockSpec((1,) + shape, lambda i: (i,) + (0,) * len(shape))
    y, f_re, f_im = pl.pallas_call(
        functools.partial(_s5_kernel, nb=nb, nc=nc),
        grid=(g,),
        in_specs=[per_g(r_tot, q * p), per_g(q * p, q * p + 4 * n), per_g(4 * n, q * p),
                  per_g(8, 2 * n), per_g(8, 2 * n), per_g(nb, 2 * n), per_g(nb, 2 * n)],
        out_specs=[per_g(r_tot, q * p), per_g(nb, 2 * n), per_g(nb, 2 * n)],
        out_shape=[jax.ShapeDtypeStruct((g, r_tot, q * p), F32),
                   jax.ShapeDtypeStruct((g, nb, 2 * n), F32),
                   jax.ShapeDtypeStruct((g, nb, 2 * n), F32)],
        compiler_params=_cparams(("arbitrary",)),
    )(ut, w1, wy, a_re, a_im, lanes(h0_re), lanes(h0_im))
    y = y.reshape(g, nb, nc, q, p).transpose(1, 2, 3, 0, 4).reshape(nb, length, g * p)

    def unlanes(f):
        return f.reshape(g, nb, 2, n).transpose(1, 2, 0, 3)

    return y, unlanes(f_re), unlanes(f_im)


def _merge_kernel(h_ref, ys_ref, at_ref, gb_ref, gc_ref, uc_ref, gcp_ref, ucp_ref, gcn_ref, ucn_ref, cw_ref,
                  wglu_ref, wg0_ref, wg1_ref, wg2_ref, bg0_ref, bg1_ref, bg2_ref, ws_ref, wa_ref, wc_ref,
                  o_ref, ssm_scr, conv_scr, *, lay, tm, halo):
    i = pl.program_id(0)
    n = pl.program_id(1)

    @pl.when(n == 0)
    def _():
        y = ys_ref[...]
        ge = 0.5 * y * (1.0 + jnp.tanh(math.sqrt(2.0 / math.pi) * (y + 0.044715 * (y * y * y))))
        glu = jnp.dot(ge.astype(BF16), wglu_ref[...], preferred_element_type=F32)
        ssm_scr[...] = (ge * _sigmoid(glu)).astype(BF16)

        z = gc_ref[...].astype(F32) * uc_ref[...].astype(F32)
        z_before = gcp_ref[halo - 1:halo, :].astype(F32) * ucp_ref[halo - 1:halo, :].astype(F32)
        z_after = gcn_ref[0:1, :].astype(F32) * ucn_ref[0:1, :].astype(F32)
        local = lax.broadcasted_iota(I32, z.shape, 0)
        pos, seq_len = lay.seq_pos(local + i * tm, i, tm)
        z_prev = jnp.where(local == 0, z_before, pltpu.roll(z, 1, 0))
        z_prev = jnp.where(pos == 0, 0.0, z_prev)
        z_next = jnp.where(local == tm - 1, z_after, pltpu.roll(z, tm - 1, 0))
        z_next = jnp.where(pos == seq_len - 1, 0.0, z_next)
        conv = cw_ref[0:1, :] * z_prev + cw_ref[1:2, :] * z + cw_ref[2:3, :] * z_next
        conv_scr[...] = (gb_ref[...].astype(F32) * conv).astype(BF16)

    h = h_ref[...]
    acc = None
    for act, wg_ref, bg_ref, wb_ref in ((ssm_scr[...], wg0_ref, bg0_ref, ws_ref),
                                        (at_ref[...], wg1_ref, bg1_ref, wa_ref),
                                        (conv_scr[...], wg2_ref, bg2_ref, wc_ref)):
        gate = _sigmoid(jnp.dot(h, wg_ref[...], preferred_element_type=F32) + bg_ref[...])
        term = gate * jnp.dot(act, wb_ref[...], preferred_element_type=F32)
        acc = term if acc is None else acc + term
    o_ref[...] = acc.astype(BF16)


def _merge(lay, h16, y_ssm, attn, y16, conv_w, wglu16, wgates16, b_gates, wbs16, wba16, wbc16):
    d = lay.d
    tm = lay.row_tile(512)
    tn = min(COL_TILE, d)
    nd = d // tn
    halo = 16
    hb = tm // halo
    last_h = lay.t // halo - 1
    c0 = (ATTN_WIDTH + 2 * KV_WIDTH + SSM_WIDTH) // CONV_WIDTH
    row = lambda cb: pl.BlockSpec((tm, CONV_WIDTH), lambda i, n: (i, cb))
    before = lambda cb: pl.BlockSpec((halo, CONV_WIDTH), lambda i, n: (jnp.maximum(i * hb - 1, 0), cb))
    after = lambda cb: pl.BlockSpec((halo, CONV_WIDTH), lambda i, n: (jnp.minimum((i + 1) * hb, last_h), cb))
    gate_w = lambda br: pl.BlockSpec((d, tn), lambda i, n: (0, br * nd + n))
    gate_b = lambda br: pl.BlockSpec((1, tn), lambda i, n: (0, br * nd + n))
    return pl.pallas_call(
        functools.partial(_merge_kernel, lay=lay, tm=tm, halo=halo),
        grid=(lay.t // tm, nd),
        in_specs=[
            pl.BlockSpec((tm, d), lambda i, n: (i, 0)),
            pl.BlockSpec((tm, SSM_WIDTH), lambda i, n: (i, 0)),
            pl.BlockSpec((tm, ATTN_WIDTH), lambda i, n: (i, 0)),
            row(c0), row(c0 + 1), row(c0 + 2),
            before(c0 + 1), before(c0 + 2), after(c0 + 1), after(c0 + 2),
            pl.BlockSpec((3, CONV_WIDTH), lambda i, n: (0, 0)),
            pl.BlockSpec((SSM_WIDTH, SSM_WIDTH), lambda i, n: (0, 0)),
            gate_w(0), gate_w(1), gate_w(2), gate_b(0), gate_b(1), gate_b(2),
            pl.BlockSpec((SSM_WIDTH, tn), lambda i, n: (0, n)),
            pl.BlockSpec((ATTN_WIDTH, tn), lambda i, n: (0, n)),
            pl.BlockSpec((CONV_WIDTH, tn), lambda i, n: (0, n)),
        ],
        out_specs=pl.BlockSpec((tm, tn), lambda i, n: (i, n)),
        out_shape=jax.ShapeDtypeStruct((lay.t, d), BF16),
        scratch_shapes=[pltpu.VMEM((tm, SSM_WIDTH), BF16), pltpu.VMEM((tm, CONV_WIDTH), BF16)],
        compiler_params=_cparams(("arbitrary", "arbitrary")),
    )(h16, y_ssm, attn, y16, y16, y16, y16, y16, y16, y16, conv_w, wglu16,
      wgates16, wgates16, wgates16, b_gates, b_gates, b_gates, wbs16, wba16, wbc16)


def _outproj_kernel(m_ref, w_ref, x_ref, gate_ref, g2_ref, shift_ref, scale_ref, wrh_ref, wrl_ref,
                    xo_ref, hp_ref, lg_ref):
    acc = jnp.dot(m_ref[...], w_ref[...], preferred_element_type=F32)
    xn = x_ref[...] + gate_ref[0] * acc
    xo_ref[...] = xn
    ms = jnp.mean(xn * xn, axis=-1, keepdims=True)
    h2 = xn * lax.rsqrt(ms + EPS) * g2_ref[...]
    h2 = h2 * (1.0 + scale_ref[0]) + shift_ref[0]
    hp_ref[...] = _pack_bf16_pairs(h2)
    h_hi = h2.astype(BF16)
    h_lo = (h2 - h_hi.astype(F32)).astype(BF16)
    logits = (jnp.dot(h_hi, wrh_ref[...], preferred_element_type=F32)
              + jnp.dot(h_hi, wrl_ref[...], preferred_element_type=F32)
              + jnp.dot(h_lo, wrh_ref[...], preferred_element_type=F32))
    lg_ref[...] = logits.T[:N_EXPERTS, :]


def _outproj(lay, merged, wout16, x, mod_l, g2, w_router):
    d = lay.d
    tm = lay.row_tile(256)
    lanes = 128
    wr = jnp.zeros((d, lanes), F32).at[:, :N_EXPERTS].set(w_router.astype(F32))
    wr_hi = wr.astype(BF16)
    wr_lo = (wr - wr_hi.astype(F32)).astype(BF16)
    return pl.pallas_call(
        _outproj_kernel,
        grid=(lay.t // tm,),
        in_specs=[
            pl.BlockSpec((tm, d), lambda i: (i, 0)),
            pl.BlockSpec((d, d), lambda i: (0, 0)),
            pl.BlockSpec((tm, d), lambda i: (i, 0)),
            _mod_spec(lay, tm, 2, d),
            pl.BlockSpec((1, d), lambda i: (0, 0)),
            _mod_spec(lay, tm, 3, d),
            _mod_spec(lay, tm, 4, d),
            pl.BlockSpec((d, lanes), lambda i: (0, 0)),
            pl.BlockSpec((d, lanes), lambda i: (0, 0)),
        ],
        out_specs=[
            pl.BlockSpec((tm, d), lambda i: (i, 0)),
            pl.BlockSpec((tm, d // 2), lambda i: (i, 0)),
            pl.BlockSpec((N_EXPERTS, tm), lambda i: (0, i)),
        ],
        out_shape=[
            jax.ShapeDtypeStruct((lay.t, d), F32),
            jax.ShapeDtypeStruct((lay.t, d // 2), U32),
            jax.ShapeDtypeStruct((N_EXPERTS, lay.t), F32),
        ],
        compiler_params=_cparams(("arbitrary",)),
    )(merged, wout16, x, mod_l, g2.reshape(1, d), mod_l, mod_l, wr_hi, wr_lo)


def _route_kernel(lg_ref, br_ref, idx_ref, w_ref, pos_ref, cnt_ref, carry):
    step = pl.program_id(0)
    tt = lg_ref.shape[1]
    per_group = N_EXPERTS // N_EXPERT_GROUPS

    @pl.when(step == 0)
    def _():
        carry[...] = jnp.zeros_like(carry)

    scores = _sigmoid(lg_ref[...])
    biased = scores + br_ref[...]
    sub = lax.broadcasted_iota(I32, (per_group, tt), 0).astype(F32)
    blocks, group_score = [], []
    for g in range(N_EXPERT_GROUPS):
        blk = biased[g * per_group:(g + 1) * per_group, :]
        m1 = jnp.max(blk, axis=0, keepdims=True)
        i1 = jnp.min(jnp.where(blk == m1, sub, float(per_group)), axis=0, keepdims=True)
        m2 = jnp.max(jnp.where(sub == i1, -jnp.inf, blk), axis=0, keepdims=True)
        blocks.append(blk)
        group_score.append(m1 + m2)
    masked = []
    for g in range(N_EXPERT_GROUPS):
        beaten_by = jnp.zeros((1, tt), F32)
        for o in range(N_EXPERT_GROUPS):
            if o == g:
                continue
            wins = (group_score[o] > group_score[g]) | ((group_score[o] == group_score[g]) & (o < g))
            beaten_by = beaten_by + wins.astype(F32)
        masked.append(jnp.where(beaten_by < TOPK_GROUPS, blocks[g], -jnp.inf))
    masked = jnp.concatenate(masked, axis=0)
    eid = lax.broadcasted_iota(I32, (N_EXPERTS, tt), 0).astype(F32)
    chosen, weights = [], []
    onehot = jnp.zeros((N_EXPERTS, tt), F32)
    for _ in range(TOP_K):
        m = jnp.max(masked, axis=0, keepdims=True)
        e = jnp.min(jnp.where(masked == m, eid, float(N_EXPERTS)), axis=0, keepdims=True)
        hit = eid == e
        chosen.append(e)
        weights.append(jnp.sum(jnp.where(hit, scores, 0.0), axis=0, keepdims=True))
        onehot = onehot + hit.astype(F32)
        masked = jnp.where(hit, -jnp.inf, masked)
    total = weights[0]
    for wk in weights[1:]:
        total = total + wk
    earlier = (lax.broadcasted_iota(I32, (tt, tt), 0) < lax.broadcasted_iota(I32, (tt, tt), 1)).astype(BF16)
    rank = carry[...][:, 0:1] + jnp.dot(onehot.astype(BF16), earlier, preferred_element_type=F32)
    for k in range(TOP_K):
        idx_ref[k:k + 1, :] = chosen[k].astype(I32)
        w_ref[k:k + 1, :] = weights[k] / total * ROUTED_SCALE
        pos_ref[k:k + 1, :] = jnp.sum(jnp.where(eid == chosen[k], rank, 0.0), axis=0, keepdims=True).astype(I32)
    for k in range(TOP_K, 8):
        idx_ref[k:k + 1, :] = jnp.zeros((1, tt), I32)
        w_ref[k:k + 1, :] = jnp.zeros((1, tt), F32)
        pos_ref[k:k + 1, :] = jnp.zeros((1, tt), I32)
    carry[...] = carry[...] + jnp.sum(onehot, axis=1, keepdims=True)
    cnt_ref[...] = carry[...]


def _route(logits_t, b_router):
    t = logits_t.shape[1]
    tt = math.gcd(t, 512)
    tok = pl.BlockSpec((8, tt), lambda i: (0, i))
    return pl.pallas_call(
        _route_kernel,
        grid=(t // tt,),
        in_specs=[pl.BlockSpec((N_EXPERTS, tt), lambda i: (0, i)),
                  pl.BlockSpec((N_EXPERTS, 1), lambda i: (0, 0))],
        out_specs=[tok, tok, tok, pl.BlockSpec((N_EXPERTS, 128), lambda i: (0, 0))],
        out_shape=[jax.ShapeDtypeStruct((8, t), I32), jax.ShapeDtypeStruct((8, t), F32),
                   jax.ShapeDtypeStruct((8, t), I32), jax.ShapeDtypeStruct((N_EXPERTS, 128), F32)],
        scratch_shapes=[pltpu.VMEM((N_EXPERTS, 128), F32)],
        compiler_params=_cparams(("arbitrary",)),
    )(logits_t, b_router.astype(F32).reshape(N_EXPERTS, 1))


def _dest_kernel(idx_ref, pos_ref, start_ref, o_ref, *, tt):
    eid = lax.broadcasted_iota(I32, (N_EXPERTS, tt), 0)
    start = start_ref[...]
    for k in range(TOP_K):
        first = jnp.sum(jnp.where(eid == idx_ref[k:k + 1, :], start, 0.0), axis=0, keepdims=True)
        o_ref[0, :, k * tt:(k + 1) * tt] = first.astype(I32) + pos_ref[k:k + 1, :]


def _dest_rows(idx_t, pos_t, pad_start, tt):
    t = idx_t.shape[1]
    tok = pl.BlockSpec((8, tt), lambda i: (0, i))
    return pl.pallas_call(
        functools.partial(_dest_kernel, tt=tt),
        grid=(t // tt,),
        in_specs=[tok, tok, pl.BlockSpec((N_EXPERTS, 1), lambda i: (0, 0))],
        out_specs=pl.BlockSpec((1, 1, TOP_K * tt), lambda i: (i, 0, 0)),
        out_shape=jax.ShapeDtypeStruct((t // tt, 1, TOP_K * tt), I32),
        compiler_params=_cparams(("arbitrary",)),
    )(idx_t, pos_t, pad_start.astype(F32).reshape(N_EXPERTS, 1))


def _row_copy(src_ref, src_row, dst_ref, dst_row, sem):
    return pltpu.make_async_copy(src_ref.at[pl.ds(src_row, 1)], dst_ref.at[pl.ds(dst_row, 1)], sem)


def _dispatch_kernel(dest_ref, h_ref, zero_ref, xs_ref, sem, *, tt):
    del zero_ref

    def body(t, carry):
        for k in range(TOP_K):
            _row_copy(h_ref, t, xs_ref, dest_ref[0, 0, k * tt + t], sem).start()
        return carry

    lax.fori_loop(0, tt, body, 0, unroll=8)
    for _ in range(TOP_K):
        pltpu.make_async_copy(h_ref, xs_ref.at[pl.ds(0, tt)], sem).wait()


def _dispatch(h2p, dest_tiles, n_rows, tt):
    t, half = h2p.shape
    zeros = jnp.zeros((n_rows, half), U32)
    return pl.pallas_call(
        functools.partial(_dispatch_kernel, tt=tt),
        grid=(t // tt,),
        in_specs=[pl.BlockSpec((1, 1, TOP_K * tt), lambda i: (i, 0, 0), memory_space=pltpu.SMEM),
                  pl.BlockSpec((tt, half), lambda i: (i, 0)),
                  pl.BlockSpec(memory_space=pl.ANY)],
        out_specs=pl.BlockSpec(memory_space=pl.ANY),
        out_shape=jax.ShapeDtypeStruct((n_rows, half), U32),
        scratch_shapes=[pltpu.SemaphoreType.DMA],
        input_output_aliases={2: 0},
        compiler_params=_cparams(("arbitrary",)),
    )(dest_tiles, h2p, zeros)


def _expert_kernel(be_ref, xs_ref, wg_ref, wu_ref, wd_ref, ys_ref, wg16, wu16, wd16):
    i = pl.program_id(0)
    changed = jnp.logical_or(i == 0, be_ref[i] != be_ref[jnp.maximum(i - 1, 0)])

    @pl.when(changed)
    def _():
        rows = 256
        d = wg16.shape[0]

        def cast_in(r, carry):
            sl = pl.ds(pl.multiple_of(r * rows, rows), rows)
            wg16[sl, :] = wg_ref[0, sl, :].astype(BF16)
            wu16[sl, :] = wu_ref[0, sl, :].astype(BF16)
            return carry

        lax.fori_loop(0, d // rows, cast_in, 0)

        def cast_down(r, carry):
            sl = pl.ds(pl.multiple_of(r * 128, 128), 128)
            wd16[sl, :] = wd_ref[0, sl, :].astype(BF16)
            return carry

        lax.fori_loop(0, D_EXPERT // 128, cast_down, 0)

    hi, lo = _unpack_bf16_pairs(xs_ref[...])
    x = jnp.concatenate([hi, lo], axis=1).astype(BF16)
    gate = jnp.dot(x, wg16[...], preferred_element_type=F32)
    up = jnp.dot(x, wu16[...], preferred_element_type=F32)
    act = (gate * _sigmoid(gate) * up).astype(BF16)
    ys_ref[...] = _pack_bf16_pairs(jnp.dot(act, wd16[...], preferred_element_type=F32))


def _experts(xs, block_e, layer, w_gate, w_up, w_down):
    n_rows, half = xs.shape
    d = 2 * half
    br = EXPERT_ROWS
    grid_spec = pltpu.PrefetchScalarGridSpec(
        num_scalar_prefetch=1,
        grid=(n_rows // br,),
        in_specs=[
            pl.BlockSpec((br, half), lambda i, be: (i, 0)),
            pl.BlockSpec((None, 1, d, D_EXPERT), lambda i, be: (layer, be[i], 0, 0)),
            pl.BlockSpec((None, 1, d, D_EXPERT), lambda i, be: (layer, be[i], 0, 0)),
            pl.BlockSpec((None, 1, D_EXPERT, d), lambda i, be: (layer, be[i], 0, 0)),
        ],
        out_specs=pl.BlockSpec((br, half), lambda i, be: (i, 0)),
        scratch_shapes=[pltpu.VMEM((d, D_EXPERT), BF16), pltpu.VMEM((d, D_EXPERT), BF16),
                        pltpu.VMEM((D_EXPERT, d), BF16)],
    )
    return pl.pallas_call(
        _expert_kernel,
        grid_spec=grid_spec,
        out_shape=jax.ShapeDtypeStruct((n_rows, half), U32),
        compiler_params=_cparams(("arbitrary",)),
    )(block_e, xs, w_gate, w_up, w_down)


def _combine_kernel(dest_ref, ys_ref, wsel_ref, hp_ref, x_ref, gate_ref, wsg_ref, wsu_ref, wsd_ref, fg_ref,
                    o_ref, buf, sem, *, tt, final):
    def issue(t, carry):
        for k in range(TOP_K):
            _row_copy(ys_ref, dest_ref[0, 0, k * tt + t], buf.at[k], t, sem).start()
        return carry

    lax.fori_loop(0, tt, issue, 0, unroll=8)

    hi, lo = _unpack_bf16_pairs(hp_ref[...])
    h2 = jnp.concatenate([hi, lo], axis=1).astype(BF16)
    sg = jnp.dot(h2, wsg_ref[...], preferred_element_type=F32)
    su = jnp.dot(h2, wsu_ref[...], preferred_element_type=F32)
    shared = jnp.dot((sg * _sigmoid(sg) * su).astype(BF16), wsd_ref[...], preferred_element_type=F32)

    for k in range(TOP_K):
        pltpu.make_async_copy(ys_ref.at[pl.ds(0, tt)], buf.at[k], sem).wait()

    half = hp_ref.shape[1]
    r_hi = jnp.zeros((tt, half), F32)
    r_lo = jnp.zeros((tt, half), F32)
    for k in range(TOP_K):
        y_hi, y_lo = _unpack_bf16_pairs(buf[k])
        wk = wsel_ref[:, k:k + 1]
        r_hi = r_hi + wk * y_hi
        r_lo = r_lo + wk * y_lo
    routed = jnp.concatenate([r_hi, r_lo], axis=1)
    out = x_ref[...] + gate_ref[0] * (routed + shared)
    if final:
        ms = jnp.mean(out * out, axis=-1, keepdims=True)
        out = out * lax.rsqrt(ms + EPS) * fg_ref[...]
    o_ref[...] = out


def _combine(lay, ys, dest_tiles, wsel, h2p, x, mod_l, wsg16, wsu16, wsd16, final_g, tt, final):
    d = lay.d
    half = d // 2
    return pl.pallas_call(
        functools.partial(_combine_kernel, tt=tt, final=final),
        grid=(lay.t // tt,),
        in_specs=[
            pl.BlockSpec((1, 1, TOP_K * tt), lambda i: (i, 0, 0), memory_space=pltpu.SMEM),
            pl.BlockSpec(memory_space=pl.ANY),
            pl.BlockSpec((tt, 8), lambda i: (i, 0)),
            pl.BlockSpec((tt, half), lambda i: (i, 0)),
            pl.BlockSpec((tt, d), lambda i: (i, 0)),
            _mod_spec(lay, tt, 5, d),
            pl.BlockSpec((d, D_EXPERT), lambda i: (0, 0)),
            pl.BlockSpec((d, D_EXPERT), lambda i: (0, 0)),
            pl.BlockSpec((D_EXPERT, d), lambda i: (0, 0)),
            pl.BlockSpec((1, d), lambda i: (0, 0)),
        ],
        out_specs=pl.BlockSpec((tt, d), lambda i: (i, 0)),
        out_shape=jax.ShapeDtypeStruct((lay.t, d), F32),
        scratch_shapes=[pltpu.VMEM((TOP_K, tt, half), U32), pltpu.SemaphoreType.DMA],
        compiler_params=_cparams(("arbitrary",)),
    )(dest_tiles, ys, wsel, h2p, x, mod_l, wsg16, wsu16, wsd16, final_g.reshape(1, d))


def _moe(lay, layer, x, h2p, logits_t, mod_l, b_router, w_e_gate, w_e_up, w_e_down, wsg16, wsu16, wsd16, final_g,
         final):
    t = lay.t
    idx_t, w_t, pos_t, counts = _route(logits_t, b_router)
    br = EXPERT_ROWS
    counts = counts[:, 0].astype(I32)
    padded = (counts + br - 1) // br * br
    pad_end = jnp.cumsum(padded)
    pad_start = pad_end - padded
    n_blocks = -(-(t * TOP_K + N_EXPERTS * (br - 1)) // br)
    n_rows = n_blocks * br
    first_row = jnp.arange(n_blocks, dtype=I32) * br
    block_e = jnp.minimum(jnp.sum((pad_end[None, :] <= first_row[:, None]).astype(I32), axis=1), N_EXPERTS - 1)
    tt = lay.row_tile(256)
    dest_tiles = _dest_rows(idx_t, pos_t, pad_start, tt)
    xs = _dispatch(h2p, dest_tiles, n_rows, tt)
    ys = _experts(xs, block_e, layer, w_e_gate, w_e_up, w_e_down)
    return _combine(lay, ys, dest_tiles, w_t.T, h2p, x, mod_l, wsg16, wsu16, wsd16, final_g, tt, final)


def kernel(x_prompt, x_sample, c, cache_k, cache_v, state_ssm_re, state_ssm_im, c_ctx, w_ada, b_ada, norm1_g, norm2_g, w_in, w_gates, b_gates, ssm_lam_re, ssm_lam_im, ssm_log_dt, ssm_b_re, ssm_b_im, ssm_c_re, ssm_c_im, ssm_d, ssm_w_glu, conv_w, attn_sink, w_br_ssm, w_br_attn, w_br_conv, w_out, w_router, b_router, w_e_gate, w_e_up, w_e_down, w_s_gate, w_s_up, w_s_down, final_g):
    bc, lc, d = x_prompt.shape
    bs, ls, _ = x_sample.shape
    depth = w_in.shape[0]
    lay = _Layout(bc, lc, bs, ls, d)
    assert 1 + bs <= MOD_ROWS

    x = jnp.concatenate([x_prompt.reshape(lay.tc, d), x_sample.reshape(lay.ts, d)], axis=0)
    cvec = jnp.zeros((MOD_ROWS, d), F32).at[0].set(c_ctx).at[1:1 + bs].set(c)
    mod = _adaln(cvec, w_ada, b_ada).reshape(depth, MOD_ROWS * 6, 1, d)
    rope_cos, rope_sin = _rope_tables(lay, lay.row_tile(1024))
    zeros_state = jnp.zeros((bc, 2, SSM_GROUPS, SSM_STATE), F32)

    ks, vs, s_re, s_im = [], [], [], []
    for l in range(depth):
        mod_l = mod[l]
        h16, y16, kv32 = _inproj(lay, x, mod_l, norm1_g[l], w_in[l].astype(BF16), rope_cos, rope_sin)
        ks.append(kv32[:lay.tc, :KV_WIDTH].reshape(bc, lc, N_KV_HEADS, HEAD_DIM))
        vs.append(kv32[:lay.tc, KV_WIDTH:].reshape(bc, lc, N_KV_HEADS, HEAD_DIM))

        attn = _attention(lay, y16, attn_sink[l].astype(F32), cache_k[:, l], cache_v[:, l])

        u0 = ATTN_WIDTH + 2 * KV_WIDTH
        u = y16[:, u0:u0 + SSM_WIDTH]
        mats = _s5_matrices(ssm_lam_re[l], ssm_lam_im[l], ssm_log_dt[l], ssm_b_re[l], ssm_b_im[l],
                            ssm_c_re[l], ssm_c_im[l], ssm_d[l], 8)
        y_c, f_re, f_im = _s5_scan(u[:lay.tc].reshape(bc, lc, SSM_WIDTH), mats, zeros_state, zeros_state)
        y_s, _, _ = _s5_scan(u[lay.tc:].reshape(bs, ls, SSM_WIDTH), mats, state_ssm_re[:, l], state_ssm_im[:, l])
        s_re.append(f_re)
        s_im.append(f_im)
        y_ssm = jnp.concatenate([y_c.reshape(lay.tc, SSM_WIDTH), y_s.reshape(lay.ts, SSM_WIDTH)], axis=0)

        merged = _merge(lay, h16, y_ssm, attn, y16, conv_w[l], ssm_w_glu[l].astype(BF16),
                        w_gates[l].astype(BF16), b_gates[l].reshape(1, -1), w_br_ssm[l].astype(BF16),
                        w_br_attn[l].astype(BF16), w_br_conv[l].astype(BF16))
        x, h2p, logits_t = _outproj(lay, merged, w_out[l].astype(BF16), x, mod_l, norm2_g[l],
                                    w_router[l])
        x = _moe(lay, l, x, h2p, logits_t, mod_l, b_router[l], w_e_gate, w_e_up, w_e_down,
                 w_s_gate[l].astype(BF16), w_s_up[l].astype(BF16), w_s_down[l].astype(BF16),
                 final_g, l == depth - 1)

    y_prompt = x[:lay.tc].reshape(bc, lc, d)
    y_sample = x[lay.tc:].reshape(bs, ls, d)
    return (y_prompt, y_sample, jnp.stack(ks, axis=1), jnp.stack(vs, axis=1),
            jnp.stack(s_re, axis=1), jnp.stack(s_im, axis=1))
```

```python
import functools
import math

import jax
import jax.numpy as jnp
from jax import lax
from jax.experimental import pallas as pl
from jax.experimental.pallas import tpu as pltpu

HEAD_DIM = 128
N_HEADS = 8
N_KV_HEADS = 2
GROUP = N_HEADS // N_KV_HEADS
ATTN_WIDTH = N_HEADS * HEAD_DIM
KV_WIDTH = N_KV_HEADS * HEAD_DIM
WINDOW = 128
ATTN_BLOCK = 128
ATTN_SCALE = HEAD_DIM ** -0.5
ROPE_BASE = 10000.0
ROT_F = HEAD_DIM // 4
GRID_W = 64
SSM_WIDTH = 512
SSM_CH = 16
SSM_GROUPS = SSM_WIDTH // SSM_CH
SSM_STATE = 64
SSM_CHUNK = 16
CONV_WIDTH = 512
N_BRANCHES = 3
IN_WIDTH = ATTN_WIDTH + 2 * KV_WIDTH + SSM_WIDTH + 3 * CONV_WIDTH
N_EXPERTS = 64
TOP_K = 6
N_EXPERT_GROUPS = 8
TOPK_GROUPS = 4
D_EXPERT = 512
ROUTED_SCALE = 2.5
EPS = 1e-6
NEG_INF = -1e30

COL_TILE = 512
MOD_ROWS = 16
EXPERT_ROWS = 256
VMEM_LIMIT_V7X = 56 * 1024 * 1024

F32 = jnp.float32
BF16 = jnp.bfloat16
I32 = jnp.int32
U32 = jnp.uint32


def _cparams(sem, vmem=VMEM_LIMIT_V7X):
    return pltpu.CompilerParams(dimension_semantics=sem, vmem_limit_bytes=vmem)


def _sigmoid(x):
    return 1.0 / (1.0 + jnp.exp(-x))


def _pack_bf16_pairs(v):
    n = v.shape[1] // 2
    hi = lax.bitcast_convert_type(v[:, :n].astype(BF16).astype(F32), U32)
    lo = lax.bitcast_convert_type(v[:, n:].astype(BF16).astype(F32), U32)
    return hi | (lo >> 16)


def _unpack_bf16_pairs(p):
    hi = lax.bitcast_convert_type(p & jnp.uint32(0xFFFF0000), F32)
    lo = lax.bitcast_convert_type(p << 16, F32)
    return hi, lo


def _adaln_kernel(c_ref, w_ref, b_ref, o_ref):
    c = c_ref[...]
    s = (c * _sigmoid(c)).astype(BF16)
    o_ref[0] = jnp.dot(s, w_ref[0].astype(BF16), preferred_element_type=F32) + b_ref[0]


def _adaln(cvec, w_ada, b_ada):
    depth, d, n6 = w_ada.shape
    tn = math.gcd(1024, n6)
    return pl.pallas_call(
        _adaln_kernel,
        grid=(depth, n6 // tn),
        in_specs=[
            pl.BlockSpec((MOD_ROWS, d), lambda l, n: (0, 0)),
            pl.BlockSpec((1, d, tn), lambda l, n: (l, 0, n)),
            pl.BlockSpec((1, 1, tn), lambda l, n: (l, 0, n)),
        ],
        out_specs=pl.BlockSpec((1, MOD_ROWS, tn), lambda l, n: (l, 0, n)),
        out_shape=jax.ShapeDtypeStruct((depth, MOD_ROWS, n6), F32),
        compiler_params=_cparams(("arbitrary", "arbitrary")),
    )(cvec, w_ada, b_ada.reshape(depth, 1, n6))


class _Layout:
    def __init__(self, n_ctx_seq, len_ctx, n_lat_seq, len_lat, d_model):
        self.bc, self.lc, self.bs, self.ls, self.d = n_ctx_seq, len_ctx, n_lat_seq, len_lat, d_model
        self.tc = n_ctx_seq * len_ctx
        self.ts = n_lat_seq * len_lat
        self.t = self.tc + self.ts

    def row_tile(self, want):
        tm = math.gcd(math.gcd(self.tc, self.ls), want)
        assert tm % 16 == 0
        return tm

    def mod_index(self, i, tm):
        nct, tps = self.tc // tm, self.ls // tm
        return jnp.where(i < nct, 0, 1 + (i - nct) // tps)

    def seq_pos(self, rows, i, tm):
        is_lat = i >= self.tc // tm
        return jnp.where(is_lat, (rows - self.tc) % self.ls, rows % self.lc), jnp.where(is_lat, self.ls, self.lc)


def _mod_spec(lay, tm, slot, d):
    return pl.BlockSpec((1, 1, d), lambda i, *_: (lay.mod_index(i, tm) * 6 + slot, 0, 0))


def _rope(z, cos, sin_signed, first_half):
    swapped = jnp.where(first_half, pltpu.roll(z, HEAD_DIM - ROT_F, 1), pltpu.roll(z, ROT_F, 1))
    return z * cos + swapped * sin_signed


def _inproj_kernel(x_ref, shift_ref, scale_ref, g_ref, w_ref, cos_ref, sin_ref, h_ref, y_ref, kv_ref):
    n = pl.program_id(1)
    n_q = ATTN_WIDTH // COL_TILE

    @pl.when(n == 0)
    def _():
        x = x_ref[...]
        ms = jnp.mean(x * x, axis=-1, keepdims=True)
        y = x * lax.rsqrt(ms + EPS) * g_ref[...]
        h_ref[...] = (y * (1.0 + scale_ref[0]) + shift_ref[0]).astype(BF16)

    acc = jnp.dot(h_ref[...], w_ref[...], preferred_element_type=F32)

    def rotated(n_heads):
        cos, sin = cos_ref[...], sin_ref[...]
        first_half = (lax.broadcasted_iota(I32, cos.shape, 1) % (2 * ROT_F)) < ROT_F
        parts = [_rope(acc[:, s * HEAD_DIM:(s + 1) * HEAD_DIM], cos, sin, first_half) for s in range(n_heads)]
        parts.append(acc[:, n_heads * HEAD_DIM:])
        return jnp.concatenate(parts, axis=1) if n_heads * HEAD_DIM < COL_TILE else jnp.concatenate(parts[:-1], axis=1)

    @pl.when(n < n_q)
    def _():
        y_ref[...] = rotated(COL_TILE // HEAD_DIM).astype(BF16)

    @pl.when(n == n_q)
    def _():
        kv_ref[...] = acc
        y_ref[...] = rotated(N_KV_HEADS).astype(BF16)

    @pl.when(n > n_q)
    def _():
        y_ref[...] = acc.astype(BF16)


def _inproj(lay, x, mod_l, g1, w_in16, rope_cos, rope_sin):
    d = lay.d
    tm = lay.row_tile(1024)
    nct, tps = lay.tc // tm, lay.ls // tm

    def rope_idx(i, n):
        return (jnp.where(i < nct, 0, 1 + (i - nct) % tps), 0)

    return pl.pallas_call(
        _inproj_kernel,
        grid=(lay.t // tm, IN_WIDTH // COL_TILE),
        in_specs=[
            pl.BlockSpec((tm, d), lambda i, n: (i, 0)),
            _mod_spec(lay, tm, 0, d),
            _mod_spec(lay, tm, 1, d),
            pl.BlockSpec((1, d), lambda i, n: (0, 0)),
            pl.BlockSpec((d, COL_TILE), lambda i, n: (0, n)),
            pl.BlockSpec((tm, HEAD_DIM), rope_idx),
            pl.BlockSpec((tm, HEAD_DIM), rope_idx),
        ],
        out_specs=[
            pl.BlockSpec((tm, d), lambda i, n: (i, 0)),
            pl.BlockSpec((tm, COL_TILE), lambda i, n: (i, n)),
            pl.BlockSpec((tm, 2 * KV_WIDTH), lambda i, n: (i, 0)),
        ],
        out_shape=[
            jax.ShapeDtypeStruct((lay.t, d), BF16),
            jax.ShapeDtypeStruct((lay.t, IN_WIDTH), BF16),
            jax.ShapeDtypeStruct((lay.t, 2 * KV_WIDTH), F32),
        ],
        compiler_params=_cparams(("arbitrary", "arbitrary")),
    )(x, mod_l, mod_l, g1.reshape(1, d), w_in16, rope_cos, rope_sin)


def _rope_tables(lay, tm):
    t = jnp.arange(lay.ls)
    row = (t // GRID_W).astype(F32)
    col = (t % GRID_W).astype(F32)
    inv = ROPE_BASE ** (-jnp.arange(ROT_F, dtype=F32) / ROT_F)
    ar, ac = row[:, None] * inv, col[:, None] * inv
    cos = jnp.concatenate([jnp.cos(ar), jnp.cos(ar), jnp.cos(ac), jnp.cos(ac)], axis=1)
    sin = jnp.concatenate([-jnp.sin(ar), jnp.sin(ar), -jnp.sin(ac), jnp.sin(ac)], axis=1)
    cos = jnp.concatenate([jnp.ones((tm, HEAD_DIM), F32), cos], axis=0)
    sin = jnp.concatenate([jnp.zeros((tm, HEAD_DIM), F32), sin], axis=0)
    return cos, sin


def _attend(q, sink_ref, j, parts):
    nq = q.shape[0]
    q4 = jnp.concatenate([q[:, g * HEAD_DIM:(g + 1) * HEAD_DIM] for g in range(GROUP)], axis=0)
    sink = jnp.concatenate([jnp.full((nq, 1), sink_ref[j * GROUP + g], F32) for g in range(GROUP)], axis=0)
    scores = []
    m = sink
    for k, _, mask in parts:
        s = lax.dot_general(q4, k, (((1,), (1,)), ((), ())), preferred_element_type=F32) * ATTN_SCALE
        if mask is not None:
            s = jnp.where(mask, s, NEG_INF)
        scores.append(s)
        m = jnp.maximum(m, jnp.max(s, axis=-1, keepdims=True))
    den = jnp.exp(sink - m)
    out = jnp.zeros((GROUP * nq, HEAD_DIM), F32)
    for s, (_, v, _) in zip(scores, parts):
        p = jnp.exp(s - m)
        den = den + jnp.sum(p, axis=-1, keepdims=True)
        out = out + jnp.dot(p.astype(BF16), v, preferred_element_type=F32)
    out = out / den
    return jnp.concatenate([out[g * nq:(g + 1) * nq] for g in range(GROUP)], axis=1)


def _head(x, j, width=HEAD_DIM):
    return x[:, j * width:(j + 1) * width]


def _attn_ctx_kernel(sink_ref, q_ref, k_ref, v_ref, o_ref):
    q, k, v = q_ref[...], k_ref[...], v_ref[...]
    outs = [_attend(_head(q, j, GROUP * HEAD_DIM), sink_ref, j, [(_head(k, j), _head(v, j), None)])
            for j in range(N_KV_HEADS)]
    o_ref[...] = jnp.concatenate(outs, axis=1).astype(o_ref.dtype)


def _attn_lat_kernel(sink_ref, q_ref, kp_ref, kc_ref, kn_ref, vp_ref, vc_ref, vn_ref, ck_ref, cv_ref, o_ref, *, seq_len):
    i = pl.program_id(1)
    q = q_ref[...]
    kw = jnp.concatenate([kp_ref[...], kc_ref[...], kn_ref[...]], axis=0)
    vw = jnp.concatenate([vp_ref[...], vc_ref[...], vn_ref[...]], axis=0)
    ck = ck_ref[...].astype(BF16)
    cv = cv_ref[...].astype(BF16)
    shape = (GROUP * ATTN_BLOCK, 3 * ATTN_BLOCK)
    qoff = lax.broadcasted_iota(I32, shape, 0) % ATTN_BLOCK
    koff = lax.broadcasted_iota(I32, shape, 1) - ATTN_BLOCK
    kabs = koff + i * ATTN_BLOCK
    mask = (jnp.abs(qoff - koff) <= WINDOW) & (kabs >= 0) & (kabs < seq_len)
    outs = [_attend(_head(q, j, GROUP * HEAD_DIM), sink_ref, j,
                    [(_head(kw, j), _head(vw, j), mask), (_head(ck, j), _head(cv, j), None)])
            for j in range(N_KV_HEADS)]
    o_ref[...] = jnp.concatenate(outs, axis=1).astype(o_ref.dtype)


def _attention(lay, y16, sink, cache_k_l, cache_v_l):
    smem = pl.BlockSpec(memory_space=pltpu.SMEM)
    kcol, vcol = ATTN_WIDTH // KV_WIDTH, (ATTN_WIDTH + KV_WIDTH) // KV_WIDTH
    ctx = pl.pallas_call(
        _attn_ctx_kernel,
        grid=(lay.bc,),
        in_specs=[
            smem,
            pl.BlockSpec((lay.lc, ATTN_WIDTH), lambda b: (b, 0)),
            pl.BlockSpec((lay.lc, KV_WIDTH), lambda b: (b, kcol)),
            pl.BlockSpec((lay.lc, KV_WIDTH), lambda b: (b, vcol)),
        ],
        out_specs=pl.BlockSpec((lay.lc, ATTN_WIDTH), lambda b: (b, 0)),
        out_shape=jax.ShapeDtypeStruct((lay.tc, ATTN_WIDTH), BF16),
        compiler_params=_cparams(("arbitrary",)),
    )(sink, y16, y16, y16)

    nblk = lay.ls // ATTN_BLOCK
    base = lay.tc // ATTN_BLOCK
    last = lay.t // ATTN_BLOCK - 1
    past = cache_k_l.shape[1]

    def rb(b, i):
        return base + b * nblk + i

    def kspec(col, delta):
        return pl.BlockSpec((ATTN_BLOCK, KV_WIDTH), lambda b, i: (jnp.clip(rb(b, i) + delta, 0, last), col))

    cspec = pl.BlockSpec((None, past, KV_WIDTH), lambda b, i: (b, 0, 0))
    lat = pl.pallas_call(
        functools.partial(_attn_lat_kernel, seq_len=lay.ls),
        grid=(lay.bs, nblk),
        in_specs=[
            smem,
            pl.BlockSpec((ATTN_BLOCK, ATTN_WIDTH), lambda b, i: (rb(b, i), 0)),
            kspec(kcol, -1), kspec(kcol, 0), kspec(kcol, 1),
            kspec(vcol, -1), kspec(vcol, 0), kspec(vcol, 1),
            cspec, cspec,
        ],
        out_specs=pl.BlockSpec((ATTN_BLOCK, ATTN_WIDTH), lambda b, i: (b * nblk + i, 0)),
        out_shape=jax.ShapeDtypeStruct((lay.ts, ATTN_WIDTH), BF16),
        compiler_params=_cparams(("arbitrary", "arbitrary")),
    )(sink, y16, y16, y16, y16, y16, y16, y16,
      cache_k_l.reshape(lay.bs, past, KV_WIDTH), cache_v_l.reshape(lay.bs, past, KV_WIDTH))
    return jnp.concatenate([ctx, lat], axis=0)


def _s5_matrices(lam_re, lam_im, log_dt, b_re, b_im, c_re, c_im, d_skip, n_steps):
    q, p, g, n = SSM_CHUNK, SSM_CH, SSM_GROUPS, SSM_STATE
    lam = lax.complex(lam_re.astype(F32), lam_im.astype(F32))
    dt = jnp.exp(log_dt.astype(F32))[..., None]
    lam_dt = lam * dt
    lam_bar = jnp.exp(lam_dt)
    b_bar = ((lam_bar - 1.0) / lam)[..., None] * lax.complex(b_re.astype(F32), b_im.astype(F32))
    c_mat = lax.complex(c_re.astype(F32), c_im.astype(F32))
    steps = jnp.arange(q + 1, dtype=F32)
    pw = jnp.exp(lam_dt[:, None] * steps[None, :, None, None])
    kern = jnp.real(jnp.einsum('dgpn,dkgn,dgnr->dkgpr', c_mat, pw[:, :q], b_bar))
    tau_in = jnp.arange(q)[:, None]
    tau_out = jnp.arange(q)[None, :]
    lag_f = tau_out - tau_in
    lag_b = tau_in - tau_out
    kf = jnp.where((lag_f >= 0)[:, :, None, None, None], kern[0][jnp.clip(lag_f, 0, q - 1)], 0.0)
    kb = jnp.where((lag_b >= 0)[:, :, None, None, None], kern[1][jnp.clip(lag_b, 0, q - 1)], 0.0)
    m = (kf + kb).transpose(2, 0, 4, 1, 3)
    eye_q = jnp.eye(q, dtype=F32)[None, :, None, :, None]
    eye_p = jnp.eye(p, dtype=F32)[None, None, :, None, :]
    m = m + eye_q * eye_p * d_skip.astype(F32).reshape(g, 1, p, 1, 1)
    m = m.reshape(g, q * p, q * p)
    ws_f = pw[0, :q][::-1][:, :, :, None] * b_bar[0][None]
    ws_b = pw[1, :q][:, :, :, None] * b_bar[1][None]

    def cols(w):
        return w.transpose(1, 0, 3, 2).reshape(g, q * p, n)

    w1 = jnp.concatenate([m, jnp.real(cols(ws_f)), jnp.real(cols(ws_b)),
                          jnp.imag(cols(ws_f)), jnp.imag(cols(ws_b))], axis=2)
    cy_f = c_mat[0][None] * pw[0, 1:][:, :, None, :]
    cy_b = c_mat[1][None] * pw[1, 1:][::-1][:, :, None, :]

    def rows(w):
        return w.transpose(1, 3, 0, 2).reshape(g, n, q * p)

    wy = jnp.concatenate([jnp.real(rows(cy_f)), jnp.real(rows(cy_b)),
                          -jnp.imag(rows(cy_f)), -jnp.imag(rows(cy_b))], axis=1)
    hops = (q * 2.0 ** jnp.arange(8, dtype=F32))[None, :, None, None]
    a = jnp.exp(lam_dt[:, None] * hops)
    a = jnp.concatenate([a[0], a[1]], axis=-1).transpose(1, 0, 2)
    assert n_steps <= 8
    return w1.astype(BF16), wy.astype(BF16), jnp.real(a), jnp.imag(a)


def _s5_kernel(u_ref, w1_ref, wy_ref, are_ref, aim_ref, h0re_ref, h0im_ref, y_ref, fre_ref, fim_ref, *, nb, nc):
    r_tot = nb * nc
    w = SSM_CHUNK * SSM_CH
    n2 = 2 * SSM_STATE
    proj = jnp.dot(u_ref[0], w1_ref[0], preferred_element_type=F32)
    y_intra = proj[:, :w]
    d_re = proj[:, w:w + n2]
    d_im = proj[:, w + n2:]
    row = lax.broadcasted_iota(I32, (r_tot, n2), 0)
    chunk = row % nc
    seq = row // nc
    fwd = lax.broadcasted_iota(I32, (r_tot, n2), 1) < SSM_STATE

    def previous(x, dist):
        valid = (fwd & (chunk >= dist)) | (~fwd & (chunk < nc - dist))
        moved = jnp.where(fwd, pltpu.roll(x, dist, 0), pltpu.roll(x, r_tot - dist, 0))
        return jnp.where(valid, moved, 0.0)

    h0_re = jnp.zeros((r_tot, n2), F32)
    h0_im = jnp.zeros((r_tot, n2), F32)
    for b in range(nb):
        h0_re = jnp.where(seq == b, h0re_ref[0, b:b + 1, :], h0_re)
        h0_im = jnp.where(seq == b, h0im_ref[0, b:b + 1, :], h0_im)
    first = (fwd & (chunk == 0)) | (~fwd & (chunk == nc - 1))
    e_re = jnp.where(first, h0_re, previous(d_re, 1))
    e_im = jnp.where(first, h0_im, previous(d_im, 1))
    k = 0
    while (1 << k) < nc:
        a_re = are_ref[0, k:k + 1, :]
        a_im = aim_ref[0, k:k + 1, :]
        p_re = previous(e_re, 1 << k)
        p_im = previous(e_im, 1 << k)
        e_re, e_im = e_re + a_re * p_re - a_im * p_im, e_im + a_re * p_im + a_im * p_re
        k += 1
    e_cat = jnp.concatenate([e_re, e_im], axis=1).astype(BF16)
    y_ref[0] = y_intra + jnp.dot(e_cat, wy_ref[0], preferred_element_type=F32)
    a_re = are_ref[0, 0:1, :]
    a_im = aim_ref[0, 0:1, :]
    f_re = a_re * e_re - a_im * e_im + d_re
    f_im = a_re * e_im + a_im * e_re + d_im
    fwd_row = fwd[0:1, :]
    for b in range(nb):
        lo, hi = b * nc, b * nc + nc - 1
        fre_ref[0, b:b + 1, :] = jnp.where(fwd_row, f_re[hi:hi + 1, :], f_re[lo:lo + 1, :])
        fim_ref[0, b:b + 1, :] = jnp.where(fwd_row, f_im[hi:hi + 1, :], f_im[lo:lo + 1, :])


def _s5_scan(u, mats, h0_re, h0_im):
    nb, length, _ = u.shape
    g, q, p, n = SSM_GROUPS, SSM_CHUNK, SSM_CH, SSM_STATE
    nc = length // q
    assert nc & (nc - 1) == 0 and nc % 8 == 0
    w1, wy, a_re, a_im = mats
    ut = u.reshape(nb, nc, q, g, p).transpose(3, 0, 1, 2, 4).reshape(g, nb * nc, q * p)

    def lanes(h):
        return h.astype(F32).transpose(2, 0, 1, 3).reshape(g, nb, 2 * n)

    r_tot = nb * nc
    per_g = lambda *shape: pl.BlockSpec((1,) + shape, lambda i: (i,) + (0,) * len(shape))
    y, f_re, f_im = pl.pallas_call(
        functools.partial(_s5_kernel, nb=nb, nc=nc),
        grid=(g,),
        in_specs=[per_g(r_tot, q * p), per_g(q * p, q * p + 4 * n), per_g(4 * n, q * p),
                  per_g(8, 2 * n), per_g(8, 2 * n), per_g(nb, 2 * n), per_g(nb, 2 * n)],
        out_specs=[per_g(r_tot, q * p), per_g(nb, 2 * n), per_g(nb, 2 * n)],
        out_shape=[jax.ShapeDtypeStruct((g, r_tot, q * p), F32),
                   jax.ShapeDtypeStruct((g, nb, 2 * n), F32),
                   jax.ShapeDtypeStruct((g, nb, 2 * n), F32)],
        compiler_params=_cparams(("arbitrary",)),
    )(ut, w1, wy, a_re, a_im, lanes(h0_re), lanes(h0_im))
    y = y.reshape(g, nb, nc, q, p).transpose(1, 2, 3, 0, 4).reshape(nb, length, g * p)

    def unlanes(f):
        return f.reshape(g, nb, 2, n).transpose(1, 2, 0, 3)

    return y, unlanes(f_re), unlanes(f_im)


def _merge_kernel(h_ref, ys_ref, at_ref, gb_ref, gc_ref, uc_ref, gcp_ref, ucp_ref, gcn_ref, ucn_ref, cw_ref,
                  wglu_ref, wg0_ref, wg1_ref, wg2_ref, bg0_ref, bg1_ref, bg2_ref, ws_ref, wa_ref, wc_ref,
                  o_ref, ssm_scr, conv_scr, *, lay, tm, halo):
    i = pl.program_id(0)
    n = pl.program_id(1)

    @pl.when(n == 0)
    def _():
        y = ys_ref[...]
        ge = 0.5 * y * (1.0 + jnp.tanh(math.sqrt(2.0 / math.pi) * (y + 0.044715 * (y * y * y))))
        glu = jnp.dot(ge.astype(BF16), wglu_ref[...], preferred_element_type=F32)
        ssm_scr[...] = (ge * _sigmoid(glu)).astype(BF16)

        z = gc_ref[...].astype(F32) * uc_ref[...].astype(F32)
        z_before = gcp_ref[halo - 1:halo, :].astype(F32) * ucp_ref[halo - 1:halo, :].astype(F32)
        z_after = gcn_ref[0:1, :].astype(F32) * ucn_ref[0:1, :].astype(F32)
        local = lax.broadcasted_iota(I32, z.shape, 0)
        pos, seq_len = lay.seq_pos(local + i * tm, i, tm)
        z_prev = jnp.where(local == 0, z_before, pltpu.roll(z, 1, 0))
        z_prev = jnp.where(pos == 0, 0.0, z_prev)
        z_next = jnp.where(local == tm - 1, z_after, pltpu.roll(z, tm - 1, 0))
        z_next = jnp.where(pos == seq_len - 1, 0.0, z_next)
        conv = cw_ref[0:1, :] * z_prev + cw_ref[1:2, :] * z + cw_ref[2:3, :] * z_next
        conv_scr[...] = (gb_ref[...].astype(F32) * conv).astype(BF16)

    h = h_ref[...]
    acc = None
    for act, wg_ref, bg_ref, wb_ref in ((ssm_scr[...], wg0_ref, bg0_ref, ws_ref),
                                        (at_ref[...], wg1_ref, bg1_ref, wa_ref),
                                        (conv_scr[...], wg2_ref, bg2_ref, wc_ref)):
        gate = _sigmoid(jnp.dot(h, wg_ref[...], preferred_element_type=F32) + bg_ref[...])
        term = gate * jnp.dot(act, wb_ref[...], preferred_element_type=F32)
        acc = term if acc is None else acc + term
    o_ref[...] = acc.astype(BF16)


def _merge(lay, h16, y_ssm, attn, y16, conv_w, wglu16, wgates16, b_gates, wbs16, wba16, wbc16):
    d = lay.d
    tm = lay.row_tile(512)
    tn = min(COL_TILE, d)
    nd = d // tn
    halo = 16
    hb = tm // halo
    last_h = lay.t // halo - 1
    c0 = (ATTN_WIDTH + 2 * KV_WIDTH + SSM_WIDTH) // CONV_WIDTH
    row = lambda cb: pl.BlockSpec((tm, CONV_WIDTH), lambda i, n: (i, cb))
    before = lambda cb: pl.BlockSpec((halo, CONV_WIDTH), lambda i, n: (jnp.maximum(i * hb - 1, 0), cb))
    after = lambda cb: pl.BlockSpec((halo, CONV_WIDTH), lambda i, n: (jnp.minimum((i + 1) * hb, last_h), cb))
    gate_w = lambda br: pl.BlockSpec((d, tn), lambda i, n: (0, br * nd + n))
    gate_b = lambda br: pl.BlockSpec((1, tn), lambda i, n: (0, br * nd + n))
    return pl.pallas_call(
        functools.partial(_merge_kernel, lay=lay, tm=tm, halo=halo),
        grid=(lay.t // tm, nd),
        in_specs=[
            pl.BlockSpec((tm, d), lambda i, n: (i, 0)),
            pl.BlockSpec((tm, SSM_WIDTH), lambda i, n: (i, 0)),
            pl.BlockSpec((tm, ATTN_WIDTH), lambda i, n: (i, 0)),
            row(c0), row(c0 + 1), row(c0 + 2),
            before(c0 + 1), before(c0 + 2), after(c0 + 1), after(c0 + 2),
            pl.BlockSpec((3, CONV_WIDTH), lambda i, n: (0, 0)),
            pl.BlockSpec((SSM_WIDTH, SSM_WIDTH), lambda i, n: (0, 0)),
            gate_w(0), gate_w(1), gate_w(2), gate_b(0), gate_b(1), gate_b(2),
            pl.BlockSpec((SSM_WIDTH, tn), lambda i, n: (0, n)),
            pl.BlockSpec((ATTN_WIDTH, tn), lambda i, n: (0, n)),
            pl.BlockSpec((CONV_WIDTH, tn), lambda i, n: (0, n)),
        ],
        out_specs=pl.BlockSpec((tm, tn), lambda i, n: (i, n)),
        out_shape=jax.ShapeDtypeStruct((lay.t, d), BF16),
        scratch_shapes=[pltpu.VMEM((tm, SSM_WIDTH), BF16), pltpu.VMEM((tm, CONV_WIDTH), BF16)],
        compiler_params=_cparams(("arbitrary", "arbitrary")),
    )(h16, y_ssm, attn, y16, y16, y16, y16, y16, y16, y16, conv_w, wglu16,
      wgates16, wgates16, wgates16, b_gates, b_gates, b_gates, wbs16, wba16, wbc16)


def _outproj_kernel(m_ref, w_ref, x_ref, gate_ref, g2_ref, shift_ref, scale_ref, wrh_ref, wrl_ref,
                    xo_ref, hp_ref, lg_ref):
    acc = jnp.dot(m_ref[...], w_ref[...], preferred_element_type=F32)
    xn = x_ref[...] + gate_ref[0] * acc
    xo_ref[...] = xn
    ms = jnp.mean(xn * xn, axis=-1, keepdims=True)
    h2 = xn * lax.rsqrt(ms + EPS) * g2_ref[...]
    h2 = h2 * (1.0 + scale_ref[0]) + shift_ref[0]
    hp_ref[...] = _pack_bf16_pairs(h2)
    h_hi = h2.astype(BF16)
    h_lo = (h2 - h_hi.astype(F32)).astype(BF16)
    logits = (jnp.dot(h_hi, wrh_ref[...], preferred_element_type=F32)
              + jnp.dot(h_hi, wrl_ref[...], preferred_element_type=F32)
              + jnp.dot(h_lo, wrh_ref[...], preferred_element_type=F32))
    lg_ref[...] = logits.T[:N_EXPERTS, :]


def _outproj(lay, merged, wout16, x, mod_l, g2, w_router):
    d = lay.d
    tm = lay.row_tile(256)
    lanes = 128
    wr = jnp.zeros((d, lanes), F32).at[:, :N_EXPERTS].set(w_router.astype(F32))
    wr_hi = wr.astype(BF16)
    wr_lo = (wr - wr_hi.astype(F32)).astype(BF16)
    return pl.pallas_call(
        _outproj_kernel,
        grid=(lay.t // tm,),
        in_specs=[
            pl.BlockSpec((tm, d), lambda i: (i, 0)),
            pl.BlockSpec((d, d), lambda i: (0, 0)),
            pl.BlockSpec((tm, d), lambda i: (i, 0)),
            _mod_spec(lay, tm, 2, d),
            pl.BlockSpec((1, d), lambda i: (0, 0)),
            _mod_spec(lay, tm, 3, d),
            _mod_spec(lay, tm, 4, d),
            pl.BlockSpec((d, lanes), lambda i: (0, 0)),
            pl.BlockSpec((d, lanes), lambda i: (0, 0)),
        ],
        out_specs=[
            pl.BlockSpec((tm, d), lambda i: (i, 0)),
            pl.BlockSpec((tm, d // 2), lambda i: (i, 0)),
            pl.BlockSpec((N_EXPERTS, tm), lambda i: (0, i)),
        ],
        out_shape=[
            jax.ShapeDtypeStruct((lay.t, d), F32),
            jax.ShapeDtypeStruct((lay.t, d // 2), U32),
            jax.ShapeDtypeStruct((N_EXPERTS, lay.t), F32),
        ],
        compiler_params=_cparams(("arbitrary",)),
    )(merged, wout16, x, mod_l, g2.reshape(1, d), mod_l, mod_l, wr_hi, wr_lo)


def _route_kernel(lg_ref, br_ref, idx_ref, w_ref, pos_ref, cnt_ref, carry):
    step = pl.program_id(0)
    tt = lg_ref.shape[1]
    per_group = N_EXPERTS // N_EXPERT_GROUPS

    @pl.when(step == 0)
    def _():
        carry[...] = jnp.zeros_like(carry)

    scores = _sigmoid(lg_ref[...])
    biased = scores + br_ref[...]
    sub = lax.broadcasted_iota(I32, (per_group, tt), 0).astype(F32)
    blocks, group_score = [], []
    for g in range(N_EXPERT_GROUPS):
        blk = biased[g * per_group:(g + 1) * per_group, :]
        m1 = jnp.max(blk, axis=0, keepdims=True)
        i1 = jnp.min(jnp.where(blk == m1, sub, float(per_group)), axis=0, keepdims=True)
        m2 = jnp.max(jnp.where(sub == i1, -jnp.inf, blk), axis=0, keepdims=True)
        blocks.append(blk)
        group_score.append(m1 + m2)
    masked = []
    for g in range(N_EXPERT_GROUPS):
        beaten_by = jnp.zeros((1, tt), F32)
        for o in range(N_EXPERT_GROUPS):
            if o == g:
                continue
            wins = (group_score[o] > group_score[g]) | ((group_score[o] == group_score[g]) & (o < g))
            beaten_by = beaten_by + wins.astype(F32)
        masked.append(jnp.where(beaten_by < TOPK_GROUPS, blocks[g], -jnp.inf))
    masked = jnp.concatenate(masked, axis=0)
    eid = lax.broadcasted_iota(I32, (N_EXPERTS, tt), 0).astype(F32)
    chosen, weights = [], []
    onehot = jnp.zeros((N_EXPERTS, tt), F32)
    for _ in range(TOP_K):
        m = jnp.max(masked, axis=0, keepdims=True)
        e = jnp.min(jnp.where(masked == m, eid, float(N_EXPERTS)), axis=0, keepdims=True)
        hit = eid == e
        chosen.append(e)
        weights.append(jnp.sum(jnp.where(hit, scores, 0.0), axis=0, keepdims=True))
        onehot = onehot + hit.astype(F32)
        masked = jnp.where(hit, -jnp.inf, masked)
    total = weights[0]
    for wk in weights[1:]:
        total = total + wk
    earlier = (lax.broadcasted_iota(I32, (tt, tt), 0) < lax.broadcasted_iota(I32, (tt, tt), 1)).astype(BF16)
    rank = carry[...][:, 0:1] + jnp.dot(onehot.astype(BF16), earlier, preferred_element_type=F32)
    for k in range(TOP_K):
        idx_ref[k:k + 1, :] = chosen[k].astype(I32)
        w_ref[k:k + 1, :] = weights[k] / total * ROUTED_SCALE
        pos_ref[k:k + 1, :] = jnp.sum(jnp.where(eid == chosen[k], rank, 0.0), axis=0, keepdims=True).astype(I32)
    for k in range(TOP_K, 8):
        idx_ref[k:k + 1, :] = jnp.zeros((1, tt), I32)
        w_ref[k:k + 1, :] = jnp.zeros((1, tt), F32)
        pos_ref[k:k + 1, :] = jnp.zeros((1, tt), I32)
    carry[...] = carry[...] + jnp.sum(onehot, axis=1, keepdims=True)
    cnt_ref[...] = carry[...]


def _route(logits_t, b_router):
    t = logits_t.shape[1]
    tt = math.gcd(t, 512)
    tok = pl.BlockSpec((8, tt), lambda i: (0, i))
    return pl.pallas_call(
        _route_kernel,
        grid=(t // tt,),
        in_specs=[pl.BlockSpec((N_EXPERTS, tt), lambda i: (0, i)),
                  pl.BlockSpec((N_EXPERTS, 1), lambda i: (0, 0))],
        out_specs=[tok, tok, tok, pl.BlockSpec((N_EXPERTS, 128), lambda i: (0, 0))],
        out_shape=[jax.ShapeDtypeStruct((8, t), I32), jax.ShapeDtypeStruct((8, t), F32),
                   jax.ShapeDtypeStruct((8, t), I32), jax.ShapeDtypeStruct((N_EXPERTS, 128), F32)],
        scratch_shapes=[pltpu.VMEM((N_EXPERTS, 128), F32)],
        compiler_params=_cparams(("arbitrary",)),
    )(logits_t, b_router.astype(F32).reshape(N_EXPERTS, 1))


def _dest_kernel(idx_ref, pos_ref, start_ref, o_ref, *, tt):
    eid = lax.broadcasted_iota(I32, (N_EXPERTS, tt), 0)
    start = start_ref[...]
    for k in range(TOP_K):
        first = jnp.sum(jnp.where(eid == idx_ref[k:k + 1, :], start, 0.0), axis=0, keepdims=True)
        o_ref[0, :, k * tt:(k + 1) * tt] = first.astype(I32) + pos_ref[k:k + 1, :]


def _dest_rows(idx_t, pos_t, pad_start, tt):
    t = idx_t.shape[1]
    tok = pl.BlockSpec((8, tt), lambda i: (0, i))
    return pl.pallas_call(
        functools.partial(_dest_kernel, tt=tt),
        grid=(t // tt,),
        in_specs=[tok, tok, pl.BlockSpec((N_EXPERTS, 1), lambda i: (0, 0))],
        out_specs=pl.BlockSpec((1, 1, TOP_K * tt), lambda i: (i, 0, 0)),
        out_shape=jax.ShapeDtypeStruct((t // tt, 1, TOP_K * tt), I32),
        compiler_params=_cparams(("arbitrary",)),
    )(idx_t, pos_t, pad_start.astype(F32).reshape(N_EXPERTS, 1))


def _row_copy(src_ref, src_row, dst_ref, dst_row, sem):
    return pltpu.make_async_copy(src_ref.at[pl.ds(src_row, 1)], dst_ref.at[pl.ds(dst_row, 1)], sem)


def _dispatch_kernel(dest_ref, h_ref, zero_ref, xs_ref, sem, *, tt):
    del zero_ref

    def body(t, carry):
        for k in range(TOP_K):
            _row_copy(h_ref, t, xs_ref, dest_ref[0, 0, k * tt + t], sem).start(priority=k % 2)
        return carry

    lax.fori_loop(0, tt, body, 0, unroll=8)
    for _ in range(TOP_K):
        pltpu.make_async_copy(h_ref, xs_ref.at[pl.ds(0, tt)], sem).wait()


def _dispatch(h2p, dest_tiles, n_rows, tt):
    t, half = h2p.shape
    zeros = jnp.zeros((n_rows, half), U32)
    return pl.pallas_call(
        functools.partial(_dispatch_kernel, tt=tt),
        grid=(t // tt,),
        in_specs=[pl.BlockSpec((1, 1, TOP_K * tt), lambda i: (i, 0, 0), memory_space=pltpu.SMEM),
                  pl.BlockSpec((tt, half), lambda i: (i, 0)),
                  pl.BlockSpec(memory_space=pl.ANY)],
        out_specs=pl.BlockSpec(memory_space=pl.ANY),
        out_shape=jax.ShapeDtypeStruct((n_rows, half), U32),
        scratch_shapes=[pltpu.SemaphoreType.DMA],
        input_output_aliases={2: 0},
        compiler_params=_cparams(("arbitrary",)),
    )(dest_tiles, h2p, zeros)


def _expert_kernel(be_ref, xs_ref, wg_ref, wu_ref, wd_ref, ys_ref, wg16, wu16, wd16):
    i = pl.program_id(0)
    changed = jnp.logical_or(i == 0, be_ref[i] != be_ref[jnp.maximum(i - 1, 0)])

    @pl.when(changed)
    def _():
        rows = 256
        d = wg16.shape[0]

        def cast_in(r, carry):
            sl = pl.ds(pl.multiple_of(r * rows, rows), rows)
            wg16[sl, :] = wg_ref[0, sl, :].astype(BF16)
            wu16[sl, :] = wu_ref[0, sl, :].astype(BF16)
            return carry

        lax.fori_loop(0, d // rows, cast_in, 0)

        def cast_down(r, carry):
            sl = pl.ds(pl.multiple_of(r * 128, 128), 128)
            wd16[sl, :] = wd_ref[0, sl, :].astype(BF16)
            return carry

        lax.fori_loop(0, D_EXPERT // 128, cast_down, 0)

    used = i < be_ref[pl.num_programs(0)]

    @pl.when(used)
    def _():
        hi, lo = _unpack_bf16_pairs(xs_ref[...])
        x = jnp.concatenate([hi, lo], axis=1).astype(BF16)
        gate = jnp.dot(x, wg16[...], preferred_element_type=F32)
        up = jnp.dot(x, wu16[...], preferred_element_type=F32)
        act = (gate * _sigmoid(gate) * up).astype(BF16)
        ys_ref[...] = _pack_bf16_pairs(jnp.dot(act, wd16[...], preferred_element_type=F32))

    @pl.when(jnp.logical_not(used))
    def _():
        ys_ref[...] = jnp.zeros_like(ys_ref)


def _experts(xs, block_e, layer, w_gate, w_up, w_down):
    n_rows, half = xs.shape
    d = 2 * half
    br = EXPERT_ROWS
    grid_spec = pltpu.PrefetchScalarGridSpec(
        num_scalar_prefetch=1,
        grid=(n_rows // br,),
        in_specs=[
            pl.BlockSpec((br, half), lambda i, be: (i, 0)),
            pl.BlockSpec((None, 1, d, D_EXPERT), lambda i, be: (layer, be[i], 0, 0)),
            pl.BlockSpec((None, 1, d, D_EXPERT), lambda i, be: (layer, be[i], 0, 0)),
            pl.BlockSpec((None, 1, D_EXPERT, d), lambda i, be: (layer, be[i], 0, 0)),
        ],
        out_specs=pl.BlockSpec((br, half), lambda i, be: (i, 0)),
        scratch_shapes=[pltpu.VMEM((d, D_EXPERT), BF16), pltpu.VMEM((d, D_EXPERT), BF16),
                        pltpu.VMEM((D_EXPERT, d), BF16)],
    )
    return pl.pallas_call(
        _expert_kernel,
        grid_spec=grid_spec,
        out_shape=jax.ShapeDtypeStruct((n_rows, half), U32),
        compiler_params=_cparams(("arbitrary",)),
    )(block_e, xs, w_gate, w_up, w_down)


def _combine_kernel(dest_ref, ys_ref, wsel_ref, hp_ref, x_ref, gate_ref, wsg_ref, wsu_ref, wsd_ref, fg_ref,
                    *rest, tt, final, n_ctx_tiles):
    buf, sem = rest[-2:]
    def issue(t, carry):
        for k in range(TOP_K):
            _row_copy(ys_ref, dest_ref[0, 0, k * tt + t], buf.at[k], t, sem).start(priority=k % 2)
        return carry

    lax.fori_loop(0, tt, issue, 0, unroll=8)

    hi, lo = _unpack_bf16_pairs(hp_ref[...])
    h2 = jnp.concatenate([hi, lo], axis=1).astype(BF16)
    sg = jnp.dot(h2, wsg_ref[...], preferred_element_type=F32)
    su = jnp.dot(h2, wsu_ref[...], preferred_element_type=F32)
    shared = jnp.dot((sg * _sigmoid(sg) * su).astype(BF16), wsd_ref[...], preferred_element_type=F32)

    for k in range(TOP_K):
        pltpu.make_async_copy(ys_ref.at[pl.ds(0, tt)], buf.at[k], sem).wait()

    half = hp_ref.shape[1]
    r_hi = jnp.zeros((tt, half), F32)
    r_lo = jnp.zeros((tt, half), F32)
    for k in range(TOP_K):
        y_hi, y_lo = _unpack_bf16_pairs(buf[k])
        wk = wsel_ref[:, k:k + 1]
        r_hi = r_hi + wk * y_hi
        r_lo = r_lo + wk * y_lo
    routed = jnp.concatenate([r_hi, r_lo], axis=1)
    out = x_ref[...] + gate_ref[0] * (routed + shared)
    if not final:
        rest[0][...] = out
        return
    ms = jnp.mean(out * out, axis=-1, keepdims=True)
    out = out * lax.rsqrt(ms + EPS) * fg_ref[...]
    ctx_ref, lat_ref = rest[:2]
    is_ctx = pl.program_id(0) < n_ctx_tiles

    @pl.when(is_ctx)
    def _():
        ctx_ref[...] = out

    @pl.when(jnp.logical_not(is_ctx))
    def _():
        lat_ref[...] = out


def _combine(lay, ys, dest_tiles, wsel, h2p, x, mod_l, wsg16, wsu16, wsd16, final_g, tt, final):
    d = lay.d
    half = d // 2
    nct = lay.tc // tt
    if final:
        out_specs = [pl.BlockSpec((tt, d), lambda i: (jnp.minimum(i, nct - 1), 0)),
                     pl.BlockSpec((tt, d), lambda i: (jnp.maximum(i - nct, 0), 0))]
        out_shape = [jax.ShapeDtypeStruct((lay.tc, d), F32), jax.ShapeDtypeStruct((lay.ts, d), F32)]
    else:
        out_specs = pl.BlockSpec((tt, d), lambda i: (i, 0))
        out_shape = jax.ShapeDtypeStruct((lay.t, d), F32)
    return pl.pallas_call(
        functools.partial(_combine_kernel, tt=tt, final=final, n_ctx_tiles=nct),
        grid=(lay.t // tt,),
        in_specs=[
            pl.BlockSpec((1, 1, TOP_K * tt), lambda i: (i, 0, 0), memory_space=pltpu.SMEM),
            pl.BlockSpec(memory_space=pl.ANY),
            pl.BlockSpec((tt, 8), lambda i: (i, 0)),
            pl.BlockSpec((tt, half), lambda i: (i, 0)),
            pl.BlockSpec((tt, d), lambda i: (i, 0)),
            _mod_spec(lay, tt, 5, d),
            pl.BlockSpec((d, D_EXPERT), lambda i: (0, 0)),
            pl.BlockSpec((d, D_EXPERT), lambda i: (0, 0)),
            pl.BlockSpec((D_EXPERT, d), lambda i: (0, 0)),
            pl.BlockSpec((1, d), lambda i: (0, 0)),
        ],
        out_specs=out_specs,
        out_shape=out_shape,
        scratch_shapes=[pltpu.VMEM((TOP_K, tt, half), U32), pltpu.SemaphoreType.DMA],
        compiler_params=_cparams(("arbitrary",)),
    )(dest_tiles, ys, wsel, h2p, x, mod_l, wsg16, wsu16, wsd16, final_g.reshape(1, d))


def _moe(lay, layer, x, h2p, logits_t, mod_l, b_router, w_e_gate, w_e_up, w_e_down, wsg16, wsu16, wsd16, final_g,
         final):
    t = lay.t
    idx_t, w_t, pos_t, counts = _route(logits_t, b_router)
    br = EXPERT_ROWS
    counts = counts[:, 0].astype(I32)
    padded = (counts + br - 1) // br * br
    pad_end = jnp.cumsum(padded)
    pad_start = pad_end - padded
    n_blocks = -(-(t * TOP_K + N_EXPERTS * (br - 1)) // br)
    n_rows = n_blocks * br
    first_row = jnp.arange(n_blocks, dtype=I32) * br
    block_e = jnp.minimum(jnp.sum((pad_end[None, :] <= first_row[:, None]).astype(I32), axis=1), N_EXPERTS - 1)
    block_e = jnp.concatenate([block_e, pad_end[-1:] // br]).astype(I32)
    tt = lay.row_tile(256)
    dest_tiles = _dest_rows(idx_t, pos_t, pad_start, tt)
    xs = _dispatch(h2p, dest_tiles, n_rows, tt)
    ys = _experts(xs, block_e, layer, w_e_gate, w_e_up, w_e_down)
    return _combine(lay, ys, dest_tiles, w_t.T, h2p, x, mod_l, wsg16, wsu16, wsd16, final_g, tt, final)


def kernel(x_prompt, x_sample, c, cache_k, cache_v, state_ssm_re, state_ssm_im, c_ctx, w_ada, b_ada, norm1_g, norm2_g, w_in, w_gates, b_gates, ssm_lam_re, ssm_lam_im, ssm_log_dt, ssm_b_re, ssm_b_im, ssm_c_re, ssm_c_im, ssm_d, ssm_w_glu, conv_w, attn_sink, w_br_ssm, w_br_attn, w_br_conv, w_out, w_router, b_router, w_e_gate, w_e_up, w_e_down, w_s_gate, w_s_up, w_s_down, final_g):
    bc, lc, d = x_prompt.shape
    bs, ls, _ = x_sample.shape
    depth = w_in.shape[0]
    lay = _Layout(bc, lc, bs, ls, d)
    assert 1 + bs <= MOD_ROWS

    x = jnp.concatenate([x_prompt.reshape(lay.tc, d), x_sample.reshape(lay.ts, d)], axis=0)
    cvec = jnp.zeros((MOD_ROWS, d), F32).at[0].set(c_ctx).at[1:1 + bs].set(c)
    mod = _adaln(cvec, w_ada, b_ada).reshape(depth, MOD_ROWS * 6, 1, d)
    rope_cos, rope_sin = _rope_tables(lay, lay.row_tile(1024))
    zeros_state = jnp.zeros((bc, 2, SSM_GROUPS, SSM_STATE), F32)

    ks, vs, s_re, s_im = [], [], [], []
    for l in range(depth):
        mod_l = mod[l]
        h16, y16, kv32 = _inproj(lay, x, mod_l, norm1_g[l], w_in[l].astype(BF16), rope_cos, rope_sin)
        ks.append(kv32[:lay.tc, :KV_WIDTH].reshape(bc, lc, N_KV_HEADS, HEAD_DIM))
        vs.append(kv32[:lay.tc, KV_WIDTH:].reshape(bc, lc, N_KV_HEADS, HEAD_DIM))

        attn = _attention(lay, y16, attn_sink[l].astype(F32), cache_k[:, l], cache_v[:, l])

        u0 = ATTN_WIDTH + 2 * KV_WIDTH
        u = y16[:, u0:u0 + SSM_WIDTH]
        mats = _s5_matrices(ssm_lam_re[l], ssm_lam_im[l], ssm_log_dt[l], ssm_b_re[l], ssm_b_im[l],
                            ssm_c_re[l], ssm_c_im[l], ssm_d[l], 8)
        y_c, f_re, f_im = _s5_scan(u[:lay.tc].reshape(bc, lc, SSM_WIDTH), mats, zeros_state, zeros_state)
        y_s, _, _ = _s5_scan(u[lay.tc:].reshape(bs, ls, SSM_WIDTH), mats, state_ssm_re[:, l], state_ssm_im[:, l])
        s_re.append(f_re)
        s_im.append(f_im)
        y_ssm = jnp.concatenate([y_c.reshape(lay.tc, SSM_WIDTH), y_s.reshape(lay.ts, SSM_WIDTH)], axis=0)

        merged = _merge(lay, h16, y_ssm, attn, y16, conv_w[l], ssm_w_glu[l].astype(BF16),
                        w_gates[l].astype(BF16), b_gates[l].reshape(1, -1), w_br_ssm[l].astype(BF16),
                        w_br_attn[l].astype(BF16), w_br_conv[l].astype(BF16))
        x, h2p, logits_t = _outproj(lay, merged, w_out[l].astype(BF16), x, mod_l, norm2_g[l],
                                    w_router[l])
        x = _moe(lay, l, x, h2p, logits_t, mod_l, b_router[l], w_e_gate, w_e_up, w_e_down,
                 w_s_gate[l].astype(BF16), w_s_up[l].astype(BF16), w_s_down[l].astype(BF16),
                 final_g, l == depth - 1)

    y_prompt = x[0].reshape(bc, lc, d)
    y_sample = x[1].reshape(bs, ls, d)
    return (y_prompt, y_sample, jnp.stack(ks, axis=1), jnp.stack(vs, axis=1),
            jnp.stack(s_re, axis=1), jnp.stack(s_im, axis=1))
```

```python
import functools
import math

import jax
import jax.numpy as jnp
from jax import lax
from jax.experimental import pallas as pl
from jax.experimental.pallas import tpu as pltpu
from jax.experimental.pallas import tpu_sc as plsc

HEAD_DIM = 128
N_HEADS = 8
N_KV_HEADS = 2
GROUP = N_HEADS // N_KV_HEADS
ATTN_WIDTH = N_HEADS * HEAD_DIM
KV_WIDTH = N_KV_HEADS * HEAD_DIM
WINDOW = 128
ATTN_BLOCK = 128
ATTN_SCALE = HEAD_DIM ** -0.5
ROPE_BASE = 10000.0
ROT_F = HEAD_DIM // 4
GRID_W = 64
SSM_WIDTH = 512
SSM_CH = 16
SSM_GROUPS = SSM_WIDTH // SSM_CH
SSM_STATE = 64
SSM_CHUNK = 16
CONV_WIDTH = 512
N_BRANCHES = 3
IN_WIDTH = ATTN_WIDTH + 2 * KV_WIDTH + SSM_WIDTH + 3 * CONV_WIDTH
N_EXPERTS = 64
TOP_K = 6
N_EXPERT_GROUPS = 8
TOPK_GROUPS = 4
D_EXPERT = 512
ROUTED_SCALE = 2.5
EPS = 1e-6
NEG_INF = -1e30

COL_TILE = 512
MOD_ROWS = 16
EXPERT_ROWS = 256
SC_STREAM_ROWS = 64
VMEM_LIMIT_V7X = 56 * 1024 * 1024

F32 = jnp.float32
BF16 = jnp.bfloat16
I32 = jnp.int32
U32 = jnp.uint32


def _cparams(sem, vmem=VMEM_LIMIT_V7X):
    return pltpu.CompilerParams(dimension_semantics=sem, vmem_limit_bytes=vmem)


def _sigmoid(x):
    return 1.0 / (1.0 + jnp.exp(-x))


def _pack_bf16_pairs(v):
    n = v.shape[1] // 2
    hi = lax.bitcast_convert_type(v[:, :n].astype(BF16).astype(F32), U32)
    lo = lax.bitcast_convert_type(v[:, n:].astype(BF16).astype(F32), U32)
    return hi | (lo >> 16)


def _unpack_bf16_pairs(p):
    hi = lax.bitcast_convert_type(p & jnp.uint32(0xFFFF0000), F32)
    lo = lax.bitcast_convert_type(p << 16, F32)
    return hi, lo


def _adaln_kernel(c_ref, w_ref, b_ref, o_ref):
    c = c_ref[...]
    s = (c * _sigmoid(c)).astype(BF16)
    o_ref[0] = jnp.dot(s, w_ref[0].astype(BF16), preferred_element_type=F32) + b_ref[0]


def _adaln(cvec, w_ada, b_ada):
    depth, d, n6 = w_ada.shape
    tn = math.gcd(1024, n6)
    return pl.pallas_call(
        _adaln_kernel,
        grid=(depth, n6 // tn),
        in_specs=[
            pl.BlockSpec((MOD_ROWS, d), lambda l, n: (0, 0)),
            pl.BlockSpec((1, d, tn), lambda l, n: (l, 0, n)),
            pl.BlockSpec((1, 1, tn), lambda l, n: (l, 0, n)),
        ],
        out_specs=pl.BlockSpec((1, MOD_ROWS, tn), lambda l, n: (l, 0, n)),
        out_shape=jax.ShapeDtypeStruct((depth, MOD_ROWS, n6), F32),
        compiler_params=_cparams(("arbitrary", "arbitrary")),
    )(cvec, w_ada, b_ada.reshape(depth, 1, n6))


class _Layout:
    def __init__(self, n_ctx_seq, len_ctx, n_lat_seq, len_lat, d_model):
        self.bc, self.lc, self.bs, self.ls, self.d = n_ctx_seq, len_ctx, n_lat_seq, len_lat, d_model
        self.tc = n_ctx_seq * len_ctx
        self.ts = n_lat_seq * len_lat
        self.t = self.tc + self.ts

    def row_tile(self, want):
        tm = math.gcd(math.gcd(self.tc, self.ls), want)
        assert tm % 16 == 0
        return tm

    def mod_index(self, i, tm):
        nct, tps = self.tc // tm, self.ls // tm
        return jnp.where(i < nct, 0, 1 + (i - nct) // tps)

    def seq_pos(self, rows, i, tm):
        is_lat = i >= self.tc // tm
        return jnp.where(is_lat, (rows - self.tc) % self.ls, rows % self.lc), jnp.where(is_lat, self.ls, self.lc)


def _mod_spec(lay, tm, slot, d):
    return pl.BlockSpec((1, 1, d), lambda i, *_: (lay.mod_index(i, tm) * 6 + slot, 0, 0))


def _rope(z, cos, sin_signed, first_half):
    swapped = jnp.where(first_half, pltpu.roll(z, HEAD_DIM - ROT_F, 1), pltpu.roll(z, ROT_F, 1))
    return z * cos + swapped * sin_signed


def _inproj_kernel(x_ref, shift_ref, scale_ref, g_ref, w_ref, cos_ref, sin_ref, h_ref, y_ref, kv_ref):
    n = pl.program_id(1)
    n_q = ATTN_WIDTH // COL_TILE

    @pl.when(n == 0)
    def _():
        x = x_ref[...]
        ms = jnp.mean(x * x, axis=-1, keepdims=True)
        y = x * lax.rsqrt(ms + EPS) * g_ref[...]
        h_ref[...] = (y * (1.0 + scale_ref[0]) + shift_ref[0]).astype(BF16)

    acc = jnp.dot(h_ref[...], w_ref[...], preferred_element_type=F32)

    def rotated(n_heads):
        cos, sin = cos_ref[...], sin_ref[...]
        first_half = (lax.broadcasted_iota(I32, cos.shape, 1) % (2 * ROT_F)) < ROT_F
        parts = [_rope(acc[:, s * HEAD_DIM:(s + 1) * HEAD_DIM], cos, sin, first_half) for s in range(n_heads)]
        parts.append(acc[:, n_heads * HEAD_DIM:])
        return jnp.concatenate(parts, axis=1) if n_heads * HEAD_DIM < COL_TILE else jnp.concatenate(parts[:-1], axis=1)

    @pl.when(n < n_q)
    def _():
        y_ref[...] = rotated(COL_TILE // HEAD_DIM).astype(BF16)

    @pl.when(n == n_q)
    def _():
        kv_ref[...] = acc
        y_ref[...] = rotated(N_KV_HEADS).astype(BF16)

    @pl.when(n > n_q)
    def _():
        y_ref[...] = acc.astype(BF16)


def _inproj(lay, x, mod_l, g1, w_in16, rope_cos, rope_sin):
    d = lay.d
    tm = lay.row_tile(1024)
    nct, tps = lay.tc // tm, lay.ls // tm

    def rope_idx(i, n):
        return (jnp.where(i < nct, 0, 1 + (i - nct) % tps), 0)

    return pl.pallas_call(
        _inproj_kernel,
        grid=(lay.t // tm, IN_WIDTH // COL_TILE),
        in_specs=[
            pl.BlockSpec((tm, d), lambda i, n: (i, 0)),
            _mod_spec(lay, tm, 0, d),
            _mod_spec(lay, tm, 1, d),
            pl.BlockSpec((1, d), lambda i, n: (0, 0)),
            pl.BlockSpec((d, COL_TILE), lambda i, n: (0, n)),
            pl.BlockSpec((tm, HEAD_DIM), rope_idx),
            pl.BlockSpec((tm, HEAD_DIM), rope_idx),
        ],
        out_specs=[
            pl.BlockSpec((tm, d), lambda i, n: (i, 0)),
            pl.BlockSpec((tm, COL_TILE), lambda i, n: (i, n)),
            pl.BlockSpec((tm, 2 * KV_WIDTH), lambda i, n: (i, 0)),
        ],
        out_shape=[
            jax.ShapeDtypeStruct((lay.t, d), BF16),
            jax.ShapeDtypeStruct((lay.t, IN_WIDTH), BF16),
            jax.ShapeDtypeStruct((lay.t, 2 * KV_WIDTH), F32),
        ],
        compiler_params=_cparams(("arbitrary", "arbitrary")),
    )(x, mod_l, mod_l, g1.reshape(1, d), w_in16, rope_cos, rope_sin)


def _rope_tables(lay, tm):
    t = jnp.arange(lay.ls)
    row = (t // GRID_W).astype(F32)
    col = (t % GRID_W).astype(F32)
    inv = ROPE_BASE ** (-jnp.arange(ROT_F, dtype=F32) / ROT_F)
    ar, ac = row[:, None] * inv, col[:, None] * inv
    cos = jnp.concatenate([jnp.cos(ar), jnp.cos(ar), jnp.cos(ac), jnp.cos(ac)], axis=1)
    sin = jnp.concatenate([-jnp.sin(ar), jnp.sin(ar), -jnp.sin(ac), jnp.sin(ac)], axis=1)
    cos = jnp.concatenate([jnp.ones((tm, HEAD_DIM), F32), cos], axis=0)
    sin = jnp.concatenate([jnp.zeros((tm, HEAD_DIM), F32), sin], axis=0)
    return cos, sin


def _attend(q, sink_ref, j, parts):
    nq = q.shape[0]
    q4 = jnp.concatenate([q[:, g * HEAD_DIM:(g + 1) * HEAD_DIM] for g in range(GROUP)], axis=0)
    sink = jnp.concatenate([jnp.full((nq, 1), sink_ref[j * GROUP + g], F32) for g in range(GROUP)], axis=0)
    scores = []
    m = sink
    for k, _, mask in parts:
        s = lax.dot_general(q4, k, (((1,), (1,)), ((), ())), preferred_element_type=F32) * ATTN_SCALE
        if mask is not None:
            s = jnp.where(mask, s, NEG_INF)
        scores.append(s)
        m = jnp.maximum(m, jnp.max(s, axis=-1, keepdims=True))
    den = jnp.exp(sink - m)
    out = jnp.zeros((GROUP * nq, HEAD_DIM), F32)
    for s, (_, v, _) in zip(scores, parts):
        p = jnp.exp(s - m)
        den = den + jnp.sum(p, axis=-1, keepdims=True)
        out = out + jnp.dot(p.astype(BF16), v, preferred_element_type=F32)
    out = out / den
    return jnp.concatenate([out[g * nq:(g + 1) * nq] for g in range(GROUP)], axis=1)


def _head(x, j, width=HEAD_DIM):
    return x[:, j * width:(j + 1) * width]


def _attn_ctx_kernel(sink_ref, q_ref, k_ref, v_ref, o_ref):
    q, k, v = q_ref[...], k_ref[...], v_ref[...]
    outs = [_attend(_head(q, j, GROUP * HEAD_DIM), sink_ref, j, [(_head(k, j), _head(v, j), None)])
            for j in range(N_KV_HEADS)]
    o_ref[...] = jnp.concatenate(outs, axis=1).astype(o_ref.dtype)


def _attn_lat_kernel(sink_ref, q_ref, kp_ref, kc_ref, kn_ref, vp_ref, vc_ref, vn_ref, ck_ref, cv_ref, o_ref, *, seq_len):
    i = pl.program_id(1)
    q = q_ref[...]
    kw = jnp.concatenate([kp_ref[...], kc_ref[...], kn_ref[...]], axis=0)
    vw = jnp.concatenate([vp_ref[...], vc_ref[...], vn_ref[...]], axis=0)
    ck = ck_ref[...].astype(BF16)
    cv = cv_ref[...].astype(BF16)
    shape = (GROUP * ATTN_BLOCK, 3 * ATTN_BLOCK)
    qoff = lax.broadcasted_iota(I32, shape, 0) % ATTN_BLOCK
    koff = lax.broadcasted_iota(I32, shape, 1) - ATTN_BLOCK
    kabs = koff + i * ATTN_BLOCK
    mask = (jnp.abs(qoff - koff) <= WINDOW) & (kabs >= 0) & (kabs < seq_len)
    outs = [_attend(_head(q, j, GROUP * HEAD_DIM), sink_ref, j,
                    [(_head(kw, j), _head(vw, j), mask), (_head(ck, j), _head(cv, j), None)])
            for j in range(N_KV_HEADS)]
    o_ref[...] = jnp.concatenate(outs, axis=1).astype(o_ref.dtype)


def _attention(lay, y16, sink, cache_k_l, cache_v_l):
    smem = pl.BlockSpec(memory_space=pltpu.SMEM)
    kcol, vcol = ATTN_WIDTH // KV_WIDTH, (ATTN_WIDTH + KV_WIDTH) // KV_WIDTH
    ctx = pl.pallas_call(
        _attn_ctx_kernel,
        grid=(lay.bc,),
        in_specs=[
            smem,
            pl.BlockSpec((lay.lc, ATTN_WIDTH), lambda b: (b, 0)),
            pl.BlockSpec((lay.lc, KV_WIDTH), lambda b: (b, kcol)),
            pl.BlockSpec((lay.lc, KV_WIDTH), lambda b: (b, vcol)),
        ],
        out_specs=pl.BlockSpec((lay.lc, ATTN_WIDTH), lambda b: (b, 0)),
        out_shape=jax.ShapeDtypeStruct((lay.tc, ATTN_WIDTH), BF16),
        compiler_params=_cparams(("arbitrary",)),
    )(sink, y16, y16, y16)

    nblk = lay.ls // ATTN_BLOCK
    base = lay.tc // ATTN_BLOCK
    last = lay.t // ATTN_BLOCK - 1
    past = cache_k_l.shape[1]

    def rb(b, i):
        return base + b * nblk + i

    def kspec(col, delta):
        return pl.BlockSpec((ATTN_BLOCK, KV_WIDTH), lambda b, i: (jnp.clip(rb(b, i) + delta, 0, last), col))

    cspec = pl.BlockSpec((None, past, KV_WIDTH), lambda b, i: (b, 0, 0))
    lat = pl.pallas_call(
        functools.partial(_attn_lat_kernel, seq_len=lay.ls),
        grid=(lay.bs, nblk),
        in_specs=[
            smem,
            pl.BlockSpec((ATTN_BLOCK, ATTN_WIDTH), lambda b, i: (rb(b, i), 0)),
            kspec(kcol, -1), kspec(kcol, 0), kspec(kcol, 1),
            kspec(vcol, -1), kspec(vcol, 0), kspec(vcol, 1),
            cspec, cspec,
        ],
        out_specs=pl.BlockSpec((ATTN_BLOCK, ATTN_WIDTH), lambda b, i: (b * nblk + i, 0)),
        out_shape=jax.ShapeDtypeStruct((lay.ts, ATTN_WIDTH), BF16),
        compiler_params=_cparams(("arbitrary", "arbitrary")),
    )(sink, y16, y16, y16, y16, y16, y16, y16,
      cache_k_l.reshape(lay.bs, past, KV_WIDTH), cache_v_l.reshape(lay.bs, past, KV_WIDTH))
    return jnp.concatenate([ctx, lat], axis=0)


def _s5_matrices(lam_re, lam_im, log_dt, b_re, b_im, c_re, c_im, d_skip, n_steps):
    q, p, g, n = SSM_CHUNK, SSM_CH, SSM_GROUPS, SSM_STATE
    lam = lax.complex(lam_re.astype(F32), lam_im.astype(F32))
    dt = jnp.exp(log_dt.astype(F32))[..., None]
    lam_dt = lam * dt
    lam_bar = jnp.exp(lam_dt)
    b_bar = ((lam_bar - 1.0) / lam)[..., None] * lax.complex(b_re.astype(F32), b_im.astype(F32))
    c_mat = lax.complex(c_re.astype(F32), c_im.astype(F32))
    steps = jnp.arange(q + 1, dtype=F32)
    pw = jnp.exp(lam_dt[:, None] * steps[None, :, None, None])
    kern = jnp.real(jnp.einsum('dgpn,dkgn,dgnr->dkgpr', c_mat, pw[:, :q], b_bar))
    tau_in = jnp.arange(q)[:, None]
    tau_out = jnp.arange(q)[None, :]
    lag_f = tau_out - tau_in
    lag_b = tau_in - tau_out
    kf = jnp.where((lag_f >= 0)[:, :, None, None, None], kern[0][jnp.clip(lag_f, 0, q - 1)], 0.0)
    kb = jnp.where((lag_b >= 0)[:, :, None, None, None], kern[1][jnp.clip(lag_b, 0, q - 1)], 0.0)
    m = (kf + kb).transpose(2, 0, 4, 1, 3)
    eye_q = jnp.eye(q, dtype=F32)[None, :, None, :, None]
    eye_p = jnp.eye(p, dtype=F32)[None, None, :, None, :]
    m = m + eye_q * eye_p * d_skip.astype(F32).reshape(g, 1, p, 1, 1)
    m = m.reshape(g, q * p, q * p)
    ws_f = pw[0, :q][::-1][:, :, :, None] * b_bar[0][None]
    ws_b = pw[1, :q][:, :, :, None] * b_bar[1][None]

    def cols(w):
        return w.transpose(1, 0, 3, 2).reshape(g, q * p, n)

    w1 = jnp.concatenate([m, jnp.real(cols(ws_f)), jnp.real(cols(ws_b)),
                          jnp.imag(cols(ws_f)), jnp.imag(cols(ws_b))], axis=2)
    cy_f = c_mat[0][None] * pw[0, 1:][:, :, None, :]
    cy_b = c_mat[1][None] * pw[1, 1:][::-1][:, :, None, :]

    def rows(w):
        return w.transpose(1, 3, 0, 2).reshape(g, n, q * p)

    wy = jnp.concatenate([jnp.real(rows(cy_f)), jnp.real(rows(cy_b)),
                          -jnp.imag(rows(cy_f)), -jnp.imag(rows(cy_b))], axis=1)
    hops = (q * 2.0 ** jnp.arange(8, dtype=F32))[None, :, None, None]
    a = jnp.exp(lam_dt[:, None] * hops)
    a = jnp.concatenate([a[0], a[1]], axis=-1).transpose(1, 0, 2)
    assert n_steps <= 8
    return w1.astype(BF16), wy.astype(BF16), jnp.real(a), jnp.imag(a)


def _s5_kernel(u_ref, w1_ref, wy_ref, are_ref, aim_ref, h0re_ref, h0im_ref, y_ref, fre_ref, fim_ref, *, nb, nc):
    r_tot = nb * nc
    w = SSM_CHUNK * SSM_CH
    n2 = 2 * SSM_STATE
    proj = jnp.dot(u_ref[0], w1_ref[0], preferred_element_type=F32)
    y_intra = proj[:, :w]
    d_re = proj[:, w:w + n2]
    d_im = proj[:, w + n2:]
    row = lax.broadcasted_iota(I32, (r_tot, n2), 0)
    chunk = row % nc
    seq = row // nc
    fwd = lax.broadcasted_iota(I32, (r_tot, n2), 1) < SSM_STATE

    def previous(x, dist):
        valid = (fwd & (chunk >= dist)) | (~fwd & (chunk < nc - dist))
        moved = jnp.where(fwd, pltpu.roll(x, dist, 0), pltpu.roll(x, r_tot - dist, 0))
        return jnp.where(valid, moved, 0.0)

    h0_re = jnp.zeros((r_tot, n2), F32)
    h0_im = jnp.zeros((r_tot, n2), F32)
    for b in range(nb):
        h0_re = jnp.where(seq == b, h0re_ref[0, b:b + 1, :], h0_re)
        h0_im = jnp.where(seq == b, h0im_ref[0, b:b + 1, :], h0_im)
    first = (fwd & (chunk == 0)) | (~fwd & (chunk == nc - 1))
    e_re = jnp.where(first, h0_re, previous(d_re, 1))
    e_im = jnp.where(first, h0_im, previous(d_im, 1))
    k = 0
    while (1 << k) < nc:
        a_re = are_ref[0, k:k + 1, :]
        a_im = aim_ref[0, k:k + 1, :]
        p_re = previous(e_re, 1 << k)
        p_im = previous(e_im, 1 << k)
        e_re, e_im = e_re + a_re * p_re - a_im * p_im, e_im + a_re * p_im + a_im * p_re
        k += 1
    e_cat = jnp.concatenate([e_re, e_im], axis=1).astype(BF16)
    y_ref[0] = y_intra + jnp.dot(e_cat, wy_ref[0], preferred_element_type=F32)
    a_re = are_ref[0, 0:1, :]
    a_im = aim_ref[0, 0:1, :]
    f_re = a_re * e_re - a_im * e_im + d_re
    f_im = a_re * e_im + a_im * e_re + d_im
    fwd_row = fwd[0:1, :]
    for b in range(nb):
        lo, hi = b * nc, b * nc + nc - 1
        fre_ref[0, b:b + 1, :] = jnp.where(fwd_row, f_re[hi:hi + 1, :], f_re[lo:lo + 1, :])
        fim_ref[0, b:b + 1, :] = jnp.where(fwd_row, f_im[hi:hi + 1, :], f_im[lo:lo + 1, :])


def _s5_scan(u, mats, h0_re, h0_im):
    nb, length, _ = u.shape
    g, q, p, n = SSM_GROUPS, SSM_CHUNK, SSM_CH, SSM_STATE
    nc = length // q
    assert nc & (nc - 1) == 0 and nc % 8 == 0
    w1, wy, a_re, a_im = mats
    ut = u.reshape(nb, nc, q, g, p).transpose(3, 0, 1, 2, 4).reshape(g, nb * nc, q * p)

    def lanes(h):
        return h.astype(F32).transpose(2, 0, 1, 3).reshape(g, nb, 2 * n)

    r_tot = nb * nc
    per_g = lambda *shape: pl.BlockSpec((1,) + shape, lambda i: (i,) + (0,) * len(shape))
    y, f_re, f_im = pl.pallas_call(
        functools.partial(_s5_kernel, nb=nb, nc=nc),
        grid=(g,),
        in_specs=[per_g(r_tot, q * p), per_g(q * p, q * p + 4 * n), per_g(4 * n, q * p),
                  per_g(8, 2 * n), per_g(8, 2 * n), per_g(nb, 2 * n), per_g(nb, 2 * n)],
        out_specs=[per_g(r_tot, q * p), per_g(nb, 2 * n), per_g(nb, 2 * n)],
        out_shape=[jax.ShapeDtypeStruct((g, r_tot, q * p), F32),
                   jax.ShapeDtypeStruct((g, nb, 2 * n), F32),
                   jax.ShapeDtypeStruct((g, nb, 2 * n), F32)],
        compiler_params=_cparams(("arbitrary",)),
    )(ut, w1, wy, a_re, a_im, lanes(h0_re), lanes(h0_im))
    y = y.reshape(g, nb, nc, q, p).transpose(1, 2, 3, 0, 4).reshape(nb, length, g * p)

    def unlanes(f):
        return f.reshape(g, nb, 2, n).transpose(1, 2, 0, 3)

    return y, unlanes(f_re), unlanes(f_im)


def _merge_kernel(h_ref, ys_ref, at_ref, gb_ref, gc_ref, uc_ref, gcp_ref, ucp_ref, gcn_ref, ucn_ref, cw_ref,
                  wglu_ref, wg0_ref, wg1_ref, wg2_ref, bg0_ref, bg1_ref, bg2_ref, ws_ref, wa_ref, wc_ref,
                  o_ref, ssm_scr, conv_scr, *, lay, tm, halo):
    i = pl.program_id(0)
    n = pl.program_id(1)

    @pl.when(n == 0)
    def _():
        y = ys_ref[...]
        ge = 0.5 * y * (1.0 + jnp.tanh(math.sqrt(2.0 / math.pi) * (y + 0.044715 * (y * y * y))))
        glu = jnp.dot(ge.astype(BF16), wglu_ref[...], preferred_element_type=F32)
        ssm_scr[...] = (ge * _sigmoid(glu)).astype(BF16)

        z = gc_ref[...].astype(F32) * uc_ref[...].astype(F32)
        z_before = gcp_ref[halo - 1:halo, :].astype(F32) * ucp_ref[halo - 1:halo, :].astype(F32)
        z_after = gcn_ref[0:1, :].astype(F32) * ucn_ref[0:1, :].astype(F32)
        local = lax.broadcasted_iota(I32, z.shape, 0)
        pos, seq_len = lay.seq_pos(local + i * tm, i, tm)
        z_prev = jnp.where(local == 0, z_before, pltpu.roll(z, 1, 0))
        z_prev = jnp.where(pos == 0, 0.0, z_prev)
        z_next = jnp.where(local == tm - 1, z_after, pltpu.roll(z, tm - 1, 0))
        z_next = jnp.where(pos == seq_len - 1, 0.0, z_next)
        conv = cw_ref[0:1, :] * z_prev + cw_ref[1:2, :] * z + cw_ref[2:3, :] * z_next
        conv_scr[...] = (gb_ref[...].astype(F32) * conv).astype(BF16)

    h = h_ref[...]
    acc = None
    for act, wg_ref, bg_ref, wb_ref in ((ssm_scr[...], wg0_ref, bg0_ref, ws_ref),
                                        (at_ref[...], wg1_ref, bg1_ref, wa_ref),
                                        (conv_scr[...], wg2_ref, bg2_ref, wc_ref)):
        gate = _sigmoid(jnp.dot(h, wg_ref[...], preferred_element_type=F32) + bg_ref[...])
        term = gate * jnp.dot(act, wb_ref[...], preferred_element_type=F32)
        acc = term if acc is None else acc + term
    o_ref[...] = acc.astype(BF16)


def _merge(lay, h16, y_ssm, attn, y16, conv_w, wglu16, wgates16, b_gates, wbs16, wba16, wbc16):
    d = lay.d
    tm = lay.row_tile(512)
    tn = min(COL_TILE, d)
    nd = d // tn
    halo = 16
    hb = tm // halo
    last_h = lay.t // halo - 1
    c0 = (ATTN_WIDTH + 2 * KV_WIDTH + SSM_WIDTH) // CONV_WIDTH
    row = lambda cb: pl.BlockSpec((tm, CONV_WIDTH), lambda i, n: (i, cb))
    before = lambda cb: pl.BlockSpec((halo, CONV_WIDTH), lambda i, n: (jnp.maximum(i * hb - 1, 0), cb))
    after = lambda cb: pl.BlockSpec((halo, CONV_WIDTH), lambda i, n: (jnp.minimum((i + 1) * hb, last_h), cb))
    gate_w = lambda br: pl.BlockSpec((d, tn), lambda i, n: (0, br * nd + n))
    gate_b = lambda br: pl.BlockSpec((1, tn), lambda i, n: (0, br * nd + n))
    return pl.pallas_call(
        functools.partial(_merge_kernel, lay=lay, tm=tm, halo=halo),
        grid=(lay.t // tm, nd),
        in_specs=[
            pl.BlockSpec((tm, d), lambda i, n: (i, 0)),
            pl.BlockSpec((tm, SSM_WIDTH), lambda i, n: (i, 0)),
            pl.BlockSpec((tm, ATTN_WIDTH), lambda i, n: (i, 0)),
            row(c0), row(c0 + 1), row(c0 + 2),
            before(c0 + 1), before(c0 + 2), after(c0 + 1), after(c0 + 2),
            pl.BlockSpec((3, CONV_WIDTH), lambda i, n: (0, 0)),
            pl.BlockSpec((SSM_WIDTH, SSM_WIDTH), lambda i, n: (0, 0)),
            gate_w(0), gate_w(1), gate_w(2), gate_b(0), gate_b(1), gate_b(2),
            pl.BlockSpec((SSM_WIDTH, tn), lambda i, n: (0, n)),
            pl.BlockSpec((ATTN_WIDTH, tn), lambda i, n: (0, n)),
            pl.BlockSpec((CONV_WIDTH, tn), lambda i, n: (0, n)),
        ],
        out_specs=pl.BlockSpec((tm, tn), lambda i, n: (i, n)),
        out_shape=jax.ShapeDtypeStruct((lay.t, d), BF16),
        scratch_shapes=[pltpu.VMEM((tm, SSM_WIDTH), BF16), pltpu.VMEM((tm, CONV_WIDTH), BF16)],
        compiler_params=_cparams(("arbitrary", "arbitrary")),
    )(h16, y_ssm, attn, y16, y16, y16, y16, y16, y16, y16, conv_w, wglu16,
      wgates16, wgates16, wgates16, b_gates, b_gates, b_gates, wbs16, wba16, wbc16)


def _outproj_kernel(m_ref, w_ref, x_ref, gate_ref, g2_ref, shift_ref, scale_ref, wrh_ref, wrl_ref,
                    xo_ref, hp_ref, lg_ref):
    acc = jnp.dot(m_ref[...], w_ref[...], preferred_element_type=F32)
    xn = x_ref[...] + gate_ref[0] * acc
    xo_ref[...] = xn
    ms = jnp.mean(xn * xn, axis=-1, keepdims=True)
    h2 = xn * lax.rsqrt(ms + EPS) * g2_ref[...]
    h2 = h2 * (1.0 + scale_ref[0]) + shift_ref[0]
    hp_ref[...] = _pack_bf16_pairs(h2)
    h_hi = h2.astype(BF16)
    h_lo = (h2 - h_hi.astype(F32)).astype(BF16)
    logits = (jnp.dot(h_hi, wrh_ref[...], preferred_element_type=F32)
              + jnp.dot(h_hi, wrl_ref[...], preferred_element_type=F32)
              + jnp.dot(h_lo, wrh_ref[...], preferred_element_type=F32))
    lg_ref[...] = logits.T[:N_EXPERTS, :]


def _outproj(lay, merged, wout16, x, mod_l, g2, w_router):
    d = lay.d
    tm = lay.row_tile(256)
    lanes = 128
    wr = jnp.zeros((d, lanes), F32).at[:, :N_EXPERTS].set(w_router.astype(F32))
    wr_hi = wr.astype(BF16)
    wr_lo = (wr - wr_hi.astype(F32)).astype(BF16)
    return pl.pallas_call(
        _outproj_kernel,
        grid=(lay.t // tm,),
        in_specs=[
            pl.BlockSpec((tm, d), lambda i: (i, 0)),
            pl.BlockSpec((d, d), lambda i: (0, 0)),
            pl.BlockSpec((tm, d), lambda i: (i, 0)),
            _mod_spec(lay, tm, 2, d),
            pl.BlockSpec((1, d), lambda i: (0, 0)),
            _mod_spec(lay, tm, 3, d),
            _mod_spec(lay, tm, 4, d),
            pl.BlockSpec((d, lanes), lambda i: (0, 0)),
            pl.BlockSpec((d, lanes), lambda i: (0, 0)),
        ],
        out_specs=[
            pl.BlockSpec((tm, d), lambda i: (i, 0)),
            pl.BlockSpec((tm, d // 2), lambda i: (i, 0)),
            pl.BlockSpec((N_EXPERTS, tm), lambda i: (0, i)),
        ],
        out_shape=[
            jax.ShapeDtypeStruct((lay.t, d), F32),
            jax.ShapeDtypeStruct((lay.t, d // 2), U32),
            jax.ShapeDtypeStruct((N_EXPERTS, lay.t), F32),
        ],
        compiler_params=_cparams(("arbitrary",)),
    )(merged, wout16, x, mod_l, g2.reshape(1, d), mod_l, mod_l, wr_hi, wr_lo)


def _route_kernel(lg_ref, br_ref, idx_ref, w_ref, pos_ref, cnt_ref, carry):
    step = pl.program_id(0)
    tt = lg_ref.shape[1]
    per_group = N_EXPERTS // N_EXPERT_GROUPS

    @pl.when(step == 0)
    def _():
        carry[...] = jnp.zeros_like(carry)

    scores = _sigmoid(lg_ref[...])
    biased = scores + br_ref[...]
    sub = lax.broadcasted_iota(I32, (per_group, tt), 0).astype(F32)
    blocks, group_score = [], []
    for g in range(N_EXPERT_GROUPS):
        blk = biased[g * per_group:(g + 1) * per_group, :]
        m1 = jnp.max(blk, axis=0, keepdims=True)
        i1 = jnp.min(jnp.where(blk == m1, sub, float(per_group)), axis=0, keepdims=True)
        m2 = jnp.max(jnp.where(sub == i1, -jnp.inf, blk), axis=0, keepdims=True)
        blocks.append(blk)
        group_score.append(m1 + m2)
    masked = []
    for g in range(N_EXPERT_GROUPS):
        beaten_by = jnp.zeros((1, tt), F32)
        for o in range(N_EXPERT_GROUPS):
            if o == g:
                continue
            wins = (group_score[o] > group_score[g]) | ((group_score[o] == group_score[g]) & (o < g))
            beaten_by = beaten_by + wins.astype(F32)
        masked.append(jnp.where(beaten_by < TOPK_GROUPS, blocks[g], -jnp.inf))
    masked = jnp.concatenate(masked, axis=0)
    eid = lax.broadcasted_iota(I32, (N_EXPERTS, tt), 0).astype(F32)
    chosen, weights = [], []
    onehot = jnp.zeros((N_EXPERTS, tt), F32)
    for _ in range(TOP_K):
        m = jnp.max(masked, axis=0, keepdims=True)
        e = jnp.min(jnp.where(masked == m, eid, float(N_EXPERTS)), axis=0, keepdims=True)
        hit = eid == e
        chosen.append(e)
        weights.append(jnp.sum(jnp.where(hit, scores, 0.0), axis=0, keepdims=True))
        onehot = onehot + hit.astype(F32)
        masked = jnp.where(hit, -jnp.inf, masked)
    total = weights[0]
    for wk in weights[1:]:
        total = total + wk
    earlier = (lax.broadcasted_iota(I32, (tt, tt), 0) < lax.broadcasted_iota(I32, (tt, tt), 1)).astype(BF16)
    rank = carry[...][:, 0:1] + jnp.dot(onehot.astype(BF16), earlier, preferred_element_type=F32)
    for k in range(TOP_K):
        idx_ref[k:k + 1, :] = chosen[k].astype(I32)
        w_ref[k:k + 1, :] = weights[k] / total * ROUTED_SCALE
        pos_ref[k:k + 1, :] = jnp.sum(jnp.where(eid == chosen[k], rank, 0.0), axis=0, keepdims=True).astype(I32)
    for k in range(TOP_K, 8):
        idx_ref[k:k + 1, :] = jnp.zeros((1, tt), I32)
        w_ref[k:k + 1, :] = jnp.zeros((1, tt), F32)
        pos_ref[k:k + 1, :] = jnp.zeros((1, tt), I32)
    carry[...] = carry[...] + jnp.sum(onehot, axis=1, keepdims=True)
    cnt_ref[...] = carry[...]


def _route(logits_t, b_router):
    t = logits_t.shape[1]
    tt = math.gcd(t, 512)
    tok = pl.BlockSpec((8, tt), lambda i: (0, i))
    return pl.pallas_call(
        _route_kernel,
        grid=(t // tt,),
        in_specs=[pl.BlockSpec((N_EXPERTS, tt), lambda i: (0, i)),
                  pl.BlockSpec((N_EXPERTS, 1), lambda i: (0, 0))],
        out_specs=[tok, tok, tok, pl.BlockSpec((N_EXPERTS, 128), lambda i: (0, 0))],
        out_shape=[jax.ShapeDtypeStruct((8, t), I32), jax.ShapeDtypeStruct((8, t), F32),
                   jax.ShapeDtypeStruct((8, t), I32), jax.ShapeDtypeStruct((N_EXPERTS, 128), F32)],
        scratch_shapes=[pltpu.VMEM((N_EXPERTS, 128), F32)],
        compiler_params=_cparams(("arbitrary",)),
    )(logits_t, b_router.astype(F32).reshape(N_EXPERTS, 1))


def _dest_kernel(idx_ref, pos_ref, start_ref, o_ref, *, tt):
    eid = lax.broadcasted_iota(I32, (N_EXPERTS, tt), 0)
    start = start_ref[...]
    for k in range(TOP_K):
        first = jnp.sum(jnp.where(eid == idx_ref[k:k + 1, :], start, 0.0), axis=0, keepdims=True)
        o_ref[k:k + 1, :] = first.astype(I32) + pos_ref[k:k + 1, :]
    for k in range(TOP_K, 8):
        o_ref[k:k + 1, :] = jnp.zeros((1, tt), I32)


def _dest_rows(idx_t, pos_t, pad_start):
    t = idx_t.shape[1]
    tt = math.gcd(t, 512)
    tok = pl.BlockSpec((8, tt), lambda i: (0, i))
    return pl.pallas_call(
        functools.partial(_dest_kernel, tt=tt),
        grid=(t // tt,),
        in_specs=[tok, tok, pl.BlockSpec((N_EXPERTS, 1), lambda i: (0, 0))],
        out_specs=tok,
        out_shape=jax.ShapeDtypeStruct((8, t), I32),
        compiler_params=_cparams(("arbitrary",)),
    )(idx_t, pos_t, pad_start.astype(F32).reshape(N_EXPERTS, 1))


def _sc_layout(n_pairs, period):
    info = plsc.get_sparse_core_info()
    workers = info.num_cores * info.num_subcores
    per_worker = n_pairs // workers
    chunk = math.gcd(math.gcd(per_worker, period), SC_STREAM_ROWS)
    assert per_worker * workers == n_pairs and chunk % 8 == 0
    return info.num_cores, per_worker, chunk


def _sc_scatter_rows(rows, dest, n_out):
    t, width = rows.shape
    n_cores, per_worker, chunk = _sc_layout(dest.shape[0], t)
    mesh = plsc.VectorSubcoreMesh(core_axis_name="core", subcore_axis_name="subcore")

    @functools.partial(
        pl.kernel, mesh=mesh, out_type=jax.ShapeDtypeStruct((n_out, width), rows.dtype),
        scratch_types=[pltpu.VMEM((chunk,), I32), pltpu.VMEM((chunk, width), rows.dtype)])
    def scatter(rows_hbm, dest_hbm, out_hbm, dest_v, rows_v):
        base = (lax.axis_index("subcore") * n_cores + lax.axis_index("core")) * per_worker

        @pl.loop(0, per_worker // chunk)
        def _(j):
            pair = base + j * chunk
            pltpu.sync_copy(dest_hbm.at[pl.ds(pair, chunk)], dest_v)
            pltpu.sync_copy(rows_hbm.at[pl.ds(lax.rem(pair, t), chunk)], rows_v)
            pltpu.sync_copy(rows_v, out_hbm.at[dest_v])

    return scatter(rows, dest)


def _sc_gather_rows(table, dest):
    width = table.shape[1]
    n_pairs = dest.shape[0]
    n_cores, per_worker, chunk = _sc_layout(n_pairs, n_pairs)
    mesh = plsc.VectorSubcoreMesh(core_axis_name="core", subcore_axis_name="subcore")

    @functools.partial(
        pl.kernel, mesh=mesh, out_type=jax.ShapeDtypeStruct((n_pairs, width), table.dtype),
        scratch_types=[pltpu.VMEM((chunk,), I32), pltpu.VMEM((chunk, width), table.dtype)])
    def gather(table_hbm, dest_hbm, out_hbm, dest_v, rows_v):
        base = (lax.axis_index("subcore") * n_cores + lax.axis_index("core")) * per_worker

        @pl.loop(0, per_worker // chunk)
        def _(j):
            pair = base + j * chunk
            pltpu.sync_copy(dest_hbm.at[pl.ds(pair, chunk)], dest_v)
            pltpu.sync_copy(table_hbm.at[dest_v], rows_v)
            pltpu.sync_copy(rows_v, out_hbm.at[pl.ds(pair, chunk)])

    return gather(table, dest)


def _expert_kernel(be_ref, xs_ref, wg_ref, wu_ref, wd_ref, ys_ref, wg16, wu16, wd16):
    i = pl.program_id(0)
    changed = jnp.logical_or(i == 0, be_ref[i] != be_ref[jnp.maximum(i - 1, 0)])

    @pl.when(changed)
    def _():
        rows = 256
        d = wg16.shape[0]

        def cast_in(r, carry):
            sl = pl.ds(pl.multiple_of(r * rows, rows), rows)
            wg16[sl, :] = wg_ref[0, sl, :].astype(BF16)
            wu16[sl, :] = wu_ref[0, sl, :].astype(BF16)
            return carry

        lax.fori_loop(0, d // rows, cast_in, 0)

        def cast_down(r, carry):
            sl = pl.ds(pl.multiple_of(r * 128, 128), 128)
            wd16[sl, :] = wd_ref[0, sl, :].astype(BF16)
            return carry

        lax.fori_loop(0, D_EXPERT // 128, cast_down, 0)

    n_blocks = pl.num_programs(0)
    used = i < be_ref[n_blocks]

    @pl.when(used)
    def _():
        held = lax.broadcasted_iota(I32, (xs_ref.shape[0], 1), 0) < be_ref[n_blocks + 1 + i]
        hi, lo = _unpack_bf16_pairs(jnp.where(held, xs_ref[...], jnp.uint32(0)))
        x = jnp.concatenate([hi, lo], axis=1).astype(BF16)
        gate = jnp.dot(x, wg16[...], preferred_element_type=F32)
        up = jnp.dot(x, wu16[...], preferred_element_type=F32)
        act = (gate * _sigmoid(gate) * up).astype(BF16)
        ys_ref[...] = _pack_bf16_pairs(jnp.dot(act, wd16[...], preferred_element_type=F32))

    @pl.when(jnp.logical_not(used))
    def _():
        ys_ref[...] = jnp.zeros_like(ys_ref)


def _experts(xs, block_e, layer, w_gate, w_up, w_down):
    n_rows, half = xs.shape
    d = 2 * half
    br = EXPERT_ROWS
    grid_spec = pltpu.PrefetchScalarGridSpec(
        num_scalar_prefetch=1,
        grid=(n_rows // br,),
        in_specs=[
            pl.BlockSpec((br, half), lambda i, be: (i, 0)),
            pl.BlockSpec((None, 1, d, D_EXPERT), lambda i, be: (layer, be[i], 0, 0)),
            pl.BlockSpec((None, 1, d, D_EXPERT), lambda i, be: (layer, be[i], 0, 0)),
            pl.BlockSpec((None, 1, D_EXPERT, d), lambda i, be: (layer, be[i], 0, 0)),
        ],
        out_specs=pl.BlockSpec((br, half), lambda i, be: (i, 0)),
        scratch_shapes=[pltpu.VMEM((d, D_EXPERT), BF16), pltpu.VMEM((d, D_EXPERT), BF16),
                        pltpu.VMEM((D_EXPERT, d), BF16)],
    )
    return pl.pallas_call(
        _expert_kernel,
        grid_spec=grid_spec,
        out_shape=jax.ShapeDtypeStruct((n_rows, half), U32),
        compiler_params=_cparams(("arbitrary",)),
    )(block_e, xs, w_gate, w_up, w_down)


def _combine_kernel(yk_ref, wsel_ref, hp_ref, x_ref, gate_ref, wsg_ref, wsu_ref, wsd_ref, fg_ref,
                    *outs, tt, final, n_ctx_tiles):
    hi, lo = _unpack_bf16_pairs(hp_ref[...])
    h2 = jnp.concatenate([hi, lo], axis=1).astype(BF16)
    sg = jnp.dot(h2, wsg_ref[...], preferred_element_type=F32)
    su = jnp.dot(h2, wsu_ref[...], preferred_element_type=F32)
    shared = jnp.dot((sg * _sigmoid(sg) * su).astype(BF16), wsd_ref[...], preferred_element_type=F32)

    half = hp_ref.shape[1]
    r_hi = jnp.zeros((tt, half), F32)
    r_lo = jnp.zeros((tt, half), F32)
    for k in range(TOP_K):
        y_hi, y_lo = _unpack_bf16_pairs(yk_ref[k])
        wk = wsel_ref[:, k:k + 1]
        r_hi = r_hi + wk * y_hi
        r_lo = r_lo + wk * y_lo
    routed = jnp.concatenate([r_hi, r_lo], axis=1)
    out = x_ref[...] + gate_ref[0] * (routed + shared)
    if not final:
        outs[0][...] = out
        return
    ms = jnp.mean(out * out, axis=-1, keepdims=True)
    out = out * lax.rsqrt(ms + EPS) * fg_ref[...]
    ctx_ref, lat_ref = outs
    is_ctx = pl.program_id(0) < n_ctx_tiles

    @pl.when(is_ctx)
    def _():
        ctx_ref[...] = out

    @pl.when(jnp.logical_not(is_ctx))
    def _():
        lat_ref[...] = out


def _combine(lay, y_by_k, wsel, h2p, x, mod_l, wsg16, wsu16, wsd16, final_g, tt, final):
    d = lay.d
    half = d // 2
    nct = lay.tc // tt
    if final:
        out_specs = [pl.BlockSpec((tt, d), lambda i: (jnp.minimum(i, nct - 1), 0)),
                     pl.BlockSpec((tt, d), lambda i: (jnp.maximum(i - nct, 0), 0))]
        out_shape = [jax.ShapeDtypeStruct((lay.tc, d), F32), jax.ShapeDtypeStruct((lay.ts, d), F32)]
    else:
        out_specs = pl.BlockSpec((tt, d), lambda i: (i, 0))
        out_shape = jax.ShapeDtypeStruct((lay.t, d), F32)
    return pl.pallas_call(
        functools.partial(_combine_kernel, tt=tt, final=final, n_ctx_tiles=nct),
        grid=(lay.t // tt,),
        in_specs=[
            pl.BlockSpec((TOP_K, tt, half), lambda i: (0, i, 0)),
            pl.BlockSpec((tt, 8), lambda i: (i, 0)),
            pl.BlockSpec((tt, half), lambda i: (i, 0)),
            pl.BlockSpec((tt, d), lambda i: (i, 0)),
            _mod_spec(lay, tt, 5, d),
            pl.BlockSpec((d, D_EXPERT), lambda i: (0, 0)),
            pl.BlockSpec((d, D_EXPERT), lambda i: (0, 0)),
            pl.BlockSpec((D_EXPERT, d), lambda i: (0, 0)),
            pl.BlockSpec((1, d), lambda i: (0, 0)),
        ],
        out_specs=out_specs,
        out_shape=out_shape,
        compiler_params=_cparams(("arbitrary",)),
    )(y_by_k, wsel, h2p, x, mod_l, wsg16, wsu16, wsd16, final_g.reshape(1, d))


def _moe(lay, layer, x, h2p, logits_t, mod_l, b_router, w_e_gate, w_e_up, w_e_down, wsg16, wsu16, wsd16, final_g,
         final):
    t = lay.t
    idx_t, w_t, pos_t, counts = _route(logits_t, b_router)
    br = EXPERT_ROWS
    counts = counts[:, 0].astype(I32)
    padded = (counts + br - 1) // br * br
    pad_end = jnp.cumsum(padded)
    pad_start = pad_end - padded
    n_blocks = -(-(t * TOP_K + N_EXPERTS * (br - 1)) // br)
    n_rows = n_blocks * br
    first_row = jnp.arange(n_blocks, dtype=I32) * br
    block_e = jnp.minimum(jnp.sum((pad_end[None, :] <= first_row[:, None]).astype(I32), axis=1), N_EXPERTS - 1)
    own = block_e[:, None] == jnp.arange(N_EXPERTS, dtype=I32)[None, :]
    rows_end = jnp.sum(jnp.where(own, (pad_start + counts)[None, :], 0), axis=1)
    held = jnp.clip(rows_end - first_row, 0, br)
    block_meta = jnp.concatenate([block_e, pad_end[-1:] // br, held]).astype(I32)
    dest = _dest_rows(idx_t, pos_t, pad_start)[:TOP_K].reshape(TOP_K * t)
    xs = _sc_scatter_rows(h2p, dest, n_rows)
    ys = _experts(xs, block_meta, layer, w_e_gate, w_e_up, w_e_down)
    y_by_k = _sc_gather_rows(ys, dest).reshape(TOP_K, t, h2p.shape[1])
    return _combine(lay, y_by_k, w_t.T, h2p, x, mod_l, wsg16, wsu16, wsd16, final_g, lay.row_tile(256), final)


def kernel(x_prompt, x_sample, c, cache_k, cache_v, state_ssm_re, state_ssm_im, c_ctx, w_ada, b_ada, norm1_g, norm2_g, w_in, w_gates, b_gates, ssm_lam_re, ssm_lam_im, ssm_log_dt, ssm_b_re, ssm_b_im, ssm_c_re, ssm_c_im, ssm_d, ssm_w_glu, conv_w, attn_sink, w_br_ssm, w_br_attn, w_br_conv, w_out, w_router, b_router, w_e_gate, w_e_up, w_e_down, w_s_gate, w_s_up, w_s_down, final_g):
    bc, lc, d = x_prompt.shape
    bs, ls, _ = x_sample.shape
    depth = w_in.shape[0]
    lay = _Layout(bc, lc, bs, ls, d)
    assert 1 + bs <= MOD_ROWS

    x = jnp.concatenate([x_prompt.reshape(lay.tc, d), x_sample.reshape(lay.ts, d)], axis=0)
    cvec = jnp.zeros((MOD_ROWS, d), F32).at[0].set(c_ctx).at[1:1 + bs].set(c)
    mod = _adaln(cvec, w_ada, b_ada).reshape(depth, MOD_ROWS * 6, 1, d)
    rope_cos, rope_sin = _rope_tables(lay, lay.row_tile(1024))
    zeros_state = jnp.zeros((bc, 2, SSM_GROUPS, SSM_STATE), F32)

    ks, vs, s_re, s_im = [], [], [], []
    for l in range(depth):
        mod_l = mod[l]
        h16, y16, kv32 = _inproj(lay, x, mod_l, norm1_g[l], w_in[l].astype(BF16), rope_cos, rope_sin)
        ks.append(kv32[:lay.tc, :KV_WIDTH].reshape(bc, lc, N_KV_HEADS, HEAD_DIM))
        vs.append(kv32[:lay.tc, KV_WIDTH:].reshape(bc, lc, N_KV_HEADS, HEAD_DIM))

        attn = _attention(lay, y16, attn_sink[l].astype(F32), cache_k[:, l], cache_v[:, l])

        u0 = ATTN_WIDTH + 2 * KV_WIDTH
        u = y16[:, u0:u0 + SSM_WIDTH]
        mats = _s5_matrices(ssm_lam_re[l], ssm_lam_im[l], ssm_log_dt[l], ssm_b_re[l], ssm_b_im[l],
                            ssm_c_re[l], ssm_c_im[l], ssm_d[l], 8)
        y_c, f_re, f_im = _s5_scan(u[:lay.tc].reshape(bc, lc, SSM_WIDTH), mats, zeros_state, zeros_state)
        y_s, _, _ = _s5_scan(u[lay.tc:].reshape(bs, ls, SSM_WIDTH), mats, state_ssm_re[:, l], state_ssm_im[:, l])
        s_re.append(f_re)
        s_im.append(f_im)
        y_ssm = jnp.concatenate([y_c.reshape(lay.tc, SSM_WIDTH), y_s.reshape(lay.ts, SSM_WIDTH)], axis=0)

        merged = _merge(lay, h16, y_ssm, attn, y16, conv_w[l], ssm_w_glu[l].astype(BF16),
                        w_gates[l].astype(BF16), b_gates[l].reshape(1, -1), w_br_ssm[l].astype(BF16),
                        w_br_attn[l].astype(BF16), w_br_conv[l].astype(BF16))
        x, h2p, logits_t = _outproj(lay, merged, w_out[l].astype(BF16), x, mod_l, norm2_g[l],
                                    w_router[l])
        x = _moe(lay, l, x, h2p, logits_t, mod_l, b_router[l], w_e_gate, w_e_up, w_e_down,
                 w_s_gate[l].astype(BF16), w_s_up[l].astype(BF16), w_s_down[l].astype(BF16),
                 final_g, l == depth - 1)

    y_prompt = x[0].reshape(bc, lc, d)
    y_sample = x[1].reshape(bs, ls, d)
    return (y_prompt, y_sample, jnp.stack(ks, axis=1), jnp.stack(vs, axis=1),
            jnp.stack(s_re, axis=1), jnp.stack(s_im, axis=1))
```

```python
import functools
import math

import jax
import jax.numpy as jnp
from jax import lax
from jax.experimental import pallas as pl
from jax.experimental.pallas import tpu as pltpu
from jax.experimental.pallas import tpu_sc as plsc

HEAD_DIM = 128
N_HEADS = 8
N_KV_HEADS = 2
GROUP = N_HEADS // N_KV_HEADS
ATTN_WIDTH = N_HEADS * HEAD_DIM
KV_WIDTH = N_KV_HEADS * HEAD_DIM
WINDOW = 128
ATTN_BLOCK = 128
ATTN_SCALE = HEAD_DIM ** -0.5
ROPE_BASE = 10000.0
ROT_F = HEAD_DIM // 4
GRID_W = 64
SSM_WIDTH = 512
SSM_CH = 16
SSM_GROUPS = SSM_WIDTH // SSM_CH
SSM_STATE = 64
SSM_CHUNK = 16
S5_BUNDLE = 8
S5_BLOCK_ROWS = 256
CONV_WIDTH = 512
N_BRANCHES = 3
IN_WIDTH = ATTN_WIDTH + 2 * KV_WIDTH + SSM_WIDTH + 3 * CONV_WIDTH
N_EXPERTS = 64
TOP_K = 6
N_EXPERT_GROUPS = 8
TOPK_GROUPS = 4
D_EXPERT = 512
ROUTED_SCALE = 2.5
EPS = 1e-6
NEG_INF = -1e30

COL_TILE = 512
MOD_ROWS = 16
EXPERT_ROWS = 512
SC_STREAM_ROWS = 64
VMEM_LIMIT_V7X = 56 * 1024 * 1024

F32 = jnp.float32
BF16 = jnp.bfloat16
I32 = jnp.int32
U32 = jnp.uint32


def _cparams(sem, vmem=VMEM_LIMIT_V7X):
    return pltpu.CompilerParams(dimension_semantics=sem, vmem_limit_bytes=vmem)


def _sigmoid(x):
    return 1.0 / (1.0 + jnp.exp(-x))


def _pack_bf16_pairs(v):
    n = v.shape[1] // 2
    hi = lax.bitcast_convert_type(v[:, :n].astype(BF16).astype(F32), U32)
    lo = lax.bitcast_convert_type(v[:, n:].astype(BF16).astype(F32), U32)
    return hi | (lo >> 16)


def _unpack_bf16_pairs(p):
    hi = lax.bitcast_convert_type(p & jnp.uint32(0xFFFF0000), F32)
    lo = lax.bitcast_convert_type(p << 16, F32)
    return hi, lo


def _adaln_kernel(c_ref, w_ref, b_ref, o_ref):
    c = c_ref[...]
    s = (c * _sigmoid(c)).astype(BF16)
    o_ref[0] = jnp.dot(s, w_ref[0].astype(BF16), preferred_element_type=F32) + b_ref[0]


def _adaln(cvec, w_ada, b_ada):
    depth, d, n6 = w_ada.shape
    tn = math.gcd(1024, n6)
    return pl.pallas_call(
        _adaln_kernel,
        grid=(depth, n6 // tn),
        in_specs=[
            pl.BlockSpec((MOD_ROWS, d), lambda l, n: (0, 0)),
            pl.BlockSpec((1, d, tn), lambda l, n: (l, 0, n)),
            pl.BlockSpec((1, 1, tn), lambda l, n: (l, 0, n)),
        ],
        out_specs=pl.BlockSpec((1, MOD_ROWS, tn), lambda l, n: (l, 0, n)),
        out_shape=jax.ShapeDtypeStruct((depth, MOD_ROWS, n6), F32),
        compiler_params=_cparams(("arbitrary", "arbitrary")),
    )(cvec, w_ada, b_ada.reshape(depth, 1, n6))


class _Layout:
    def __init__(self, n_ctx_seq, len_ctx, n_lat_seq, len_lat, d_model):
        self.bc, self.lc, self.bs, self.ls, self.d = n_ctx_seq, len_ctx, n_lat_seq, len_lat, d_model
        self.tc = n_ctx_seq * len_ctx
        self.ts = n_lat_seq * len_lat
        self.t = self.tc + self.ts

    def row_tile(self, want):
        tm = math.gcd(math.gcd(self.tc, self.ls), want)
        assert tm % 16 == 0
        return tm

    def mod_index(self, i, tm):
        nct, tps = self.tc // tm, self.ls // tm
        return jnp.where(i < nct, 0, 1 + (i - nct) // tps)

    def seq_pos(self, rows, i, tm):
        is_lat = i >= self.tc // tm
        return jnp.where(is_lat, (rows - self.tc) % self.ls, rows % self.lc), jnp.where(is_lat, self.ls, self.lc)


def _mod_spec(lay, tm, slot, d):
    return pl.BlockSpec((1, 1, d), lambda i, *_: (lay.mod_index(i, tm) * 6 + slot, 0, 0))


def _rope(z, cos, sin_signed, first_half):
    swapped = jnp.where(first_half, pltpu.roll(z, HEAD_DIM - ROT_F, 1), pltpu.roll(z, ROT_F, 1))
    return z * cos + swapped * sin_signed


def _inproj_kernel(x_ref, shift_ref, scale_ref, g_ref, w_ref, cos_ref, sin_ref, h_ref, y_ref, kv_ref):
    n = pl.program_id(1)
    n_q = ATTN_WIDTH // COL_TILE

    @pl.when(n == 0)
    def _():
        x = x_ref[...]
        ms = jnp.mean(x * x, axis=-1, keepdims=True)
        y = x * lax.rsqrt(ms + EPS) * g_ref[...]
        h_ref[...] = (y * (1.0 + scale_ref[0]) + shift_ref[0]).astype(BF16)

    acc = jnp.dot(h_ref[...], w_ref[...], preferred_element_type=F32)

    def rotated(n_heads):
        cos, sin = cos_ref[...], sin_ref[...]
        first_half = (lax.broadcasted_iota(I32, cos.shape, 1) % (2 * ROT_F)) < ROT_F
        parts = [_rope(acc[:, s * HEAD_DIM:(s + 1) * HEAD_DIM], cos, sin, first_half) for s in range(n_heads)]
        parts.append(acc[:, n_heads * HEAD_DIM:])
        return jnp.concatenate(parts, axis=1) if n_heads * HEAD_DIM < COL_TILE else jnp.concatenate(parts[:-1], axis=1)

    @pl.when(n < n_q)
    def _():
        y_ref[...] = rotated(COL_TILE // HEAD_DIM).astype(BF16)

    @pl.when(n == n_q)
    def _():
        kv_ref[...] = acc
        y_ref[...] = rotated(N_KV_HEADS).astype(BF16)

    @pl.when(n > n_q)
    def _():
        y_ref[...] = acc.astype(BF16)


def _inproj(lay, x, mod_l, g1, w_in16, rope_cos, rope_sin):
    d = lay.d
    tm = lay.row_tile(1024)
    nct, tps = lay.tc // tm, lay.ls // tm

    def rope_idx(i, n):
        return (jnp.where(i < nct, 0, 1 + (i - nct) % tps), 0)

    return pl.pallas_call(
        _inproj_kernel,
        grid=(lay.t // tm, IN_WIDTH // COL_TILE),
        in_specs=[
            pl.BlockSpec((tm, d), lambda i, n: (i, 0)),
            _mod_spec(lay, tm, 0, d),
            _mod_spec(lay, tm, 1, d),
            pl.BlockSpec((1, d), lambda i, n: (0, 0)),
            pl.BlockSpec((d, COL_TILE), lambda i, n: (0, n)),
            pl.BlockSpec((tm, HEAD_DIM), rope_idx),
            pl.BlockSpec((tm, HEAD_DIM), rope_idx),
        ],
        out_specs=[
            pl.BlockSpec((tm, d), lambda i, n: (i, 0)),
            pl.BlockSpec((tm, COL_TILE), lambda i, n: (i, n)),
            pl.BlockSpec((tm, 2 * KV_WIDTH), lambda i, n: (i, 0)),
        ],
        out_shape=[
            jax.ShapeDtypeStruct((lay.t, d), BF16),
            jax.ShapeDtypeStruct((lay.t, IN_WIDTH), BF16),
            jax.ShapeDtypeStruct((lay.t, 2 * KV_WIDTH), F32),
        ],
        compiler_params=_cparams(("arbitrary", "arbitrary")),
    )(x, mod_l, mod_l, g1.reshape(1, d), w_in16, rope_cos, rope_sin)


def _rope_tables(lay, tm):
    t = jnp.arange(lay.ls)
    row = (t // GRID_W).astype(F32)
    col = (t % GRID_W).astype(F32)
    inv = ROPE_BASE ** (-jnp.arange(ROT_F, dtype=F32) / ROT_F)
    ar, ac = row[:, None] * inv, col[:, None] * inv
    cos = jnp.concatenate([jnp.cos(ar), jnp.cos(ar), jnp.cos(ac), jnp.cos(ac)], axis=1)
    sin = jnp.concatenate([-jnp.sin(ar), jnp.sin(ar), -jnp.sin(ac), jnp.sin(ac)], axis=1)
    cos = jnp.concatenate([jnp.ones((tm, HEAD_DIM), F32), cos], axis=0)
    sin = jnp.concatenate([jnp.zeros((tm, HEAD_DIM), F32), sin], axis=0)
    return cos, sin


def _attend(q, sink_ref, j, parts):
    nq = q.shape[0]
    q4 = jnp.concatenate([q[:, g * HEAD_DIM:(g + 1) * HEAD_DIM] for g in range(GROUP)], axis=0)
    sink = jnp.concatenate([jnp.full((nq, 1), sink_ref[j * GROUP + g], F32) for g in range(GROUP)], axis=0)
    scores = []
    m = sink
    for k, _, mask in parts:
        s = lax.dot_general(q4, k, (((1,), (1,)), ((), ())), preferred_element_type=F32) * ATTN_SCALE
        if mask is not None:
            s = jnp.where(mask, s, NEG_INF)
        scores.append(s)
        m = jnp.maximum(m, jnp.max(s, axis=-1, keepdims=True))
    den = jnp.exp(sink - m)
    out = jnp.zeros((GROUP * nq, HEAD_DIM), F32)
    for s, (_, v, _) in zip(scores, parts):
        p = jnp.exp(s - m)
        den = den + jnp.sum(p, axis=-1, keepdims=True)
        out = out + jnp.dot(p.astype(BF16), v, preferred_element_type=F32)
    out = out / den
    return jnp.concatenate([out[g * nq:(g + 1) * nq] for g in range(GROUP)], axis=1)


def _head(x, j, width=HEAD_DIM):
    return x[:, j * width:(j + 1) * width]


def _attn_ctx_kernel(sink_ref, q_ref, k_ref, v_ref, o_ref):
    q, k, v = q_ref[...], k_ref[...], v_ref[...]
    outs = [_attend(_head(q, j, GROUP * HEAD_DIM), sink_ref, j, [(_head(k, j), _head(v, j), None)])
            for j in range(N_KV_HEADS)]
    o_ref[...] = jnp.concatenate(outs, axis=1).astype(o_ref.dtype)


def _attn_lat_kernel(sink_ref, q_ref, kp_ref, kc_ref, kn_ref, vp_ref, vc_ref, vn_ref, ck_ref, cv_ref, o_ref, *, seq_len):
    i = pl.program_id(1)
    q = q_ref[...]
    kw = jnp.concatenate([kp_ref[...], kc_ref[...], kn_ref[...]], axis=0)
    vw = jnp.concatenate([vp_ref[...], vc_ref[...], vn_ref[...]], axis=0)
    ck = ck_ref[...].astype(BF16)
    cv = cv_ref[...].astype(BF16)
    shape = (GROUP * ATTN_BLOCK, 3 * ATTN_BLOCK)
    qoff = lax.broadcasted_iota(I32, shape, 0) % ATTN_BLOCK
    koff = lax.broadcasted_iota(I32, shape, 1) - ATTN_BLOCK
    kabs = koff + i * ATTN_BLOCK
    mask = (jnp.abs(qoff - koff) <= WINDOW) & (kabs >= 0) & (kabs < seq_len)
    outs = [_attend(_head(q, j, GROUP * HEAD_DIM), sink_ref, j,
                    [(_head(kw, j), _head(vw, j), mask), (_head(ck, j), _head(cv, j), None)])
            for j in range(N_KV_HEADS)]
    o_ref[...] = jnp.concatenate(outs, axis=1).astype(o_ref.dtype)


def _attention(lay, y16, sink, cache_k_l, cache_v_l):
    smem = pl.BlockSpec(memory_space=pltpu.SMEM)
    kcol, vcol = ATTN_WIDTH // KV_WIDTH, (ATTN_WIDTH + KV_WIDTH) // KV_WIDTH
    ctx = pl.pallas_call(
        _attn_ctx_kernel,
        grid=(lay.bc,),
        in_specs=[
            smem,
            pl.BlockSpec((lay.lc, ATTN_WIDTH), lambda b: (b, 0)),
            pl.BlockSpec((lay.lc, KV_WIDTH), lambda b: (b, kcol)),
            pl.BlockSpec((lay.lc, KV_WIDTH), lambda b: (b, vcol)),
        ],
        out_specs=pl.BlockSpec((lay.lc, ATTN_WIDTH), lambda b: (b, 0)),
        out_shape=jax.ShapeDtypeStruct((lay.tc, ATTN_WIDTH), BF16),
        compiler_params=_cparams(("arbitrary",)),
    )(sink, y16, y16, y16)

    nblk = lay.ls // ATTN_BLOCK
    base = lay.tc // ATTN_BLOCK
    last = lay.t // ATTN_BLOCK - 1
    past = cache_k_l.shape[1]

    def rb(b, i):
        return base + b * nblk + i

    def kspec(col, delta):
        return pl.BlockSpec((ATTN_BLOCK, KV_WIDTH), lambda b, i: (jnp.clip(rb(b, i) + delta, 0, last), col))

    cspec = pl.BlockSpec((None, past, KV_WIDTH), lambda b, i: (b, 0, 0))
    lat = pl.pallas_call(
        functools.partial(_attn_lat_kernel, seq_len=lay.ls),
        grid=(lay.bs, nblk),
        in_specs=[
            smem,
            pl.BlockSpec((ATTN_BLOCK, ATTN_WIDTH), lambda b, i: (rb(b, i), 0)),
            kspec(kcol, -1), kspec(kcol, 0), kspec(kcol, 1),
            kspec(vcol, -1), kspec(vcol, 0), kspec(vcol, 1),
            cspec, cspec,
        ],
        out_specs=pl.BlockSpec((ATTN_BLOCK, ATTN_WIDTH), lambda b, i: (b * nblk + i, 0)),
        out_shape=jax.ShapeDtypeStruct((lay.ts, ATTN_WIDTH), BF16),
        compiler_params=_cparams(("arbitrary", "arbitrary")),
    )(sink, y16, y16, y16, y16, y16, y16, y16,
      cache_k_l.reshape(lay.bs, past, KV_WIDTH), cache_v_l.reshape(lay.bs, past, KV_WIDTH))
    return jnp.concatenate([ctx, lat], axis=0)


def _s5_matrices(lam_re, lam_im, log_dt, b_re, b_im, c_re, c_im, d_skip, n_steps):
    q, p, g, n = SSM_CHUNK, SSM_CH, SSM_GROUPS, SSM_STATE
    lam = lax.complex(lam_re.astype(F32), lam_im.astype(F32))
    dt = jnp.exp(log_dt.astype(F32))[..., None]
    lam_dt = lam * dt
    lam_bar = jnp.exp(lam_dt)
    b_bar = ((lam_bar - 1.0) / lam)[..., None] * lax.complex(b_re.astype(F32), b_im.astype(F32))
    c_mat = lax.complex(c_re.astype(F32), c_im.astype(F32))
    steps = jnp.arange(q + 1, dtype=F32)
    pw = jnp.exp(lam_dt[:, None] * steps[None, :, None, None])
    kern = jnp.real(jnp.einsum('dgpn,dkgn,dgnr->dkgpr', c_mat, pw[:, :q], b_bar))
    tau_in = jnp.arange(q)[:, None]
    tau_out = jnp.arange(q)[None, :]
    lag_f = tau_out - tau_in
    lag_b = tau_in - tau_out
    kf = jnp.where((lag_f >= 0)[:, :, None, None, None], kern[0][jnp.clip(lag_f, 0, q - 1)], 0.0)
    kb = jnp.where((lag_b >= 0)[:, :, None, None, None], kern[1][jnp.clip(lag_b, 0, q - 1)], 0.0)
    m = (kf + kb).transpose(2, 0, 4, 1, 3)
    eye_q = jnp.eye(q, dtype=F32)[None, :, None, :, None]
    eye_p = jnp.eye(p, dtype=F32)[None, None, :, None, :]
    m = m + eye_q * eye_p * d_skip.astype(F32).reshape(g, 1, p, 1, 1)
    m = m.reshape(g, q * p, q * p)
    ws_f = pw[0, :q][::-1][:, :, :, None] * b_bar[0][None]
    ws_b = pw[1, :q][:, :, :, None] * b_bar[1][None]

    def cols(w):
        return w.transpose(1, 0, 3, 2).reshape(g, q * p, n)

    w1 = jnp.concatenate([m, jnp.real(cols(ws_f)), jnp.real(cols(ws_b)),
                          jnp.imag(cols(ws_f)), jnp.imag(cols(ws_b))], axis=2)
    cy_f = c_mat[0][None] * pw[0, 1:][:, :, None, :]
    cy_b = c_mat[1][None] * pw[1, 1:][::-1][:, :, None, :]

    def rows(w):
        return w.transpose(1, 3, 0, 2).reshape(g, n, q * p)

    wy = jnp.concatenate([jnp.real(rows(cy_f)), jnp.real(rows(cy_b)),
                          -jnp.imag(rows(cy_f)), -jnp.imag(rows(cy_b))], axis=1)
    hops = (q * 2.0 ** jnp.arange(8, dtype=F32))[None, :, None, None]
    a = jnp.exp(lam_dt[:, None] * hops)
    a = jnp.concatenate([a[0], a[1]], axis=-1).transpose(1, 0, 2)
    assert n_steps <= 8
    gb, nbun = S5_BUNDLE, g // S5_BUNDLE
    eye = jnp.eye(gb, dtype=F32)
    m6 = w1[:, :, :q * p].reshape(nbun, gb, q, p, q, p)
    wm = jnp.einsum('ogabcd,gh->oagbchd', m6, eye).reshape(nbun, q * gb * p, q * gb * p)
    d6 = w1[:, :, q * p:].reshape(nbun, gb, q, p, 2, 2 * n)
    wd = jnp.einsum('ogabrn,gh->oagbrhn', d6, eye).reshape(nbun, q * gb * p, 2 * gb * 2 * n)
    y6 = wy.reshape(nbun, gb, 2, 2 * n, q, p)
    wyb = jnp.einsum('ogrncd,gh->orgnchd', y6, eye).reshape(nbun, 2 * gb * 2 * n, q * gb * p)
    ab = a.reshape(nbun, gb, 8, 2 * n).transpose(0, 2, 1, 3).reshape(nbun, 8, gb * 2 * n)
    return wm.astype(BF16), wd.astype(BF16), wyb.astype(BF16), jnp.real(ab), jnp.imag(ab)


def _s5_kernel(u_ref, wm_ref, wd_ref, wy_ref, are_ref, aim_ref, h0re_ref, h0im_ref, y_ref, fre_ref, fim_ref, *,
               nb, nc):
    r_tot = nb * nc
    n2 = S5_BUNDLE * 2 * SSM_STATE
    u = u_ref[0]
    y_intra = jnp.dot(u, wm_ref[0], preferred_element_type=F32)
    delta = jnp.dot(u, wd_ref[0], preferred_element_type=F32)
    d_re = delta[:, :n2]
    d_im = delta[:, n2:]
    h0re_ref, h0im_ref, fre_ref, fim_ref = (r.at[0] for r in (h0re_ref, h0im_ref, fre_ref, fim_ref))
    row = lax.broadcasted_iota(I32, (r_tot, n2), 0)
    chunk = row % nc
    seq = row // nc
    fwd = (lax.broadcasted_iota(I32, (r_tot, n2), 1) % (2 * SSM_STATE)) < SSM_STATE

    def previous(x, dist):
        valid = (fwd & (chunk >= dist)) | (~fwd & (chunk < nc - dist))
        moved = jnp.where(fwd, pltpu.roll(x, dist, 0), pltpu.roll(x, r_tot - dist, 0))
        return jnp.where(valid, moved, 0.0)

    h0_re = jnp.zeros((r_tot, n2), F32)
    h0_im = jnp.zeros((r_tot, n2), F32)
    for b in range(nb):
        h0_re = jnp.where(seq == b, h0re_ref[0, b:b + 1, :], h0_re)
        h0_im = jnp.where(seq == b, h0im_ref[0, b:b + 1, :], h0_im)
    first = (fwd & (chunk == 0)) | (~fwd & (chunk == nc - 1))
    e_re = jnp.where(first, h0_re, previous(d_re, 1))
    e_im = jnp.where(first, h0_im, previous(d_im, 1))
    k = 0
    while (1 << k) < nc:
        a_re = are_ref[0, k:k + 1, :]
        a_im = aim_ref[0, k:k + 1, :]
        p_re = previous(e_re, 1 << k)
        p_im = previous(e_im, 1 << k)
        e_re, e_im = e_re + a_re * p_re - a_im * p_im, e_im + a_re * p_im + a_im * p_re
        k += 1
    e_cat = jnp.concatenate([e_re, e_im], axis=1).astype(BF16)
    y_ref[0] = y_intra + jnp.dot(e_cat, wy_ref[0], preferred_element_type=F32)
    a_re = are_ref[0, 0:1, :]
    a_im = aim_ref[0, 0:1, :]
    f_re = a_re * e_re - a_im * e_im + d_re
    f_im = a_re * e_im + a_im * e_re + d_im
    fwd_row = fwd[0:1, :]
    for b in range(nb):
        lo, hi = b * nc, b * nc + nc - 1
        fre_ref[0, b:b + 1, :] = jnp.where(fwd_row, f_re[hi:hi + 1, :], f_re[lo:lo + 1, :])
        fim_ref[0, b:b + 1, :] = jnp.where(fwd_row, f_im[hi:hi + 1, :], f_im[lo:lo + 1, :])


def _s5_scan(u, mats, h0_re, h0_im):
    nb, length, _ = u.shape
    g, q, p, n = SSM_GROUPS, SSM_CHUNK, SSM_CH, SSM_STATE
    gb, nbun = S5_BUNDLE, SSM_GROUPS // S5_BUNDLE
    nc = length // q
    assert nc & (nc - 1) == 0 and nc % 8 == 0
    seqs = max(1, min(nb, S5_BLOCK_ROWS // nc))
    assert nb % seqs == 0
    n_rb, rows = nb // seqs, seqs * nc
    kw, sw = q * gb * p, gb * 2 * n
    wm, wd, wy, a_re, a_im = mats
    ub = u.reshape(nb * nc, q, nbun, gb * p).transpose(2, 0, 1, 3).reshape(nbun, nb * nc, kw)

    def lanes(h):
        h = h.astype(F32).reshape(n_rb, seqs, 2, nbun, gb, n).transpose(3, 0, 1, 4, 2, 5)
        return h.reshape(nbun, n_rb, seqs, sw)

    once = pl.Buffered(1)
    weight = lambda r, c: pl.BlockSpec((1, r, c), lambda o, i: (o, 0, 0), pipeline_mode=once)
    coeff = pl.BlockSpec((1, 8, sw), lambda o, i: (o, 0, 0))
    state = pl.BlockSpec((1, 1, seqs, sw), lambda o, i: (o, i, 0, 0))
    y, f_re, f_im = pl.pallas_call(
        functools.partial(_s5_kernel, nb=seqs, nc=nc),
        grid=(nbun, n_rb),
        in_specs=[pl.BlockSpec((1, rows, kw), lambda o, i: (o, i, 0)),
                  weight(kw, kw), weight(kw, 2 * sw), weight(2 * sw, kw), coeff, coeff, state, state],
        out_specs=[pl.BlockSpec((1, rows, kw), lambda o, i: (o, i, 0)), state, state],
        out_shape=[jax.ShapeDtypeStruct((nbun, nb * nc, kw), F32),
                   jax.ShapeDtypeStruct((nbun, n_rb, seqs, sw), F32),
                   jax.ShapeDtypeStruct((nbun, n_rb, seqs, sw), F32)],
        compiler_params=_cparams(("arbitrary", "arbitrary")),
    )(ub, wm, wd, wy, a_re, a_im, lanes(h0_re), lanes(h0_im))
    y = y.reshape(nbun, nb * nc, q, gb * p).transpose(1, 2, 0, 3).reshape(nb, length, g * p)

    def unlanes(f):
        f = f.reshape(nbun, n_rb, seqs, gb, 2, n).transpose(1, 2, 4, 0, 3, 5)
        return f.reshape(nb, 2, g, n)

    return y, unlanes(f_re), unlanes(f_im)


def _merge_kernel(h_ref, ys_ref, at_ref, gb_ref, gc_ref, uc_ref, gcp_ref, ucp_ref, gcn_ref, ucn_ref, cw_ref,
                  wglu_ref, wg0_ref, wg1_ref, wg2_ref, bg0_ref, bg1_ref, bg2_ref, ws_ref, wa_ref, wc_ref,
                  o_ref, ssm_scr, conv_scr, *, lay, tm, halo):
    i = pl.program_id(0)
    n = pl.program_id(1)

    @pl.when(n == 0)
    def _():
        y = ys_ref[...]
        ge = 0.5 * y * (1.0 + jnp.tanh(math.sqrt(2.0 / math.pi) * (y + 0.044715 * (y * y * y))))
        glu = jnp.dot(ge.astype(BF16), wglu_ref[...], preferred_element_type=F32)
        ssm_scr[...] = (ge * _sigmoid(glu)).astype(BF16)

        z = gc_ref[...].astype(F32) * uc_ref[...].astype(F32)
        z_before = gcp_ref[halo - 1:halo, :].astype(F32) * ucp_ref[halo - 1:halo, :].astype(F32)
        z_after = gcn_ref[0:1, :].astype(F32) * ucn_ref[0:1, :].astype(F32)
        local = lax.broadcasted_iota(I32, z.shape, 0)
        pos, seq_len = lay.seq_pos(local + i * tm, i, tm)
        z_prev = jnp.where(local == 0, z_before, pltpu.roll(z, 1, 0))
        z_prev = jnp.where(pos == 0, 0.0, z_prev)
        z_next = jnp.where(local == tm - 1, z_after, pltpu.roll(z, tm - 1, 0))
        z_next = jnp.where(pos == seq_len - 1, 0.0, z_next)
        conv = cw_ref[0:1, :] * z_prev + cw_ref[1:2, :] * z + cw_ref[2:3, :] * z_next
        conv_scr[...] = (gb_ref[...].astype(F32) * conv).astype(BF16)

    h = h_ref[...]
    acc = None
    for act, wg_ref, bg_ref, wb_ref in ((ssm_scr[...], wg0_ref, bg0_ref, ws_ref),
                                        (at_ref[...], wg1_ref, bg1_ref, wa_ref),
                                        (conv_scr[...], wg2_ref, bg2_ref, wc_ref)):
        gate = _sigmoid(jnp.dot(h, wg_ref[...], preferred_element_type=F32) + bg_ref[...])
        term = gate * jnp.dot(act, wb_ref[...], preferred_element_type=F32)
        acc = term if acc is None else acc + term
    o_ref[...] = acc.astype(BF16)


def _merge(lay, h16, y_ssm, attn, y16, conv_w, wglu16, wgates16, b_gates, wbs16, wba16, wbc16):
    d = lay.d
    tm = lay.row_tile(512)
    tn = min(COL_TILE, d)
    nd = d // tn
    halo = 16
    hb = tm // halo
    last_h = lay.t // halo - 1
    c0 = (ATTN_WIDTH + 2 * KV_WIDTH + SSM_WIDTH) // CONV_WIDTH
    row = lambda cb: pl.BlockSpec((tm, CONV_WIDTH), lambda i, n: (i, cb))
    before = lambda cb: pl.BlockSpec((halo, CONV_WIDTH), lambda i, n: (jnp.maximum(i * hb - 1, 0), cb))
    after = lambda cb: pl.BlockSpec((halo, CONV_WIDTH), lambda i, n: (jnp.minimum((i + 1) * hb, last_h), cb))
    gate_w = lambda br: pl.BlockSpec((d, tn), lambda i, n: (0, br * nd + n))
    gate_b = lambda br: pl.BlockSpec((1, tn), lambda i, n: (0, br * nd + n))
    return pl.pallas_call(
        functools.partial(_merge_kernel, lay=lay, tm=tm, halo=halo),
        grid=(lay.t // tm, nd),
        in_specs=[
            pl.BlockSpec((tm, d), lambda i, n: (i, 0)),
            pl.BlockSpec((tm, SSM_WIDTH), lambda i, n: (i, 0)),
            pl.BlockSpec((tm, ATTN_WIDTH), lambda i, n: (i, 0)),
            row(c0), row(c0 + 1), row(c0 + 2),
            before(c0 + 1), before(c0 + 2), after(c0 + 1), after(c0 + 2),
            pl.BlockSpec((3, CONV_WIDTH), lambda i, n: (0, 0)),
            pl.BlockSpec((SSM_WIDTH, SSM_WIDTH), lambda i, n: (0, 0)),
            gate_w(0), gate_w(1), gate_w(2), gate_b(0), gate_b(1), gate_b(2),
            pl.BlockSpec((SSM_WIDTH, tn), lambda i, n: (0, n)),
            pl.BlockSpec((ATTN_WIDTH, tn), lambda i, n: (0, n)),
            pl.BlockSpec((CONV_WIDTH, tn), lambda i, n: (0, n)),
        ],
        out_specs=pl.BlockSpec((tm, tn), lambda i, n: (i, n)),
        out_shape=jax.ShapeDtypeStruct((lay.t, d), BF16),
        scratch_shapes=[pltpu.VMEM((tm, SSM_WIDTH), BF16), pltpu.VMEM((tm, CONV_WIDTH), BF16)],
        compiler_params=_cparams(("arbitrary", "arbitrary")),
    )(h16, y_ssm, attn, y16, y16, y16, y16, y16, y16, y16, conv_w, wglu16,
      wgates16, wgates16, wgates16, b_gates, b_gates, b_gates, wbs16, wba16, wbc16)


def _outproj_kernel(m_ref, w_ref, x_ref, gate_ref, g2_ref, shift_ref, scale_ref, wrh_ref, wrl_ref,
                    xo_ref, hp_ref, lg_ref):
    acc = jnp.dot(m_ref[...], w_ref[...], preferred_element_type=F32)
    xn = x_ref[...] + gate_ref[0] * acc
    xo_ref[...] = xn
    ms = jnp.mean(xn * xn, axis=-1, keepdims=True)
    h2 = xn * lax.rsqrt(ms + EPS) * g2_ref[...]
    h2 = h2 * (1.0 + scale_ref[0]) + shift_ref[0]
    hp_ref[...] = _pack_bf16_pairs(h2)
    h_hi = h2.astype(BF16)
    h_lo = (h2 - h_hi.astype(F32)).astype(BF16)
    logits = (jnp.dot(h_hi, wrh_ref[...], preferred_element_type=F32)
              + jnp.dot(h_hi, wrl_ref[...], preferred_element_type=F32)
              + jnp.dot(h_lo, wrh_ref[...], preferred_element_type=F32))
    lg_ref[...] = logits.T[:N_EXPERTS, :]


def _outproj(lay, merged, wout16, x, mod_l, g2, w_router):
    d = lay.d
    tm = lay.row_tile(512)
    lanes = 128
    wr = jnp.zeros((d, lanes), F32).at[:, :N_EXPERTS].set(w_router.astype(F32))
    wr_hi = wr.astype(BF16)
    wr_lo = (wr - wr_hi.astype(F32)).astype(BF16)
    return pl.pallas_call(
        _outproj_kernel,
        grid=(lay.t // tm,),
        in_specs=[
            pl.BlockSpec((tm, d), lambda i: (i, 0)),
            pl.BlockSpec((d, d), lambda i: (0, 0)),
            pl.BlockSpec((tm, d), lambda i: (i, 0)),
            _mod_spec(lay, tm, 2, d),
            pl.BlockSpec((1, d), lambda i: (0, 0)),
            _mod_spec(lay, tm, 3, d),
            _mod_spec(lay, tm, 4, d),
            pl.BlockSpec((d, lanes), lambda i: (0, 0)),
            pl.BlockSpec((d, lanes), lambda i: (0, 0)),
        ],
        out_specs=[
            pl.BlockSpec((tm, d), lambda i: (i, 0)),
            pl.BlockSpec((tm, d // 2), lambda i: (i, 0)),
            pl.BlockSpec((N_EXPERTS, tm), lambda i: (0, i)),
        ],
        out_shape=[
            jax.ShapeDtypeStruct((lay.t, d), F32),
            jax.ShapeDtypeStruct((lay.t, d // 2), U32),
            jax.ShapeDtypeStruct((N_EXPERTS, lay.t), F32),
        ],
        compiler_params=_cparams(("arbitrary",)),
    )(merged, wout16, x, mod_l, g2.reshape(1, d), mod_l, mod_l, wr_hi, wr_lo)


def _route_kernel(lg_ref, br_ref, idx_ref, w_ref, pos_ref, cnt_ref, carry):
    step = pl.program_id(0)
    tt = lg_ref.shape[1]
    per_group = N_EXPERTS // N_EXPERT_GROUPS

    @pl.when(step == 0)
    def _():
        carry[...] = jnp.zeros_like(carry)

    scores = _sigmoid(lg_ref[...])
    biased = scores + br_ref[...]
    sub = lax.broadcasted_iota(I32, (per_group, tt), 0).astype(F32)
    blocks, group_score = [], []
    for g in range(N_EXPERT_GROUPS):
        blk = biased[g * per_group:(g + 1) * per_group, :]
        m1 = jnp.max(blk, axis=0, keepdims=True)
        i1 = jnp.min(jnp.where(blk == m1, sub, float(per_group)), axis=0, keepdims=True)
        m2 = jnp.max(jnp.where(sub == i1, -jnp.inf, blk), axis=0, keepdims=True)
        blocks.append(blk)
        group_score.append(m1 + m2)
    masked = []
    for g in range(N_EXPERT_GROUPS):
        beaten_by = jnp.zeros((1, tt), F32)
        for o in range(N_EXPERT_GROUPS):
            if o == g:
                continue
            wins = (group_score[o] > group_score[g]) | ((group_score[o] == group_score[g]) & (o < g))
            beaten_by = beaten_by + wins.astype(F32)
        masked.append(jnp.where(beaten_by < TOPK_GROUPS, blocks[g], -jnp.inf))
    masked = jnp.concatenate(masked, axis=0)
    eid = lax.broadcasted_iota(I32, (N_EXPERTS, tt), 0).astype(F32)
    chosen, weights = [], []
    onehot = jnp.zeros((N_EXPERTS, tt), F32)
    for _ in range(TOP_K):
        m = jnp.max(masked, axis=0, keepdims=True)
        e = jnp.min(jnp.where(masked == m, eid, float(N_EXPERTS)), axis=0, keepdims=True)
        hit = eid == e
        chosen.append(e)
        weights.append(jnp.sum(jnp.where(hit, scores, 0.0), axis=0, keepdims=True))
        onehot = onehot + hit.astype(F32)
        masked = jnp.where(hit, -jnp.inf, masked)
    total = weights[0]
    for wk in weights[1:]:
        total = total + wk
    earlier = (lax.broadcasted_iota(I32, (tt, tt), 0) < lax.broadcasted_iota(I32, (tt, tt), 1)).astype(BF16)
    rank = carry[...][:, 0:1] + jnp.dot(onehot.astype(BF16), earlier, preferred_element_type=F32)
    for k in range(TOP_K):
        idx_ref[k:k + 1, :] = chosen[k].astype(I32)
        w_ref[k:k + 1, :] = weights[k] / total * ROUTED_SCALE
        pos_ref[k:k + 1, :] = jnp.sum(jnp.where(eid == chosen[k], rank, 0.0), axis=0, keepdims=True).astype(I32)
    for k in range(TOP_K, 8):
        idx_ref[k:k + 1, :] = jnp.zeros((1, tt), I32)
        w_ref[k:k + 1, :] = jnp.zeros((1, tt), F32)
        pos_ref[k:k + 1, :] = jnp.zeros((1, tt), I32)
    carry[...] = carry[...] + jnp.sum(onehot, axis=1, keepdims=True)
    cnt_ref[...] = carry[...]


def _route(logits_t, b_router):
    t = logits_t.shape[1]
    tt = math.gcd(t, 512)
    tok = pl.BlockSpec((8, tt), lambda i: (0, i))
    return pl.pallas_call(
        _route_kernel,
        grid=(t // tt,),
        in_specs=[pl.BlockSpec((N_EXPERTS, tt), lambda i: (0, i)),
                  pl.BlockSpec((N_EXPERTS, 1), lambda i: (0, 0))],
        out_specs=[tok, tok, tok, pl.BlockSpec((N_EXPERTS, 128), lambda i: (0, 0))],
        out_shape=[jax.ShapeDtypeStruct((8, t), I32), jax.ShapeDtypeStruct((8, t), F32),
                   jax.ShapeDtypeStruct((8, t), I32), jax.ShapeDtypeStruct((N_EXPERTS, 128), F32)],
        scratch_shapes=[pltpu.VMEM((N_EXPERTS, 128), F32)],
        compiler_params=_cparams(("arbitrary",)),
    )(logits_t, b_router.astype(F32).reshape(N_EXPERTS, 1))


def _dest_kernel(idx_ref, pos_ref, start_ref, o_ref, *, tt):
    eid = lax.broadcasted_iota(I32, (N_EXPERTS, tt), 0)
    start = start_ref[...]
    for k in range(TOP_K):
        first = jnp.sum(jnp.where(eid == idx_ref[k:k + 1, :], start, 0.0), axis=0, keepdims=True)
        o_ref[k:k + 1, :] = first.astype(I32) + pos_ref[k:k + 1, :]
    for k in range(TOP_K, 8):
        o_ref[k:k + 1, :] = jnp.zeros((1, tt), I32)


def _dest_rows(idx_t, pos_t, pad_start):
    t = idx_t.shape[1]
    tt = math.gcd(t, 512)
    tok = pl.BlockSpec((8, tt), lambda i: (0, i))
    return pl.pallas_call(
        functools.partial(_dest_kernel, tt=tt),
        grid=(t // tt,),
        in_specs=[tok, tok, pl.BlockSpec((N_EXPERTS, 1), lambda i: (0, 0))],
        out_specs=tok,
        out_shape=jax.ShapeDtypeStruct((8, t), I32),
        compiler_params=_cparams(("arbitrary",)),
    )(idx_t, pos_t, pad_start.astype(F32).reshape(N_EXPERTS, 1))


def _sc_layout(n_pairs, period):
    info = plsc.get_sparse_core_info()
    workers = info.num_cores * info.num_subcores
    per_worker = n_pairs // workers
    chunk = math.gcd(math.gcd(per_worker, period), SC_STREAM_ROWS)
    assert per_worker * workers == n_pairs and chunk % 8 == 0
    return info.num_cores, per_worker, chunk


def _sc_scatter_rows(rows, dest, n_out):
    t, width = rows.shape
    n_cores, per_worker, chunk = _sc_layout(dest.shape[0], t)
    mesh = plsc.VectorSubcoreMesh(core_axis_name="core", subcore_axis_name="subcore")

    @functools.partial(
        pl.kernel, mesh=mesh, out_type=jax.ShapeDtypeStruct((n_out, width), rows.dtype),
        scratch_types=[pltpu.VMEM((chunk,), I32), pltpu.VMEM((chunk, width), rows.dtype)])
    def scatter(rows_hbm, dest_hbm, out_hbm, dest_v, rows_v):
        base = (lax.axis_index("subcore") * n_cores + lax.axis_index("core")) * per_worker

        @pl.loop(0, per_worker // chunk)
        def _(j):
            pair = base + j * chunk
            pltpu.sync_copy(dest_hbm.at[pl.ds(pair, chunk)], dest_v)
            pltpu.sync_copy(rows_hbm.at[pl.ds(lax.rem(pair, t), chunk)], rows_v)
            pltpu.sync_copy(rows_v, out_hbm.at[dest_v])

    return scatter(rows, dest)


def _sc_gather_rows(table, dest):
    width = table.shape[1]
    n_pairs = dest.shape[0]
    n_cores, per_worker, chunk = _sc_layout(n_pairs, n_pairs)
    mesh = plsc.VectorSubcoreMesh(core_axis_name="core", subcore_axis_name="subcore")

    @functools.partial(
        pl.kernel, mesh=mesh, out_type=jax.ShapeDtypeStruct((n_pairs, width), table.dtype),
        scratch_types=[pltpu.VMEM((chunk,), I32), pltpu.VMEM((chunk, width), table.dtype)])
    def gather(table_hbm, dest_hbm, out_hbm, dest_v, rows_v):
        base = (lax.axis_index("subcore") * n_cores + lax.axis_index("core")) * per_worker

        @pl.loop(0, per_worker // chunk)
        def _(j):
            pair = base + j * chunk
            pltpu.sync_copy(dest_hbm.at[pl.ds(pair, chunk)], dest_v)
            pltpu.sync_copy(table_hbm.at[dest_v], rows_v)
            pltpu.sync_copy(rows_v, out_hbm.at[pl.ds(pair, chunk)])

    return gather(table, dest)


def _expert_kernel(be_ref, xs_ref, wg_ref, wu_ref, wd_ref, ys_ref, wg16, wu16, wd16):
    i = pl.program_id(0)
    changed = jnp.logical_or(i == 0, be_ref[i] != be_ref[jnp.maximum(i - 1, 0)])

    @pl.when(changed)
    def _():
        rows = 256
        d = wg16.shape[0]

        def cast_in(r, carry):
            sl = pl.ds(pl.multiple_of(r * rows, rows), rows)
            wg16[sl, :] = wg_ref[0, sl, :].astype(BF16)
            wu16[sl, :] = wu_ref[0, sl, :].astype(BF16)
            return carry

        lax.fori_loop(0, d // rows, cast_in, 0)

        def cast_down(r, carry):
            sl = pl.ds(pl.multiple_of(r * 128, 128), 128)
            wd16[sl, :] = wd_ref[0, sl, :].astype(BF16)
            return carry

        lax.fori_loop(0, D_EXPERT // 128, cast_down, 0)

    n_blocks = pl.num_programs(0)
    used = i < be_ref[n_blocks]

    @pl.when(used)
    def _():
        held = lax.broadcasted_iota(I32, (xs_ref.shape[0], 1), 0) < be_ref[n_blocks + 1 + i]
        hi, lo = _unpack_bf16_pairs(jnp.where(held, xs_ref[...], jnp.uint32(0)))
        x = jnp.concatenate([hi, lo], axis=1).astype(BF16)
        gate = jnp.dot(x, wg16[...], preferred_element_type=F32)
        up = jnp.dot(x, wu16[...], preferred_element_type=F32)
        act = (gate * _sigmoid(gate) * up).astype(BF16)
        ys_ref[...] = _pack_bf16_pairs(jnp.dot(act, wd16[...], preferred_element_type=F32))

    @pl.when(jnp.logical_not(used))
    def _():
        ys_ref[...] = jnp.zeros_like(ys_ref)


def _experts(xs, block_e, layer, w_gate, w_up, w_down):
    n_rows, half = xs.shape
    d = 2 * half
    br = EXPERT_ROWS
    grid_spec = pltpu.PrefetchScalarGridSpec(
        num_scalar_prefetch=1,
        grid=(n_rows // br,),
        in_specs=[
            pl.BlockSpec((br, half), lambda i, be: (i, 0)),
            pl.BlockSpec((None, 1, d, D_EXPERT), lambda i, be: (layer, be[i], 0, 0)),
            pl.BlockSpec((None, 1, d, D_EXPERT), lambda i, be: (layer, be[i], 0, 0)),
            pl.BlockSpec((None, 1, D_EXPERT, d), lambda i, be: (layer, be[i], 0, 0)),
        ],
        out_specs=pl.BlockSpec((br, half), lambda i, be: (i, 0)),
        scratch_shapes=[pltpu.VMEM((d, D_EXPERT), BF16), pltpu.VMEM((d, D_EXPERT), BF16),
                        pltpu.VMEM((D_EXPERT, d), BF16)],
    )
    return pl.pallas_call(
        _expert_kernel,
        grid_spec=grid_spec,
        out_shape=jax.ShapeDtypeStruct((n_rows, half), U32),
        compiler_params=_cparams(("arbitrary",)),
    )(block_e, xs, w_gate, w_up, w_down)


def _shared_kernel(hp_ref, wsg_ref, wsu_ref, wsd_ref, o_ref):
    hi, lo = _unpack_bf16_pairs(hp_ref[...])
    h2 = jnp.concatenate([hi, lo], axis=1).astype(BF16)
    sg = jnp.dot(h2, wsg_ref[...], preferred_element_type=F32)
    su = jnp.dot(h2, wsu_ref[...], preferred_element_type=F32)
    shared = jnp.dot((sg * _sigmoid(sg) * su).astype(BF16), wsd_ref[...], preferred_element_type=F32)
    o_ref[...] = shared.astype(o_ref.dtype)


def _shared_expert(lay, h2p, wsg16, wsu16, wsd16):
    d = lay.d
    tm = lay.row_tile(512)
    return pl.pallas_call(
        _shared_kernel,
        grid=(lay.t // tm,),
        in_specs=[pl.BlockSpec((tm, d // 2), lambda i: (i, 0)),
                  pl.BlockSpec((d, D_EXPERT), lambda i: (0, 0)),
                  pl.BlockSpec((d, D_EXPERT), lambda i: (0, 0)),
                  pl.BlockSpec((D_EXPERT, d), lambda i: (0, 0))],
        out_specs=pl.BlockSpec((tm, d), lambda i: (i, 0)),
        out_shape=jax.ShapeDtypeStruct((lay.t, d), BF16),
        compiler_params=_cparams(("arbitrary",)),
    )(h2p, wsg16, wsu16, wsd16)


def _combine_kernel(yk_ref, wsel_ref, sh_ref, x_ref, gate_ref, fg_ref, *outs, tt, final, n_ctx_tiles):
    shared = sh_ref[...].astype(F32)
    half = yk_ref.shape[2]
    r_hi = jnp.zeros((tt, half), F32)
    r_lo = jnp.zeros((tt, half), F32)
    for k in range(TOP_K):
        y_hi, y_lo = _unpack_bf16_pairs(yk_ref[k])
        wk = wsel_ref[:, k:k + 1]
        r_hi = r_hi + wk * y_hi
        r_lo = r_lo + wk * y_lo
    routed = jnp.concatenate([r_hi, r_lo], axis=1)
    out = x_ref[...] + gate_ref[0] * (routed + shared)
    if not final:
        outs[0][...] = out
        return
    ms = jnp.mean(out * out, axis=-1, keepdims=True)
    out = out * lax.rsqrt(ms + EPS) * fg_ref[...]
    ctx_ref, lat_ref = outs
    is_ctx = pl.program_id(0) < n_ctx_tiles

    @pl.when(is_ctx)
    def _():
        ctx_ref[...] = out

    @pl.when(jnp.logical_not(is_ctx))
    def _():
        lat_ref[...] = out


def _combine(lay, y_by_k, wsel, shared, x, mod_l, final_g, tt, final):
    d = lay.d
    half = d // 2
    nct = lay.tc // tt
    if final:
        out_specs = [pl.BlockSpec((tt, d), lambda i: (jnp.minimum(i, nct - 1), 0)),
                     pl.BlockSpec((tt, d), lambda i: (jnp.maximum(i - nct, 0), 0))]
        out_shape = [jax.ShapeDtypeStruct((lay.tc, d), F32), jax.ShapeDtypeStruct((lay.ts, d), F32)]
    else:
        out_specs = pl.BlockSpec((tt, d), lambda i: (i, 0))
        out_shape = jax.ShapeDtypeStruct((lay.t, d), F32)
    return pl.pallas_call(
        functools.partial(_combine_kernel, tt=tt, final=final, n_ctx_tiles=nct),
        grid=(lay.t // tt,),
        in_specs=[
            pl.BlockSpec((TOP_K, tt, half), lambda i: (0, i, 0)),
            pl.BlockSpec((tt, 8), lambda i: (i, 0)),
            pl.BlockSpec((tt, d), lambda i: (i, 0)),
            pl.BlockSpec((tt, d), lambda i: (i, 0)),
            _mod_spec(lay, tt, 5, d),
            pl.BlockSpec((1, d), lambda i: (0, 0)),
        ],
        out_specs=out_specs,
        out_shape=out_shape,
        compiler_params=_cparams(("arbitrary",)),
    )(y_by_k, wsel, shared, x, mod_l, final_g.reshape(1, d))


def _moe(lay, layer, x, h2p, logits_t, mod_l, b_router, w_e_gate, w_e_up, w_e_down, wsg16, wsu16, wsd16, final_g,
         final):
    t = lay.t
    idx_t, w_t, pos_t, counts = _route(logits_t, b_router)
    br = EXPERT_ROWS
    counts = counts[:, 0].astype(I32)
    padded = (counts + br - 1) // br * br
    pad_end = jnp.cumsum(padded)
    pad_start = pad_end - padded
    n_blocks = -(-(t * TOP_K + N_EXPERTS * (br - 1)) // br)
    n_rows = n_blocks * br
    first_row = jnp.arange(n_blocks, dtype=I32) * br
    block_e = jnp.minimum(jnp.sum((pad_end[None, :] <= first_row[:, None]).astype(I32), axis=1), N_EXPERTS - 1)
    own = block_e[:, None] == jnp.arange(N_EXPERTS, dtype=I32)[None, :]
    rows_end = jnp.sum(jnp.where(own, (pad_start + counts)[None, :], 0), axis=1)
    held = jnp.clip(rows_end - first_row, 0, br)
    block_meta = jnp.concatenate([block_e, pad_end[-1:] // br, held]).astype(I32)
    dest = _dest_rows(idx_t, pos_t, pad_start)[:TOP_K].reshape(TOP_K * t)
    xs = _sc_scatter_rows(h2p, dest, n_rows)
    shared = _shared_expert(lay, h2p, wsg16, wsu16, wsd16)
    ys = _experts(xs, block_meta, layer, w_e_gate, w_e_up, w_e_down)
    y_by_k = _sc_gather_rows(ys, dest).reshape(TOP_K, t, h2p.shape[1])
    return _combine(lay, y_by_k, w_t.T, shared, x, mod_l, final_g, lay.row_tile(256), final)


def kernel(x_prompt, x_sample, c, cache_k, cache_v, state_ssm_re, state_ssm_im, c_ctx, w_ada, b_ada, norm1_g, norm2_g, w_in, w_gates, b_gates, ssm_lam_re, ssm_lam_im, ssm_log_dt, ssm_b_re, ssm_b_im, ssm_c_re, ssm_c_im, ssm_d, ssm_w_glu, conv_w, attn_sink, w_br_ssm, w_br_attn, w_br_conv, w_out, w_router, b_router, w_e_gate, w_e_up, w_e_down, w_s_gate, w_s_up, w_s_down, final_g):
    bc, lc, d = x_prompt.shape
    bs, ls, _ = x_sample.shape
    depth = w_in.shape[0]
    lay = _Layout(bc, lc, bs, ls, d)
    assert 1 + bs <= MOD_ROWS

    x = jnp.concatenate([x_prompt.reshape(lay.tc, d), x_sample.reshape(lay.ts, d)], axis=0)
    cvec = jnp.zeros((MOD_ROWS, d), F32).at[0].set(c_ctx).at[1:1 + bs].set(c)
    mod = _adaln(cvec, w_ada, b_ada).reshape(depth, MOD_ROWS * 6, 1, d)
    rope_cos, rope_sin = _rope_tables(lay, lay.row_tile(1024))
    zeros_state = jnp.zeros((bc, 2, SSM_GROUPS, SSM_STATE), F32)

    ks, vs, s_re, s_im = [], [], [], []
    for l in range(depth):
        mod_l = mod[l]
        h16, y16, kv32 = _inproj(lay, x, mod_l, norm1_g[l], w_in[l].astype(BF16), rope_cos, rope_sin)
        ks.append(kv32[:lay.tc, :KV_WIDTH].reshape(bc, lc, N_KV_HEADS, HEAD_DIM))
        vs.append(kv32[:lay.tc, KV_WIDTH:].reshape(bc, lc, N_KV_HEADS, HEAD_DIM))

        attn = _attention(lay, y16, attn_sink[l].astype(F32), cache_k[:, l], cache_v[:, l])

        u0 = ATTN_WIDTH + 2 * KV_WIDTH
        u = y16[:, u0:u0 + SSM_WIDTH]
        mats = _s5_matrices(ssm_lam_re[l], ssm_lam_im[l], ssm_log_dt[l], ssm_b_re[l], ssm_b_im[l],
                            ssm_c_re[l], ssm_c_im[l], ssm_d[l], 8)
        y_c, f_re, f_im = _s5_scan(u[:lay.tc].reshape(bc, lc, SSM_WIDTH), mats, zeros_state, zeros_state)
        y_s, _, _ = _s5_scan(u[lay.tc:].reshape(bs, ls, SSM_WIDTH), mats, state_ssm_re[:, l], state_ssm_im[:, l])
        s_re.append(f_re)
        s_im.append(f_im)
        y_ssm = jnp.concatenate([y_c.reshape(lay.tc, SSM_WIDTH), y_s.reshape(lay.ts, SSM_WIDTH)], axis=0)

        merged = _merge(lay, h16, y_ssm, attn, y16, conv_w[l], ssm_w_glu[l].astype(BF16),
                        w_gates[l].astype(BF16), b_gates[l].reshape(1, -1), w_br_ssm[l].astype(BF16),
                        w_br_attn[l].astype(BF16), w_br_conv[l].astype(BF16))
        x, h2p, logits_t = _outproj(lay, merged, w_out[l].astype(BF16), x, mod_l, norm2_g[l],
                                    w_router[l])
        x = _moe(lay, l, x, h2p, logits_t, mod_l, b_router[l], w_e_gate, w_e_up, w_e_down,
                 w_s_gate[l].astype(BF16), w_s_up[l].astype(BF16), w_s_down[l].astype(BF16),
                 final_g, l == depth - 1)

    y_prompt = x[0].reshape(bc, lc, d)
    y_sample = x[1].reshape(bs, ls, d)
    return (y_prompt, y_sample, jnp.stack(ks, axis=1), jnp.stack(vs, axis=1),
            jnp.stack(s_re, axis=1), jnp.stack(s_im, axis=1))
```

```python
import functools
import math

import jax
import jax.numpy as jnp
from jax import lax
from jax.experimental import pallas as pl
from jax.experimental.pallas import tpu as pltpu
from jax.experimental.pallas import tpu_sc as plsc

HEAD_DIM = 128
N_HEADS = 8
N_KV_HEADS = 2
GROUP = N_HEADS // N_KV_HEADS
ATTN_WIDTH = N_HEADS * HEAD_DIM
KV_WIDTH = N_KV_HEADS * HEAD_DIM
WINDOW = 128
ATTN_BLOCK = 128
ATTN_SCALE = HEAD_DIM ** -0.5
ROPE_BASE = 10000.0
ROT_F = HEAD_DIM // 4
GRID_W = 64
SSM_WIDTH = 512
SSM_CH = 16
SSM_GROUPS = SSM_WIDTH // SSM_CH
SSM_STATE = 64
SSM_CHUNK = 16
S5_BUNDLE = 8
S5_BLOCK_ROWS = 256
CONV_WIDTH = 512
N_BRANCHES = 3
IN_WIDTH = ATTN_WIDTH + 2 * KV_WIDTH + SSM_WIDTH + 3 * CONV_WIDTH
N_EXPERTS = 64
TOP_K = 6
N_EXPERT_GROUPS = 8
TOPK_GROUPS = 4
D_EXPERT = 512
ROUTED_SCALE = 2.5
EPS = 1e-6
NEG_INF = -1e30

COL_TILE = 512
MOD_ROWS = 16
EXPERT_ROWS = 512
SC_STREAM_ROWS = 64
VMEM_LIMIT_V7X = 56 * 1024 * 1024

F32 = jnp.float32
BF16 = jnp.bfloat16
I32 = jnp.int32
U32 = jnp.uint32


def _cparams(sem, vmem=VMEM_LIMIT_V7X):
    return pltpu.CompilerParams(dimension_semantics=sem, vmem_limit_bytes=vmem)


def _sigmoid(x):
    return 1.0 / (1.0 + jnp.exp(-x))


def _pack_bf16_pairs(v):
    n = v.shape[1] // 2
    hi = lax.bitcast_convert_type(v[:, :n].astype(BF16).astype(F32), U32)
    lo = lax.bitcast_convert_type(v[:, n:].astype(BF16).astype(F32), U32)
    return hi | (lo >> 16)


def _unpack_bf16_pairs(p):
    hi = lax.bitcast_convert_type(p & jnp.uint32(0xFFFF0000), F32)
    lo = lax.bitcast_convert_type(p << 16, F32)
    return hi, lo


def _adaln_kernel(c_ref, w_ref, b_ref, o_ref):
    c = c_ref[...]
    s = (c * _sigmoid(c)).astype(BF16)
    o_ref[0] = jnp.dot(s, w_ref[0].astype(BF16), preferred_element_type=F32) + b_ref[0]


def _adaln(cvec, w_ada, b_ada):
    depth, d, n6 = w_ada.shape
    tn = math.gcd(1024, n6)
    return pl.pallas_call(
        _adaln_kernel,
        grid=(depth, n6 // tn),
        in_specs=[
            pl.BlockSpec((MOD_ROWS, d), lambda l, n: (0, 0)),
            pl.BlockSpec((1, d, tn), lambda l, n: (l, 0, n)),
            pl.BlockSpec((1, 1, tn), lambda l, n: (l, 0, n)),
        ],
        out_specs=pl.BlockSpec((1, MOD_ROWS, tn), lambda l, n: (l, 0, n)),
        out_shape=jax.ShapeDtypeStruct((depth, MOD_ROWS, n6), F32),
        compiler_params=_cparams(("arbitrary", "arbitrary")),
    )(cvec, w_ada, b_ada.reshape(depth, 1, n6))


class _Layout:
    def __init__(self, n_ctx_seq, len_ctx, n_lat_seq, len_lat, d_model):
        self.bc, self.lc, self.bs, self.ls, self.d = n_ctx_seq, len_ctx, n_lat_seq, len_lat, d_model
        self.tc = n_ctx_seq * len_ctx
        self.ts = n_lat_seq * len_lat
        self.t = self.tc + self.ts

    def row_tile(self, want):
        tm = math.gcd(math.gcd(self.tc, self.ls), want)
        assert tm % 16 == 0
        return tm

    def mod_index(self, i, tm):
        nct, tps = self.tc // tm, self.ls // tm
        return jnp.where(i < nct, 0, 1 + (i - nct) // tps)

    def seq_pos(self, rows, i, tm):
        is_lat = i >= self.tc // tm
        return jnp.where(is_lat, (rows - self.tc) % self.ls, rows % self.lc), jnp.where(is_lat, self.ls, self.lc)


def _mod_spec(lay, tm, slot, d):
    return pl.BlockSpec((1, 1, d), lambda i, *_: (lay.mod_index(i, tm) * 6 + slot, 0, 0))


def _rope(z, cos, sin_signed, first_half):
    swapped = jnp.where(first_half, pltpu.roll(z, HEAD_DIM - ROT_F, 1), pltpu.roll(z, ROT_F, 1))
    return z * cos + swapped * sin_signed


def _inproj_kernel(x_ref, shift_ref, scale_ref, g_ref, w_ref, cos_ref, sin_ref, h_ref, y_ref, kv_ref):
    n = pl.program_id(1)
    n_q = ATTN_WIDTH // COL_TILE

    @pl.when(n == 0)
    def _():
        x = x_ref[...]
        ms = jnp.mean(x * x, axis=-1, keepdims=True)
        y = x * lax.rsqrt(ms + EPS) * g_ref[...]
        h_ref[...] = (y * (1.0 + scale_ref[0]) + shift_ref[0]).astype(BF16)

    acc = jnp.dot(h_ref[...], w_ref[...], preferred_element_type=F32)

    def rotated(n_heads):
        cos, sin = cos_ref[...], sin_ref[...]
        first_half = (lax.broadcasted_iota(I32, cos.shape, 1) % (2 * ROT_F)) < ROT_F
        parts = [_rope(acc[:, s * HEAD_DIM:(s + 1) * HEAD_DIM], cos, sin, first_half) for s in range(n_heads)]
        parts.append(acc[:, n_heads * HEAD_DIM:])
        return jnp.concatenate(parts, axis=1) if n_heads * HEAD_DIM < COL_TILE else jnp.concatenate(parts[:-1], axis=1)

    @pl.when(n < n_q)
    def _():
        y_ref[...] = rotated(COL_TILE // HEAD_DIM).astype(BF16)

    @pl.when(n == n_q)
    def _():
        kv_ref[...] = acc
        y_ref[...] = rotated(N_KV_HEADS).astype(BF16)

    @pl.when(n > n_q)
    def _():
        y_ref[...] = acc.astype(BF16)


def _inproj(lay, x, mod_l, g1, w_in16, rope_cos, rope_sin):
    d = lay.d
    tm = lay.row_tile(1024)
    nct, tps = lay.tc // tm, lay.ls // tm

    def rope_idx(i, n):
        return (jnp.where(i < nct, 0, 1 + (i - nct) % tps), 0)

    return pl.pallas_call(
        _inproj_kernel,
        grid=(lay.t // tm, IN_WIDTH // COL_TILE),
        in_specs=[
            pl.BlockSpec((tm, d), lambda i, n: (i, 0)),
            _mod_spec(lay, tm, 0, d),
            _mod_spec(lay, tm, 1, d),
            pl.BlockSpec((1, d), lambda i, n: (0, 0)),
            pl.BlockSpec((d, COL_TILE), lambda i, n: (0, n)),
            pl.BlockSpec((tm, HEAD_DIM), rope_idx),
            pl.BlockSpec((tm, HEAD_DIM), rope_idx),
        ],
        out_specs=[
            pl.BlockSpec((tm, d), lambda i, n: (i, 0)),
            pl.BlockSpec((tm, COL_TILE), lambda i, n: (i, n)),
            pl.BlockSpec((tm, 2 * KV_WIDTH), lambda i, n: (i, 0)),
        ],
        out_shape=[
            jax.ShapeDtypeStruct((lay.t, d), BF16),
            jax.ShapeDtypeStruct((lay.t, IN_WIDTH), BF16),
            jax.ShapeDtypeStruct((lay.t, 2 * KV_WIDTH), F32),
        ],
        compiler_params=_cparams(("arbitrary", "arbitrary")),
    )(x, mod_l, mod_l, g1.reshape(1, d), w_in16, rope_cos, rope_sin)


def _rope_tables(lay, tm):
    t = jnp.arange(lay.ls)
    row = (t // GRID_W).astype(F32)
    col = (t % GRID_W).astype(F32)
    inv = ROPE_BASE ** (-jnp.arange(ROT_F, dtype=F32) / ROT_F)
    ar, ac = row[:, None] * inv, col[:, None] * inv
    cos = jnp.concatenate([jnp.cos(ar), jnp.cos(ar), jnp.cos(ac), jnp.cos(ac)], axis=1)
    sin = jnp.concatenate([-jnp.sin(ar), jnp.sin(ar), -jnp.sin(ac), jnp.sin(ac)], axis=1)
    cos = jnp.concatenate([jnp.ones((tm, HEAD_DIM), F32), cos], axis=0)
    sin = jnp.concatenate([jnp.zeros((tm, HEAD_DIM), F32), sin], axis=0)
    return cos, sin


def _attend(q, sink_ref, j, parts):
    nq = q.shape[0]
    q4 = jnp.concatenate([q[:, g * HEAD_DIM:(g + 1) * HEAD_DIM] for g in range(GROUP)], axis=0)
    sink = jnp.concatenate([jnp.full((nq, 1), sink_ref[j * GROUP + g], F32) for g in range(GROUP)], axis=0)
    scores = []
    m = sink
    for k, _, mask in parts:
        s = lax.dot_general(q4, k, (((1,), (1,)), ((), ())), preferred_element_type=F32) * ATTN_SCALE
        if mask is not None:
            s = jnp.where(mask, s, NEG_INF)
        scores.append(s)
        m = jnp.maximum(m, jnp.max(s, axis=-1, keepdims=True))
    den = jnp.exp(sink - m)
    out = jnp.zeros((GROUP * nq, HEAD_DIM), F32)
    for s, (_, v, _) in zip(scores, parts):
        p = jnp.exp(s - m)
        den = den + jnp.sum(p, axis=-1, keepdims=True)
        out = out + jnp.dot(p.astype(BF16), v, preferred_element_type=F32)
    out = out / den
    return jnp.concatenate([out[g * nq:(g + 1) * nq] for g in range(GROUP)], axis=1)


def _head(x, j, width=HEAD_DIM):
    return x[:, j * width:(j + 1) * width]


def _attn_ctx_kernel(sink_ref, q_ref, k_ref, v_ref, o_ref):
    q, k, v = q_ref[...], k_ref[...], v_ref[...]
    outs = [_attend(_head(q, j, GROUP * HEAD_DIM), sink_ref, j, [(_head(k, j), _head(v, j), None)])
            for j in range(N_KV_HEADS)]
    o_ref[...] = jnp.concatenate(outs, axis=1).astype(o_ref.dtype)


def _attn_lat_kernel(sink_ref, q_ref, kp_ref, kc_ref, kn_ref, vp_ref, vc_ref, vn_ref, ck_ref, cv_ref, o_ref, *, seq_len):
    i = pl.program_id(1)
    q = q_ref[...]
    kw = jnp.concatenate([kp_ref[...], kc_ref[...], kn_ref[...]], axis=0)
    vw = jnp.concatenate([vp_ref[...], vc_ref[...], vn_ref[...]], axis=0)
    ck = ck_ref[...].astype(BF16)
    cv = cv_ref[...].astype(BF16)
    shape = (GROUP * ATTN_BLOCK, 3 * ATTN_BLOCK)
    qoff = lax.broadcasted_iota(I32, shape, 0) % ATTN_BLOCK
    koff = lax.broadcasted_iota(I32, shape, 1) - ATTN_BLOCK
    kabs = koff + i * ATTN_BLOCK
    mask = (jnp.abs(qoff - koff) <= WINDOW) & (kabs >= 0) & (kabs < seq_len)
    outs = [_attend(_head(q, j, GROUP * HEAD_DIM), sink_ref, j,
                    [(_head(kw, j), _head(vw, j), mask), (_head(ck, j), _head(cv, j), None)])
            for j in range(N_KV_HEADS)]
    o_ref[...] = jnp.concatenate(outs, axis=1).astype(o_ref.dtype)


def _attention(lay, y16, sink, cache_k_l, cache_v_l):
    smem = pl.BlockSpec(memory_space=pltpu.SMEM)
    kcol, vcol = ATTN_WIDTH // KV_WIDTH, (ATTN_WIDTH + KV_WIDTH) // KV_WIDTH
    ctx = pl.pallas_call(
        _attn_ctx_kernel,
        grid=(lay.bc,),
        in_specs=[
            smem,
            pl.BlockSpec((lay.lc, ATTN_WIDTH), lambda b: (b, 0)),
            pl.BlockSpec((lay.lc, KV_WIDTH), lambda b: (b, kcol)),
            pl.BlockSpec((lay.lc, KV_WIDTH), lambda b: (b, vcol)),
        ],
        out_specs=pl.BlockSpec((lay.lc, ATTN_WIDTH), lambda b: (b, 0)),
        out_shape=jax.ShapeDtypeStruct((lay.tc, ATTN_WIDTH), BF16),
        compiler_params=_cparams(("arbitrary",)),
    )(sink, y16, y16, y16)

    nblk = lay.ls // ATTN_BLOCK
    base = lay.tc // ATTN_BLOCK
    last = lay.t // ATTN_BLOCK - 1
    past = cache_k_l.shape[1]

    def rb(b, i):
        return base + b * nblk + i

    def kspec(col, delta):
        return pl.BlockSpec((ATTN_BLOCK, KV_WIDTH), lambda b, i: (jnp.clip(rb(b, i) + delta, 0, last), col))

    cspec = pl.BlockSpec((None, past, KV_WIDTH), lambda b, i: (b, 0, 0))
    lat = pl.pallas_call(
        functools.partial(_attn_lat_kernel, seq_len=lay.ls),
        grid=(lay.bs, nblk),
        in_specs=[
            smem,
            pl.BlockSpec((ATTN_BLOCK, ATTN_WIDTH), lambda b, i: (rb(b, i), 0)),
            kspec(kcol, -1), kspec(kcol, 0), kspec(kcol, 1),
            kspec(vcol, -1), kspec(vcol, 0), kspec(vcol, 1),
            cspec, cspec,
        ],
        out_specs=pl.BlockSpec((ATTN_BLOCK, ATTN_WIDTH), lambda b, i: (b * nblk + i, 0)),
        out_shape=jax.ShapeDtypeStruct((lay.ts, ATTN_WIDTH), BF16),
        compiler_params=_cparams(("arbitrary", "arbitrary")),
    )(sink, y16, y16, y16, y16, y16, y16, y16,
      cache_k_l.reshape(lay.bs, past, KV_WIDTH), cache_v_l.reshape(lay.bs, past, KV_WIDTH))
    return jnp.concatenate([ctx, lat], axis=0)


def _s5_matrices(lam_re, lam_im, log_dt, b_re, b_im, c_re, c_im, d_skip, n_steps):
    q, p, g, n = SSM_CHUNK, SSM_CH, SSM_GROUPS, SSM_STATE
    lam = lax.complex(lam_re.astype(F32), lam_im.astype(F32))
    dt = jnp.exp(log_dt.astype(F32))[..., None]
    lam_dt = lam * dt
    lam_bar = jnp.exp(lam_dt)
    b_bar = ((lam_bar - 1.0) / lam)[..., None] * lax.complex(b_re.astype(F32), b_im.astype(F32))
    c_mat = lax.complex(c_re.astype(F32), c_im.astype(F32))
    steps = jnp.arange(q + 1, dtype=F32)
    pw = jnp.exp(lam_dt[:, None] * steps[None, :, None, None])
    kern = jnp.real(jnp.einsum('dgpn,dkgn,dgnr->dkgpr', c_mat, pw[:, :q], b_bar))
    tau_in = jnp.arange(q)[:, None]
    tau_out = jnp.arange(q)[None, :]
    lag_f = tau_out - tau_in
    lag_b = tau_in - tau_out
    kf = jnp.where((lag_f >= 0)[:, :, None, None, None], kern[0][jnp.clip(lag_f, 0, q - 1)], 0.0)
    kb = jnp.where((lag_b >= 0)[:, :, None, None, None], kern[1][jnp.clip(lag_b, 0, q - 1)], 0.0)
    m = (kf + kb).transpose(2, 0, 4, 1, 3)
    eye_q = jnp.eye(q, dtype=F32)[None, :, None, :, None]
    eye_p = jnp.eye(p, dtype=F32)[None, None, :, None, :]
    m = m + eye_q * eye_p * d_skip.astype(F32).reshape(g, 1, p, 1, 1)
    m = m.reshape(g, q * p, q * p)
    ws_f = pw[0, :q][::-1][:, :, :, None] * b_bar[0][None]
    ws_b = pw[1, :q][:, :, :, None] * b_bar[1][None]

    def cols(w):
        return w.transpose(1, 0, 3, 2).reshape(g, q * p, n)

    w1 = jnp.concatenate([m, jnp.real(cols(ws_f)), jnp.real(cols(ws_b)),
                          jnp.imag(cols(ws_f)), jnp.imag(cols(ws_b))], axis=2)
    cy_f = c_mat[0][None] * pw[0, 1:][:, :, None, :]
    cy_b = c_mat[1][None] * pw[1, 1:][::-1][:, :, None, :]

    def rows(w):
        return w.transpose(1, 3, 0, 2).reshape(g, n, q * p)

    wy = jnp.concatenate([jnp.real(rows(cy_f)), jnp.real(rows(cy_b)),
                          -jnp.imag(rows(cy_f)), -jnp.imag(rows(cy_b))], axis=1)
    hops = (q * 2.0 ** jnp.arange(8, dtype=F32))[None, :, None, None]
    a = jnp.exp(lam_dt[:, None] * hops)
    a = jnp.concatenate([a[0], a[1]], axis=-1).transpose(1, 0, 2)
    assert n_steps <= 8
    gb, nbun = S5_BUNDLE, g // S5_BUNDLE
    lane = jnp.arange(q * gb * p)
    target = ((lane % (gb * p)) // p) * (q * p) + (lane // (gb * p)) * p + lane % p
    perm = (target[:, None] == lane[None, :]).astype(BF16)
    ab = a.reshape(nbun, gb, 8, 2 * n).transpose(0, 2, 1, 3).reshape(nbun, 8, gb * 2 * n)
    return (perm, w1.astype(BF16).reshape(nbun, gb, q * p, q * p + 4 * n),
            wy.astype(BF16).reshape(nbun, gb, 4 * n, q * p), jnp.real(ab), jnp.imag(ab))


def _s5_kernel(u_ref, perm_ref, w1_ref, wy_ref, are_ref, aim_ref, h0re_ref, h0im_ref, y_ref, fre_ref, fim_ref, *,
               nb, nc):
    r_tot = nb * nc
    w = SSM_CHUNK * SSM_CH
    ns = 2 * SSM_STATE
    n2 = S5_BUNDLE * ns
    perm = perm_ref[...]
    x = jnp.dot(u_ref[0], perm, preferred_element_type=F32).astype(BF16)
    proj = [jnp.dot(x[:, g * w:(g + 1) * w], w1_ref[0, g], preferred_element_type=F32) for g in range(S5_BUNDLE)]
    d_re = jnp.concatenate([pg[:, w:w + ns] for pg in proj], axis=1)
    d_im = jnp.concatenate([pg[:, w + ns:] for pg in proj], axis=1)
    h0re_ref, h0im_ref, fre_ref, fim_ref = (r.at[0] for r in (h0re_ref, h0im_ref, fre_ref, fim_ref))
    row = lax.broadcasted_iota(I32, (r_tot, n2), 0)
    chunk = row % nc
    seq = row // nc
    fwd = (lax.broadcasted_iota(I32, (r_tot, n2), 1) % (2 * SSM_STATE)) < SSM_STATE

    def previous(x, dist):
        valid = (fwd & (chunk >= dist)) | (~fwd & (chunk < nc - dist))
        moved = jnp.where(fwd, pltpu.roll(x, dist, 0), pltpu.roll(x, r_tot - dist, 0))
        return jnp.where(valid, moved, 0.0)

    h0_re = jnp.zeros((r_tot, n2), F32)
    h0_im = jnp.zeros((r_tot, n2), F32)
    for b in range(nb):
        h0_re = jnp.where(seq == b, h0re_ref[0, b:b + 1, :], h0_re)
        h0_im = jnp.where(seq == b, h0im_ref[0, b:b + 1, :], h0_im)
    first = (fwd & (chunk == 0)) | (~fwd & (chunk == nc - 1))
    e_re = jnp.where(first, h0_re, previous(d_re, 1))
    e_im = jnp.where(first, h0_im, previous(d_im, 1))
    k = 0
    while (1 << k) < nc:
        a_re = are_ref[0, k:k + 1, :]
        a_im = aim_ref[0, k:k + 1, :]
        p_re = previous(e_re, 1 << k)
        p_im = previous(e_im, 1 << k)
        e_re, e_im = e_re + a_re * p_re - a_im * p_im, e_im + a_re * p_im + a_im * p_re
        k += 1
    ys = []
    for g in range(S5_BUNDLE):
        e_g = jnp.concatenate([e_re[:, g * ns:(g + 1) * ns], e_im[:, g * ns:(g + 1) * ns]], axis=1).astype(BF16)
        ys.append(proj[g][:, :w] + jnp.dot(e_g, wy_ref[0, g], preferred_element_type=F32))
    y = jnp.concatenate(ys, axis=1)
    y_hi = y.astype(BF16)
    y_lo = (y - y_hi.astype(F32)).astype(BF16)
    back = (((1,), (1,)), ((), ()))
    y_ref[0] = (lax.dot_general(y_hi, perm, back, preferred_element_type=F32)
                + lax.dot_general(y_lo, perm, back, preferred_element_type=F32))
    a_re = are_ref[0, 0:1, :]
    a_im = aim_ref[0, 0:1, :]
    f_re = a_re * e_re - a_im * e_im + d_re
    f_im = a_re * e_im + a_im * e_re + d_im
    fwd_row = fwd[0:1, :]
    for b in range(nb):
        lo, hi = b * nc, b * nc + nc - 1
        fre_ref[0, b:b + 1, :] = jnp.where(fwd_row, f_re[hi:hi + 1, :], f_re[lo:lo + 1, :])
        fim_ref[0, b:b + 1, :] = jnp.where(fwd_row, f_im[hi:hi + 1, :], f_im[lo:lo + 1, :])


def _s5_scan(u, mats, h0_re, h0_im):
    nb, length, _ = u.shape
    g, q, p, n = SSM_GROUPS, SSM_CHUNK, SSM_CH, SSM_STATE
    gb, nbun = S5_BUNDLE, SSM_GROUPS // S5_BUNDLE
    nc = length // q
    assert nc & (nc - 1) == 0 and nc % 8 == 0
    seqs = max(1, min(nb, S5_BLOCK_ROWS // nc))
    assert nb % seqs == 0
    n_rb, rows = nb // seqs, seqs * nc
    kw, sw = q * gb * p, gb * 2 * n
    perm, w1, wy, a_re, a_im = mats
    ub = u.reshape(nb * nc, q, nbun, gb * p).transpose(2, 0, 1, 3).reshape(nbun, nb * nc, kw)

    def lanes(h):
        h = h.astype(F32).reshape(n_rb, seqs, 2, nbun, gb, n).transpose(3, 0, 1, 4, 2, 5)
        return h.reshape(nbun, n_rb, seqs, sw)

    weight = lambda r, c: pl.BlockSpec((1, gb, r, c), lambda o, i: (o, 0, 0, 0))
    coeff = pl.BlockSpec((1, 8, sw), lambda o, i: (o, 0, 0))
    state = pl.BlockSpec((1, 1, seqs, sw), lambda o, i: (o, i, 0, 0))
    y, f_re, f_im = pl.pallas_call(
        functools.partial(_s5_kernel, nb=seqs, nc=nc),
        grid=(nbun, n_rb),
        in_specs=[pl.BlockSpec((1, rows, kw), lambda o, i: (o, i, 0)),
                  pl.BlockSpec((kw, kw), lambda o, i: (0, 0), pipeline_mode=pl.Buffered(1)),
                  weight(q * p, q * p + 4 * n), weight(4 * n, q * p), coeff, coeff, state, state],
        out_specs=[pl.BlockSpec((1, rows, kw), lambda o, i: (o, i, 0)), state, state],
        out_shape=[jax.ShapeDtypeStruct((nbun, nb * nc, kw), F32),
                   jax.ShapeDtypeStruct((nbun, n_rb, seqs, sw), F32),
                   jax.ShapeDtypeStruct((nbun, n_rb, seqs, sw), F32)],
        compiler_params=_cparams(("arbitrary", "arbitrary")),
    )(ub, perm, w1, wy, a_re, a_im, lanes(h0_re), lanes(h0_im))
    y = y.reshape(nbun, nb * nc, q, gb * p).transpose(1, 2, 0, 3).reshape(nb, length, g * p)

    def unlanes(f):
        f = f.reshape(nbun, n_rb, seqs, gb, 2, n).transpose(1, 2, 4, 0, 3, 5)
        return f.reshape(nb, 2, g, n)

    return y, unlanes(f_re), unlanes(f_im)


def _merge_kernel(h_ref, ys_ref, at_ref, gb_ref, gc_ref, uc_ref, gcp_ref, ucp_ref, gcn_ref, ucn_ref, cw_ref,
                  wglu_ref, wg0_ref, wg1_ref, wg2_ref, bg0_ref, bg1_ref, bg2_ref, ws_ref, wa_ref, wc_ref,
                  o_ref, ssm_scr, conv_scr, *, lay, tm, halo):
    i = pl.program_id(0)
    n = pl.program_id(1)

    @pl.when(n == 0)
    def _():
        y = ys_ref[...]
        ge = 0.5 * y * (1.0 + jnp.tanh(math.sqrt(2.0 / math.pi) * (y + 0.044715 * (y * y * y))))
        glu = jnp.dot(ge.astype(BF16), wglu_ref[...], preferred_element_type=F32)
        ssm_scr[...] = (ge * _sigmoid(glu)).astype(BF16)

        z = gc_ref[...].astype(F32) * uc_ref[...].astype(F32)
        z_before = gcp_ref[halo - 1:halo, :].astype(F32) * ucp_ref[halo - 1:halo, :].astype(F32)
        z_after = gcn_ref[0:1, :].astype(F32) * ucn_ref[0:1, :].astype(F32)
        local = lax.broadcasted_iota(I32, z.shape, 0)
        pos, seq_len = lay.seq_pos(local + i * tm, i, tm)
        z_prev = jnp.where(local == 0, z_before, pltpu.roll(z, 1, 0))
        z_prev = jnp.where(pos == 0, 0.0, z_prev)
        z_next = jnp.where(local == tm - 1, z_after, pltpu.roll(z, tm - 1, 0))
        z_next = jnp.where(pos == seq_len - 1, 0.0, z_next)
        conv = cw_ref[0:1, :] * z_prev + cw_ref[1:2, :] * z + cw_ref[2:3, :] * z_next
        conv_scr[...] = (gb_ref[...].astype(F32) * conv).astype(BF16)

    h = h_ref[...]
    acc = None
    for act, wg_ref, bg_ref, wb_ref in ((ssm_scr[...], wg0_ref, bg0_ref, ws_ref),
                                        (at_ref[...], wg1_ref, bg1_ref, wa_ref),
                                        (conv_scr[...], wg2_ref, bg2_ref, wc_ref)):
        gate = _sigmoid(jnp.dot(h, wg_ref[...], preferred_element_type=F32) + bg_ref[...])
        term = gate * jnp.dot(act, wb_ref[...], preferred_element_type=F32)
        acc = term if acc is None else acc + term
    o_ref[...] = acc.astype(BF16)


def _merge(lay, h16, y_ssm, attn, y16, conv_w, wglu16, wgates16, b_gates, wbs16, wba16, wbc16):
    d = lay.d
    tm = lay.row_tile(512)
    tn = min(COL_TILE, d)
    nd = d // tn
    halo = 16
    hb = tm // halo
    last_h = lay.t // halo - 1
    c0 = (ATTN_WIDTH + 2 * KV_WIDTH + SSM_WIDTH) // CONV_WIDTH
    row = lambda cb: pl.BlockSpec((tm, CONV_WIDTH), lambda i, n: (i, cb))
    before = lambda cb: pl.BlockSpec((halo, CONV_WIDTH), lambda i, n: (jnp.maximum(i * hb - 1, 0), cb))
    after = lambda cb: pl.BlockSpec((halo, CONV_WIDTH), lambda i, n: (jnp.minimum((i + 1) * hb, last_h), cb))
    gate_w = lambda br: pl.BlockSpec((d, tn), lambda i, n: (0, br * nd + n))
    gate_b = lambda br: pl.BlockSpec((1, tn), lambda i, n: (0, br * nd + n))
    return pl.pallas_call(
        functools.partial(_merge_kernel, lay=lay, tm=tm, halo=halo),
        grid=(lay.t // tm, nd),
        in_specs=[
            pl.BlockSpec((tm, d), lambda i, n: (i, 0)),
            pl.BlockSpec((tm, SSM_WIDTH), lambda i, n: (i, 0)),
            pl.BlockSpec((tm, ATTN_WIDTH), lambda i, n: (i, 0)),
            row(c0), row(c0 + 1), row(c0 + 2),
            before(c0 + 1), before(c0 + 2), after(c0 + 1), after(c0 + 2),
            pl.BlockSpec((3, CONV_WIDTH), lambda i, n: (0, 0)),
            pl.BlockSpec((SSM_WIDTH, SSM_WIDTH), lambda i, n: (0, 0)),
            gate_w(0), gate_w(1), gate_w(2), gate_b(0), gate_b(1), gate_b(2),
            pl.BlockSpec((SSM_WIDTH, tn), lambda i, n: (0, n)),
            pl.BlockSpec((ATTN_WIDTH, tn), lambda i, n: (0, n)),
            pl.BlockSpec((CONV_WIDTH, tn), lambda i, n: (0, n)),
        ],
        out_specs=pl.BlockSpec((tm, tn), lambda i, n: (i, n)),
        out_shape=jax.ShapeDtypeStruct((lay.t, d), BF16),
        scratch_shapes=[pltpu.VMEM((tm, SSM_WIDTH), BF16), pltpu.VMEM((tm, CONV_WIDTH), BF16)],
        compiler_params=_cparams(("arbitrary", "arbitrary")),
    )(h16, y_ssm, attn, y16, y16, y16, y16, y16, y16, y16, conv_w, wglu16,
      wgates16, wgates16, wgates16, b_gates, b_gates, b_gates, wbs16, wba16, wbc16)


def _outproj_kernel(m_ref, w_ref, x_ref, gate_ref, g2_ref, shift_ref, scale_ref, wrh_ref, wrl_ref,
                    xo_ref, hp_ref, lg_ref):
    acc = jnp.dot(m_ref[...], w_ref[...], preferred_element_type=F32)
    xn = x_ref[...] + gate_ref[0] * acc
    xo_ref[...] = xn
    ms = jnp.mean(xn * xn, axis=-1, keepdims=True)
    h2 = xn * lax.rsqrt(ms + EPS) * g2_ref[...]
    h2 = h2 * (1.0 + scale_ref[0]) + shift_ref[0]
    hp_ref[...] = _pack_bf16_pairs(h2)
    h_hi = h2.astype(BF16)
    h_lo = (h2 - h_hi.astype(F32)).astype(BF16)
    logits = (jnp.dot(h_hi, wrh_ref[...], preferred_element_type=F32)
              + jnp.dot(h_hi, wrl_ref[...], preferred_element_type=F32)
              + jnp.dot(h_lo, wrh_ref[...], preferred_element_type=F32))
    lg_ref[...] = logits.T[:N_EXPERTS, :]


def _outproj(lay, merged, wout16, x, mod_l, g2, w_router):
    d = lay.d
    tm = lay.row_tile(256)
    lanes = 128
    wr = jnp.zeros((d, lanes), F32).at[:, :N_EXPERTS].set(w_router.astype(F32))
    wr_hi = wr.astype(BF16)
    wr_lo = (wr - wr_hi.astype(F32)).astype(BF16)
    return pl.pallas_call(
        _outproj_kernel,
        grid=(lay.t // tm,),
        in_specs=[
            pl.BlockSpec((tm, d), lambda i: (i, 0)),
            pl.BlockSpec((d, d), lambda i: (0, 0)),
            pl.BlockSpec((tm, d), lambda i: (i, 0)),
            _mod_spec(lay, tm, 2, d),
            pl.BlockSpec((1, d), lambda i: (0, 0)),
            _mod_spec(lay, tm, 3, d),
            _mod_spec(lay, tm, 4, d),
            pl.BlockSpec((d, lanes), lambda i: (0, 0)),
            pl.BlockSpec((d, lanes), lambda i: (0, 0)),
        ],
        out_specs=[
            pl.BlockSpec((tm, d), lambda i: (i, 0)),
            pl.BlockSpec((tm, d // 2), lambda i: (i, 0)),
            pl.BlockSpec((N_EXPERTS, tm), lambda i: (0, i)),
        ],
        out_shape=[
            jax.ShapeDtypeStruct((lay.t, d), F32),
            jax.ShapeDtypeStruct((lay.t, d // 2), U32),
            jax.ShapeDtypeStruct((N_EXPERTS, lay.t), F32),
        ],
        compiler_params=_cparams(("arbitrary",)),
    )(merged, wout16, x, mod_l, g2.reshape(1, d), mod_l, mod_l, wr_hi, wr_lo)


def _route_kernel(lg_ref, br_ref, idx_ref, w_ref, pos_ref, cnt_ref, carry):
    step = pl.program_id(0)
    tt = lg_ref.shape[1]
    per_group = N_EXPERTS // N_EXPERT_GROUPS

    @pl.when(step == 0)
    def _():
        carry[...] = jnp.zeros_like(carry)

    scores = _sigmoid(lg_ref[...])
    biased = scores + br_ref[...]
    sub = lax.broadcasted_iota(I32, (per_group, tt), 0).astype(F32)
    blocks, group_score = [], []
    for g in range(N_EXPERT_GROUPS):
        blk = biased[g * per_group:(g + 1) * per_group, :]
        m1 = jnp.max(blk, axis=0, keepdims=True)
        i1 = jnp.min(jnp.where(blk == m1, sub, float(per_group)), axis=0, keepdims=True)
        m2 = jnp.max(jnp.where(sub == i1, -jnp.inf, blk), axis=0, keepdims=True)
        blocks.append(blk)
        group_score.append(m1 + m2)
    masked = []
    for g in range(N_EXPERT_GROUPS):
        beaten_by = jnp.zeros((1, tt), F32)
        for o in range(N_EXPERT_GROUPS):
            if o == g:
                continue
            wins = (group_score[o] > group_score[g]) | ((group_score[o] == group_score[g]) & (o < g))
            beaten_by = beaten_by + wins.astype(F32)
        masked.append(jnp.where(beaten_by < TOPK_GROUPS, blocks[g], -jnp.inf))
    masked = jnp.concatenate(masked, axis=0)
    eid = lax.broadcasted_iota(I32, (N_EXPERTS, tt), 0).astype(F32)
    chosen, weights = [], []
    onehot = jnp.zeros((N_EXPERTS, tt), F32)
    for _ in range(TOP_K):
        m = jnp.max(masked, axis=0, keepdims=True)
        e = jnp.min(jnp.where(masked == m, eid, float(N_EXPERTS)), axis=0, keepdims=True)
        hit = eid == e
        chosen.append(e)
        weights.append(jnp.sum(jnp.where(hit, scores, 0.0), axis=0, keepdims=True))
        onehot = onehot + hit.astype(F32)
        masked = jnp.where(hit, -jnp.inf, masked)
    total = weights[0]
    for wk in weights[1:]:
        total = total + wk
    earlier = (lax.broadcasted_iota(I32, (tt, tt), 0) < lax.broadcasted_iota(I32, (tt, tt), 1)).astype(BF16)
    rank = carry[...][:, 0:1] + jnp.dot(onehot.astype(BF16), earlier, preferred_element_type=F32)
    for k in range(TOP_K):
        idx_ref[k:k + 1, :] = chosen[k].astype(I32)
        w_ref[k:k + 1, :] = weights[k] / total * ROUTED_SCALE
        pos_ref[k:k + 1, :] = jnp.sum(jnp.where(eid == chosen[k], rank, 0.0), axis=0, keepdims=True).astype(I32)
    for k in range(TOP_K, 8):
        idx_ref[k:k + 1, :] = jnp.zeros((1, tt), I32)
        w_ref[k:k + 1, :] = jnp.zeros((1, tt), F32)
        pos_ref[k:k + 1, :] = jnp.zeros((1, tt), I32)
    carry[...] = carry[...] + jnp.sum(onehot, axis=1, keepdims=True)
    cnt_ref[...] = carry[...]


def _route(logits_t, b_router):
    t = logits_t.shape[1]
    tt = math.gcd(t, 512)
    tok = pl.BlockSpec((8, tt), lambda i: (0, i))
    return pl.pallas_call(
        _route_kernel,
        grid=(t // tt,),
        in_specs=[pl.BlockSpec((N_EXPERTS, tt), lambda i: (0, i)),
                  pl.BlockSpec((N_EXPERTS, 1), lambda i: (0, 0))],
        out_specs=[tok, tok, tok, pl.BlockSpec((N_EXPERTS, 128), lambda i: (0, 0))],
        out_shape=[jax.ShapeDtypeStruct((8, t), I32), jax.ShapeDtypeStruct((8, t), F32),
                   jax.ShapeDtypeStruct((8, t), I32), jax.ShapeDtypeStruct((N_EXPERTS, 128), F32)],
        scratch_shapes=[pltpu.VMEM((N_EXPERTS, 128), F32)],
        compiler_params=_cparams(("arbitrary",)),
    )(logits_t, b_router.astype(F32).reshape(N_EXPERTS, 1))


def _dest_kernel(idx_ref, pos_ref, start_ref, o_ref, *, tt):
    eid = lax.broadcasted_iota(I32, (N_EXPERTS, tt), 0)
    start = start_ref[...]
    for k in range(TOP_K):
        first = jnp.sum(jnp.where(eid == idx_ref[k:k + 1, :], start, 0.0), axis=0, keepdims=True)
        o_ref[k:k + 1, :] = first.astype(I32) + pos_ref[k:k + 1, :]
    for k in range(TOP_K, 8):
        o_ref[k:k + 1, :] = jnp.zeros((1, tt), I32)


def _dest_rows(idx_t, pos_t, pad_start):
    t = idx_t.shape[1]
    tt = math.gcd(t, 512)
    tok = pl.BlockSpec((8, tt), lambda i: (0, i))
    return pl.pallas_call(
        functools.partial(_dest_kernel, tt=tt),
        grid=(t // tt,),
        in_specs=[tok, tok, pl.BlockSpec((N_EXPERTS, 1), lambda i: (0, 0))],
        out_specs=tok,
        out_shape=jax.ShapeDtypeStruct((8, t), I32),
        compiler_params=_cparams(("arbitrary",)),
    )(idx_t, pos_t, pad_start.astype(F32).reshape(N_EXPERTS, 1))


def _sc_layout(n_pairs, period):
    info = plsc.get_sparse_core_info()
    workers = info.num_cores * info.num_subcores
    per_worker = n_pairs // workers
    chunk = math.gcd(math.gcd(per_worker, period), SC_STREAM_ROWS)
    assert per_worker * workers == n_pairs and chunk % 8 == 0
    return info.num_cores, per_worker, chunk


def _sc_scatter_rows(rows, dest, n_out):
    t, width = rows.shape
    n_cores, per_worker, chunk = _sc_layout(dest.shape[0], t)
    mesh = plsc.VectorSubcoreMesh(core_axis_name="core", subcore_axis_name="subcore")

    @functools.partial(
        pl.kernel, mesh=mesh, out_type=jax.ShapeDtypeStruct((n_out, width), rows.dtype),
        scratch_types=[pltpu.VMEM((chunk,), I32), pltpu.VMEM((chunk, width), rows.dtype)])
    def scatter(rows_hbm, dest_hbm, out_hbm, dest_v, rows_v):
        base = (lax.axis_index("subcore") * n_cores + lax.axis_index("core")) * per_worker

        @pl.loop(0, per_worker // chunk)
        def _(j):
            pair = base + j * chunk
            pltpu.sync_copy(dest_hbm.at[pl.ds(pair, chunk)], dest_v)
            pltpu.sync_copy(rows_hbm.at[pl.ds(lax.rem(pair, t), chunk)], rows_v)
            pltpu.sync_copy(rows_v, out_hbm.at[dest_v])

    return scatter(rows, dest)


def _sc_gather_rows(table, dest):
    width = table.shape[1]
    n_pairs = dest.shape[0]
    n_cores, per_worker, chunk = _sc_layout(n_pairs, n_pairs)
    mesh = plsc.VectorSubcoreMesh(core_axis_name="core", subcore_axis_name="subcore")

    @functools.partial(
        pl.kernel, mesh=mesh, out_type=jax.ShapeDtypeStruct((n_pairs, width), table.dtype),
        scratch_types=[pltpu.VMEM((chunk,), I32), pltpu.VMEM((chunk, width), table.dtype)])
    def gather(table_hbm, dest_hbm, out_hbm, dest_v, rows_v):
        base = (lax.axis_index("subcore") * n_cores + lax.axis_index("core")) * per_worker

        @pl.loop(0, per_worker // chunk)
        def _(j):
            pair = base + j * chunk
            pltpu.sync_copy(dest_hbm.at[pl.ds(pair, chunk)], dest_v)
            pltpu.sync_copy(table_hbm.at[dest_v], rows_v)
            pltpu.sync_copy(rows_v, out_hbm.at[pl.ds(pair, chunk)])

    return gather(table, dest)


def _expert_kernel(be_ref, xs_ref, wg_ref, wu_ref, wd_ref, ys_ref, wg16, wu16, wd16):
    i = pl.program_id(0)
    changed = jnp.logical_or(i == 0, be_ref[i] != be_ref[jnp.maximum(i - 1, 0)])

    @pl.when(changed)
    def _():
        rows = 256
        d = wg16.shape[0]

        def cast_in(r, carry):
            sl = pl.ds(pl.multiple_of(r * rows, rows), rows)
            wg16[sl, :] = wg_ref[0, sl, :].astype(BF16)
            wu16[sl, :] = wu_ref[0, sl, :].astype(BF16)
            return carry

        lax.fori_loop(0, d // rows, cast_in, 0)

        def cast_down(r, carry):
            sl = pl.ds(pl.multiple_of(r * 128, 128), 128)
            wd16[sl, :] = wd_ref[0, sl, :].astype(BF16)
            return carry

        lax.fori_loop(0, D_EXPERT // 128, cast_down, 0)

    n_blocks = pl.num_programs(0)
    used = i < be_ref[n_blocks]

    @pl.when(used)
    def _():
        held = lax.broadcasted_iota(I32, (xs_ref.shape[0], 1), 0) < be_ref[n_blocks + 1 + i]
        hi, lo = _unpack_bf16_pairs(jnp.where(held, xs_ref[...], jnp.uint32(0)))
        x = jnp.concatenate([hi, lo], axis=1).astype(BF16)
        gate = jnp.dot(x, wg16[...], preferred_element_type=F32)
        up = jnp.dot(x, wu16[...], preferred_element_type=F32)
        act = (gate * _sigmoid(gate) * up).astype(BF16)
        ys_ref[...] = _pack_bf16_pairs(jnp.dot(act, wd16[...], preferred_element_type=F32))

    @pl.when(jnp.logical_not(used))
    def _():
        ys_ref[...] = jnp.zeros_like(ys_ref)


def _experts(xs, block_e, layer, w_gate, w_up, w_down):
    n_rows, half = xs.shape
    d = 2 * half
    br = EXPERT_ROWS
    grid_spec = pltpu.PrefetchScalarGridSpec(
        num_scalar_prefetch=1,
        grid=(n_rows // br,),
        in_specs=[
            pl.BlockSpec((br, half), lambda i, be: (i, 0)),
            pl.BlockSpec((None, 1, d, D_EXPERT), lambda i, be: (layer, be[i], 0, 0)),
            pl.BlockSpec((None, 1, d, D_EXPERT), lambda i, be: (layer, be[i], 0, 0)),
            pl.BlockSpec((None, 1, D_EXPERT, d), lambda i, be: (layer, be[i], 0, 0)),
        ],
        out_specs=pl.BlockSpec((br, half), lambda i, be: (i, 0)),
        scratch_shapes=[pltpu.VMEM((d, D_EXPERT), BF16), pltpu.VMEM((d, D_EXPERT), BF16),
                        pltpu.VMEM((D_EXPERT, d), BF16)],
    )
    return pl.pallas_call(
        _expert_kernel,
        grid_spec=grid_spec,
        out_shape=jax.ShapeDtypeStruct((n_rows, half), U32),
        compiler_params=_cparams(("arbitrary",)),
    )(block_e, xs, w_gate, w_up, w_down)


def _shared_kernel(hp_ref, wsg_ref, wsu_ref, wsd_ref, o_ref):
    hi, lo = _unpack_bf16_pairs(hp_ref[...])
    h2 = jnp.concatenate([hi, lo], axis=1).astype(BF16)
    sg = jnp.dot(h2, wsg_ref[...], preferred_element_type=F32)
    su = jnp.dot(h2, wsu_ref[...], preferred_element_type=F32)
    shared = jnp.dot((sg * _sigmoid(sg) * su).astype(BF16), wsd_ref[...], preferred_element_type=F32)
    o_ref[...] = shared.astype(o_ref.dtype)


def _shared_expert(lay, h2p, wsg16, wsu16, wsd16):
    d = lay.d
    tm = lay.row_tile(512)
    return pl.pallas_call(
        _shared_kernel,
        grid=(lay.t // tm,),
        in_specs=[pl.BlockSpec((tm, d // 2), lambda i: (i, 0)),
                  pl.BlockSpec((d, D_EXPERT), lambda i: (0, 0)),
                  pl.BlockSpec((d, D_EXPERT), lambda i: (0, 0)),
                  pl.BlockSpec((D_EXPERT, d), lambda i: (0, 0))],
        out_specs=pl.BlockSpec((tm, d), lambda i: (i, 0)),
        out_shape=jax.ShapeDtypeStruct((lay.t, d), BF16),
        compiler_params=_cparams(("arbitrary",)),
    )(h2p, wsg16, wsu16, wsd16)


def _combine_kernel(yk_ref, wsel_ref, sh_ref, x_ref, gate_ref, fg_ref, *outs, tt, final, n_ctx_tiles):
    shared = sh_ref[...].astype(F32)
    half = yk_ref.shape[2]
    r_hi = jnp.zeros((tt, half), F32)
    r_lo = jnp.zeros((tt, half), F32)
    for k in range(TOP_K):
        y_hi, y_lo = _unpack_bf16_pairs(yk_ref[k])
        wk = wsel_ref[:, k:k + 1]
        r_hi = r_hi + wk * y_hi
        r_lo = r_lo + wk * y_lo
    routed = jnp.concatenate([r_hi, r_lo], axis=1)
    out = x_ref[...] + gate_ref[0] * (routed + shared)
    if not final:
        outs[0][...] = out
        return
    ms = jnp.mean(out * out, axis=-1, keepdims=True)
    out = out * lax.rsqrt(ms + EPS) * fg_ref[...]
    ctx_ref, lat_ref = outs
    is_ctx = pl.program_id(0) < n_ctx_tiles

    @pl.when(is_ctx)
    def _():
        ctx_ref[...] = out

    @pl.when(jnp.logical_not(is_ctx))
    def _():
        lat_ref[...] = out


def _combine(lay, y_by_k, wsel, shared, x, mod_l, final_g, tt, final):
    d = lay.d
    half = d // 2
    nct = lay.tc // tt
    if final:
        out_specs = [pl.BlockSpec((tt, d), lambda i: (jnp.minimum(i, nct - 1), 0)),
                     pl.BlockSpec((tt, d), lambda i: (jnp.maximum(i - nct, 0), 0))]
        out_shape = [jax.ShapeDtypeStruct((lay.tc, d), F32), jax.ShapeDtypeStruct((lay.ts, d), F32)]
    else:
        out_specs = pl.BlockSpec((tt, d), lambda i: (i, 0))
        out_shape = jax.ShapeDtypeStruct((lay.t, d), F32)
    return pl.pallas_call(
        functools.partial(_combine_kernel, tt=tt, final=final, n_ctx_tiles=nct),
        grid=(lay.t // tt,),
        in_specs=[
            pl.BlockSpec((TOP_K, tt, half), lambda i: (0, i, 0)),
            pl.BlockSpec((tt, 8), lambda i: (i, 0)),
            pl.BlockSpec((tt, d), lambda i: (i, 0)),
            pl.BlockSpec((tt, d), lambda i: (i, 0)),
            _mod_spec(lay, tt, 5, d),
            pl.BlockSpec((1, d), lambda i: (0, 0)),
        ],
        out_specs=out_specs,
        out_shape=out_shape,
        compiler_params=_cparams(("arbitrary",)),
    )(y_by_k, wsel, shared, x, mod_l, final_g.reshape(1, d))


def _moe(lay, layer, x, h2p, logits_t, mod_l, b_router, w_e_gate, w_e_up, w_e_down, wsg16, wsu16, wsd16, final_g,
         final):
    t = lay.t
    idx_t, w_t, pos_t, counts = _route(logits_t, b_router)
    br = EXPERT_ROWS
    counts = counts[:, 0].astype(I32)
    padded = (counts + br - 1) // br * br
    pad_end = jnp.cumsum(padded)
    pad_start = pad_end - padded
    n_blocks = -(-(t * TOP_K + N_EXPERTS * (br - 1)) // br)
    n_rows = n_blocks * br
    first_row = jnp.arange(n_blocks, dtype=I32) * br
    block_e = jnp.minimum(jnp.sum((pad_end[None, :] <= first_row[:, None]).astype(I32), axis=1), N_EXPERTS - 1)
    own = block_e[:, None] == jnp.arange(N_EXPERTS, dtype=I32)[None, :]
    rows_end = jnp.sum(jnp.where(own, (pad_start + counts)[None, :], 0), axis=1)
    held = jnp.clip(rows_end - first_row, 0, br)
    block_meta = jnp.concatenate([block_e, pad_end[-1:] // br, held]).astype(I32)
    dest = _dest_rows(idx_t, pos_t, pad_start)[:TOP_K].reshape(TOP_K * t)
    xs = _sc_scatter_rows(h2p, dest, n_rows)
    shared = _shared_expert(lay, h2p, wsg16, wsu16, wsd16)
    ys = _experts(xs, block_meta, layer, w_e_gate, w_e_up, w_e_down)
    y_by_k = _sc_gather_rows(ys, dest).reshape(TOP_K, t, h2p.shape[1])
    return _combine(lay, y_by_k, w_t.T, shared, x, mod_l, final_g, lay.row_tile(256), final)


def kernel(x_prompt, x_sample, c, cache_k, cache_v, state_ssm_re, state_ssm_im, c_ctx, w_ada, b_ada, norm1_g, norm2_g, w_in, w_gates, b_gates, ssm_lam_re, ssm_lam_im, ssm_log_dt, ssm_b_re, ssm_b_im, ssm_c_re, ssm_c_im, ssm_d, ssm_w_glu, conv_w, attn_sink, w_br_ssm, w_br_attn, w_br_conv, w_out, w_router, b_router, w_e_gate, w_e_up, w_e_down, w_s_gate, w_s_up, w_s_down, final_g):
    bc, lc, d = x_prompt.shape
    bs, ls, _ = x_sample.shape
    depth = w_in.shape[0]
    lay = _Layout(bc, lc, bs, ls, d)
    assert 1 + bs <= MOD_ROWS

    x = jnp.concatenate([x_prompt.reshape(lay.tc, d), x_sample.reshape(lay.ts, d)], axis=0)
    cvec = jnp.zeros((MOD_ROWS, d), F32).at[0].set(c_ctx).at[1:1 + bs].set(c)
    mod = _adaln(cvec, w_ada, b_ada).reshape(depth, MOD_ROWS * 6, 1, d)
    rope_cos, rope_sin = _rope_tables(lay, lay.row_tile(1024))
    zeros_state = jnp.zeros((bc, 2, SSM_GROUPS, SSM_STATE), F32)

    ks, vs, s_re, s_im = [], [], [], []
    for l in range(depth):
        mod_l = mod[l]
        h16, y16, kv32 = _inproj(lay, x, mod_l, norm1_g[l], w_in[l].astype(BF16), rope_cos, rope_sin)
        ks.append(kv32[:lay.tc, :KV_WIDTH].reshape(bc, lc, N_KV_HEADS, HEAD_DIM))
        vs.append(kv32[:lay.tc, KV_WIDTH:].reshape(bc, lc, N_KV_HEADS, HEAD_DIM))

        attn = _attention(lay, y16, attn_sink[l].astype(F32), cache_k[:, l], cache_v[:, l])

        u0 = ATTN_WIDTH + 2 * KV_WIDTH
        u = y16[:, u0:u0 + SSM_WIDTH]
        mats = _s5_matrices(ssm_lam_re[l], ssm_lam_im[l], ssm_log_dt[l], ssm_b_re[l], ssm_b_im[l],
                            ssm_c_re[l], ssm_c_im[l], ssm_d[l], 8)
        y_c, f_re, f_im = _s5_scan(u[:lay.tc].reshape(bc, lc, SSM_WIDTH), mats, zeros_state, zeros_state)
        y_s, _, _ = _s5_scan(u[lay.tc:].reshape(bs, ls, SSM_WIDTH), mats, state_ssm_re[:, l], state_ssm_im[:, l])
        s_re.append(f_re)
        s_im.append(f_im)
        y_ssm = jnp.concatenate([y_c.reshape(lay.tc, SSM_WIDTH), y_s.reshape(lay.ts, SSM_WIDTH)], axis=0)

        merged = _merge(lay, h16, y_ssm, attn, y16, conv_w[l], ssm_w_glu[l].astype(BF16),
                        w_gates[l].astype(BF16), b_gates[l].reshape(1, -1), w_br_ssm[l].astype(BF16),
                        w_br_attn[l].astype(BF16), w_br_conv[l].astype(BF16))
        x, h2p, logits_t = _outproj(lay, merged, w_out[l].astype(BF16), x, mod_l, norm2_g[l],
                                    w_router[l])
        x = _moe(lay, l, x, h2p, logits_t, mod_l, b_router[l], w_e_gate, w_e_up, w_e_down,
                 w_s_gate[l].astype(BF16), w_s_up[l].astype(BF16), w_s_down[l].astype(BF16),
                 final_g, l == depth - 1)

    y_prompt = x[0].reshape(bc, lc, d)
    y_sample = x[1].reshape(bs, ls, d)
    return (y_prompt, y_sample, jnp.stack(ks, axis=1), jnp.stack(vs, axis=1),
            jnp.stack(s_re, axis=1), jnp.stack(s_im, axis=1))
```

```python
import functools
import math

import jax
import jax.numpy as jnp
from jax import lax
from jax.experimental import pallas as pl
from jax.experimental.pallas import tpu as pltpu
from jax.experimental.pallas import tpu_sc as plsc

HEAD_DIM = 128
N_HEADS = 8
N_KV_HEADS = 2
GROUP = N_HEADS // N_KV_HEADS
ATTN_WIDTH = N_HEADS * HEAD_DIM
KV_WIDTH = N_KV_HEADS * HEAD_DIM
WINDOW = 128
ATTN_BLOCK = 128
ATTN_Q_ROWS = 256
ATTN_SCALE = HEAD_DIM ** -0.5
ROPE_BASE = 10000.0
ROT_F = HEAD_DIM // 4
GRID_W = 64
SSM_WIDTH = 512
SSM_CH = 16
SSM_GROUPS = SSM_WIDTH // SSM_CH
SSM_STATE = 64
SSM_CHUNK = 16
S5_BUNDLE = 8
S5_BLOCK_ROWS = 256
CONV_WIDTH = 512
N_BRANCHES = 3
IN_WIDTH = ATTN_WIDTH + 2 * KV_WIDTH + SSM_WIDTH + 3 * CONV_WIDTH
N_EXPERTS = 64
TOP_K = 6
N_EXPERT_GROUPS = 8
TOPK_GROUPS = 4
D_EXPERT = 512
ROUTED_SCALE = 2.5
EPS = 1e-6
NEG_INF = -1e30

COL_TILE = 512
MOD_ROWS = 16
EXPERT_ROWS = 512
SC_STREAM_ROWS = 64
VMEM_LIMIT_V7X = 56 * 1024 * 1024

F32 = jnp.float32
BF16 = jnp.bfloat16
I32 = jnp.int32
U32 = jnp.uint32


def _cparams(sem, vmem=VMEM_LIMIT_V7X):
    return pltpu.CompilerParams(dimension_semantics=sem, vmem_limit_bytes=vmem)


def _sigmoid(x):
    return 1.0 / (1.0 + jnp.exp(-x))


def _pack_bf16_pairs(v):
    n = v.shape[1] // 2
    hi = lax.bitcast_convert_type(v[:, :n].astype(BF16).astype(F32), U32)
    lo = lax.bitcast_convert_type(v[:, n:].astype(BF16).astype(F32), U32)
    return hi | (lo >> 16)


def _unpack_bf16_pairs(p):
    hi = lax.bitcast_convert_type(p & jnp.uint32(0xFFFF0000), F32)
    lo = lax.bitcast_convert_type(p << 16, F32)
    return hi, lo


def _adaln_kernel(c_ref, w_ref, b_ref, o_ref):
    c = c_ref[...]
    s = (c * _sigmoid(c)).astype(BF16)
    o_ref[0] = jnp.dot(s, w_ref[0].astype(BF16), preferred_element_type=F32) + b_ref[0]


def _adaln(cvec, w_ada, b_ada):
    depth, d, n6 = w_ada.shape
    tn = math.gcd(1024, n6)
    return pl.pallas_call(
        _adaln_kernel,
        grid=(depth, n6 // tn),
        in_specs=[
            pl.BlockSpec((MOD_ROWS, d), lambda l, n: (0, 0)),
            pl.BlockSpec((1, d, tn), lambda l, n: (l, 0, n)),
            pl.BlockSpec((1, 1, tn), lambda l, n: (l, 0, n)),
        ],
        out_specs=pl.BlockSpec((1, MOD_ROWS, tn), lambda l, n: (l, 0, n)),
        out_shape=jax.ShapeDtypeStruct((depth, MOD_ROWS, n6), F32),
        compiler_params=_cparams(("arbitrary", "arbitrary")),
    )(cvec, w_ada, b_ada.reshape(depth, 1, n6))


class _Layout:
    def __init__(self, n_ctx_seq, len_ctx, n_lat_seq, len_lat, d_model):
        self.bc, self.lc, self.bs, self.ls, self.d = n_ctx_seq, len_ctx, n_lat_seq, len_lat, d_model
        self.tc = n_ctx_seq * len_ctx
        self.ts = n_lat_seq * len_lat
        self.t = self.tc + self.ts

    def row_tile(self, want):
        tm = math.gcd(math.gcd(self.tc, self.ls), want)
        assert tm % 16 == 0
        return tm

    def mod_index(self, i, tm):
        nct, tps = self.tc // tm, self.ls // tm
        return jnp.where(i < nct, 0, 1 + (i - nct) // tps)

    def seq_pos(self, rows, i, tm):
        is_lat = i >= self.tc // tm
        return jnp.where(is_lat, (rows - self.tc) % self.ls, rows % self.lc), jnp.where(is_lat, self.ls, self.lc)


def _mod_spec(lay, tm, slot, d):
    return pl.BlockSpec((1, 1, d), lambda i, *_: (lay.mod_index(i, tm) * 6 + slot, 0, 0))


def _rope(z, cos, sin_signed, first_half):
    swapped = jnp.where(first_half, pltpu.roll(z, HEAD_DIM - ROT_F, 1), pltpu.roll(z, ROT_F, 1))
    return z * cos + swapped * sin_signed


def _inproj_kernel(x_ref, shift_ref, scale_ref, g_ref, w_ref, cos_ref, sin_ref, h_ref, y_ref, kv_ref):
    n = pl.program_id(1)
    n_q = ATTN_WIDTH // COL_TILE

    @pl.when(n == 0)
    def _():
        x = x_ref[...]
        ms = jnp.mean(x * x, axis=-1, keepdims=True)
        y = x * lax.rsqrt(ms + EPS) * g_ref[...]
        h_ref[...] = (y * (1.0 + scale_ref[0]) + shift_ref[0]).astype(BF16)

    acc = jnp.dot(h_ref[...], w_ref[...], preferred_element_type=F32)

    def rotated(n_heads):
        cos, sin = cos_ref[...], sin_ref[...]
        first_half = (lax.broadcasted_iota(I32, cos.shape, 1) % (2 * ROT_F)) < ROT_F
        parts = [_rope(acc[:, s * HEAD_DIM:(s + 1) * HEAD_DIM], cos, sin, first_half) for s in range(n_heads)]
        parts.append(acc[:, n_heads * HEAD_DIM:])
        return jnp.concatenate(parts, axis=1) if n_heads * HEAD_DIM < COL_TILE else jnp.concatenate(parts[:-1], axis=1)

    @pl.when(n < n_q)
    def _():
        y_ref[...] = rotated(COL_TILE // HEAD_DIM).astype(BF16)

    @pl.when(n == n_q)
    def _():
        kv_ref[...] = acc
        y_ref[...] = rotated(N_KV_HEADS).astype(BF16)

    @pl.when(n > n_q)
    def _():
        y_ref[...] = acc.astype(BF16)


def _inproj(lay, x, mod_l, g1, w_in16, rope_cos, rope_sin):
    d = lay.d
    tm = lay.row_tile(1024)
    nct, tps = lay.tc // tm, lay.ls // tm

    def rope_idx(i, n):
        return (jnp.where(i < nct, 0, 1 + (i - nct) % tps), 0)

    return pl.pallas_call(
        _inproj_kernel,
        grid=(lay.t // tm, IN_WIDTH // COL_TILE),
        in_specs=[
            pl.BlockSpec((tm, d), lambda i, n: (i, 0)),
            _mod_spec(lay, tm, 0, d),
            _mod_spec(lay, tm, 1, d),
            pl.BlockSpec((1, d), lambda i, n: (0, 0)),
            pl.BlockSpec((d, COL_TILE), lambda i, n: (0, n)),
            pl.BlockSpec((tm, HEAD_DIM), rope_idx),
            pl.BlockSpec((tm, HEAD_DIM), rope_idx),
        ],
        out_specs=[
            pl.BlockSpec((tm, d), lambda i, n: (i, 0)),
            pl.BlockSpec((tm, COL_TILE), lambda i, n: (i, n)),
            pl.BlockSpec((tm, 2 * KV_WIDTH), lambda i, n: (i, 0)),
        ],
        out_shape=[
            jax.ShapeDtypeStruct((lay.t, d), BF16),
            jax.ShapeDtypeStruct((lay.t, IN_WIDTH), BF16),
            jax.ShapeDtypeStruct((lay.t, 2 * KV_WIDTH), F32),
        ],
        compiler_params=_cparams(("arbitrary", "arbitrary")),
    )(x, mod_l, mod_l, g1.reshape(1, d), w_in16, rope_cos, rope_sin)


def _rope_tables(lay, tm):
    t = jnp.arange(lay.ls)
    row = (t // GRID_W).astype(F32)
    col = (t % GRID_W).astype(F32)
    inv = ROPE_BASE ** (-jnp.arange(ROT_F, dtype=F32) / ROT_F)
    ar, ac = row[:, None] * inv, col[:, None] * inv
    cos = jnp.concatenate([jnp.cos(ar), jnp.cos(ar), jnp.cos(ac), jnp.cos(ac)], axis=1)
    sin = jnp.concatenate([-jnp.sin(ar), jnp.sin(ar), -jnp.sin(ac), jnp.sin(ac)], axis=1)
    cos = jnp.concatenate([jnp.ones((tm, HEAD_DIM), F32), cos], axis=0)
    sin = jnp.concatenate([jnp.zeros((tm, HEAD_DIM), F32), sin], axis=0)
    return cos, sin


def _attend(q, sink_ref, j, parts):
    nq = q.shape[0]
    q4 = jnp.concatenate([q[:, g * HEAD_DIM:(g + 1) * HEAD_DIM] for g in range(GROUP)], axis=0)
    sink = jnp.concatenate([jnp.full((nq, 1), sink_ref[j * GROUP + g], F32) for g in range(GROUP)], axis=0)
    scores = []
    m = sink
    for k, _, mask in parts:
        s = lax.dot_general(q4, k, (((1,), (1,)), ((), ())), preferred_element_type=F32) * ATTN_SCALE
        if mask is not None:
            s = jnp.where(mask, s, NEG_INF)
        scores.append(s)
        m = jnp.maximum(m, jnp.max(s, axis=-1, keepdims=True))
    den = jnp.exp(sink - m)
    out = jnp.zeros((GROUP * nq, HEAD_DIM), F32)
    for s, (_, v, _) in zip(scores, parts):
        p = jnp.exp(s - m)
        den = den + jnp.sum(p, axis=-1, keepdims=True)
        out = out + jnp.dot(p.astype(BF16), v, preferred_element_type=F32)
    out = out / den
    return jnp.concatenate([out[g * nq:(g + 1) * nq] for g in range(GROUP)], axis=1)


def _head(x, j, width=HEAD_DIM):
    return x[:, j * width:(j + 1) * width]


def _attn_ctx_kernel(sink_ref, q_ref, k_ref, v_ref, o_ref):
    q, k, v = q_ref[...], k_ref[...], v_ref[...]
    outs = [_attend(_head(q, j, GROUP * HEAD_DIM), sink_ref, j, [(_head(k, j), _head(v, j), None)])
            for j in range(N_KV_HEADS)]
    o_ref[...] = jnp.concatenate(outs, axis=1).astype(o_ref.dtype)


def _attn_lat_kernel(sink_ref, q_ref, kp_ref, kc_ref, kn_ref, vp_ref, vc_ref, vn_ref, ck_ref, cv_ref, o_ref, *, seq_len):
    i = pl.program_id(1)
    q = q_ref[...]
    kw = jnp.concatenate([kp_ref[...], kc_ref[...], kn_ref[...]], axis=0)
    vw = jnp.concatenate([vp_ref[...], vc_ref[...], vn_ref[...]], axis=0)
    ck = ck_ref[...].astype(BF16)
    cv = cv_ref[...].astype(BF16)
    shape = (GROUP * ATTN_Q_ROWS, ATTN_Q_ROWS + 2 * ATTN_BLOCK)
    qoff = lax.broadcasted_iota(I32, shape, 0) % ATTN_Q_ROWS
    koff = lax.broadcasted_iota(I32, shape, 1) - ATTN_BLOCK
    kabs = koff + i * ATTN_Q_ROWS
    mask = (jnp.abs(qoff - koff) <= WINDOW) & (kabs >= 0) & (kabs < seq_len)
    outs = [_attend(_head(q, j, GROUP * HEAD_DIM), sink_ref, j,
                    [(_head(kw, j), _head(vw, j), mask), (_head(ck, j), _head(cv, j), None)])
            for j in range(N_KV_HEADS)]
    o_ref[...] = jnp.concatenate(outs, axis=1).astype(o_ref.dtype)


def _attention(lay, y16, sink, cache_k_l, cache_v_l):
    smem = pl.BlockSpec(memory_space=pltpu.SMEM)
    kcol, vcol = ATTN_WIDTH // KV_WIDTH, (ATTN_WIDTH + KV_WIDTH) // KV_WIDTH
    ctx = pl.pallas_call(
        _attn_ctx_kernel,
        grid=(lay.bc,),
        in_specs=[
            smem,
            pl.BlockSpec((lay.lc, ATTN_WIDTH), lambda b: (b, 0)),
            pl.BlockSpec((lay.lc, KV_WIDTH), lambda b: (b, kcol)),
            pl.BlockSpec((lay.lc, KV_WIDTH), lambda b: (b, vcol)),
        ],
        out_specs=pl.BlockSpec((lay.lc, ATTN_WIDTH), lambda b: (b, 0)),
        out_shape=jax.ShapeDtypeStruct((lay.tc, ATTN_WIDTH), BF16),
        compiler_params=_cparams(("arbitrary",)),
    )(sink, y16, y16, y16)

    qr = ATTN_Q_ROWS
    assert lay.ls % qr == 0 and lay.tc % qr == 0 and qr % ATTN_BLOCK == 0
    nblk = lay.ls // qr
    base = lay.tc // qr
    per = qr // ATTN_BLOCK
    last = lay.t // ATTN_BLOCK - 1
    past = cache_k_l.shape[1]

    def rb(b, i):
        return base + b * nblk + i

    def edge(col, delta):
        return pl.BlockSpec((ATTN_BLOCK, KV_WIDTH), lambda b, i: (jnp.clip(rb(b, i) * per + delta, 0, last), col))

    def own(col):
        return pl.BlockSpec((qr, KV_WIDTH), lambda b, i: (rb(b, i), col))

    cspec = pl.BlockSpec((None, past, KV_WIDTH), lambda b, i: (b, 0, 0))
    lat = pl.pallas_call(
        functools.partial(_attn_lat_kernel, seq_len=lay.ls),
        grid=(lay.bs, nblk),
        in_specs=[
            smem,
            pl.BlockSpec((qr, ATTN_WIDTH), lambda b, i: (rb(b, i), 0)),
            edge(kcol, -1), own(kcol), edge(kcol, per),
            edge(vcol, -1), own(vcol), edge(vcol, per),
            cspec, cspec,
        ],
        out_specs=pl.BlockSpec((qr, ATTN_WIDTH), lambda b, i: (b * nblk + i, 0)),
        out_shape=jax.ShapeDtypeStruct((lay.ts, ATTN_WIDTH), BF16),
        compiler_params=_cparams(("arbitrary", "arbitrary")),
    )(sink, y16, y16, y16, y16, y16, y16, y16,
      cache_k_l.reshape(lay.bs, past, KV_WIDTH), cache_v_l.reshape(lay.bs, past, KV_WIDTH))
    return jnp.concatenate([ctx, lat], axis=0)


def _s5_matrices(lam_re, lam_im, log_dt, b_re, b_im, c_re, c_im, d_skip, n_steps):
    q, p, g, n = SSM_CHUNK, SSM_CH, SSM_GROUPS, SSM_STATE
    lam = lax.complex(lam_re.astype(F32), lam_im.astype(F32))
    dt = jnp.exp(log_dt.astype(F32))[..., None]
    lam_dt = lam * dt
    lam_bar = jnp.exp(lam_dt)
    b_bar = ((lam_bar - 1.0) / lam)[..., None] * lax.complex(b_re.astype(F32), b_im.astype(F32))
    c_mat = lax.complex(c_re.astype(F32), c_im.astype(F32))
    steps = jnp.arange(q + 1, dtype=F32)
    pw = jnp.exp(lam_dt[:, None] * steps[None, :, None, None])
    kern = jnp.real(jnp.einsum('dgpn,dkgn,dgnr->dkgpr', c_mat, pw[:, :q], b_bar))
    tau_in = jnp.arange(q)[:, None]
    tau_out = jnp.arange(q)[None, :]
    lag_f = tau_out - tau_in
    lag_b = tau_in - tau_out
    lags = jnp.arange(q)
    pick_f = (lag_f[:, :, None] == lags).astype(F32)
    pick_b = (lag_b[:, :, None] == lags).astype(F32)
    exact = lax.Precision.HIGHEST
    kf = jnp.einsum('abk,kgpr->abgpr', pick_f, kern[0], precision=exact)
    kb = jnp.einsum('abk,kgpr->abgpr', pick_b, kern[1], precision=exact)
    m = (kf + kb).transpose(2, 0, 4, 1, 3)
    eye_q = jnp.eye(q, dtype=F32)[None, :, None, :, None]
    eye_p = jnp.eye(p, dtype=F32)[None, None, :, None, :]
    m = m + eye_q * eye_p * d_skip.astype(F32).reshape(g, 1, p, 1, 1)
    m = m.reshape(g, q * p, q * p)
    ws_f = pw[0, :q][::-1][:, :, :, None] * b_bar[0][None]
    ws_b = pw[1, :q][:, :, :, None] * b_bar[1][None]

    def cols(w):
        return w.transpose(1, 0, 3, 2).reshape(g, q * p, n)

    w1 = jnp.concatenate([m, jnp.real(cols(ws_f)), jnp.real(cols(ws_b)),
                          jnp.imag(cols(ws_f)), jnp.imag(cols(ws_b))], axis=2)
    cy_f = c_mat[0][None] * pw[0, 1:][:, :, None, :]
    cy_b = c_mat[1][None] * pw[1, 1:][::-1][:, :, None, :]

    def rows(w):
        return w.transpose(1, 3, 0, 2).reshape(g, n, q * p)

    wy = jnp.concatenate([jnp.real(rows(cy_f)), jnp.real(rows(cy_b)),
                          -jnp.imag(rows(cy_f)), -jnp.imag(rows(cy_b))], axis=1)
    hops = (q * 2.0 ** jnp.arange(8, dtype=F32))[None, :, None, None]
    a = jnp.exp(lam_dt[:, None] * hops)
    a = jnp.concatenate([a[0], a[1]], axis=-1).transpose(1, 0, 2)
    assert n_steps <= 8
    gb, nbun = S5_BUNDLE, g // S5_BUNDLE
    lane = jnp.arange(q * gb * p)
    target = ((lane % (gb * p)) // p) * (q * p) + (lane // (gb * p)) * p + lane % p
    perm = (target[:, None] == lane[None, :]).astype(BF16)
    ab = a.reshape(nbun, gb, 8, 2 * n).transpose(0, 2, 1, 3).reshape(nbun, 8, gb * 2 * n)
    return (perm, w1.astype(BF16).reshape(nbun, gb, q * p, q * p + 4 * n),
            wy.astype(BF16).reshape(nbun, gb, 4 * n, q * p), jnp.real(ab), jnp.imag(ab))


def _s5_kernel(u_ref, perm_ref, w1_ref, wy_ref, are_ref, aim_ref, h0re_ref, h0im_ref, y_ref, fre_ref, fim_ref, *,
               nb, nc):
    r_tot = nb * nc
    w = SSM_CHUNK * SSM_CH
    ns = 2 * SSM_STATE
    n2 = S5_BUNDLE * ns
    perm = perm_ref[...]
    x = jnp.dot(u_ref[0], perm, preferred_element_type=F32).astype(BF16)
    proj = [jnp.dot(x[:, g * w:(g + 1) * w], w1_ref[0, g], preferred_element_type=F32) for g in range(S5_BUNDLE)]
    d_re = jnp.concatenate([pg[:, w:w + ns] for pg in proj], axis=1)
    d_im = jnp.concatenate([pg[:, w + ns:] for pg in proj], axis=1)
    h0re_ref, h0im_ref, fre_ref, fim_ref = (r.at[0] for r in (h0re_ref, h0im_ref, fre_ref, fim_ref))
    row = lax.broadcasted_iota(I32, (r_tot, n2), 0)
    chunk = row % nc
    seq = row // nc
    fwd = (lax.broadcasted_iota(I32, (r_tot, n2), 1) % (2 * SSM_STATE)) < SSM_STATE

    def previous(x, dist):
        valid = (fwd & (chunk >= dist)) | (~fwd & (chunk < nc - dist))
        moved = jnp.where(fwd, pltpu.roll(x, dist, 0), pltpu.roll(x, r_tot - dist, 0))
        return jnp.where(valid, moved, 0.0)

    h0_re = jnp.zeros((r_tot, n2), F32)
    h0_im = jnp.zeros((r_tot, n2), F32)
    for b in range(nb):
        h0_re = jnp.where(seq == b, h0re_ref[0, b:b + 1, :], h0_re)
        h0_im = jnp.where(seq == b, h0im_ref[0, b:b + 1, :], h0_im)
    first = (fwd & (chunk == 0)) | (~fwd & (chunk == nc - 1))
    e_re = jnp.where(first, h0_re, previous(d_re, 1))
    e_im = jnp.where(first, h0_im, previous(d_im, 1))
    k = 0
    while (1 << k) < nc:
        a_re = are_ref[0, k:k + 1, :]
        a_im = aim_ref[0, k:k + 1, :]
        p_re = previous(e_re, 1 << k)
        p_im = previous(e_im, 1 << k)
        e_re, e_im = e_re + a_re * p_re - a_im * p_im, e_im + a_re * p_im + a_im * p_re
        k += 1
    ys = []
    for g in range(S5_BUNDLE):
        e_g = jnp.concatenate([e_re[:, g * ns:(g + 1) * ns], e_im[:, g * ns:(g + 1) * ns]], axis=1).astype(BF16)
        ys.append(proj[g][:, :w] + jnp.dot(e_g, wy_ref[0, g], preferred_element_type=F32))
    y = jnp.concatenate(ys, axis=1)
    y_hi = y.astype(BF16)
    y_lo = (y - y_hi.astype(F32)).astype(BF16)
    back = (((1,), (1,)), ((), ()))
    y_ref[0] = (lax.dot_general(y_hi, perm, back, preferred_element_type=F32)
                + lax.dot_general(y_lo, perm, back, preferred_element_type=F32))
    a_re = are_ref[0, 0:1, :]
    a_im = aim_ref[0, 0:1, :]
    f_re = a_re * e_re - a_im * e_im + d_re
    f_im = a_re * e_im + a_im * e_re + d_im
    fwd_row = fwd[0:1, :]
    for b in range(nb):
        lo, hi = b * nc, b * nc + nc - 1
        fre_ref[0, b:b + 1, :] = jnp.where(fwd_row, f_re[hi:hi + 1, :], f_re[lo:lo + 1, :])
        fim_ref[0, b:b + 1, :] = jnp.where(fwd_row, f_im[hi:hi + 1, :], f_im[lo:lo + 1, :])


def _s5_scan(u, mats, h0_re, h0_im):
    nb, length, _ = u.shape
    g, q, p, n = SSM_GROUPS, SSM_CHUNK, SSM_CH, SSM_STATE
    gb, nbun = S5_BUNDLE, SSM_GROUPS // S5_BUNDLE
    nc = length // q
    assert nc & (nc - 1) == 0 and nc % 8 == 0
    seqs = max(1, min(nb, S5_BLOCK_ROWS // nc))
    assert nb % seqs == 0
    n_rb, rows = nb // seqs, seqs * nc
    kw, sw = q * gb * p, gb * 2 * n
    perm, w1, wy, a_re, a_im = mats
    ub = u.reshape(nb * nc, q, nbun, gb * p).transpose(2, 0, 1, 3).reshape(nbun, nb * nc, kw)

    def lanes(h):
        h = h.astype(F32).reshape(n_rb, seqs, 2, nbun, gb, n).transpose(3, 0, 1, 4, 2, 5)
        return h.reshape(nbun, n_rb, seqs, sw)

    weight = lambda r, c: pl.BlockSpec((1, gb, r, c), lambda o, i: (o, 0, 0, 0))
    coeff = pl.BlockSpec((1, 8, sw), lambda o, i: (o, 0, 0))
    state = pl.BlockSpec((1, 1, seqs, sw), lambda o, i: (o, i, 0, 0))
    y, f_re, f_im = pl.pallas_call(
        functools.partial(_s5_kernel, nb=seqs, nc=nc),
        grid=(nbun, n_rb),
        in_specs=[pl.BlockSpec((1, rows, kw), lambda o, i: (o, i, 0)),
                  pl.BlockSpec((kw, kw), lambda o, i: (0, 0), pipeline_mode=pl.Buffered(1)),
                  weight(q * p, q * p + 4 * n), weight(4 * n, q * p), coeff, coeff, state, state],
        out_specs=[pl.BlockSpec((1, rows, kw), lambda o, i: (o, i, 0)), state, state],
        out_shape=[jax.ShapeDtypeStruct((nbun, nb * nc, kw), F32),
                   jax.ShapeDtypeStruct((nbun, n_rb, seqs, sw), F32),
                   jax.ShapeDtypeStruct((nbun, n_rb, seqs, sw), F32)],
        compiler_params=_cparams(("arbitrary", "arbitrary")),
    )(ub, perm, w1, wy, a_re, a_im, lanes(h0_re), lanes(h0_im))
    y = y.reshape(nbun, nb * nc, q, gb * p).transpose(1, 2, 0, 3).reshape(nb, length, g * p)

    def unlanes(f):
        f = f.reshape(nbun, n_rb, seqs, gb, 2, n).transpose(1, 2, 4, 0, 3, 5)
        return f.reshape(nb, 2, g, n)

    return y, unlanes(f_re), unlanes(f_im)


def _merge_kernel(h_ref, ys_ref, at_ref, gb_ref, gc_ref, uc_ref, gcp_ref, ucp_ref, gcn_ref, ucn_ref, cw_ref,
                  wglu_ref, wg0_ref, wg1_ref, wg2_ref, bg0_ref, bg1_ref, bg2_ref, ws_ref, wa_ref, wc_ref,
                  o_ref, ssm_scr, conv_scr, *, lay, tm, halo):
    i = pl.program_id(0)
    n = pl.program_id(1)

    @pl.when(n == 0)
    def _():
        y = ys_ref[...]
        ge = 0.5 * y * (1.0 + jnp.tanh(math.sqrt(2.0 / math.pi) * (y + 0.044715 * (y * y * y))))
        glu = jnp.dot(ge.astype(BF16), wglu_ref[...], preferred_element_type=F32)
        ssm_scr[...] = (ge * _sigmoid(glu)).astype(BF16)

        z = gc_ref[...].astype(F32) * uc_ref[...].astype(F32)
        z_before = gcp_ref[halo - 1:halo, :].astype(F32) * ucp_ref[halo - 1:halo, :].astype(F32)
        z_after = gcn_ref[0:1, :].astype(F32) * ucn_ref[0:1, :].astype(F32)
        local = lax.broadcasted_iota(I32, z.shape, 0)
        pos, seq_len = lay.seq_pos(local + i * tm, i, tm)
        z_prev = jnp.where(local == 0, z_before, pltpu.roll(z, 1, 0))
        z_prev = jnp.where(pos == 0, 0.0, z_prev)
        z_next = jnp.where(local == tm - 1, z_after, pltpu.roll(z, tm - 1, 0))
        z_next = jnp.where(pos == seq_len - 1, 0.0, z_next)
        conv = cw_ref[0:1, :] * z_prev + cw_ref[1:2, :] * z + cw_ref[2:3, :] * z_next
        conv_scr[...] = (gb_ref[...].astype(F32) * conv).astype(BF16)

    h = h_ref[...]
    acc = None
    for act, wg_ref, bg_ref, wb_ref in ((ssm_scr[...], wg0_ref, bg0_ref, ws_ref),
                                        (at_ref[...], wg1_ref, bg1_ref, wa_ref),
                                        (conv_scr[...], wg2_ref, bg2_ref, wc_ref)):
        gate = _sigmoid(jnp.dot(h, wg_ref[...], preferred_element_type=F32) + bg_ref[...])
        term = gate * jnp.dot(act, wb_ref[...], preferred_element_type=F32)
        acc = term if acc is None else acc + term
    o_ref[...] = acc.astype(BF16)


def _merge(lay, h16, y_ssm, attn, y16, conv_w, wglu16, wgates16, b_gates, wbs16, wba16, wbc16):
    d = lay.d
    tm = lay.row_tile(512)
    tn = min(COL_TILE, d)
    nd = d // tn
    halo = 16
    hb = tm // halo
    last_h = lay.t // halo - 1
    c0 = (ATTN_WIDTH + 2 * KV_WIDTH + SSM_WIDTH) // CONV_WIDTH
    row = lambda cb: pl.BlockSpec((tm, CONV_WIDTH), lambda i, n: (i, cb))
    before = lambda cb: pl.BlockSpec((halo, CONV_WIDTH), lambda i, n: (jnp.maximum(i * hb - 1, 0), cb))
    after = lambda cb: pl.BlockSpec((halo, CONV_WIDTH), lambda i, n: (jnp.minimum((i + 1) * hb, last_h), cb))
    gate_w = lambda br: pl.BlockSpec((d, tn), lambda i, n: (0, br * nd + n))
    gate_b = lambda br: pl.BlockSpec((1, tn), lambda i, n: (0, br * nd + n))
    return pl.pallas_call(
        functools.partial(_merge_kernel, lay=lay, tm=tm, halo=halo),
        grid=(lay.t // tm, nd),
        in_specs=[
            pl.BlockSpec((tm, d), lambda i, n: (i, 0)),
            pl.BlockSpec((tm, SSM_WIDTH), lambda i, n: (i, 0)),
            pl.BlockSpec((tm, ATTN_WIDTH), lambda i, n: (i, 0)),
            row(c0), row(c0 + 1), row(c0 + 2),
            before(c0 + 1), before(c0 + 2), after(c0 + 1), after(c0 + 2),
            pl.BlockSpec((3, CONV_WIDTH), lambda i, n: (0, 0)),
            pl.BlockSpec((SSM_WIDTH, SSM_WIDTH), lambda i, n: (0, 0)),
            gate_w(0), gate_w(1), gate_w(2), gate_b(0), gate_b(1), gate_b(2),
            pl.BlockSpec((SSM_WIDTH, tn), lambda i, n: (0, n)),
            pl.BlockSpec((ATTN_WIDTH, tn), lambda i, n: (0, n)),
            pl.BlockSpec((CONV_WIDTH, tn), lambda i, n: (0, n)),
        ],
        out_specs=pl.BlockSpec((tm, tn), lambda i, n: (i, n)),
        out_shape=jax.ShapeDtypeStruct((lay.t, d), BF16),
        scratch_shapes=[pltpu.VMEM((tm, SSM_WIDTH), BF16), pltpu.VMEM((tm, CONV_WIDTH), BF16)],
        compiler_params=_cparams(("arbitrary", "arbitrary")),
    )(h16, y_ssm, attn, y16, y16, y16, y16, y16, y16, y16, conv_w, wglu16,
      wgates16, wgates16, wgates16, b_gates, b_gates, b_gates, wbs16, wba16, wbc16)


def _outproj_kernel(m_ref, w_ref, x_ref, gate_ref, g2_ref, shift_ref, scale_ref, wrh_ref, wrl_ref,
                    xo_ref, hp_ref, lg_ref):
    acc = jnp.dot(m_ref[...], w_ref[...], preferred_element_type=F32)
    xn = x_ref[...] + gate_ref[0] * acc
    xo_ref[...] = xn
    ms = jnp.mean(xn * xn, axis=-1, keepdims=True)
    h2 = xn * lax.rsqrt(ms + EPS) * g2_ref[...]
    h2 = h2 * (1.0 + scale_ref[0]) + shift_ref[0]
    hp_ref[...] = _pack_bf16_pairs(h2)
    h_hi = h2.astype(BF16)
    h_lo = (h2 - h_hi.astype(F32)).astype(BF16)
    logits = (jnp.dot(h_hi, wrh_ref[...], preferred_element_type=F32)
              + jnp.dot(h_hi, wrl_ref[...], preferred_element_type=F32)
              + jnp.dot(h_lo, wrh_ref[...], preferred_element_type=F32))
    lg_ref[...] = logits.T[:N_EXPERTS, :]


def _outproj(lay, merged, wout16, x, mod_l, g2, w_router):
    d = lay.d
    tm = lay.row_tile(256)
    lanes = 128
    wr = jnp.zeros((d, lanes), F32).at[:, :N_EXPERTS].set(w_router.astype(F32))
    wr_hi = wr.astype(BF16)
    wr_lo = (wr - wr_hi.astype(F32)).astype(BF16)
    return pl.pallas_call(
        _outproj_kernel,
        grid=(lay.t // tm,),
        in_specs=[
            pl.BlockSpec((tm, d), lambda i: (i, 0)),
            pl.BlockSpec((d, d), lambda i: (0, 0)),
            pl.BlockSpec((tm, d), lambda i: (i, 0)),
            _mod_spec(lay, tm, 2, d),
            pl.BlockSpec((1, d), lambda i: (0, 0)),
            _mod_spec(lay, tm, 3, d),
            _mod_spec(lay, tm, 4, d),
            pl.BlockSpec((d, lanes), lambda i: (0, 0)),
            pl.BlockSpec((d, lanes), lambda i: (0, 0)),
        ],
        out_specs=[
            pl.BlockSpec((tm, d), lambda i: (i, 0)),
            pl.BlockSpec((tm, d // 2), lambda i: (i, 0)),
            pl.BlockSpec((N_EXPERTS, tm), lambda i: (0, i)),
        ],
        out_shape=[
            jax.ShapeDtypeStruct((lay.t, d), F32),
            jax.ShapeDtypeStruct((lay.t, d // 2), U32),
            jax.ShapeDtypeStruct((N_EXPERTS, lay.t), F32),
        ],
        compiler_params=_cparams(("arbitrary",)),
    )(merged, wout16, x, mod_l, g2.reshape(1, d), mod_l, mod_l, wr_hi, wr_lo)


def _route_kernel(lg_ref, br_ref, idx_ref, w_ref, pos_ref, cnt_ref, carry):
    step = pl.program_id(0)
    tt = lg_ref.shape[1]
    per_group = N_EXPERTS // N_EXPERT_GROUPS

    @pl.when(step == 0)
    def _():
        carry[...] = jnp.zeros_like(carry)

    scores = _sigmoid(lg_ref[...])
    biased = scores + br_ref[...]
    sub = lax.broadcasted_iota(I32, (per_group, tt), 0).astype(F32)
    blocks, group_score = [], []
    for g in range(N_EXPERT_GROUPS):
        blk = biased[g * per_group:(g + 1) * per_group, :]
        m1 = jnp.max(blk, axis=0, keepdims=True)
        i1 = jnp.min(jnp.where(blk == m1, sub, float(per_group)), axis=0, keepdims=True)
        m2 = jnp.max(jnp.where(sub == i1, -jnp.inf, blk), axis=0, keepdims=True)
        blocks.append(blk)
        group_score.append(m1 + m2)
    masked = []
    for g in range(N_EXPERT_GROUPS):
        beaten_by = jnp.zeros((1, tt), F32)
        for o in range(N_EXPERT_GROUPS):
            if o == g:
                continue
            wins = (group_score[o] > group_score[g]) | ((group_score[o] == group_score[g]) & (o < g))
            beaten_by = beaten_by + wins.astype(F32)
        masked.append(jnp.where(beaten_by < TOPK_GROUPS, blocks[g], -jnp.inf))
    masked = jnp.concatenate(masked, axis=0)
    eid = lax.broadcasted_iota(I32, (N_EXPERTS, tt), 0).astype(F32)
    chosen, weights = [], []
    onehot = jnp.zeros((N_EXPERTS, tt), F32)
    for _ in range(TOP_K):
        m = jnp.max(masked, axis=0, keepdims=True)
        e = jnp.min(jnp.where(masked == m, eid, float(N_EXPERTS)), axis=0, keepdims=True)
        hit = eid == e
        chosen.append(e)
        weights.append(jnp.sum(jnp.where(hit, scores, 0.0), axis=0, keepdims=True))
        onehot = onehot + hit.astype(F32)
        masked = jnp.where(hit, -jnp.inf, masked)
    total = weights[0]
    for wk in weights[1:]:
        total = total + wk
    earlier = (lax.broadcasted_iota(I32, (tt, tt), 0) < lax.broadcasted_iota(I32, (tt, tt), 1)).astype(BF16)
    rank = carry[...][:, 0:1] + jnp.dot(onehot.astype(BF16), earlier, preferred_element_type=F32)
    for k in range(TOP_K):
        idx_ref[k:k + 1, :] = chosen[k].astype(I32)
        w_ref[k:k + 1, :] = weights[k] / total * ROUTED_SCALE
        pos_ref[k:k + 1, :] = jnp.sum(jnp.where(eid == chosen[k], rank, 0.0), axis=0, keepdims=True).astype(I32)
    for k in range(TOP_K, 8):
        idx_ref[k:k + 1, :] = jnp.zeros((1, tt), I32)
        w_ref[k:k + 1, :] = jnp.zeros((1, tt), F32)
        pos_ref[k:k + 1, :] = jnp.zeros((1, tt), I32)
    carry[...] = carry[...] + jnp.sum(onehot, axis=1, keepdims=True)
    cnt_ref[...] = carry[...]


def _route(logits_t, b_router):
    t = logits_t.shape[1]
    tt = math.gcd(t, 512)
    tok = pl.BlockSpec((8, tt), lambda i: (0, i))
    return pl.pallas_call(
        _route_kernel,
        grid=(t // tt,),
        in_specs=[pl.BlockSpec((N_EXPERTS, tt), lambda i: (0, i)),
                  pl.BlockSpec((N_EXPERTS, 1), lambda i: (0, 0))],
        out_specs=[tok, tok, tok, pl.BlockSpec((N_EXPERTS, 128), lambda i: (0, 0))],
        out_shape=[jax.ShapeDtypeStruct((8, t), I32), jax.ShapeDtypeStruct((8, t), F32),
                   jax.ShapeDtypeStruct((8, t), I32), jax.ShapeDtypeStruct((N_EXPERTS, 128), F32)],
        scratch_shapes=[pltpu.VMEM((N_EXPERTS, 128), F32)],
        compiler_params=_cparams(("arbitrary",)),
    )(logits_t, b_router.astype(F32).reshape(N_EXPERTS, 1))


def _dest_kernel(idx_ref, pos_ref, start_ref, o_ref, *, tt):
    eid = lax.broadcasted_iota(I32, (N_EXPERTS, tt), 0)
    start = start_ref[...]
    for k in range(TOP_K):
        first = jnp.sum(jnp.where(eid == idx_ref[k:k + 1, :], start, 0.0), axis=0, keepdims=True)
        o_ref[k:k + 1, :] = first.astype(I32) + pos_ref[k:k + 1, :]
    for k in range(TOP_K, 8):
        o_ref[k:k + 1, :] = jnp.zeros((1, tt), I32)


def _dest_rows(idx_t, pos_t, pad_start):
    t = idx_t.shape[1]
    tt = math.gcd(t, 512)
    tok = pl.BlockSpec((8, tt), lambda i: (0, i))
    return pl.pallas_call(
        functools.partial(_dest_kernel, tt=tt),
        grid=(t // tt,),
        in_specs=[tok, tok, pl.BlockSpec((N_EXPERTS, 1), lambda i: (0, 0))],
        out_specs=tok,
        out_shape=jax.ShapeDtypeStruct((8, t), I32),
        compiler_params=_cparams(("arbitrary",)),
    )(idx_t, pos_t, pad_start.astype(F32).reshape(N_EXPERTS, 1))


def _sc_layout(n_pairs, period):
    info = plsc.get_sparse_core_info()
    workers = info.num_cores * info.num_subcores
    per_worker = n_pairs // workers
    chunk = math.gcd(math.gcd(per_worker, period), SC_STREAM_ROWS)
    assert per_worker * workers == n_pairs and chunk % 8 == 0
    return info.num_cores, per_worker, chunk


def _sc_scatter_rows(rows, dest, n_out):
    t, width = rows.shape
    copies = dest.shape[0] // t
    n_cores, per_worker, chunk = _sc_layout(t, t)
    mesh = plsc.VectorSubcoreMesh(core_axis_name="core", subcore_axis_name="subcore")

    @functools.partial(
        pl.kernel, mesh=mesh, out_type=jax.ShapeDtypeStruct((n_out, width), rows.dtype),
        scratch_types=[pltpu.VMEM((chunk,), I32), pltpu.VMEM((chunk, width), rows.dtype)])
    def scatter(rows_hbm, dest_hbm, out_hbm, dest_v, rows_v):
        base = (lax.axis_index("subcore") * n_cores + lax.axis_index("core")) * per_worker

        @pl.loop(0, per_worker // chunk)
        def _(j):
            tok = base + j * chunk
            pltpu.sync_copy(rows_hbm.at[pl.ds(tok, chunk)], rows_v)
            for k in range(copies):
                pltpu.sync_copy(dest_hbm.at[pl.ds(k * t + tok, chunk)], dest_v)
                pltpu.sync_copy(rows_v, out_hbm.at[dest_v])

    return scatter(rows, dest)


def _sc_gather_rows(table, dest):
    width = table.shape[1]
    n_pairs = dest.shape[0]
    n_cores, per_worker, chunk = _sc_layout(n_pairs, n_pairs)
    mesh = plsc.VectorSubcoreMesh(core_axis_name="core", subcore_axis_name="subcore")

    @functools.partial(
        pl.kernel, mesh=mesh, out_type=jax.ShapeDtypeStruct((n_pairs, width), table.dtype),
        scratch_types=[pltpu.VMEM((chunk,), I32), pltpu.VMEM((chunk, width), table.dtype)])
    def gather(table_hbm, dest_hbm, out_hbm, dest_v, rows_v):
        base = (lax.axis_index("subcore") * n_cores + lax.axis_index("core")) * per_worker

        @pl.loop(0, per_worker // chunk)
        def _(j):
            pair = base + j * chunk
            pltpu.sync_copy(dest_hbm.at[pl.ds(pair, chunk)], dest_v)
            pltpu.sync_copy(table_hbm.at[dest_v], rows_v)
            pltpu.sync_copy(rows_v, out_hbm.at[pl.ds(pair, chunk)])

    return gather(table, dest)


def _expert_kernel(be_ref, xs_ref, wg_ref, wu_ref, wd_ref, ys_ref, wg16, wu16, wd16):
    i = pl.program_id(0)
    changed = jnp.logical_or(i == 0, be_ref[i] != be_ref[jnp.maximum(i - 1, 0)])

    @pl.when(changed)
    def _():
        rows = 256
        d = wg16.shape[0]

        def cast_in(r, carry):
            sl = pl.ds(pl.multiple_of(r * rows, rows), rows)
            wg16[sl, :] = wg_ref[0, sl, :].astype(BF16)
            wu16[sl, :] = wu_ref[0, sl, :].astype(BF16)
            return carry

        lax.fori_loop(0, d // rows, cast_in, 0)

        def cast_down(r, carry):
            sl = pl.ds(pl.multiple_of(r * 128, 128), 128)
            wd16[sl, :] = wd_ref[0, sl, :].astype(BF16)
            return carry

        lax.fori_loop(0, D_EXPERT // 128, cast_down, 0)

    n_blocks = pl.num_programs(0)
    used = i < be_ref[n_blocks]

    @pl.when(used)
    def _():
        held = lax.broadcasted_iota(I32, (xs_ref.shape[0], 1), 0) < be_ref[n_blocks + 1 + i]
        hi, lo = _unpack_bf16_pairs(jnp.where(held, xs_ref[...], jnp.uint32(0)))
        x = jnp.concatenate([hi, lo], axis=1).astype(BF16)
        gate = jnp.dot(x, wg16[...], preferred_element_type=F32)
        up = jnp.dot(x, wu16[...], preferred_element_type=F32)
        act = (gate * _sigmoid(gate) * up).astype(BF16)
        ys_ref[...] = _pack_bf16_pairs(jnp.dot(act, wd16[...], preferred_element_type=F32))

    @pl.when(jnp.logical_not(used))
    def _():
        ys_ref[...] = jnp.zeros_like(ys_ref)


def _experts(xs, block_e, layer, w_gate, w_up, w_down):
    n_rows, half = xs.shape
    d = 2 * half
    br = EXPERT_ROWS
    grid_spec = pltpu.PrefetchScalarGridSpec(
        num_scalar_prefetch=1,
        grid=(n_rows // br,),
        in_specs=[
            pl.BlockSpec((br, half), lambda i, be: (i, 0)),
            pl.BlockSpec((None, 1, d, D_EXPERT), lambda i, be: (layer, be[i], 0, 0)),
            pl.BlockSpec((None, 1, d, D_EXPERT), lambda i, be: (layer, be[i], 0, 0)),
            pl.BlockSpec((None, 1, D_EXPERT, d), lambda i, be: (layer, be[i], 0, 0)),
        ],
        out_specs=pl.BlockSpec((br, half), lambda i, be: (i, 0)),
        scratch_shapes=[pltpu.VMEM((d, D_EXPERT), BF16), pltpu.VMEM((d, D_EXPERT), BF16),
                        pltpu.VMEM((D_EXPERT, d), BF16)],
    )
    return pl.pallas_call(
        _expert_kernel,
        grid_spec=grid_spec,
        out_shape=jax.ShapeDtypeStruct((n_rows, half), U32),
        compiler_params=_cparams(("arbitrary",)),
    )(block_e, xs, w_gate, w_up, w_down)


def _shared_kernel(hp_ref, wsg_ref, wsu_ref, wsd_ref, o_ref):
    hi, lo = _unpack_bf16_pairs(hp_ref[...])
    h2 = jnp.concatenate([hi, lo], axis=1).astype(BF16)
    sg = jnp.dot(h2, wsg_ref[...], preferred_element_type=F32)
    su = jnp.dot(h2, wsu_ref[...], preferred_element_type=F32)
    shared = jnp.dot((sg * _sigmoid(sg) * su).astype(BF16), wsd_ref[...], preferred_element_type=F32)
    o_ref[...] = shared.astype(o_ref.dtype)


def _shared_expert(lay, h2p, wsg16, wsu16, wsd16):
    d = lay.d
    tm = lay.row_tile(512)
    return pl.pallas_call(
        _shared_kernel,
        grid=(lay.t // tm,),
        in_specs=[pl.BlockSpec((tm, d // 2), lambda i: (i, 0)),
                  pl.BlockSpec((d, D_EXPERT), lambda i: (0, 0)),
                  pl.BlockSpec((d, D_EXPERT), lambda i: (0, 0)),
                  pl.BlockSpec((D_EXPERT, d), lambda i: (0, 0))],
        out_specs=pl.BlockSpec((tm, d), lambda i: (i, 0)),
        out_shape=jax.ShapeDtypeStruct((lay.t, d), BF16),
        compiler_params=_cparams(("arbitrary",)),
    )(h2p, wsg16, wsu16, wsd16)


def _combine_kernel(yk_ref, wsel_ref, sh_ref, x_ref, gate_ref, fg_ref, *outs, tt, final, n_ctx_tiles):
    shared = sh_ref[...].astype(F32)
    half = yk_ref.shape[2]
    r_hi = jnp.zeros((tt, half), F32)
    r_lo = jnp.zeros((tt, half), F32)
    for k in range(TOP_K):
        y_hi, y_lo = _unpack_bf16_pairs(yk_ref[k])
        wk = wsel_ref[:, k:k + 1]
        r_hi = r_hi + wk * y_hi
        r_lo = r_lo + wk * y_lo
    routed = jnp.concatenate([r_hi, r_lo], axis=1)
    out = x_ref[...] + gate_ref[0] * (routed + shared)
    if not final:
        outs[0][...] = out
        return
    ms = jnp.mean(out * out, axis=-1, keepdims=True)
    out = out * lax.rsqrt(ms + EPS) * fg_ref[...]
    ctx_ref, lat_ref = outs
    is_ctx = pl.program_id(0) < n_ctx_tiles

    @pl.when(is_ctx)
    def _():
        ctx_ref[...] = out

    @pl.when(jnp.logical_not(is_ctx))
    def _():
        lat_ref[...] = out


def _combine(lay, y_by_k, wsel, shared, x, mod_l, final_g, tt, final):
    d = lay.d
    half = d // 2
    nct = lay.tc // tt
    if final:
        out_specs = [pl.BlockSpec((tt, d), lambda i: (jnp.minimum(i, nct - 1), 0)),
                     pl.BlockSpec((tt, d), lambda i: (jnp.maximum(i - nct, 0), 0))]
        out_shape = [jax.ShapeDtypeStruct((lay.tc, d), F32), jax.ShapeDtypeStruct((lay.ts, d), F32)]
    else:
        out_specs = pl.BlockSpec((tt, d), lambda i: (i, 0))
        out_shape = jax.ShapeDtypeStruct((lay.t, d), F32)
    return pl.pallas_call(
        functools.partial(_combine_kernel, tt=tt, final=final, n_ctx_tiles=nct),
        grid=(lay.t // tt,),
        in_specs=[
            pl.BlockSpec((TOP_K, tt, half), lambda i: (0, i, 0)),
            pl.BlockSpec((tt, 8), lambda i: (i, 0)),
            pl.BlockSpec((tt, d), lambda i: (i, 0)),
            pl.BlockSpec((tt, d), lambda i: (i, 0)),
            _mod_spec(lay, tt, 5, d),
            pl.BlockSpec((1, d), lambda i: (0, 0)),
        ],
        out_specs=out_specs,
        out_shape=out_shape,
        compiler_params=_cparams(("arbitrary",)),
    )(y_by_k, wsel, shared, x, mod_l, final_g.reshape(1, d))


def _moe(lay, layer, x, h2p, logits_t, mod_l, b_router, w_e_gate, w_e_up, w_e_down, wsg16, wsu16, wsd16, final_g,
         final):
    t = lay.t
    idx_t, w_t, pos_t, counts = _route(logits_t, b_router)
    br = EXPERT_ROWS
    counts = counts[:, 0].astype(I32)
    padded = (counts + br - 1) // br * br
    pad_end = jnp.cumsum(padded)
    pad_start = pad_end - padded
    n_blocks = -(-(t * TOP_K + N_EXPERTS * (br - 1)) // br)
    n_rows = n_blocks * br
    first_row = jnp.arange(n_blocks, dtype=I32) * br
    block_e = jnp.minimum(jnp.sum((pad_end[None, :] <= first_row[:, None]).astype(I32), axis=1), N_EXPERTS - 1)
    own = block_e[:, None] == jnp.arange(N_EXPERTS, dtype=I32)[None, :]
    rows_end = jnp.sum(jnp.where(own, (pad_start + counts)[None, :], 0), axis=1)
    held = jnp.clip(rows_end - first_row, 0, br)
    block_meta = jnp.concatenate([block_e, pad_end[-1:] // br, held]).astype(I32)
    dest = _dest_rows(idx_t, pos_t, pad_start)[:TOP_K].reshape(TOP_K * t)
    xs = _sc_scatter_rows(h2p, dest, n_rows)
    shared = _shared_expert(lay, h2p, wsg16, wsu16, wsd16)
    ys = _experts(xs, block_meta, layer, w_e_gate, w_e_up, w_e_down)
    y_by_k = _sc_gather_rows(ys, dest).reshape(TOP_K, t, h2p.shape[1])
    return _combine(lay, y_by_k, w_t.T, shared, x, mod_l, final_g, lay.row_tile(256), final)


def kernel(x_prompt, x_sample, c, cache_k, cache_v, state_ssm_re, state_ssm_im, c_ctx, w_ada, b_ada, norm1_g, norm2_g, w_in, w_gates, b_gates, ssm_lam_re, ssm_lam_im, ssm_log_dt, ssm_b_re, ssm_b_im, ssm_c_re, ssm_c_im, ssm_d, ssm_w_glu, conv_w, attn_sink, w_br_ssm, w_br_attn, w_br_conv, w_out, w_router, b_router, w_e_gate, w_e_up, w_e_down, w_s_gate, w_s_up, w_s_down, final_g):
    bc, lc, d = x_prompt.shape
    bs, ls, _ = x_sample.shape
    depth = w_in.shape[0]
    lay = _Layout(bc, lc, bs, ls, d)
    assert 1 + bs <= MOD_ROWS

    x = jnp.concatenate([x_prompt.reshape(lay.tc, d), x_sample.reshape(lay.ts, d)], axis=0)
    cvec = jnp.zeros((MOD_ROWS, d), F32).at[0].set(c_ctx).at[1:1 + bs].set(c)
    mod = _adaln(cvec, w_ada, b_ada).reshape(depth, MOD_ROWS * 6, 1, d)
    rope_cos, rope_sin = _rope_tables(lay, lay.row_tile(1024))
    zeros_state = jnp.zeros((bc, 2, SSM_GROUPS, SSM_STATE), F32)

    ks, vs, s_re, s_im = [], [], [], []
    for l in range(depth):
        mod_l = mod[l]
        h16, y16, kv32 = _inproj(lay, x, mod_l, norm1_g[l], w_in[l].astype(BF16), rope_cos, rope_sin)
        ks.append(kv32[:lay.tc, :KV_WIDTH].reshape(bc, lc, N_KV_HEADS, HEAD_DIM))
        vs.append(kv32[:lay.tc, KV_WIDTH:].reshape(bc, lc, N_KV_HEADS, HEAD_DIM))

        attn = _attention(lay, y16, attn_sink[l].astype(F32), cache_k[:, l], cache_v[:, l])

        u0 = ATTN_WIDTH + 2 * KV_WIDTH
        u = y16[:, u0:u0 + SSM_WIDTH]
        mats = _s5_matrices(ssm_lam_re[l], ssm_lam_im[l], ssm_log_dt[l], ssm_b_re[l], ssm_b_im[l],
                            ssm_c_re[l], ssm_c_im[l], ssm_d[l], 8)
        y_c, f_re, f_im = _s5_scan(u[:lay.tc].reshape(bc, lc, SSM_WIDTH), mats, zeros_state, zeros_state)
        y_s, _, _ = _s5_scan(u[lay.tc:].reshape(bs, ls, SSM_WIDTH), mats, state_ssm_re[:, l], state_ssm_im[:, l])
        s_re.append(f_re)
        s_im.append(f_im)
        y_ssm = jnp.concatenate([y_c.reshape(lay.tc, SSM_WIDTH), y_s.reshape(lay.ts, SSM_WIDTH)], axis=0)

        merged = _merge(lay, h16, y_ssm, attn, y16, conv_w[l], ssm_w_glu[l].astype(BF16),
                        w_gates[l].astype(BF16), b_gates[l].reshape(1, -1), w_br_ssm[l].astype(BF16),
                        w_br_attn[l].astype(BF16), w_br_conv[l].astype(BF16))
        x, h2p, logits_t = _outproj(lay, merged, w_out[l].astype(BF16), x, mod_l, norm2_g[l],
                                    w_router[l])
        x = _moe(lay, l, x, h2p, logits_t, mod_l, b_router[l], w_e_gate, w_e_up, w_e_down,
                 w_s_gate[l].astype(BF16), w_s_up[l].astype(BF16), w_s_down[l].astype(BF16),
                 final_g, l == depth - 1)

    y_prompt = x[0].reshape(bc, lc, d)
    y_sample = x[1].reshape(bs, ls, d)
    return (y_prompt, y_sample, jnp.stack(ks, axis=1), jnp.stack(vs, axis=1),
            jnp.stack(s_re, axis=1), jnp.stack(s_im, axis=1))
```

```python
import functools
import math

import jax
import jax.numpy as jnp
from jax import lax
from jax.experimental import pallas as pl
from jax.experimental.pallas import tpu as pltpu
from jax.experimental.pallas import tpu_sc as plsc

HEAD_DIM = 128
N_HEADS = 8
N_KV_HEADS = 2
GROUP = N_HEADS // N_KV_HEADS
ATTN_WIDTH = N_HEADS * HEAD_DIM
KV_WIDTH = N_KV_HEADS * HEAD_DIM
WINDOW = 128
ATTN_BLOCK = 128
ATTN_Q_ROWS = 256
ATTN_SCALE = HEAD_DIM ** -0.5
LOG2_E = 1.4426950408889634
ROPE_BASE = 10000.0
ROT_F = HEAD_DIM // 4
GRID_W = 64
SSM_WIDTH = 512
SSM_CH = 16
SSM_GROUPS = SSM_WIDTH // SSM_CH
SSM_STATE = 64
SSM_CHUNK = 16
S5_BUNDLE = 8
S5_BLOCK_ROWS = 256
CONV_WIDTH = 512
N_BRANCHES = 3
IN_WIDTH = ATTN_WIDTH + 2 * KV_WIDTH + SSM_WIDTH + 3 * CONV_WIDTH
N_EXPERTS = 64
TOP_K = 6
N_EXPERT_GROUPS = 8
TOPK_GROUPS = 4
D_EXPERT = 512
ROUTED_SCALE = 2.5
EPS = 1e-6
NEG_INF = -1e30

COL_TILE = 512
MOD_ROWS = 16
EXPERT_ROWS = 512
SC_STREAM_ROWS = 64
VMEM_LIMIT_V7X = 56 * 1024 * 1024

F32 = jnp.float32
BF16 = jnp.bfloat16
I32 = jnp.int32
U32 = jnp.uint32


def _cparams(sem, vmem=VMEM_LIMIT_V7X):
    return pltpu.CompilerParams(dimension_semantics=sem, vmem_limit_bytes=vmem)


def _sigmoid(x):
    return 1.0 / (1.0 + jnp.exp(-x))


def _pack_bf16_pairs(v):
    n = v.shape[1] // 2
    hi = lax.bitcast_convert_type(v[:, :n].astype(BF16).astype(F32), U32)
    lo = lax.bitcast_convert_type(v[:, n:].astype(BF16).astype(F32), U32)
    return hi | (lo >> 16)


def _unpack_bf16_pairs(p):
    hi = lax.bitcast_convert_type(p & jnp.uint32(0xFFFF0000), F32)
    lo = lax.bitcast_convert_type(p << 16, F32)
    return hi, lo


def _adaln_kernel(c_ref, w_ref, b_ref, o_ref):
    c = c_ref[...]
    s = (c * _sigmoid(c)).astype(BF16)
    o_ref[0] = jnp.dot(s, w_ref[0].astype(BF16), preferred_element_type=F32) + b_ref[0]


def _adaln(cvec, w_ada, b_ada):
    depth, d, n6 = w_ada.shape
    tn = math.gcd(1024, n6)
    return pl.pallas_call(
        _adaln_kernel,
        grid=(depth, n6 // tn),
        in_specs=[
            pl.BlockSpec((MOD_ROWS, d), lambda l, n: (0, 0)),
            pl.BlockSpec((1, d, tn), lambda l, n: (l, 0, n)),
            pl.BlockSpec((1, 1, tn), lambda l, n: (l, 0, n)),
        ],
        out_specs=pl.BlockSpec((1, MOD_ROWS, tn), lambda l, n: (l, 0, n)),
        out_shape=jax.ShapeDtypeStruct((depth, MOD_ROWS, n6), F32),
        compiler_params=_cparams(("arbitrary", "arbitrary")),
    )(cvec, w_ada, b_ada.reshape(depth, 1, n6))


class _Layout:
    def __init__(self, n_ctx_seq, len_ctx, n_lat_seq, len_lat, d_model):
        self.bc, self.lc, self.bs, self.ls, self.d = n_ctx_seq, len_ctx, n_lat_seq, len_lat, d_model
        self.tc = n_ctx_seq * len_ctx
        self.ts = n_lat_seq * len_lat
        self.t = self.tc + self.ts

    def row_tile(self, want):
        tm = math.gcd(math.gcd(self.tc, self.ls), want)
        assert tm % 16 == 0
        return tm

    def mod_index(self, i, tm):
        nct, tps = self.tc // tm, self.ls // tm
        return jnp.where(i < nct, 0, 1 + (i - nct) // tps)

    def seq_pos(self, rows, i, tm):
        is_lat = i >= self.tc // tm
        return jnp.where(is_lat, (rows - self.tc) % self.ls, rows % self.lc), jnp.where(is_lat, self.ls, self.lc)


def _mod_spec(lay, tm, slot, d):
    return pl.BlockSpec((1, 1, d), lambda i, *_: (lay.mod_index(i, tm) * 6 + slot, 0, 0))


def _rope(z, cos, sin_signed, first_half):
    swapped = jnp.where(first_half, pltpu.roll(z, HEAD_DIM - ROT_F, 1), pltpu.roll(z, ROT_F, 1))
    return z * cos + swapped * sin_signed


def _inproj_kernel(x_ref, shift_ref, scale_ref, g_ref, w_ref, cos_ref, sin_ref, h_ref, y_ref, kv_ref):
    n = pl.program_id(1)
    n_q = ATTN_WIDTH // COL_TILE

    @pl.when(n == 0)
    def _():
        x = x_ref[...]
        ms = jnp.mean(x * x, axis=-1, keepdims=True)
        y = x * lax.rsqrt(ms + EPS) * g_ref[...]
        h_ref[...] = (y * (1.0 + scale_ref[0]) + shift_ref[0]).astype(BF16)

    acc = jnp.dot(h_ref[...], w_ref[...], preferred_element_type=F32)

    def rotated(n_heads):
        cos, sin = cos_ref[...], sin_ref[...]
        first_half = (lax.broadcasted_iota(I32, cos.shape, 1) % (2 * ROT_F)) < ROT_F
        parts = [_rope(acc[:, s * HEAD_DIM:(s + 1) * HEAD_DIM], cos, sin, first_half) for s in range(n_heads)]
        parts.append(acc[:, n_heads * HEAD_DIM:])
        return jnp.concatenate(parts, axis=1) if n_heads * HEAD_DIM < COL_TILE else jnp.concatenate(parts[:-1], axis=1)

    @pl.when(n < n_q)
    def _():
        y_ref[...] = (rotated(COL_TILE // HEAD_DIM) * (ATTN_SCALE * LOG2_E)).astype(BF16)

    @pl.when(n == n_q)
    def _():
        kv_ref[...] = acc
        y_ref[...] = rotated(N_KV_HEADS).astype(BF16)

    @pl.when(n > n_q)
    def _():
        y_ref[...] = acc.astype(BF16)


def _inproj(lay, x, mod_l, g1, w_in16, rope_cos, rope_sin):
    d = lay.d
    tm = lay.row_tile(1024)
    nct, tps = lay.tc // tm, lay.ls // tm

    def rope_idx(i, n):
        return (jnp.where(i < nct, 0, 1 + (i - nct) % tps), 0)

    return pl.pallas_call(
        _inproj_kernel,
        grid=(lay.t // tm, IN_WIDTH // COL_TILE),
        in_specs=[
            pl.BlockSpec((tm, d), lambda i, n: (i, 0)),
            _mod_spec(lay, tm, 0, d),
            _mod_spec(lay, tm, 1, d),
            pl.BlockSpec((1, d), lambda i, n: (0, 0)),
            pl.BlockSpec((d, COL_TILE), lambda i, n: (0, n)),
            pl.BlockSpec((tm, HEAD_DIM), rope_idx),
            pl.BlockSpec((tm, HEAD_DIM), rope_idx),
        ],
        out_specs=[
            pl.BlockSpec((tm, d), lambda i, n: (i, 0)),
            pl.BlockSpec((tm, COL_TILE), lambda i, n: (i, n)),
            pl.BlockSpec((tm, 2 * KV_WIDTH), lambda i, n: (i, 0)),
        ],
        out_shape=[
            jax.ShapeDtypeStruct((lay.t, d), BF16),
            jax.ShapeDtypeStruct((lay.t, IN_WIDTH), BF16),
            jax.ShapeDtypeStruct((lay.t, 2 * KV_WIDTH), F32),
        ],
        compiler_params=_cparams(("arbitrary", "arbitrary")),
    )(x, mod_l, mod_l, g1.reshape(1, d), w_in16, rope_cos, rope_sin)


def _rope_tables(lay, tm):
    t = jnp.arange(lay.ls)
    row = (t // GRID_W).astype(F32)
    col = (t % GRID_W).astype(F32)
    inv = ROPE_BASE ** (-jnp.arange(ROT_F, dtype=F32) / ROT_F)
    ar, ac = row[:, None] * inv, col[:, None] * inv
    cos = jnp.concatenate([jnp.cos(ar), jnp.cos(ar), jnp.cos(ac), jnp.cos(ac)], axis=1)
    sin = jnp.concatenate([-jnp.sin(ar), jnp.sin(ar), -jnp.sin(ac), jnp.sin(ac)], axis=1)
    cos = jnp.concatenate([jnp.ones((tm, HEAD_DIM), F32), cos], axis=0)
    sin = jnp.concatenate([jnp.zeros((tm, HEAD_DIM), F32), sin], axis=0)
    return cos, sin


def _attend(q, sink_ref, j, parts):
    nq = q.shape[0]
    q4 = jnp.concatenate([q[:, g * HEAD_DIM:(g + 1) * HEAD_DIM] for g in range(GROUP)], axis=0)
    sink = jnp.concatenate([jnp.full((nq, 1), sink_ref[j * GROUP + g] * LOG2_E, F32) for g in range(GROUP)], axis=0)
    scores = []
    m = sink
    for k, _, mask in parts:
        s = lax.dot_general(q4, k, (((1,), (1,)), ((), ())), preferred_element_type=F32)
        if mask is not None:
            s = jnp.where(mask, s, NEG_INF)
        scores.append(s)
        m = jnp.maximum(m, jnp.max(s, axis=-1, keepdims=True))
    den = jnp.exp2(sink - m)
    out = jnp.zeros((GROUP * nq, HEAD_DIM), F32)
    for s, (_, v, _) in zip(scores, parts):
        p = jnp.exp2(s - m)
        den = den + jnp.sum(p, axis=-1, keepdims=True)
        out = out + jnp.dot(p.astype(BF16), v, preferred_element_type=F32)
    out = out / den
    return jnp.concatenate([out[g * nq:(g + 1) * nq] for g in range(GROUP)], axis=1)


def _head(x, j, width=HEAD_DIM):
    return x[:, j * width:(j + 1) * width]


def _attn_ctx_kernel(sink_ref, q_ref, k_ref, v_ref, o_ref):
    q, k, v = q_ref[...], k_ref[...], v_ref[...]
    outs = [_attend(_head(q, j, GROUP * HEAD_DIM), sink_ref, j, [(_head(k, j), _head(v, j), None)])
            for j in range(N_KV_HEADS)]
    o_ref[...] = jnp.concatenate(outs, axis=1).astype(o_ref.dtype)


def _attn_lat_kernel(sink_ref, q_ref, kp_ref, kc_ref, kn_ref, vp_ref, vc_ref, vn_ref, ck_ref, cv_ref, ctx_out_ref,
                     o_ref, *, seq_len):
    del ctx_out_ref
    i = pl.program_id(1)
    q = q_ref[...]
    kw = jnp.concatenate([kp_ref[...], kc_ref[...], kn_ref[...]], axis=0)
    vw = jnp.concatenate([vp_ref[...], vc_ref[...], vn_ref[...]], axis=0)
    ck = ck_ref[...].astype(BF16)
    cv = cv_ref[...].astype(BF16)
    shape = (GROUP * ATTN_Q_ROWS, ATTN_Q_ROWS + 2 * ATTN_BLOCK)
    qoff = lax.broadcasted_iota(I32, shape, 0) % ATTN_Q_ROWS
    koff = lax.broadcasted_iota(I32, shape, 1) - ATTN_BLOCK
    kabs = koff + i * ATTN_Q_ROWS
    mask = (jnp.abs(qoff - koff) <= WINDOW) & (kabs >= 0) & (kabs < seq_len)
    outs = [_attend(_head(q, j, GROUP * HEAD_DIM), sink_ref, j,
                    [(_head(kw, j), _head(vw, j), mask), (_head(ck, j), _head(cv, j), None)])
            for j in range(N_KV_HEADS)]
    o_ref[...] = jnp.concatenate(outs, axis=1).astype(o_ref.dtype)


def _attention(lay, y16, sink, cache_k_l, cache_v_l):
    smem = pl.BlockSpec(memory_space=pltpu.SMEM)
    kcol, vcol = ATTN_WIDTH // KV_WIDTH, (ATTN_WIDTH + KV_WIDTH) // KV_WIDTH
    ctx = pl.pallas_call(
        _attn_ctx_kernel,
        grid=(lay.bc,),
        in_specs=[
            smem,
            pl.BlockSpec((lay.lc, ATTN_WIDTH), lambda b: (b, 0)),
            pl.BlockSpec((lay.lc, KV_WIDTH), lambda b: (b, kcol)),
            pl.BlockSpec((lay.lc, KV_WIDTH), lambda b: (b, vcol)),
        ],
        out_specs=pl.BlockSpec((lay.lc, ATTN_WIDTH), lambda b: (b, 0)),
        out_shape=jax.ShapeDtypeStruct((lay.t, ATTN_WIDTH), BF16),
        compiler_params=_cparams(("arbitrary",)),
    )(sink, y16, y16, y16)

    qr = ATTN_Q_ROWS
    assert lay.ls % qr == 0 and lay.tc % qr == 0 and qr % ATTN_BLOCK == 0
    nblk = lay.ls // qr
    base = lay.tc // qr
    per = qr // ATTN_BLOCK
    last = lay.t // ATTN_BLOCK - 1
    past = cache_k_l.shape[1]

    def rb(b, i):
        return base + b * nblk + i

    def edge(col, delta):
        return pl.BlockSpec((ATTN_BLOCK, KV_WIDTH), lambda b, i: (jnp.clip(rb(b, i) * per + delta, 0, last), col))

    def own(col):
        return pl.BlockSpec((qr, KV_WIDTH), lambda b, i: (rb(b, i), col))

    cspec = pl.BlockSpec((None, past, KV_WIDTH), lambda b, i: (b, 0, 0))
    return pl.pallas_call(
        functools.partial(_attn_lat_kernel, seq_len=lay.ls),
        grid=(lay.bs, nblk),
        in_specs=[
            smem,
            pl.BlockSpec((qr, ATTN_WIDTH), lambda b, i: (rb(b, i), 0)),
            edge(kcol, -1), own(kcol), edge(kcol, per),
            edge(vcol, -1), own(vcol), edge(vcol, per),
            cspec, cspec,
            pl.BlockSpec(memory_space=pl.ANY),
        ],
        out_specs=pl.BlockSpec((qr, ATTN_WIDTH), lambda b, i: (rb(b, i), 0)),
        out_shape=jax.ShapeDtypeStruct((lay.t, ATTN_WIDTH), BF16),
        input_output_aliases={10: 0},
        compiler_params=_cparams(("arbitrary", "arbitrary")),
    )(sink, y16, y16, y16, y16, y16, y16, y16,
      cache_k_l.reshape(lay.bs, past, KV_WIDTH), cache_v_l.reshape(lay.bs, past, KV_WIDTH), ctx)


def _s5_matrices(lam_re, lam_im, log_dt, b_re, b_im, c_re, c_im, d_skip, n_steps):
    q, p, g, n = SSM_CHUNK, SSM_CH, SSM_GROUPS, SSM_STATE
    lam = lax.complex(lam_re.astype(F32), lam_im.astype(F32))
    dt = jnp.exp(log_dt.astype(F32))[..., None]
    lam_dt = lam * dt
    lam_bar = jnp.exp(lam_dt)
    b_bar = ((lam_bar - 1.0) / lam)[..., None] * lax.complex(b_re.astype(F32), b_im.astype(F32))
    c_mat = lax.complex(c_re.astype(F32), c_im.astype(F32))
    steps = jnp.arange(q + 1, dtype=F32)
    pw = jnp.exp(lam_dt[:, None] * steps[None, :, None, None])
    kern = jnp.real(jnp.einsum('dgpn,dkgn,dgnr->dkgpr', c_mat, pw[:, :q], b_bar))
    tau_in = jnp.arange(q)[:, None]
    tau_out = jnp.arange(q)[None, :]
    lag_f = tau_out - tau_in
    lag_b = tau_in - tau_out
    lags = jnp.arange(q)
    pick_f = (lag_f[:, :, None] == lags).astype(F32)
    pick_b = (lag_b[:, :, None] == lags).astype(F32)
    exact = lax.Precision.HIGHEST
    kf = jnp.einsum('abk,kgpr->abgpr', pick_f, kern[0], precision=exact)
    kb = jnp.einsum('abk,kgpr->abgpr', pick_b, kern[1], precision=exact)
    m = (kf + kb).transpose(2, 0, 4, 1, 3)
    eye_q = jnp.eye(q, dtype=F32)[None, :, None, :, None]
    eye_p = jnp.eye(p, dtype=F32)[None, None, :, None, :]
    m = m + eye_q * eye_p * d_skip.astype(F32).reshape(g, 1, p, 1, 1)
    m = m.reshape(g, q * p, q * p)
    ws_f = pw[0, :q][::-1][:, :, :, None] * b_bar[0][None]
    ws_b = pw[1, :q][:, :, :, None] * b_bar[1][None]

    def cols(w):
        return w.transpose(1, 0, 3, 2).reshape(g, q * p, n)

    w1 = jnp.concatenate([m, jnp.real(cols(ws_f)), jnp.real(cols(ws_b)),
                          jnp.imag(cols(ws_f)), jnp.imag(cols(ws_b))], axis=2)
    cy_f = c_mat[0][None] * pw[0, 1:][:, :, None, :]
    cy_b = c_mat[1][None] * pw[1, 1:][::-1][:, :, None, :]

    def rows(w):
        return w.transpose(1, 3, 0, 2).reshape(g, n, q * p)

    wy = jnp.concatenate([jnp.real(rows(cy_f)), jnp.real(rows(cy_b)),
                          -jnp.imag(rows(cy_f)), -jnp.imag(rows(cy_b))], axis=1)
    hops = (q * 2.0 ** jnp.arange(8, dtype=F32))[None, :, None, None]
    a = jnp.exp(lam_dt[:, None] * hops)
    a = jnp.concatenate([a[0], a[1]], axis=-1).transpose(1, 0, 2)
    assert n_steps <= 8
    gb, nbun = S5_BUNDLE, g // S5_BUNDLE
    lane = jnp.arange(q * gb * p)
    target = ((lane % (gb * p)) // p) * (q * p) + (lane // (gb * p)) * p + lane % p
    perm = (target[:, None] == lane[None, :]).astype(BF16)
    ab = a.reshape(nbun, gb, 8, 2 * n).transpose(0, 2, 1, 3).reshape(nbun, 8, gb * 2 * n)
    return (perm, w1.astype(BF16).reshape(nbun, gb, q * p, q * p + 4 * n),
            wy.astype(BF16).reshape(nbun, gb, 4 * n, q * p), jnp.real(ab), jnp.imag(ab))


def _s5_kernel(u_ref, perm_ref, w1_ref, wy_ref, are_ref, aim_ref, h0re_ref, h0im_ref, *rest, nb, nc):
    y_ref, fre_ref, fim_ref = rest[-3:]
    r_tot = nb * nc
    w = SSM_CHUNK * SSM_CH
    ns = 2 * SSM_STATE
    n2 = S5_BUNDLE * ns
    perm = perm_ref[...]
    x = jnp.dot(u_ref[0], perm, preferred_element_type=F32).astype(BF16)
    proj = [jnp.dot(x[:, g * w:(g + 1) * w], w1_ref[0, g], preferred_element_type=F32) for g in range(S5_BUNDLE)]
    d_re = jnp.concatenate([pg[:, w:w + ns] for pg in proj], axis=1)
    d_im = jnp.concatenate([pg[:, w + ns:] for pg in proj], axis=1)
    h0re_ref, h0im_ref, fre_ref, fim_ref = (r.at[0] for r in (h0re_ref, h0im_ref, fre_ref, fim_ref))
    row = lax.broadcasted_iota(I32, (r_tot, n2), 0)
    chunk = row % nc
    seq = row // nc
    fwd = (lax.broadcasted_iota(I32, (r_tot, n2), 1) % (2 * SSM_STATE)) < SSM_STATE

    def previous(x, dist):
        valid = (fwd & (chunk >= dist)) | (~fwd & (chunk < nc - dist))
        moved = jnp.where(fwd, pltpu.roll(x, dist, 0), pltpu.roll(x, r_tot - dist, 0))
        return jnp.where(valid, moved, 0.0)

    h0_re = jnp.zeros((r_tot, n2), F32)
    h0_im = jnp.zeros((r_tot, n2), F32)
    for b in range(nb):
        h0_re = jnp.where(seq == b, h0re_ref[0, b:b + 1, :], h0_re)
        h0_im = jnp.where(seq == b, h0im_ref[0, b:b + 1, :], h0_im)
    first = (fwd & (chunk == 0)) | (~fwd & (chunk == nc - 1))
    e_re = jnp.where(first, h0_re, previous(d_re, 1))
    e_im = jnp.where(first, h0_im, previous(d_im, 1))
    k = 0
    while (1 << k) < nc:
        a_re = are_ref[0, k:k + 1, :]
        a_im = aim_ref[0, k:k + 1, :]
        p_re = previous(e_re, 1 << k)
        p_im = previous(e_im, 1 << k)
        e_re, e_im = e_re + a_re * p_re - a_im * p_im, e_im + a_re * p_im + a_im * p_re
        k += 1
    ys = []
    for g in range(S5_BUNDLE):
        e_g = jnp.concatenate([e_re[:, g * ns:(g + 1) * ns], e_im[:, g * ns:(g + 1) * ns]], axis=1).astype(BF16)
        ys.append(proj[g][:, :w] + jnp.dot(e_g, wy_ref[0, g], preferred_element_type=F32))
    y = jnp.concatenate(ys, axis=1)
    y_hi = y.astype(BF16)
    y_lo = (y - y_hi.astype(F32)).astype(BF16)
    back = (((1,), (1,)), ((), ()))
    y_ref[0] = (lax.dot_general(y_hi, perm, back, preferred_element_type=F32)
                + lax.dot_general(y_lo, perm, back, preferred_element_type=F32))
    a_re = are_ref[0, 0:1, :]
    a_im = aim_ref[0, 0:1, :]
    f_re = a_re * e_re - a_im * e_im + d_re
    f_im = a_re * e_im + a_im * e_re + d_im
    fwd_row = fwd[0:1, :]
    for b in range(nb):
        lo, hi = b * nc, b * nc + nc - 1
        fre_ref[0, b:b + 1, :] = jnp.where(fwd_row, f_re[hi:hi + 1, :], f_re[lo:lo + 1, :])
        fim_ref[0, b:b + 1, :] = jnp.where(fwd_row, f_im[hi:hi + 1, :], f_im[lo:lo + 1, :])


def _s5_scan(ub, y_prev, row0, nb, nc, mats, h0_re, h0_im):
    g, q, p, n = SSM_GROUPS, SSM_CHUNK, SSM_CH, SSM_STATE
    gb, nbun = S5_BUNDLE, SSM_GROUPS // S5_BUNDLE
    assert nc & (nc - 1) == 0 and nc % 8 == 0
    seqs = max(1, min(nb, S5_BLOCK_ROWS // nc))
    while nb % seqs or row0 % (seqs * nc):
        seqs -= 1
    n_rb, rows = nb // seqs, seqs * nc
    rb0 = row0 // rows
    kw, sw = q * gb * p, gb * 2 * n
    perm, w1, wy, a_re, a_im = mats

    def lanes(h):
        h = h.astype(F32).reshape(n_rb, seqs, 2, nbun, gb, n).transpose(3, 0, 1, 4, 2, 5)
        return h.reshape(nbun, n_rb, seqs, sw)

    weight = lambda r, c: pl.BlockSpec((1, gb, r, c), lambda o, i: (o, 0, 0, 0))
    coeff = pl.BlockSpec((1, 8, sw), lambda o, i: (o, 0, 0))
    state = pl.BlockSpec((1, 1, seqs, sw), lambda o, i: (o, i, 0, 0))
    chunk_rows = pl.BlockSpec((1, rows, kw), lambda o, i: (o, rb0 + i, 0))
    in_specs = [chunk_rows,
                pl.BlockSpec((kw, kw), lambda o, i: (0, 0), pipeline_mode=pl.Buffered(1)),
                weight(q * p, q * p + 4 * n), weight(4 * n, q * p), coeff, coeff, state, state]
    args = [ub, perm, w1, wy, a_re, a_im, lanes(h0_re), lanes(h0_im)]
    aliases = {}
    if y_prev is not None:
        in_specs.append(pl.BlockSpec(memory_space=pl.ANY))
        args.append(y_prev)
        aliases = {len(args) - 1: 0}
    y, f_re, f_im = pl.pallas_call(
        functools.partial(_s5_kernel, nb=seqs, nc=nc),
        grid=(nbun, n_rb),
        in_specs=in_specs,
        out_specs=[chunk_rows, state, state],
        out_shape=[jax.ShapeDtypeStruct(ub.shape, F32),
                   jax.ShapeDtypeStruct((nbun, n_rb, seqs, sw), F32),
                   jax.ShapeDtypeStruct((nbun, n_rb, seqs, sw), F32)],
        input_output_aliases=aliases,
        compiler_params=_cparams(("arbitrary", "arbitrary")),
    )(*args)

    def unlanes(f):
        f = f.reshape(nbun, n_rb, seqs, gb, 2, n).transpose(1, 2, 4, 0, 3, 5)
        return f.reshape(nb, 2, g, n)

    return y, unlanes(f_re), unlanes(f_im)


def _s5_both(lay, u, mats, h0_re, h0_im):
    q, nbun, lanes = SSM_CHUNK, SSM_GROUPS // S5_BUNDLE, S5_BUNDLE * SSM_CH
    n_rows = lay.t // q
    ub = u.reshape(n_rows, q, nbun, lanes).transpose(2, 0, 1, 3).reshape(nbun, n_rows, q * lanes)
    zeros = jnp.zeros((lay.bc, 2, SSM_GROUPS, SSM_STATE), F32)
    y, f_re, f_im = _s5_scan(ub, None, 0, lay.bc, lay.lc // q, mats, zeros, zeros)
    y, _, _ = _s5_scan(ub, y, lay.tc // q, lay.bs, lay.ls // q, mats, h0_re, h0_im)
    y = y.reshape(nbun, n_rows, q, lanes).transpose(1, 2, 0, 3).reshape(lay.t, SSM_WIDTH)
    return y, f_re, f_im


def _merge_kernel(h_ref, ys_ref, at_ref, gb_ref, gc_ref, uc_ref, gcp_ref, ucp_ref, gcn_ref, ucn_ref, cw_ref,
                  wglu_ref, wg0_ref, wg1_ref, wg2_ref, bg0_ref, bg1_ref, bg2_ref, ws_ref, wa_ref, wc_ref,
                  o_ref, ssm_scr, conv_scr, *, lay, tm, halo):
    i = pl.program_id(0)
    n = pl.program_id(1)

    @pl.when(n == 0)
    def _():
        y = ys_ref[...]
        ge = 0.5 * y * (1.0 + jnp.tanh(math.sqrt(2.0 / math.pi) * (y + 0.044715 * (y * y * y))))
        glu = jnp.dot(ge.astype(BF16), wglu_ref[...], preferred_element_type=F32)
        ssm_scr[...] = (ge * _sigmoid(glu)).astype(BF16)

        z = gc_ref[...].astype(F32) * uc_ref[...].astype(F32)
        z_before = gcp_ref[halo - 1:halo, :].astype(F32) * ucp_ref[halo - 1:halo, :].astype(F32)
        z_after = gcn_ref[0:1, :].astype(F32) * ucn_ref[0:1, :].astype(F32)
        local = lax.broadcasted_iota(I32, z.shape, 0)
        pos, seq_len = lay.seq_pos(local + i * tm, i, tm)
        z_prev = jnp.where(local == 0, z_before, pltpu.roll(z, 1, 0))
        z_prev = jnp.where(pos == 0, 0.0, z_prev)
        z_next = jnp.where(local == tm - 1, z_after, pltpu.roll(z, tm - 1, 0))
        z_next = jnp.where(pos == seq_len - 1, 0.0, z_next)
        conv = cw_ref[0:1, :] * z_prev + cw_ref[1:2, :] * z + cw_ref[2:3, :] * z_next
        conv_scr[...] = (gb_ref[...].astype(F32) * conv).astype(BF16)

    h = h_ref[...]
    acc = None
    for act, wg_ref, bg_ref, wb_ref in ((ssm_scr[...], wg0_ref, bg0_ref, ws_ref),
                                        (at_ref[...], wg1_ref, bg1_ref, wa_ref),
                                        (conv_scr[...], wg2_ref, bg2_ref, wc_ref)):
        gate = _sigmoid(jnp.dot(h, wg_ref[...], preferred_element_type=F32) + bg_ref[...])
        term = gate * jnp.dot(act, wb_ref[...], preferred_element_type=F32)
        acc = term if acc is None else acc + term
    o_ref[...] = acc.astype(BF16)


def _merge(lay, h16, y_ssm, attn, y16, conv_w, wglu16, wgates16, b_gates, wbs16, wba16, wbc16):
    d = lay.d
    tm = lay.row_tile(512)
    tn = min(COL_TILE, d)
    nd = d // tn
    halo = 16
    hb = tm // halo
    last_h = lay.t // halo - 1
    c0 = (ATTN_WIDTH + 2 * KV_WIDTH + SSM_WIDTH) // CONV_WIDTH
    row = lambda cb: pl.BlockSpec((tm, CONV_WIDTH), lambda i, n: (i, cb))
    before = lambda cb: pl.BlockSpec((halo, CONV_WIDTH), lambda i, n: (jnp.maximum(i * hb - 1, 0), cb))
    after = lambda cb: pl.BlockSpec((halo, CONV_WIDTH), lambda i, n: (jnp.minimum((i + 1) * hb, last_h), cb))
    gate_w = lambda br: pl.BlockSpec((d, tn), lambda i, n: (0, br * nd + n))
    gate_b = lambda br: pl.BlockSpec((1, tn), lambda i, n: (0, br * nd + n))
    return pl.pallas_call(
        functools.partial(_merge_kernel, lay=lay, tm=tm, halo=halo),
        grid=(lay.t // tm, nd),
        in_specs=[
            pl.BlockSpec((tm, d), lambda i, n: (i, 0)),
            pl.BlockSpec((tm, SSM_WIDTH), lambda i, n: (i, 0)),
            pl.BlockSpec((tm, ATTN_WIDTH), lambda i, n: (i, 0)),
            row(c0), row(c0 + 1), row(c0 + 2),
            before(c0 + 1), before(c0 + 2), after(c0 + 1), after(c0 + 2),
            pl.BlockSpec((3, CONV_WIDTH), lambda i, n: (0, 0)),
            pl.BlockSpec((SSM_WIDTH, SSM_WIDTH), lambda i, n: (0, 0)),
            gate_w(0), gate_w(1), gate_w(2), gate_b(0), gate_b(1), gate_b(2),
            pl.BlockSpec((SSM_WIDTH, tn), lambda i, n: (0, n)),
            pl.BlockSpec((ATTN_WIDTH, tn), lambda i, n: (0, n)),
            pl.BlockSpec((CONV_WIDTH, tn), lambda i, n: (0, n)),
        ],
        out_specs=pl.BlockSpec((tm, tn), lambda i, n: (i, n)),
        out_shape=jax.ShapeDtypeStruct((lay.t, d), BF16),
        scratch_shapes=[pltpu.VMEM((tm, SSM_WIDTH), BF16), pltpu.VMEM((tm, CONV_WIDTH), BF16)],
        compiler_params=_cparams(("arbitrary", "arbitrary")),
    )(h16, y_ssm, attn, y16, y16, y16, y16, y16, y16, y16, conv_w, wglu16,
      wgates16, wgates16, wgates16, b_gates, b_gates, b_gates, wbs16, wba16, wbc16)


def _outproj_kernel(m_ref, w_ref, x_ref, gate_ref, g2_ref, shift_ref, scale_ref, wrh_ref, wrl_ref,
                    xo_ref, hp_ref, lg_ref):
    acc = jnp.dot(m_ref[...], w_ref[...], preferred_element_type=F32)
    xn = x_ref[...] + gate_ref[0] * acc
    xo_ref[...] = xn
    ms = jnp.mean(xn * xn, axis=-1, keepdims=True)
    h2 = xn * lax.rsqrt(ms + EPS) * g2_ref[...]
    h2 = h2 * (1.0 + scale_ref[0]) + shift_ref[0]
    hp_ref[...] = _pack_bf16_pairs(h2)
    h_hi = h2.astype(BF16)
    h_lo = (h2 - h_hi.astype(F32)).astype(BF16)
    logits = (jnp.dot(h_hi, wrh_ref[...], preferred_element_type=F32)
              + jnp.dot(h_hi, wrl_ref[...], preferred_element_type=F32)
              + jnp.dot(h_lo, wrh_ref[...], preferred_element_type=F32))
    lg_ref[...] = logits.T[:N_EXPERTS, :]


def _outproj(lay, merged, wout16, x, mod_l, g2, w_router):
    d = lay.d
    tm = lay.row_tile(256)
    lanes = 128
    wr = jnp.zeros((d, lanes), F32).at[:, :N_EXPERTS].set(w_router.astype(F32))
    wr_hi = wr.astype(BF16)
    wr_lo = (wr - wr_hi.astype(F32)).astype(BF16)
    return pl.pallas_call(
        _outproj_kernel,
        grid=(lay.t // tm,),
        in_specs=[
            pl.BlockSpec((tm, d), lambda i: (i, 0)),
            pl.BlockSpec((d, d), lambda i: (0, 0)),
            pl.BlockSpec((tm, d), lambda i: (i, 0)),
            _mod_spec(lay, tm, 2, d),
            pl.BlockSpec((1, d), lambda i: (0, 0)),
            _mod_spec(lay, tm, 3, d),
            _mod_spec(lay, tm, 4, d),
            pl.BlockSpec((d, lanes), lambda i: (0, 0)),
            pl.BlockSpec((d, lanes), lambda i: (0, 0)),
        ],
        out_specs=[
            pl.BlockSpec((tm, d), lambda i: (i, 0)),
            pl.BlockSpec((tm, d // 2), lambda i: (i, 0)),
            pl.BlockSpec((N_EXPERTS, tm), lambda i: (0, i)),
        ],
        out_shape=[
            jax.ShapeDtypeStruct((lay.t, d), F32),
            jax.ShapeDtypeStruct((lay.t, d // 2), U32),
            jax.ShapeDtypeStruct((N_EXPERTS, lay.t), F32),
        ],
        compiler_params=_cparams(("arbitrary",)),
    )(merged, wout16, x, mod_l, g2.reshape(1, d), mod_l, mod_l, wr_hi, wr_lo)


def _route_kernel(lg_ref, br_ref, idx_ref, w_ref, pos_ref, cnt_ref, carry):
    step = pl.program_id(0)
    tt = lg_ref.shape[1]
    per_group = N_EXPERTS // N_EXPERT_GROUPS

    @pl.when(step == 0)
    def _():
        carry[...] = jnp.zeros_like(carry)

    scores = _sigmoid(lg_ref[...])
    biased = scores + br_ref[...]
    sub = lax.broadcasted_iota(I32, (per_group, tt), 0).astype(F32)
    blocks, group_score = [], []
    for g in range(N_EXPERT_GROUPS):
        blk = biased[g * per_group:(g + 1) * per_group, :]
        m1 = jnp.max(blk, axis=0, keepdims=True)
        i1 = jnp.min(jnp.where(blk == m1, sub, float(per_group)), axis=0, keepdims=True)
        m2 = jnp.max(jnp.where(sub == i1, -jnp.inf, blk), axis=0, keepdims=True)
        blocks.append(blk)
        group_score.append(m1 + m2)
    masked = []
    for g in range(N_EXPERT_GROUPS):
        beaten_by = jnp.zeros((1, tt), F32)
        for o in range(N_EXPERT_GROUPS):
            if o == g:
                continue
            wins = (group_score[o] > group_score[g]) | ((group_score[o] == group_score[g]) & (o < g))
            beaten_by = beaten_by + wins.astype(F32)
        masked.append(jnp.where(beaten_by < TOPK_GROUPS, blocks[g], -jnp.inf))
    masked = jnp.concatenate(masked, axis=0)
    eid = lax.broadcasted_iota(I32, (N_EXPERTS, tt), 0).astype(F32)
    chosen, weights = [], []
    onehot = jnp.zeros((N_EXPERTS, tt), F32)
    for _ in range(TOP_K):
        m = jnp.max(masked, axis=0, keepdims=True)
        e = jnp.min(jnp.where(masked == m, eid, float(N_EXPERTS)), axis=0, keepdims=True)
        hit = eid == e
        chosen.append(e)
        weights.append(jnp.sum(jnp.where(hit, scores, 0.0), axis=0, keepdims=True))
        onehot = onehot + hit.astype(F32)
        masked = jnp.where(hit, -jnp.inf, masked)
    total = weights[0]
    for wk in weights[1:]:
        total = total + wk
    earlier = (lax.broadcasted_iota(I32, (tt, tt), 0) < lax.broadcasted_iota(I32, (tt, tt), 1)).astype(BF16)
    rank = carry[...][:, 0:1] + jnp.dot(onehot.astype(BF16), earlier, preferred_element_type=F32)
    for k in range(TOP_K):
        idx_ref[k:k + 1, :] = chosen[k].astype(I32)
        w_ref[k:k + 1, :] = weights[k] / total * ROUTED_SCALE
        pos_ref[k:k + 1, :] = jnp.sum(jnp.where(eid == chosen[k], rank, 0.0), axis=0, keepdims=True).astype(I32)
    for k in range(TOP_K, 8):
        idx_ref[k:k + 1, :] = jnp.zeros((1, tt), I32)
        w_ref[k:k + 1, :] = jnp.zeros((1, tt), F32)
        pos_ref[k:k + 1, :] = jnp.zeros((1, tt), I32)
    carry[...] = carry[...] + jnp.sum(onehot, axis=1, keepdims=True)
    cnt_ref[...] = carry[...]


def _route(logits_t, b_router):
    t = logits_t.shape[1]
    tt = math.gcd(t, 512)
    tok = pl.BlockSpec((8, tt), lambda i: (0, i))
    return pl.pallas_call(
        _route_kernel,
        grid=(t // tt,),
        in_specs=[pl.BlockSpec((N_EXPERTS, tt), lambda i: (0, i)),
                  pl.BlockSpec((N_EXPERTS, 1), lambda i: (0, 0))],
        out_specs=[tok, tok, tok, pl.BlockSpec((N_EXPERTS, 128), lambda i: (0, 0))],
        out_shape=[jax.ShapeDtypeStruct((8, t), I32), jax.ShapeDtypeStruct((8, t), F32),
                   jax.ShapeDtypeStruct((8, t), I32), jax.ShapeDtypeStruct((N_EXPERTS, 128), F32)],
        scratch_shapes=[pltpu.VMEM((N_EXPERTS, 128), F32)],
        compiler_params=_cparams(("arbitrary",)),
    )(logits_t, b_router.astype(F32).reshape(N_EXPERTS, 1))


def _dest_kernel(idx_ref, pos_ref, start_ref, o_ref, *, tt):
    eid = lax.broadcasted_iota(I32, (N_EXPERTS, tt), 0)
    start = start_ref[...]
    for k in range(TOP_K):
        first = jnp.sum(jnp.where(eid == idx_ref[k:k + 1, :], start, 0.0), axis=0, keepdims=True)
        o_ref[k:k + 1, :] = first.astype(I32) + pos_ref[k:k + 1, :]
    for k in range(TOP_K, 8):
        o_ref[k:k + 1, :] = jnp.zeros((1, tt), I32)


def _dest_rows(idx_t, pos_t, pad_start):
    t = idx_t.shape[1]
    tt = math.gcd(t, 512)
    tok = pl.BlockSpec((8, tt), lambda i: (0, i))
    return pl.pallas_call(
        functools.partial(_dest_kernel, tt=tt),
        grid=(t // tt,),
        in_specs=[tok, tok, pl.BlockSpec((N_EXPERTS, 1), lambda i: (0, 0))],
        out_specs=tok,
        out_shape=jax.ShapeDtypeStruct((8, t), I32),
        compiler_params=_cparams(("arbitrary",)),
    )(idx_t, pos_t, pad_start.astype(F32).reshape(N_EXPERTS, 1))


def _sc_layout(n_pairs, period):
    info = plsc.get_sparse_core_info()
    workers = info.num_cores * info.num_subcores
    per_worker = n_pairs // workers
    chunk = math.gcd(math.gcd(per_worker, period), SC_STREAM_ROWS)
    assert per_worker * workers == n_pairs and chunk % 8 == 0
    return info.num_cores, per_worker, chunk


def _sc_scatter_rows(rows, dest, n_out):
    t, width = rows.shape
    copies = dest.shape[0] // t
    n_cores, per_worker, chunk = _sc_layout(t, t)
    mesh = plsc.VectorSubcoreMesh(core_axis_name="core", subcore_axis_name="subcore")

    @functools.partial(
        pl.kernel, mesh=mesh, out_type=jax.ShapeDtypeStruct((n_out, width), rows.dtype),
        scratch_types=[pltpu.VMEM((chunk,), I32), pltpu.VMEM((chunk, width), rows.dtype)])
    def scatter(rows_hbm, dest_hbm, out_hbm, dest_v, rows_v):
        base = (lax.axis_index("subcore") * n_cores + lax.axis_index("core")) * per_worker

        @pl.loop(0, per_worker // chunk)
        def _(j):
            tok = base + j * chunk
            pltpu.sync_copy(rows_hbm.at[pl.ds(tok, chunk)], rows_v)
            for k in range(copies):
                pltpu.sync_copy(dest_hbm.at[pl.ds(k * t + tok, chunk)], dest_v)
                pltpu.sync_copy(rows_v, out_hbm.at[dest_v])

    return scatter(rows, dest)


def _sc_gather_rows(table, dest):
    width = table.shape[1]
    n_pairs = dest.shape[0]
    n_cores, per_worker, chunk = _sc_layout(n_pairs, n_pairs)
    mesh = plsc.VectorSubcoreMesh(core_axis_name="core", subcore_axis_name="subcore")

    @functools.partial(
        pl.kernel, mesh=mesh, out_type=jax.ShapeDtypeStruct((n_pairs, width), table.dtype),
        scratch_types=[pltpu.VMEM((chunk,), I32), pltpu.VMEM((chunk, width), table.dtype)])
    def gather(table_hbm, dest_hbm, out_hbm, dest_v, rows_v):
        base = (lax.axis_index("subcore") * n_cores + lax.axis_index("core")) * per_worker

        @pl.loop(0, per_worker // chunk)
        def _(j):
            pair = base + j * chunk
            pltpu.sync_copy(dest_hbm.at[pl.ds(pair, chunk)], dest_v)
            pltpu.sync_copy(table_hbm.at[dest_v], rows_v)
            pltpu.sync_copy(rows_v, out_hbm.at[pl.ds(pair, chunk)])

    return gather(table, dest)


def _expert_kernel(be_ref, xs_ref, wg_ref, wu_ref, wd_ref, ys_ref, wg16, wu16, wd16):
    i = pl.program_id(0)
    changed = jnp.logical_or(i == 0, be_ref[i] != be_ref[jnp.maximum(i - 1, 0)])

    @pl.when(changed)
    def _():
        rows = 256
        d = wg16.shape[0]

        def cast_in(r, carry):
            sl = pl.ds(pl.multiple_of(r * rows, rows), rows)
            wg16[sl, :] = wg_ref[0, sl, :].astype(BF16)
            wu16[sl, :] = wu_ref[0, sl, :].astype(BF16)
            return carry

        lax.fori_loop(0, d // rows, cast_in, 0)

        def cast_down(r, carry):
            sl = pl.ds(pl.multiple_of(r * 128, 128), 128)
            wd16[sl, :] = wd_ref[0, sl, :].astype(BF16)
            return carry

        lax.fori_loop(0, D_EXPERT // 128, cast_down, 0)

    n_blocks = pl.num_programs(0)
    used = i < be_ref[n_blocks]

    @pl.when(used)
    def _():
        held = lax.broadcasted_iota(I32, (xs_ref.shape[0], 1), 0) < be_ref[n_blocks + 1 + i]
        hi, lo = _unpack_bf16_pairs(jnp.where(held, xs_ref[...], jnp.uint32(0)))
        x = jnp.concatenate([hi, lo], axis=1).astype(BF16)
        gate = jnp.dot(x, wg16[...], preferred_element_type=F32)
        up = jnp.dot(x, wu16[...], preferred_element_type=F32)
        act = (gate * _sigmoid(gate) * up).astype(BF16)
        ys_ref[...] = _pack_bf16_pairs(jnp.dot(act, wd16[...], preferred_element_type=F32))

    @pl.when(jnp.logical_not(used))
    def _():
        ys_ref[...] = jnp.zeros_like(ys_ref)


def _experts(xs, block_e, layer, w_gate, w_up, w_down):
    n_rows, half = xs.shape
    d = 2 * half
    br = EXPERT_ROWS
    grid_spec = pltpu.PrefetchScalarGridSpec(
        num_scalar_prefetch=1,
        grid=(n_rows // br,),
        in_specs=[
            pl.BlockSpec((br, half), lambda i, be: (i, 0)),
            pl.BlockSpec((None, 1, d, D_EXPERT), lambda i, be: (layer, be[i], 0, 0)),
            pl.BlockSpec((None, 1, d, D_EXPERT), lambda i, be: (layer, be[i], 0, 0)),
            pl.BlockSpec((None, 1, D_EXPERT, d), lambda i, be: (layer, be[i], 0, 0)),
        ],
        out_specs=pl.BlockSpec((br, half), lambda i, be: (i, 0)),
        scratch_shapes=[pltpu.VMEM((d, D_EXPERT), BF16), pltpu.VMEM((d, D_EXPERT), BF16),
                        pltpu.VMEM((D_EXPERT, d), BF16)],
    )
    return pl.pallas_call(
        _expert_kernel,
        grid_spec=grid_spec,
        out_shape=jax.ShapeDtypeStruct((n_rows, half), U32),
        compiler_params=_cparams(("arbitrary",)),
    )(block_e, xs, w_gate, w_up, w_down)


def _shared_kernel(hp_ref, wsg_ref, wsu_ref, wsd_ref, o_ref):
    hi, lo = _unpack_bf16_pairs(hp_ref[...])
    h2 = jnp.concatenate([hi, lo], axis=1).astype(BF16)
    sg = jnp.dot(h2, wsg_ref[...], preferred_element_type=F32)
    su = jnp.dot(h2, wsu_ref[...], preferred_element_type=F32)
    shared = jnp.dot((sg * _sigmoid(sg) * su).astype(BF16), wsd_ref[...], preferred_element_type=F32)
    o_ref[...] = shared.astype(o_ref.dtype)


def _shared_expert(lay, h2p, wsg16, wsu16, wsd16):
    d = lay.d
    tm = lay.row_tile(512)
    return pl.pallas_call(
        _shared_kernel,
        grid=(lay.t // tm,),
        in_specs=[pl.BlockSpec((tm, d // 2), lambda i: (i, 0)),
                  pl.BlockSpec((d, D_EXPERT), lambda i: (0, 0)),
                  pl.BlockSpec((d, D_EXPERT), lambda i: (0, 0)),
                  pl.BlockSpec((D_EXPERT, d), lambda i: (0, 0))],
        out_specs=pl.BlockSpec((tm, d), lambda i: (i, 0)),
        out_shape=jax.ShapeDtypeStruct((lay.t, d), BF16),
        compiler_params=_cparams(("arbitrary",)),
    )(h2p, wsg16, wsu16, wsd16)


def _combine_kernel(yk_ref, wsel_ref, sh_ref, x_ref, gate_ref, fg_ref, *outs, tt, final, n_ctx_tiles):
    shared = sh_ref[...].astype(F32)
    half = yk_ref.shape[2]
    r_hi = jnp.zeros((tt, half), F32)
    r_lo = jnp.zeros((tt, half), F32)
    for k in range(TOP_K):
        y_hi, y_lo = _unpack_bf16_pairs(yk_ref[k])
        wk = wsel_ref[:, k:k + 1]
        r_hi = r_hi + wk * y_hi
        r_lo = r_lo + wk * y_lo
    routed = jnp.concatenate([r_hi, r_lo], axis=1)
    out = x_ref[...] + gate_ref[0] * (routed + shared)
    if not final:
        outs[0][...] = out
        return
    ms = jnp.mean(out * out, axis=-1, keepdims=True)
    out = out * lax.rsqrt(ms + EPS) * fg_ref[...]
    ctx_ref, lat_ref = outs
    is_ctx = pl.program_id(0) < n_ctx_tiles

    @pl.when(is_ctx)
    def _():
        ctx_ref[...] = out

    @pl.when(jnp.logical_not(is_ctx))
    def _():
        lat_ref[...] = out


def _combine(lay, y_by_k, wsel, shared, x, mod_l, final_g, tt, final):
    d = lay.d
    half = d // 2
    nct = lay.tc // tt
    if final:
        out_specs = [pl.BlockSpec((tt, d), lambda i: (jnp.minimum(i, nct - 1), 0)),
                     pl.BlockSpec((tt, d), lambda i: (jnp.maximum(i - nct, 0), 0))]
        out_shape = [jax.ShapeDtypeStruct((lay.tc, d), F32), jax.ShapeDtypeStruct((lay.ts, d), F32)]
    else:
        out_specs = pl.BlockSpec((tt, d), lambda i: (i, 0))
        out_shape = jax.ShapeDtypeStruct((lay.t, d), F32)
    return pl.pallas_call(
        functools.partial(_combine_kernel, tt=tt, final=final, n_ctx_tiles=nct),
        grid=(lay.t // tt,),
        in_specs=[
            pl.BlockSpec((TOP_K, tt, half), lambda i: (0, i, 0)),
            pl.BlockSpec((tt, 8), lambda i: (i, 0)),
            pl.BlockSpec((tt, d), lambda i: (i, 0)),
            pl.BlockSpec((tt, d), lambda i: (i, 0)),
            _mod_spec(lay, tt, 5, d),
            pl.BlockSpec((1, d), lambda i: (0, 0)),
        ],
        out_specs=out_specs,
        out_shape=out_shape,
        compiler_params=_cparams(("arbitrary",)),
    )(y_by_k, wsel, shared, x, mod_l, final_g.reshape(1, d))


def _moe(lay, layer, x, h2p, logits_t, mod_l, b_router, w_e_gate, w_e_up, w_e_down, wsg16, wsu16, wsd16, final_g,
         final):
    t = lay.t
    idx_t, w_t, pos_t, counts = _route(logits_t, b_router)
    br = EXPERT_ROWS
    counts = counts[:, 0].astype(I32)
    padded = (counts + br - 1) // br * br
    pad_end = jnp.cumsum(padded)
    pad_start = pad_end - padded
    n_blocks = -(-(t * TOP_K + N_EXPERTS * (br - 1)) // br)
    n_rows = n_blocks * br
    first_row = jnp.arange(n_blocks, dtype=I32) * br
    block_e = jnp.minimum(jnp.sum((pad_end[None, :] <= first_row[:, None]).astype(I32), axis=1), N_EXPERTS - 1)
    own = block_e[:, None] == jnp.arange(N_EXPERTS, dtype=I32)[None, :]
    rows_end = jnp.sum(jnp.where(own, (pad_start + counts)[None, :], 0), axis=1)
    held = jnp.clip(rows_end - first_row, 0, br)
    block_meta = jnp.concatenate([block_e, pad_end[-1:] // br, held]).astype(I32)
    dest = _dest_rows(idx_t, pos_t, pad_start)[:TOP_K].reshape(TOP_K * t)
    xs = _sc_scatter_rows(h2p, dest, n_rows)
    shared = _shared_expert(lay, h2p, wsg16, wsu16, wsd16)
    ys = _experts(xs, block_meta, layer, w_e_gate, w_e_up, w_e_down)
    y_by_k = _sc_gather_rows(ys, dest).reshape(TOP_K, t, h2p.shape[1])
    return _combine(lay, y_by_k, w_t.T, shared, x, mod_l, final_g, lay.row_tile(256), final)


def kernel(x_prompt, x_sample, c, cache_k, cache_v, state_ssm_re, state_ssm_im, c_ctx, w_ada, b_ada, norm1_g, norm2_g, w_in, w_gates, b_gates, ssm_lam_re, ssm_lam_im, ssm_log_dt, ssm_b_re, ssm_b_im, ssm_c_re, ssm_c_im, ssm_d, ssm_w_glu, conv_w, attn_sink, w_br_ssm, w_br_attn, w_br_conv, w_out, w_router, b_router, w_e_gate, w_e_up, w_e_down, w_s_gate, w_s_up, w_s_down, final_g):
    bc, lc, d = x_prompt.shape
    bs, ls, _ = x_sample.shape
    depth = w_in.shape[0]
    lay = _Layout(bc, lc, bs, ls, d)
    assert 1 + bs <= MOD_ROWS

    x = jnp.concatenate([x_prompt.reshape(lay.tc, d), x_sample.reshape(lay.ts, d)], axis=0)
    cvec = jnp.zeros((MOD_ROWS, d), F32).at[0].set(c_ctx).at[1:1 + bs].set(c)
    mod = _adaln(cvec, w_ada, b_ada).reshape(depth, MOD_ROWS * 6, 1, d)
    rope_cos, rope_sin = _rope_tables(lay, lay.row_tile(1024))

    ks, vs, s_re, s_im = [], [], [], []
    for l in range(depth):
        mod_l = mod[l]
        h16, y16, kv32 = _inproj(lay, x, mod_l, norm1_g[l], w_in[l].astype(BF16), rope_cos, rope_sin)
        ks.append(kv32[:lay.tc, :KV_WIDTH].reshape(bc, lc, N_KV_HEADS, HEAD_DIM))
        vs.append(kv32[:lay.tc, KV_WIDTH:].reshape(bc, lc, N_KV_HEADS, HEAD_DIM))

        attn = _attention(lay, y16, attn_sink[l].astype(F32), cache_k[:, l], cache_v[:, l])

        u0 = ATTN_WIDTH + 2 * KV_WIDTH
        u = y16[:, u0:u0 + SSM_WIDTH]
        mats = _s5_matrices(ssm_lam_re[l], ssm_lam_im[l], ssm_log_dt[l], ssm_b_re[l], ssm_b_im[l],
                            ssm_c_re[l], ssm_c_im[l], ssm_d[l], 8)
        y_ssm, f_re, f_im = _s5_both(lay, u, mats, state_ssm_re[:, l], state_ssm_im[:, l])
        s_re.append(f_re)
        s_im.append(f_im)

        merged = _merge(lay, h16, y_ssm, attn, y16, conv_w[l], ssm_w_glu[l].astype(BF16),
                        w_gates[l].astype(BF16), b_gates[l].reshape(1, -1), w_br_ssm[l].astype(BF16),
                        w_br_attn[l].astype(BF16), w_br_conv[l].astype(BF16))
        x, h2p, logits_t = _outproj(lay, merged, w_out[l].astype(BF16), x, mod_l, norm2_g[l],
                                    w_router[l])
        x = _moe(lay, l, x, h2p, logits_t, mod_l, b_router[l], w_e_gate, w_e_up, w_e_down,
                 w_s_gate[l].astype(BF16), w_s_up[l].astype(BF16), w_s_down[l].astype(BF16),
                 final_g, l == depth - 1)

    y_prompt = x[0].reshape(bc, lc, d)
    y_sample = x[1].reshape(bs, ls, d)
    return (y_prompt, y_sample, jnp.stack(ks, axis=1), jnp.stack(vs, axis=1),
            jnp.stack(s_re, axis=1), jnp.stack(s_im, axis=1))
```

```python
import functools
import math

import jax
import jax.numpy as jnp
from jax import lax
from jax.experimental import pallas as pl
from jax.experimental.pallas import tpu as pltpu
from jax.experimental.pallas import tpu_sc as plsc

HEAD_DIM = 128
N_HEADS = 8
N_KV_HEADS = 2
GROUP = N_HEADS // N_KV_HEADS
ATTN_WIDTH = N_HEADS * HEAD_DIM
KV_WIDTH = N_KV_HEADS * HEAD_DIM
WINDOW = 128
ATTN_BLOCK = 128
ATTN_Q_ROWS = 256
ATTN_SCALE = HEAD_DIM ** -0.5
LOG2_E = 1.4426950408889634
ROPE_BASE = 10000.0
ROT_F = HEAD_DIM // 4
GRID_W = 64
SSM_WIDTH = 512
SSM_CH = 16
SSM_GROUPS = SSM_WIDTH // SSM_CH
SSM_STATE = 64
SSM_CHUNK = 16
S5_BUNDLE = 8
S5_BLOCK_ROWS = 256
CONV_WIDTH = 512
N_BRANCHES = 3
IN_WIDTH = ATTN_WIDTH + 2 * KV_WIDTH + SSM_WIDTH + 3 * CONV_WIDTH
N_EXPERTS = 64
TOP_K = 6
N_EXPERT_GROUPS = 8
TOPK_GROUPS = 4
D_EXPERT = 512
ROUTED_SCALE = 2.5
EPS = 1e-6
NEG_INF = -1e30

COL_TILE = 512
MOD_ROWS = 16
EXPERT_ROWS = 512
SC_STREAM_ROWS = 64
VMEM_LIMIT_V7X = 56 * 1024 * 1024

F32 = jnp.float32
BF16 = jnp.bfloat16
I32 = jnp.int32
U32 = jnp.uint32


def _cparams(sem, vmem=VMEM_LIMIT_V7X):
    return pltpu.CompilerParams(dimension_semantics=sem, vmem_limit_bytes=vmem)


def _sigmoid(x):
    return 1.0 / (1.0 + jnp.exp(-x))


def _pack_bf16_pairs(v):
    n = v.shape[1] // 2
    hi = lax.bitcast_convert_type(v[:, :n].astype(BF16).astype(F32), U32)
    lo = lax.bitcast_convert_type(v[:, n:].astype(BF16).astype(F32), U32)
    return hi | (lo >> 16)


def _unpack_bf16_pairs(p):
    hi = lax.bitcast_convert_type(p & jnp.uint32(0xFFFF0000), F32)
    lo = lax.bitcast_convert_type(p << 16, F32)
    return hi, lo


def _adaln_kernel(c_ref, w_ref, b_ref, o_ref):
    c = c_ref[...]
    s = (c * _sigmoid(c)).astype(BF16)
    o_ref[0] = jnp.dot(s, w_ref[0].astype(BF16), preferred_element_type=F32) + b_ref[0]


def _adaln(cvec, w_ada, b_ada):
    depth, d, n6 = w_ada.shape
    tn = math.gcd(1024, n6)
    return pl.pallas_call(
        _adaln_kernel,
        grid=(depth, n6 // tn),
        in_specs=[
            pl.BlockSpec((MOD_ROWS, d), lambda l, n: (0, 0)),
            pl.BlockSpec((1, d, tn), lambda l, n: (l, 0, n)),
            pl.BlockSpec((1, 1, tn), lambda l, n: (l, 0, n)),
        ],
        out_specs=pl.BlockSpec((1, MOD_ROWS, tn), lambda l, n: (l, 0, n)),
        out_shape=jax.ShapeDtypeStruct((depth, MOD_ROWS, n6), F32),
        compiler_params=_cparams(("arbitrary", "arbitrary")),
    )(cvec, w_ada, b_ada.reshape(depth, 1, n6))


class _Layout:
    def __init__(self, n_ctx_seq, len_ctx, n_lat_seq, len_lat, d_model):
        self.bc, self.lc, self.bs, self.ls, self.d = n_ctx_seq, len_ctx, n_lat_seq, len_lat, d_model
        self.tc = n_ctx_seq * len_ctx
        self.ts = n_lat_seq * len_lat
        self.t = self.tc + self.ts

    def row_tile(self, want):
        tm = math.gcd(math.gcd(self.tc, self.ls), want)
        assert tm % 16 == 0
        return tm

    def mod_index(self, i, tm):
        nct, tps = self.tc // tm, self.ls // tm
        return jnp.where(i < nct, 0, 1 + (i - nct) // tps)

    def seq_pos(self, rows, i, tm):
        is_lat = i >= self.tc // tm
        return jnp.where(is_lat, (rows - self.tc) % self.ls, rows % self.lc), jnp.where(is_lat, self.ls, self.lc)


def _mod_spec(lay, tm, slot, d):
    return pl.BlockSpec((1, 1, d), lambda i, *_: (lay.mod_index(i, tm) * 6 + slot, 0, 0))


def _rope(z, cos, sin_signed, first_half):
    swapped = jnp.where(first_half, pltpu.roll(z, HEAD_DIM - ROT_F, 1), pltpu.roll(z, ROT_F, 1))
    return z * cos + swapped * sin_signed


def _inproj_kernel(x_ref, shift_ref, scale_ref, g_ref, w_ref, cos_ref, sin_ref, h_ref, y_ref, kv_ref):
    n = pl.program_id(1)
    n_q = ATTN_WIDTH // COL_TILE

    @pl.when(n == 0)
    def _():
        x = x_ref[...]
        ms = jnp.mean(x * x, axis=-1, keepdims=True)
        y = x * lax.rsqrt(ms + EPS) * g_ref[...]
        h_ref[...] = (y * (1.0 + scale_ref[0]) + shift_ref[0]).astype(BF16)

    acc = jnp.dot(h_ref[...], w_ref[...], preferred_element_type=F32)

    def rotated(n_heads):
        cos, sin = cos_ref[...], sin_ref[...]
        first_half = (lax.broadcasted_iota(I32, cos.shape, 1) % (2 * ROT_F)) < ROT_F
        parts = [_rope(acc[:, s * HEAD_DIM:(s + 1) * HEAD_DIM], cos, sin, first_half) for s in range(n_heads)]
        parts.append(acc[:, n_heads * HEAD_DIM:])
        return jnp.concatenate(parts, axis=1) if n_heads * HEAD_DIM < COL_TILE else jnp.concatenate(parts[:-1], axis=1)

    @pl.when(n < n_q)
    def _():
        y_ref[...] = (rotated(COL_TILE // HEAD_DIM) * (ATTN_SCALE * LOG2_E)).astype(BF16)

    @pl.when(n == n_q)
    def _():
        kv_ref[...] = acc
        y_ref[...] = rotated(N_KV_HEADS).astype(BF16)

    @pl.when(n > n_q)
    def _():
        y_ref[...] = acc.astype(BF16)


def _inproj(lay, x, mod_l, g1, w_in16, rope_cos, rope_sin):
    d = lay.d
    tm = lay.row_tile(1024)
    nct, tps = lay.tc // tm, lay.ls // tm

    def rope_idx(i, n):
        return (jnp.where(i < nct, 0, 1 + (i - nct) % tps), 0)

    return pl.pallas_call(
        _inproj_kernel,
        grid=(lay.t // tm, IN_WIDTH // COL_TILE),
        in_specs=[
            pl.BlockSpec((tm, d), lambda i, n: (i, 0)),
            _mod_spec(lay, tm, 0, d),
            _mod_spec(lay, tm, 1, d),
            pl.BlockSpec((1, d), lambda i, n: (0, 0)),
            pl.BlockSpec((d, COL_TILE), lambda i, n: (0, n)),
            pl.BlockSpec((tm, HEAD_DIM), rope_idx),
            pl.BlockSpec((tm, HEAD_DIM), rope_idx),
        ],
        out_specs=[
            pl.BlockSpec((tm, d), lambda i, n: (i, 0)),
            pl.BlockSpec((tm, COL_TILE), lambda i, n: (i, n)),
            pl.BlockSpec((tm, 2 * KV_WIDTH), lambda i, n: (i, 0)),
        ],
        out_shape=[
            jax.ShapeDtypeStruct((lay.t, d), BF16),
            jax.ShapeDtypeStruct((lay.t, IN_WIDTH), BF16),
            jax.ShapeDtypeStruct((lay.t, 2 * KV_WIDTH), F32),
        ],
        compiler_params=_cparams(("arbitrary", "arbitrary")),
    )(x, mod_l, mod_l, g1.reshape(1, d), w_in16, rope_cos, rope_sin)


def _rope_tables(lay, tm):
    t = jnp.arange(lay.ls)
    row = (t // GRID_W).astype(F32)
    col = (t % GRID_W).astype(F32)
    inv = ROPE_BASE ** (-jnp.arange(ROT_F, dtype=F32) / ROT_F)
    ar, ac = row[:, None] * inv, col[:, None] * inv
    cos = jnp.concatenate([jnp.cos(ar), jnp.cos(ar), jnp.cos(ac), jnp.cos(ac)], axis=1)
    sin = jnp.concatenate([-jnp.sin(ar), jnp.sin(ar), -jnp.sin(ac), jnp.sin(ac)], axis=1)
    cos = jnp.concatenate([jnp.ones((tm, HEAD_DIM), F32), cos], axis=0)
    sin = jnp.concatenate([jnp.zeros((tm, HEAD_DIM), F32), sin], axis=0)
    return cos, sin


def _attend(q, sink_ref, j, parts):
    nq = q.shape[0]
    q4 = jnp.concatenate([q[:, g * HEAD_DIM:(g + 1) * HEAD_DIM] for g in range(GROUP)], axis=0)
    sink = jnp.concatenate([jnp.full((nq, 1), sink_ref[j * GROUP + g] * LOG2_E, F32) for g in range(GROUP)], axis=0)
    scores = []
    m = sink
    for k, _, mask in parts:
        s = lax.dot_general(q4, k, (((1,), (1,)), ((), ())), preferred_element_type=F32)
        if mask is not None:
            s = jnp.where(mask, s, NEG_INF)
        scores.append(s)
        m = jnp.maximum(m, jnp.max(s, axis=-1, keepdims=True))
    den = jnp.exp2(sink - m)
    out = jnp.zeros((GROUP * nq, HEAD_DIM), F32)
    for s, (_, v, _) in zip(scores, parts):
        p = jnp.exp2(s - m)
        den = den + jnp.sum(p, axis=-1, keepdims=True)
        out = out + jnp.dot(p.astype(BF16), v, preferred_element_type=F32)
    out = out / den
    return jnp.concatenate([out[g * nq:(g + 1) * nq] for g in range(GROUP)], axis=1)


def _head(x, j, width=HEAD_DIM):
    return x[:, j * width:(j + 1) * width]


def _attn_ctx_kernel(sink_ref, q_ref, k_ref, v_ref, o_ref):
    q, k, v = q_ref[...], k_ref[...], v_ref[...]
    outs = [_attend(_head(q, j, GROUP * HEAD_DIM), sink_ref, j, [(_head(k, j), _head(v, j), None)])
            for j in range(N_KV_HEADS)]
    o_ref[...] = jnp.concatenate(outs, axis=1).astype(o_ref.dtype)


def _attn_lat_kernel(sink_ref, q_ref, kp_ref, kc_ref, kn_ref, vp_ref, vc_ref, vn_ref, ck_ref, cv_ref, ctx_out_ref,
                     o_ref, *, seq_len):
    del ctx_out_ref
    i = pl.program_id(1)
    q = q_ref[...]
    kw = jnp.concatenate([kp_ref[...], kc_ref[...], kn_ref[...]], axis=0)
    vw = jnp.concatenate([vp_ref[...], vc_ref[...], vn_ref[...]], axis=0)
    ck = ck_ref[...].astype(BF16)
    cv = cv_ref[...].astype(BF16)
    shape = (GROUP * ATTN_Q_ROWS, ATTN_Q_ROWS + 2 * ATTN_BLOCK)
    qoff = lax.broadcasted_iota(I32, shape, 0) % ATTN_Q_ROWS
    koff = lax.broadcasted_iota(I32, shape, 1) - ATTN_BLOCK
    kabs = koff + i * ATTN_Q_ROWS
    mask = (jnp.abs(qoff - koff) <= WINDOW) & (kabs >= 0) & (kabs < seq_len)
    outs = [_attend(_head(q, j, GROUP * HEAD_DIM), sink_ref, j,
                    [(_head(kw, j), _head(vw, j), mask), (_head(ck, j), _head(cv, j), None)])
            for j in range(N_KV_HEADS)]
    o_ref[...] = jnp.concatenate(outs, axis=1).astype(o_ref.dtype)


def _attention(lay, y16, sink, cache_k_l, cache_v_l):
    smem = pl.BlockSpec(memory_space=pltpu.SMEM)
    kcol, vcol = ATTN_WIDTH // KV_WIDTH, (ATTN_WIDTH + KV_WIDTH) // KV_WIDTH
    ctx = pl.pallas_call(
        _attn_ctx_kernel,
        grid=(lay.bc,),
        in_specs=[
            smem,
            pl.BlockSpec((lay.lc, ATTN_WIDTH), lambda b: (b, 0)),
            pl.BlockSpec((lay.lc, KV_WIDTH), lambda b: (b, kcol)),
            pl.BlockSpec((lay.lc, KV_WIDTH), lambda b: (b, vcol)),
        ],
        out_specs=pl.BlockSpec((lay.lc, ATTN_WIDTH), lambda b: (b, 0)),
        out_shape=jax.ShapeDtypeStruct((lay.t, ATTN_WIDTH), BF16),
        compiler_params=_cparams(("arbitrary",)),
    )(sink, y16, y16, y16)

    qr = ATTN_Q_ROWS
    assert lay.ls % qr == 0 and lay.tc % qr == 0 and qr % ATTN_BLOCK == 0
    nblk = lay.ls // qr
    base = lay.tc // qr
    per = qr // ATTN_BLOCK
    last = lay.t // ATTN_BLOCK - 1
    past = cache_k_l.shape[1]

    def rb(b, i):
        return base + b * nblk + i

    def edge(col, delta):
        return pl.BlockSpec((ATTN_BLOCK, KV_WIDTH), lambda b, i: (jnp.clip(rb(b, i) * per + delta, 0, last), col))

    def own(col):
        return pl.BlockSpec((qr, KV_WIDTH), lambda b, i: (rb(b, i), col))

    cspec = pl.BlockSpec((None, past, KV_WIDTH), lambda b, i: (b, 0, 0))
    return pl.pallas_call(
        functools.partial(_attn_lat_kernel, seq_len=lay.ls),
        grid=(lay.bs, nblk),
        in_specs=[
            smem,
            pl.BlockSpec((qr, ATTN_WIDTH), lambda b, i: (rb(b, i), 0)),
            edge(kcol, -1), own(kcol), edge(kcol, per),
            edge(vcol, -1), own(vcol), edge(vcol, per),
            cspec, cspec,
            pl.BlockSpec(memory_space=pl.ANY),
        ],
        out_specs=pl.BlockSpec((qr, ATTN_WIDTH), lambda b, i: (rb(b, i), 0)),
        out_shape=jax.ShapeDtypeStruct((lay.t, ATTN_WIDTH), BF16),
        input_output_aliases={10: 0},
        compiler_params=_cparams(("arbitrary", "arbitrary")),
    )(sink, y16, y16, y16, y16, y16, y16, y16,
      cache_k_l.reshape(lay.bs, past, KV_WIDTH), cache_v_l.reshape(lay.bs, past, KV_WIDTH), ctx)


def _s5_matrices(lam_re, lam_im, log_dt, b_re, b_im, c_re, c_im, d_skip, n_steps):
    q, p, g, n = SSM_CHUNK, SSM_CH, SSM_GROUPS, SSM_STATE
    lam = lax.complex(lam_re.astype(F32), lam_im.astype(F32))
    dt = jnp.exp(log_dt.astype(F32))[..., None]
    lam_dt = lam * dt
    lam_bar = jnp.exp(lam_dt)
    b_bar = ((lam_bar - 1.0) / lam)[..., None] * lax.complex(b_re.astype(F32), b_im.astype(F32))
    c_mat = lax.complex(c_re.astype(F32), c_im.astype(F32))
    steps = jnp.arange(q + 1, dtype=F32)
    pw = jnp.exp(lam_dt[:, None] * steps[None, :, None, None])
    kern = jnp.real(jnp.einsum('dgpn,dkgn,dgnr->dkgpr', c_mat, pw[:, :q], b_bar))
    tau_in = jnp.arange(q)[:, None]
    tau_out = jnp.arange(q)[None, :]
    lag_f = tau_out - tau_in
    lag_b = tau_in - tau_out
    lags = jnp.arange(q)
    pick_f = (lag_f[:, :, None] == lags).astype(F32)
    pick_b = (lag_b[:, :, None] == lags).astype(F32)
    exact = lax.Precision.HIGHEST
    kf = jnp.einsum('abk,kgpr->garbp', pick_f, kern[0], precision=exact)
    kb = jnp.einsum('abk,kgpr->garbp', pick_b, kern[1], precision=exact)
    m = kf + kb
    eye_q = jnp.eye(q, dtype=F32)[None, :, None, :, None]
    eye_p = jnp.eye(p, dtype=F32)[None, None, :, None, :]
    m = m + eye_q * eye_p * d_skip.astype(F32).reshape(g, 1, p, 1, 1)
    m = m.reshape(g, q * p, q * p)
    ws_f = pw[0, :q][::-1][:, :, :, None] * b_bar[0][None]
    ws_b = pw[1, :q][:, :, :, None] * b_bar[1][None]

    def cols(w):
        return w.transpose(1, 0, 3, 2).reshape(g, q * p, n)

    w1 = jnp.concatenate([m, jnp.real(cols(ws_f)), jnp.real(cols(ws_b)),
                          jnp.imag(cols(ws_f)), jnp.imag(cols(ws_b))], axis=2)
    cy_f = c_mat[0][None] * pw[0, 1:][:, :, None, :]
    cy_b = c_mat[1][None] * pw[1, 1:][::-1][:, :, None, :]

    def rows(w):
        return w.transpose(1, 3, 0, 2).reshape(g, n, q * p)

    wy = jnp.concatenate([jnp.real(rows(cy_f)), jnp.real(rows(cy_b)),
                          -jnp.imag(rows(cy_f)), -jnp.imag(rows(cy_b))], axis=1)
    hops = (q * 2.0 ** jnp.arange(8, dtype=F32))[None, :, None, None]
    a = jnp.exp(lam_dt[:, None] * hops)
    a = jnp.concatenate([a[0], a[1]], axis=-1).transpose(1, 0, 2)
    assert n_steps <= 8
    gb, nbun = S5_BUNDLE, g // S5_BUNDLE
    ab = a.reshape(nbun, gb, 8, 2 * n).transpose(0, 2, 1, 3).reshape(nbun, 8, gb * 2 * n)
    return (w1.astype(BF16).reshape(nbun, gb, q * p, q * p + 4 * n),
            wy.astype(BF16).reshape(nbun, gb, 4 * n, q * p), jnp.real(ab), jnp.imag(ab))


def _s5_lane_permutation():
    q, p, gb = SSM_CHUNK, SSM_CH, S5_BUNDLE
    lane = jnp.arange(q * gb * p)
    target = ((lane % (gb * p)) // p) * (q * p) + (lane // (gb * p)) * p + lane % p
    return (target[:, None] == lane[None, :]).astype(BF16)


def _s5_kernel(u_ref, perm_ref, w1_ref, wy_ref, are_ref, aim_ref, h0re_ref, h0im_ref, *rest, nb, nc):
    y_ref, fre_ref, fim_ref = rest[-3:]
    r_tot = nb * nc
    w = SSM_CHUNK * SSM_CH
    ns = 2 * SSM_STATE
    n2 = S5_BUNDLE * ns
    perm = perm_ref[...]
    x = jnp.dot(u_ref[0], perm, preferred_element_type=F32).astype(BF16)
    proj = [jnp.dot(x[:, g * w:(g + 1) * w], w1_ref[0, g], preferred_element_type=F32) for g in range(S5_BUNDLE)]
    d_re = jnp.concatenate([pg[:, w:w + ns] for pg in proj], axis=1)
    d_im = jnp.concatenate([pg[:, w + ns:] for pg in proj], axis=1)
    h0re_ref, h0im_ref, fre_ref, fim_ref = (r.at[0] for r in (h0re_ref, h0im_ref, fre_ref, fim_ref))
    row = lax.broadcasted_iota(I32, (r_tot, n2), 0)
    chunk = row % nc
    seq = row // nc
    fwd = (lax.broadcasted_iota(I32, (r_tot, n2), 1) % (2 * SSM_STATE)) < SSM_STATE

    def previous(x, dist):
        valid = (fwd & (chunk >= dist)) | (~fwd & (chunk < nc - dist))
        moved = jnp.where(fwd, pltpu.roll(x, dist, 0), pltpu.roll(x, r_tot - dist, 0))
        return jnp.where(valid, moved, 0.0)

    h0_re = jnp.zeros((r_tot, n2), F32)
    h0_im = jnp.zeros((r_tot, n2), F32)
    for b in range(nb):
        h0_re = jnp.where(seq == b, h0re_ref[0, b:b + 1, :], h0_re)
        h0_im = jnp.where(seq == b, h0im_ref[0, b:b + 1, :], h0_im)
    first = (fwd & (chunk == 0)) | (~fwd & (chunk == nc - 1))
    e_re = jnp.where(first, h0_re, previous(d_re, 1))
    e_im = jnp.where(first, h0_im, previous(d_im, 1))
    k = 0
    while (1 << k) < nc:
        a_re = are_ref[0, k:k + 1, :]
        a_im = aim_ref[0, k:k + 1, :]
        p_re = previous(e_re, 1 << k)
        p_im = previous(e_im, 1 << k)
        e_re, e_im = e_re + a_re * p_re - a_im * p_im, e_im + a_re * p_im + a_im * p_re
        k += 1
    ys = []
    for g in range(S5_BUNDLE):
        e_g = jnp.concatenate([e_re[:, g * ns:(g + 1) * ns], e_im[:, g * ns:(g + 1) * ns]], axis=1).astype(BF16)
        ys.append(proj[g][:, :w] + jnp.dot(e_g, wy_ref[0, g], preferred_element_type=F32))
    y = jnp.concatenate(ys, axis=1)
    y_hi = y.astype(BF16)
    y_lo = (y - y_hi.astype(F32)).astype(BF16)
    back = (((1,), (1,)), ((), ()))
    y_ref[0] = (lax.dot_general(y_hi, perm, back, preferred_element_type=F32)
                + lax.dot_general(y_lo, perm, back, preferred_element_type=F32))
    a_re = are_ref[0, 0:1, :]
    a_im = aim_ref[0, 0:1, :]
    f_re = a_re * e_re - a_im * e_im + d_re
    f_im = a_re * e_im + a_im * e_re + d_im
    fwd_row = fwd[0:1, :]
    for b in range(nb):
        lo, hi = b * nc, b * nc + nc - 1
        fre_ref[0, b:b + 1, :] = jnp.where(fwd_row, f_re[hi:hi + 1, :], f_re[lo:lo + 1, :])
        fim_ref[0, b:b + 1, :] = jnp.where(fwd_row, f_im[hi:hi + 1, :], f_im[lo:lo + 1, :])


def _s5_scan(ub, y_prev, row0, nb, nc, mats, h0_re, h0_im):
    g, q, p, n = SSM_GROUPS, SSM_CHUNK, SSM_CH, SSM_STATE
    gb, nbun = S5_BUNDLE, SSM_GROUPS // S5_BUNDLE
    assert nc & (nc - 1) == 0 and nc % 8 == 0
    seqs = max(1, min(nb, S5_BLOCK_ROWS // nc))
    while nb % seqs or row0 % (seqs * nc):
        seqs -= 1
    n_rb, rows = nb // seqs, seqs * nc
    rb0 = row0 // rows
    kw, sw = q * gb * p, gb * 2 * n
    perm, w1, wy, a_re, a_im = mats

    def lanes(h):
        h = h.astype(F32).reshape(n_rb, seqs, 2, nbun, gb, n).transpose(3, 0, 1, 4, 2, 5)
        return h.reshape(nbun, n_rb, seqs, sw)

    weight = lambda r, c: pl.BlockSpec((1, gb, r, c), lambda o, i: (o, 0, 0, 0))
    coeff = pl.BlockSpec((1, 8, sw), lambda o, i: (o, 0, 0))
    state = pl.BlockSpec((1, 1, seqs, sw), lambda o, i: (o, i, 0, 0))
    chunk_rows = pl.BlockSpec((1, rows, kw), lambda o, i: (o, rb0 + i, 0))
    in_specs = [chunk_rows,
                pl.BlockSpec((kw, kw), lambda o, i: (0, 0), pipeline_mode=pl.Buffered(1)),
                weight(q * p, q * p + 4 * n), weight(4 * n, q * p), coeff, coeff, state, state]
    args = [ub, perm, w1, wy, a_re, a_im, lanes(h0_re), lanes(h0_im)]
    aliases = {}
    if y_prev is not None:
        in_specs.append(pl.BlockSpec(memory_space=pl.ANY))
        args.append(y_prev)
        aliases = {len(args) - 1: 0}
    y, f_re, f_im = pl.pallas_call(
        functools.partial(_s5_kernel, nb=seqs, nc=nc),
        grid=(nbun, n_rb),
        in_specs=in_specs,
        out_specs=[chunk_rows, state, state],
        out_shape=[jax.ShapeDtypeStruct(ub.shape, F32),
                   jax.ShapeDtypeStruct((nbun, n_rb, seqs, sw), F32),
                   jax.ShapeDtypeStruct((nbun, n_rb, seqs, sw), F32)],
        input_output_aliases=aliases,
        compiler_params=_cparams(("arbitrary", "arbitrary")),
    )(*args)

    def unlanes(f):
        f = f.reshape(nbun, n_rb, seqs, gb, 2, n).transpose(1, 2, 4, 0, 3, 5)
        return f.reshape(nb, 2, g, n)

    return y, unlanes(f_re), unlanes(f_im)


def _s5_both(lay, u, mats, h0_re, h0_im):
    q, nbun, lanes = SSM_CHUNK, SSM_GROUPS // S5_BUNDLE, S5_BUNDLE * SSM_CH
    n_rows = lay.t // q
    ub = u.reshape(n_rows, q, nbun, lanes).transpose(2, 0, 1, 3).reshape(nbun, n_rows, q * lanes)
    zeros = jnp.zeros((lay.bc, 2, SSM_GROUPS, SSM_STATE), F32)
    y, f_re, f_im = _s5_scan(ub, None, 0, lay.bc, lay.lc // q, mats, zeros, zeros)
    y, _, _ = _s5_scan(ub, y, lay.tc // q, lay.bs, lay.ls // q, mats, h0_re, h0_im)
    y = y.reshape(nbun, n_rows, q, lanes).transpose(1, 2, 0, 3).reshape(lay.t, SSM_WIDTH)
    return y, f_re, f_im


def _merge_kernel(h_ref, ys_ref, at_ref, gb_ref, gc_ref, uc_ref, gcp_ref, ucp_ref, gcn_ref, ucn_ref, cw_ref,
                  wglu_ref, wg0_ref, wg1_ref, wg2_ref, bg0_ref, bg1_ref, bg2_ref, ws_ref, wa_ref, wc_ref,
                  o_ref, ssm_scr, conv_scr, *, lay, tm, halo):
    i = pl.program_id(0)
    n = pl.program_id(1)

    @pl.when(n == 0)
    def _():
        y = ys_ref[...]
        ge = 0.5 * y * (1.0 + jnp.tanh(math.sqrt(2.0 / math.pi) * (y + 0.044715 * (y * y * y))))
        glu = jnp.dot(ge.astype(BF16), wglu_ref[...], preferred_element_type=F32)
        ssm_scr[...] = (ge * _sigmoid(glu)).astype(BF16)

        z = gc_ref[...].astype(F32) * uc_ref[...].astype(F32)
        z_before = gcp_ref[halo - 1:halo, :].astype(F32) * ucp_ref[halo - 1:halo, :].astype(F32)
        z_after = gcn_ref[0:1, :].astype(F32) * ucn_ref[0:1, :].astype(F32)
        local = lax.broadcasted_iota(I32, z.shape, 0)
        pos, seq_len = lay.seq_pos(local + i * tm, i, tm)
        z_prev = jnp.where(local == 0, z_before, pltpu.roll(z, 1, 0))
        z_prev = jnp.where(pos == 0, 0.0, z_prev)
        z_next = jnp.where(local == tm - 1, z_after, pltpu.roll(z, tm - 1, 0))
        z_next = jnp.where(pos == seq_len - 1, 0.0, z_next)
        conv = cw_ref[0:1, :] * z_prev + cw_ref[1:2, :] * z + cw_ref[2:3, :] * z_next
        conv_scr[...] = (gb_ref[...].astype(F32) * conv).astype(BF16)

    h = h_ref[...]
    acc = None
    for act, wg_ref, bg_ref, wb_ref in ((ssm_scr[...], wg0_ref, bg0_ref, ws_ref),
                                        (at_ref[...], wg1_ref, bg1_ref, wa_ref),
                                        (conv_scr[...], wg2_ref, bg2_ref, wc_ref)):
        gate = _sigmoid(jnp.dot(h, wg_ref[...], preferred_element_type=F32) + bg_ref[...])
        term = gate * jnp.dot(act, wb_ref[...], preferred_element_type=F32)
        acc = term if acc is None else acc + term
    o_ref[...] = acc.astype(BF16)


def _merge(lay, h16, y_ssm, attn, y16, conv_w, wglu16, wgates16, b_gates, wbs16, wba16, wbc16):
    d = lay.d
    tm = lay.row_tile(512)
    tn = min(COL_TILE, d)
    nd = d // tn
    halo = 16
    hb = tm // halo
    last_h = lay.t // halo - 1
    c0 = (ATTN_WIDTH + 2 * KV_WIDTH + SSM_WIDTH) // CONV_WIDTH
    row = lambda cb: pl.BlockSpec((tm, CONV_WIDTH), lambda i, n: (i, cb))
    before = lambda cb: pl.BlockSpec((halo, CONV_WIDTH), lambda i, n: (jnp.maximum(i * hb - 1, 0), cb))
    after = lambda cb: pl.BlockSpec((halo, CONV_WIDTH), lambda i, n: (jnp.minimum((i + 1) * hb, last_h), cb))
    gate_w = lambda br: pl.BlockSpec((d, tn), lambda i, n: (0, br * nd + n))
    gate_b = lambda br: pl.BlockSpec((1, tn), lambda i, n: (0, br * nd + n))
    return pl.pallas_call(
        functools.partial(_merge_kernel, lay=lay, tm=tm, halo=halo),
        grid=(lay.t // tm, nd),
        in_specs=[
            pl.BlockSpec((tm, d), lambda i, n: (i, 0)),
            pl.BlockSpec((tm, SSM_WIDTH), lambda i, n: (i, 0)),
            pl.BlockSpec((tm, ATTN_WIDTH), lambda i, n: (i, 0)),
            row(c0), row(c0 + 1), row(c0 + 2),
            before(c0 + 1), before(c0 + 2), after(c0 + 1), after(c0 + 2),
            pl.BlockSpec((3, CONV_WIDTH), lambda i, n: (0, 0)),
            pl.BlockSpec((SSM_WIDTH, SSM_WIDTH), lambda i, n: (0, 0)),
            gate_w(0), gate_w(1), gate_w(2), gate_b(0), gate_b(1), gate_b(2),
            pl.BlockSpec((SSM_WIDTH, tn), lambda i, n: (0, n)),
            pl.BlockSpec((ATTN_WIDTH, tn), lambda i, n: (0, n)),
            pl.BlockSpec((CONV_WIDTH, tn), lambda i, n: (0, n)),
        ],
        out_specs=pl.BlockSpec((tm, tn), lambda i, n: (i, n)),
        out_shape=jax.ShapeDtypeStruct((lay.t, d), BF16),
        scratch_shapes=[pltpu.VMEM((tm, SSM_WIDTH), BF16), pltpu.VMEM((tm, CONV_WIDTH), BF16)],
        compiler_params=_cparams(("arbitrary", "arbitrary")),
    )(h16, y_ssm, attn, y16, y16, y16, y16, y16, y16, y16, conv_w, wglu16,
      wgates16, wgates16, wgates16, b_gates, b_gates, b_gates, wbs16, wba16, wbc16)


def _outproj_kernel(m_ref, w_ref, x_ref, gate_ref, g2_ref, shift_ref, scale_ref, wrh_ref, wrl_ref,
                    xo_ref, hp_ref, lg_ref):
    acc = jnp.dot(m_ref[...], w_ref[...], preferred_element_type=F32)
    xn = x_ref[...] + gate_ref[0] * acc
    xo_ref[...] = xn
    ms = jnp.mean(xn * xn, axis=-1, keepdims=True)
    h2 = xn * lax.rsqrt(ms + EPS) * g2_ref[...]
    h2 = h2 * (1.0 + scale_ref[0]) + shift_ref[0]
    hp_ref[...] = _pack_bf16_pairs(h2)
    h_hi = h2.astype(BF16)
    h_lo = (h2 - h_hi.astype(F32)).astype(BF16)
    logits = (jnp.dot(h_hi, wrh_ref[...], preferred_element_type=F32)
              + jnp.dot(h_hi, wrl_ref[...], preferred_element_type=F32)
              + jnp.dot(h_lo, wrh_ref[...], preferred_element_type=F32))
    lg_ref[...] = logits.T[:N_EXPERTS, :]


def _outproj(lay, merged, wout16, x, mod_l, g2, w_router):
    d = lay.d
    tm = lay.row_tile(256)
    lanes = 128
    wr = jnp.zeros((d, lanes), F32).at[:, :N_EXPERTS].set(w_router.astype(F32))
    wr_hi = wr.astype(BF16)
    wr_lo = (wr - wr_hi.astype(F32)).astype(BF16)
    return pl.pallas_call(
        _outproj_kernel,
        grid=(lay.t // tm,),
        in_specs=[
            pl.BlockSpec((tm, d), lambda i: (i, 0)),
            pl.BlockSpec((d, d), lambda i: (0, 0)),
            pl.BlockSpec((tm, d), lambda i: (i, 0)),
            _mod_spec(lay, tm, 2, d),
            pl.BlockSpec((1, d), lambda i: (0, 0)),
            _mod_spec(lay, tm, 3, d),
            _mod_spec(lay, tm, 4, d),
            pl.BlockSpec((d, lanes), lambda i: (0, 0)),
            pl.BlockSpec((d, lanes), lambda i: (0, 0)),
        ],
        out_specs=[
            pl.BlockSpec((tm, d), lambda i: (i, 0)),
            pl.BlockSpec((tm, d // 2), lambda i: (i, 0)),
            pl.BlockSpec((N_EXPERTS, tm), lambda i: (0, i)),
        ],
        out_shape=[
            jax.ShapeDtypeStruct((lay.t, d), F32),
            jax.ShapeDtypeStruct((lay.t, d // 2), U32),
            jax.ShapeDtypeStruct((N_EXPERTS, lay.t), F32),
        ],
        compiler_params=_cparams(("arbitrary",)),
    )(merged, wout16, x, mod_l, g2.reshape(1, d), mod_l, mod_l, wr_hi, wr_lo)


def _route_kernel(lg_ref, br_ref, idx_ref, w_ref, pos_ref, cnt_ref, carry):
    step = pl.program_id(0)
    tt = lg_ref.shape[1]
    per_group = N_EXPERTS // N_EXPERT_GROUPS

    @pl.when(step == 0)
    def _():
        carry[...] = jnp.zeros_like(carry)

    scores = _sigmoid(lg_ref[...])
    biased = scores + br_ref[...]
    sub = lax.broadcasted_iota(I32, (per_group, tt), 0).astype(F32)
    blocks, group_score = [], []
    for g in range(N_EXPERT_GROUPS):
        blk = biased[g * per_group:(g + 1) * per_group, :]
        m1 = jnp.max(blk, axis=0, keepdims=True)
        i1 = jnp.min(jnp.where(blk == m1, sub, float(per_group)), axis=0, keepdims=True)
        m2 = jnp.max(jnp.where(sub == i1, -jnp.inf, blk), axis=0, keepdims=True)
        blocks.append(blk)
        group_score.append(m1 + m2)
    masked = []
    for g in range(N_EXPERT_GROUPS):
        beaten_by = jnp.zeros((1, tt), F32)
        for o in range(N_EXPERT_GROUPS):
            if o == g:
                continue
            wins = (group_score[o] > group_score[g]) | ((group_score[o] == group_score[g]) & (o < g))
            beaten_by = beaten_by + wins.astype(F32)
        masked.append(jnp.where(beaten_by < TOPK_GROUPS, blocks[g], -jnp.inf))
    masked = jnp.concatenate(masked, axis=0)
    eid = lax.broadcasted_iota(I32, (N_EXPERTS, tt), 0).astype(F32)
    chosen, weights = [], []
    onehot = jnp.zeros((N_EXPERTS, tt), F32)
    for _ in range(TOP_K):
        m = jnp.max(masked, axis=0, keepdims=True)
        e = jnp.min(jnp.where(masked == m, eid, float(N_EXPERTS)), axis=0, keepdims=True)
        hit = eid == e
        chosen.append(e)
        weights.append(jnp.sum(jnp.where(hit, scores, 0.0), axis=0, keepdims=True))
        onehot = onehot + hit.astype(F32)
        masked = jnp.where(hit, -jnp.inf, masked)
    total = weights[0]
    for wk in weights[1:]:
        total = total + wk
    earlier = (lax.broadcasted_iota(I32, (tt, tt), 0) < lax.broadcasted_iota(I32, (tt, tt), 1)).astype(BF16)
    rank = carry[...][:, 0:1] + jnp.dot(onehot.astype(BF16), earlier, preferred_element_type=F32)
    for k in range(TOP_K):
        idx_ref[k:k + 1, :] = chosen[k].astype(I32)
        w_ref[k:k + 1, :] = weights[k] / total * ROUTED_SCALE
        pos_ref[k:k + 1, :] = jnp.sum(jnp.where(eid == chosen[k], rank, 0.0), axis=0, keepdims=True).astype(I32)
    for k in range(TOP_K, 8):
        idx_ref[k:k + 1, :] = jnp.zeros((1, tt), I32)
        w_ref[k:k + 1, :] = jnp.zeros((1, tt), F32)
        pos_ref[k:k + 1, :] = jnp.zeros((1, tt), I32)
    carry[...] = carry[...] + jnp.sum(onehot, axis=1, keepdims=True)
    cnt_ref[...] = carry[...]


def _route(logits_t, b_router):
    t = logits_t.shape[1]
    tt = math.gcd(t, 512)
    tok = pl.BlockSpec((8, tt), lambda i: (0, i))
    return pl.pallas_call(
        _route_kernel,
        grid=(t // tt,),
        in_specs=[pl.BlockSpec((N_EXPERTS, tt), lambda i: (0, i)),
                  pl.BlockSpec((N_EXPERTS, 1), lambda i: (0, 0))],
        out_specs=[tok, tok, tok, pl.BlockSpec((N_EXPERTS, 128), lambda i: (0, 0))],
        out_shape=[jax.ShapeDtypeStruct((8, t), I32), jax.ShapeDtypeStruct((8, t), F32),
                   jax.ShapeDtypeStruct((8, t), I32), jax.ShapeDtypeStruct((N_EXPERTS, 128), F32)],
        scratch_shapes=[pltpu.VMEM((N_EXPERTS, 128), F32)],
        compiler_params=_cparams(("arbitrary",)),
    )(logits_t, b_router.astype(F32).reshape(N_EXPERTS, 1))


def _dest_kernel(idx_ref, pos_ref, start_ref, o_ref, *, tt):
    eid = lax.broadcasted_iota(I32, (N_EXPERTS, tt), 0)
    start = start_ref[...]
    for k in range(TOP_K):
        first = jnp.sum(jnp.where(eid == idx_ref[k:k + 1, :], start, 0.0), axis=0, keepdims=True)
        o_ref[k:k + 1, :] = first.astype(I32) + pos_ref[k:k + 1, :]
    for k in range(TOP_K, 8):
        o_ref[k:k + 1, :] = jnp.zeros((1, tt), I32)


def _dest_rows(idx_t, pos_t, pad_start):
    t = idx_t.shape[1]
    tt = math.gcd(t, 512)
    tok = pl.BlockSpec((8, tt), lambda i: (0, i))
    return pl.pallas_call(
        functools.partial(_dest_kernel, tt=tt),
        grid=(t // tt,),
        in_specs=[tok, tok, pl.BlockSpec((N_EXPERTS, 1), lambda i: (0, 0))],
        out_specs=tok,
        out_shape=jax.ShapeDtypeStruct((8, t), I32),
        compiler_params=_cparams(("arbitrary",)),
    )(idx_t, pos_t, pad_start.astype(F32).reshape(N_EXPERTS, 1))


def _sc_layout(n_pairs, period):
    info = plsc.get_sparse_core_info()
    workers = info.num_cores * info.num_subcores
    per_worker = n_pairs // workers
    chunk = math.gcd(math.gcd(per_worker, period), SC_STREAM_ROWS)
    assert per_worker * workers == n_pairs and chunk % 8 == 0
    return info.num_cores, per_worker, chunk


def _sc_scatter_rows(rows, dest, n_out):
    t, width = rows.shape
    copies = dest.shape[0] // t
    n_cores, per_worker, chunk = _sc_layout(t, t)
    mesh = plsc.VectorSubcoreMesh(core_axis_name="core", subcore_axis_name="subcore")

    @functools.partial(
        pl.kernel, mesh=mesh, out_type=jax.ShapeDtypeStruct((n_out, width), rows.dtype),
        scratch_types=[pltpu.VMEM((chunk,), I32), pltpu.VMEM((chunk, width), rows.dtype)])
    def scatter(rows_hbm, dest_hbm, out_hbm, dest_v, rows_v):
        base = (lax.axis_index("subcore") * n_cores + lax.axis_index("core")) * per_worker

        @pl.loop(0, per_worker // chunk)
        def _(j):
            tok = base + j * chunk
            pltpu.sync_copy(rows_hbm.at[pl.ds(tok, chunk)], rows_v)
            for k in range(copies):
                pltpu.sync_copy(dest_hbm.at[pl.ds(k * t + tok, chunk)], dest_v)
                pltpu.sync_copy(rows_v, out_hbm.at[dest_v])

    return scatter(rows, dest)


def _sc_gather_rows(table, dest):
    width = table.shape[1]
    n_pairs = dest.shape[0]
    n_cores, per_worker, chunk = _sc_layout(n_pairs, n_pairs)
    mesh = plsc.VectorSubcoreMesh(core_axis_name="core", subcore_axis_name="subcore")

    @functools.partial(
        pl.kernel, mesh=mesh, out_type=jax.ShapeDtypeStruct((n_pairs, width), table.dtype),
        scratch_types=[pltpu.VMEM((chunk,), I32), pltpu.VMEM((chunk, width), table.dtype)])
    def gather(table_hbm, dest_hbm, out_hbm, dest_v, rows_v):
        base = (lax.axis_index("subcore") * n_cores + lax.axis_index("core")) * per_worker

        @pl.loop(0, per_worker // chunk)
        def _(j):
            pair = base + j * chunk
            pltpu.sync_copy(dest_hbm.at[pl.ds(pair, chunk)], dest_v)
            pltpu.sync_copy(table_hbm.at[dest_v], rows_v)
            pltpu.sync_copy(rows_v, out_hbm.at[pl.ds(pair, chunk)])

    return gather(table, dest)


def _expert_kernel(be_ref, xs_ref, wg_ref, wu_ref, wd_ref, ys_ref, wg16, wu16, wd16):
    i = pl.program_id(0)
    changed = jnp.logical_or(i == 0, be_ref[i] != be_ref[jnp.maximum(i - 1, 0)])

    @pl.when(changed)
    def _():
        rows = 256
        d = wg16.shape[0]

        def cast_in(r, carry):
            sl = pl.ds(pl.multiple_of(r * rows, rows), rows)
            wg16[sl, :] = wg_ref[0, sl, :].astype(BF16)
            wu16[sl, :] = wu_ref[0, sl, :].astype(BF16)
            return carry

        lax.fori_loop(0, d // rows, cast_in, 0)

        def cast_down(r, carry):
            sl = pl.ds(pl.multiple_of(r * 128, 128), 128)
            wd16[sl, :] = wd_ref[0, sl, :].astype(BF16)
            return carry

        lax.fori_loop(0, D_EXPERT // 128, cast_down, 0)

    n_blocks = pl.num_programs(0)
    used = i < be_ref[n_blocks]
    held = be_ref[n_blocks + 1 + i]
    rows = xs_ref.shape[0]
    half_rows = rows // 2

    def ffn(n):
        live = lax.broadcasted_iota(I32, (n, 1), 0) < held
        hi, lo = _unpack_bf16_pairs(jnp.where(live, xs_ref[0:n, :], jnp.uint32(0)))
        x = jnp.concatenate([hi, lo], axis=1).astype(BF16)
        gate = jnp.dot(x, wg16[...], preferred_element_type=F32)
        up = jnp.dot(x, wu16[...], preferred_element_type=F32)
        act = (gate * _sigmoid(gate) * up).astype(BF16)
        ys_ref[0:n, :] = _pack_bf16_pairs(jnp.dot(act, wd16[...], preferred_element_type=F32))

    @pl.when(jnp.logical_and(used, held > half_rows))
    def _():
        ffn(rows)

    @pl.when(jnp.logical_and(used, held <= half_rows))
    def _():
        ffn(half_rows)
        ys_ref[half_rows:, :] = jnp.zeros((rows - half_rows, ys_ref.shape[1]), ys_ref.dtype)

    @pl.when(jnp.logical_not(used))
    def _():
        ys_ref[...] = jnp.zeros_like(ys_ref)


def _experts(xs, block_e, layer, w_gate, w_up, w_down):
    n_rows, half = xs.shape
    d = 2 * half
    br = EXPERT_ROWS
    grid_spec = pltpu.PrefetchScalarGridSpec(
        num_scalar_prefetch=1,
        grid=(n_rows // br,),
        in_specs=[
            pl.BlockSpec((br, half), lambda i, be: (i, 0)),
            pl.BlockSpec((None, 1, d, D_EXPERT), lambda i, be: (layer, be[i], 0, 0)),
            pl.BlockSpec((None, 1, d, D_EXPERT), lambda i, be: (layer, be[i], 0, 0)),
            pl.BlockSpec((None, 1, D_EXPERT, d), lambda i, be: (layer, be[i], 0, 0)),
        ],
        out_specs=pl.BlockSpec((br, half), lambda i, be: (i, 0)),
        scratch_shapes=[pltpu.VMEM((d, D_EXPERT), BF16), pltpu.VMEM((d, D_EXPERT), BF16),
                        pltpu.VMEM((D_EXPERT, d), BF16)],
    )
    return pl.pallas_call(
        _expert_kernel,
        grid_spec=grid_spec,
        out_shape=jax.ShapeDtypeStruct((n_rows, half), U32),
        compiler_params=_cparams(("arbitrary",)),
    )(block_e, xs, w_gate, w_up, w_down)


def _shared_kernel(hp_ref, wsg_ref, wsu_ref, wsd_ref, o_ref):
    hi, lo = _unpack_bf16_pairs(hp_ref[...])
    h2 = jnp.concatenate([hi, lo], axis=1).astype(BF16)
    sg = jnp.dot(h2, wsg_ref[...], preferred_element_type=F32)
    su = jnp.dot(h2, wsu_ref[...], preferred_element_type=F32)
    shared = jnp.dot((sg * _sigmoid(sg) * su).astype(BF16), wsd_ref[...], preferred_element_type=F32)
    o_ref[...] = shared.astype(o_ref.dtype)


def _shared_expert(lay, h2p, wsg16, wsu16, wsd16):
    d = lay.d
    tm = lay.row_tile(512)
    return pl.pallas_call(
        _shared_kernel,
        grid=(lay.t // tm,),
        in_specs=[pl.BlockSpec((tm, d // 2), lambda i: (i, 0)),
                  pl.BlockSpec((d, D_EXPERT), lambda i: (0, 0)),
                  pl.BlockSpec((d, D_EXPERT), lambda i: (0, 0)),
                  pl.BlockSpec((D_EXPERT, d), lambda i: (0, 0))],
        out_specs=pl.BlockSpec((tm, d), lambda i: (i, 0)),
        out_shape=jax.ShapeDtypeStruct((lay.t, d), BF16),
        compiler_params=_cparams(("arbitrary",)),
    )(h2p, wsg16, wsu16, wsd16)


def _combine_kernel(yk_ref, wsel_ref, sh_ref, x_ref, gate_ref, fg_ref, *outs, tt, final, n_ctx_tiles):
    shared = sh_ref[...].astype(F32)
    half = yk_ref.shape[2]
    r_hi = jnp.zeros((tt, half), F32)
    r_lo = jnp.zeros((tt, half), F32)
    for k in range(TOP_K):
        y_hi, y_lo = _unpack_bf16_pairs(yk_ref[k])
        wk = wsel_ref[:, k:k + 1]
        r_hi = r_hi + wk * y_hi
        r_lo = r_lo + wk * y_lo
    routed = jnp.concatenate([r_hi, r_lo], axis=1)
    out = x_ref[...] + gate_ref[0] * (routed + shared)
    if not final:
        outs[0][...] = out
        return
    ms = jnp.mean(out * out, axis=-1, keepdims=True)
    out = out * lax.rsqrt(ms + EPS) * fg_ref[...]
    ctx_ref, lat_ref = outs
    is_ctx = pl.program_id(0) < n_ctx_tiles

    @pl.when(is_ctx)
    def _():
        ctx_ref[...] = out

    @pl.when(jnp.logical_not(is_ctx))
    def _():
        lat_ref[...] = out


def _combine(lay, y_by_k, wsel, shared, x, mod_l, final_g, tt, final):
    d = lay.d
    half = d // 2
    nct = lay.tc // tt
    if final:
        out_specs = [pl.BlockSpec((tt, d), lambda i: (jnp.minimum(i, nct - 1), 0)),
                     pl.BlockSpec((tt, d), lambda i: (jnp.maximum(i - nct, 0), 0))]
        out_shape = [jax.ShapeDtypeStruct((lay.tc, d), F32), jax.ShapeDtypeStruct((lay.ts, d), F32)]
    else:
        out_specs = pl.BlockSpec((tt, d), lambda i: (i, 0))
        out_shape = jax.ShapeDtypeStruct((lay.t, d), F32)
    return pl.pallas_call(
        functools.partial(_combine_kernel, tt=tt, final=final, n_ctx_tiles=nct),
        grid=(lay.t // tt,),
        in_specs=[
            pl.BlockSpec((TOP_K, tt, half), lambda i: (0, i, 0)),
            pl.BlockSpec((tt, 8), lambda i: (i, 0)),
            pl.BlockSpec((tt, d), lambda i: (i, 0)),
            pl.BlockSpec((tt, d), lambda i: (i, 0)),
            _mod_spec(lay, tt, 5, d),
            pl.BlockSpec((1, d), lambda i: (0, 0)),
        ],
        out_specs=out_specs,
        out_shape=out_shape,
        compiler_params=_cparams(("arbitrary",)),
    )(y_by_k, wsel, shared, x, mod_l, final_g.reshape(1, d))


def _moe(lay, layer, x, h2p, logits_t, mod_l, b_router, w_e_gate, w_e_up, w_e_down, wsg16, wsu16, wsd16, final_g,
         final):
    t = lay.t
    idx_t, w_t, pos_t, counts = _route(logits_t, b_router)
    br = EXPERT_ROWS
    counts = counts[:, 0].astype(I32)
    padded = (counts + br - 1) // br * br
    pad_end = jnp.cumsum(padded)
    pad_start = pad_end - padded
    n_blocks = -(-(t * TOP_K + N_EXPERTS * (br - 1)) // br)
    n_rows = n_blocks * br
    first_row = jnp.arange(n_blocks, dtype=I32) * br
    block_e = jnp.minimum(jnp.sum((pad_end[None, :] <= first_row[:, None]).astype(I32), axis=1), N_EXPERTS - 1)
    own = block_e[:, None] == jnp.arange(N_EXPERTS, dtype=I32)[None, :]
    rows_end = jnp.sum(jnp.where(own, (pad_start + counts)[None, :], 0), axis=1)
    held = jnp.clip(rows_end - first_row, 0, br)
    block_meta = jnp.concatenate([block_e, pad_end[-1:] // br, held]).astype(I32)
    dest = _dest_rows(idx_t, pos_t, pad_start)[:TOP_K].reshape(TOP_K * t)
    xs = _sc_scatter_rows(h2p, dest, n_rows)
    shared = _shared_expert(lay, h2p, wsg16, wsu16, wsd16)
    ys = _experts(xs, block_meta, layer, w_e_gate, w_e_up, w_e_down)
    y_by_k = _sc_gather_rows(ys, dest).reshape(TOP_K, t, h2p.shape[1])
    return _combine(lay, y_by_k, w_t.T, shared, x, mod_l, final_g, lay.row_tile(256), final)


def kernel(x_prompt, x_sample, c, cache_k, cache_v, state_ssm_re, state_ssm_im, c_ctx, w_ada, b_ada, norm1_g, norm2_g, w_in, w_gates, b_gates, ssm_lam_re, ssm_lam_im, ssm_log_dt, ssm_b_re, ssm_b_im, ssm_c_re, ssm_c_im, ssm_d, ssm_w_glu, conv_w, attn_sink, w_br_ssm, w_br_attn, w_br_conv, w_out, w_router, b_router, w_e_gate, w_e_up, w_e_down, w_s_gate, w_s_up, w_s_down, final_g):
    bc, lc, d = x_prompt.shape
    bs, ls, _ = x_sample.shape
    depth = w_in.shape[0]
    lay = _Layout(bc, lc, bs, ls, d)
    assert 1 + bs <= MOD_ROWS

    x = jnp.concatenate([x_prompt.reshape(lay.tc, d), x_sample.reshape(lay.ts, d)], axis=0)
    cvec = jnp.zeros((MOD_ROWS, d), F32).at[0].set(c_ctx).at[1:1 + bs].set(c)
    mod = _adaln(cvec, w_ada, b_ada).reshape(depth, MOD_ROWS * 6, 1, d)
    rope_cos, rope_sin = _rope_tables(lay, lay.row_tile(1024))
    s5_perm = _s5_lane_permutation()
    s5_mats = jax.vmap(functools.partial(_s5_matrices, n_steps=8))(
        ssm_lam_re, ssm_lam_im, ssm_log_dt, ssm_b_re, ssm_b_im, ssm_c_re, ssm_c_im, ssm_d)

    ks, vs, s_re, s_im = [], [], [], []
    for l in range(depth):
        mod_l = mod[l]
        h16, y16, kv32 = _inproj(lay, x, mod_l, norm1_g[l], w_in[l].astype(BF16), rope_cos, rope_sin)
        ks.append(kv32[:lay.tc, :KV_WIDTH].reshape(bc, lc, N_KV_HEADS, HEAD_DIM))
        vs.append(kv32[:lay.tc, KV_WIDTH:].reshape(bc, lc, N_KV_HEADS, HEAD_DIM))

        attn = _attention(lay, y16, attn_sink[l].astype(F32), cache_k[:, l], cache_v[:, l])

        u0 = ATTN_WIDTH + 2 * KV_WIDTH
        u = y16[:, u0:u0 + SSM_WIDTH]
        mats = (s5_perm,) + tuple(m[l] for m in s5_mats)
        y_ssm, f_re, f_im = _s5_both(lay, u, mats, state_ssm_re[:, l], state_ssm_im[:, l])
        s_re.append(f_re)
        s_im.append(f_im)

        merged = _merge(lay, h16, y_ssm, attn, y16, conv_w[l], ssm_w_glu[l].astype(BF16),
                        w_gates[l].astype(BF16), b_gates[l].reshape(1, -1), w_br_ssm[l].astype(BF16),
                        w_br_attn[l].astype(BF16), w_br_conv[l].astype(BF16))
        x, h2p, logits_t = _outproj(lay, merged, w_out[l].astype(BF16), x, mod_l, norm2_g[l],
                                    w_router[l])
        x = _moe(lay, l, x, h2p, logits_t, mod_l, b_router[l], w_e_gate, w_e_up, w_e_down,
                 w_s_gate[l].astype(BF16), w_s_up[l].astype(BF16), w_s_down[l].astype(BF16),
                 final_g, l == depth - 1)

    y_prompt = x[0].reshape(bc, lc, d)
    y_sample = x[1].reshape(bs, ls, d)
    return (y_prompt, y_sample, jnp.stack(ks, axis=1), jnp.stack(vs, axis=1),
            jnp.stack(s_re, axis=1), jnp.stack(s_im, axis=1))
```

```python
import functools
import math

import jax
import jax.numpy as jnp
from jax import lax
from jax.experimental import pallas as pl
from jax.experimental.pallas import tpu as pltpu
from jax.experimental.pallas import tpu_sc as plsc

HEAD_DIM = 128
N_HEADS = 8
N_KV_HEADS = 2
GROUP = N_HEADS // N_KV_HEADS
ATTN_WIDTH = N_HEADS * HEAD_DIM
KV_WIDTH = N_KV_HEADS * HEAD_DIM
WINDOW = 128
ATTN_BLOCK = 128
ATTN_Q_ROWS = 256
ATTN_SCALE = HEAD_DIM ** -0.5
LOG2_E = 1.4426950408889634
ROPE_BASE = 10000.0
ROT_F = HEAD_DIM // 4
GRID_W = 64
SSM_WIDTH = 512
SSM_CH = 16
SSM_GROUPS = SSM_WIDTH // SSM_CH
SSM_STATE = 64
SSM_CHUNK = 16
S5_BUNDLE = 8
S5_BLOCK_ROWS = 256
CONV_WIDTH = 512
N_BRANCHES = 3
IN_WIDTH = ATTN_WIDTH + 2 * KV_WIDTH + SSM_WIDTH + 3 * CONV_WIDTH
N_EXPERTS = 64
TOP_K = 6
N_EXPERT_GROUPS = 8
TOPK_GROUPS = 4
D_EXPERT = 512
ROUTED_SCALE = 2.5
EPS = 1e-6
NEG_INF = -1e30

COL_TILE = 512
MOD_ROWS = 16
EXPERT_ROWS = 512
SC_STREAM_ROWS = 64
VMEM_LIMIT_V7X = 56 * 1024 * 1024

F32 = jnp.float32
BF16 = jnp.bfloat16
I32 = jnp.int32
U32 = jnp.uint32


def _cparams(sem, vmem=VMEM_LIMIT_V7X):
    return pltpu.CompilerParams(dimension_semantics=sem, vmem_limit_bytes=vmem)


def _sigmoid(x):
    return 1.0 / (1.0 + jnp.exp(-x))


def _pack_bf16_pairs(v):
    n = v.shape[1] // 2
    hi = lax.bitcast_convert_type(v[:, :n].astype(BF16).astype(F32), U32)
    lo = lax.bitcast_convert_type(v[:, n:].astype(BF16).astype(F32), U32)
    return hi | (lo >> 16)


def _unpack_bf16_pairs(p):
    hi = lax.bitcast_convert_type(p & jnp.uint32(0xFFFF0000), F32)
    lo = lax.bitcast_convert_type(p << 16, F32)
    return hi, lo


def _adaln_kernel(c_ref, w_ref, b_ref, o_ref):
    c = c_ref[...]
    s = (c * _sigmoid(c)).astype(BF16)
    o_ref[0] = jnp.dot(s, w_ref[0].astype(BF16), preferred_element_type=F32) + b_ref[0]


def _adaln(cvec, w_ada, b_ada):
    depth, d, n6 = w_ada.shape
    tn = math.gcd(1024, n6)
    return pl.pallas_call(
        _adaln_kernel,
        grid=(depth, n6 // tn),
        in_specs=[
            pl.BlockSpec((MOD_ROWS, d), lambda l, n: (0, 0)),
            pl.BlockSpec((1, d, tn), lambda l, n: (l, 0, n)),
            pl.BlockSpec((1, 1, tn), lambda l, n: (l, 0, n)),
        ],
        out_specs=pl.BlockSpec((1, MOD_ROWS, tn), lambda l, n: (l, 0, n)),
        out_shape=jax.ShapeDtypeStruct((depth, MOD_ROWS, n6), F32),
        compiler_params=_cparams(("arbitrary", "arbitrary")),
    )(cvec, w_ada, b_ada.reshape(depth, 1, n6))


class _Layout:
    def __init__(self, n_ctx_seq, len_ctx, n_lat_seq, len_lat, d_model):
        self.bc, self.lc, self.bs, self.ls, self.d = n_ctx_seq, len_ctx, n_lat_seq, len_lat, d_model
        self.tc = n_ctx_seq * len_ctx
        self.ts = n_lat_seq * len_lat
        self.t = self.tc + self.ts

    def row_tile(self, want):
        tm = math.gcd(math.gcd(self.tc, self.ls), want)
        assert tm % 16 == 0
        return tm

    def mod_index(self, i, tm):
        nct, tps = self.tc // tm, self.ls // tm
        return jnp.where(i < nct, 0, 1 + (i - nct) // tps)

    def seq_pos(self, rows, i, tm):
        is_lat = i >= self.tc // tm
        return jnp.where(is_lat, (rows - self.tc) % self.ls, rows % self.lc), jnp.where(is_lat, self.ls, self.lc)


def _mod_spec(lay, tm, slot, d):
    return pl.BlockSpec((1, 1, d), lambda i, *_: (lay.mod_index(i, tm) * 6 + slot, 0, 0))


def _rope(z, cos, sin_signed, first_half):
    swapped = jnp.where(first_half, pltpu.roll(z, HEAD_DIM - ROT_F, 1), pltpu.roll(z, ROT_F, 1))
    return z * cos + swapped * sin_signed


def _inproj_kernel(x_ref, shift_ref, scale_ref, g_ref, w_ref, cos_ref, sin_ref, h_ref, y_ref, kv_ref):
    n = pl.program_id(1)
    n_q = ATTN_WIDTH // COL_TILE

    @pl.when(n == 0)
    def _():
        x = x_ref[...]
        ms = jnp.mean(x * x, axis=-1, keepdims=True)
        y = x * lax.rsqrt(ms + EPS) * g_ref[...]
        h_ref[...] = (y * (1.0 + scale_ref[0]) + shift_ref[0]).astype(BF16)

    acc = jnp.dot(h_ref[...], w_ref[...], preferred_element_type=F32)

    def rotated(n_heads):
        cos, sin = cos_ref[...], sin_ref[...]
        first_half = (lax.broadcasted_iota(I32, cos.shape, 1) % (2 * ROT_F)) < ROT_F
        parts = [_rope(acc[:, s * HEAD_DIM:(s + 1) * HEAD_DIM], cos, sin, first_half) for s in range(n_heads)]
        parts.append(acc[:, n_heads * HEAD_DIM:])
        return jnp.concatenate(parts, axis=1) if n_heads * HEAD_DIM < COL_TILE else jnp.concatenate(parts[:-1], axis=1)

    @pl.when(n < n_q)
    def _():
        y_ref[...] = (rotated(COL_TILE // HEAD_DIM) * (ATTN_SCALE * LOG2_E)).astype(BF16)

    @pl.when(n == n_q)
    def _():
        kv_ref[...] = acc
        y_ref[...] = rotated(N_KV_HEADS).astype(BF16)

    @pl.when(n > n_q)
    def _():
        y_ref[...] = acc.astype(BF16)


def _inproj(lay, x, mod_l, g1, w_in16, rope_cos, rope_sin):
    d = lay.d
    tm = lay.row_tile(1024)
    nct, tps = lay.tc // tm, lay.ls // tm

    def rope_idx(i, n):
        return (jnp.where(i < nct, 0, 1 + (i - nct) % tps), 0)

    return pl.pallas_call(
        _inproj_kernel,
        grid=(lay.t // tm, IN_WIDTH // COL_TILE),
        in_specs=[
            pl.BlockSpec((tm, d), lambda i, n: (i, 0)),
            _mod_spec(lay, tm, 0, d),
            _mod_spec(lay, tm, 1, d),
            pl.BlockSpec((1, d), lambda i, n: (0, 0)),
            pl.BlockSpec((d, COL_TILE), lambda i, n: (0, n)),
            pl.BlockSpec((tm, HEAD_DIM), rope_idx),
            pl.BlockSpec((tm, HEAD_DIM), rope_idx),
        ],
        out_specs=[
            pl.BlockSpec((tm, d), lambda i, n: (i, 0)),
            pl.BlockSpec((tm, COL_TILE), lambda i, n: (i, n)),
            pl.BlockSpec((tm, 2 * KV_WIDTH), lambda i, n: (i, 0)),
        ],
        out_shape=[
            jax.ShapeDtypeStruct((lay.t, d), BF16),
            jax.ShapeDtypeStruct((lay.t, IN_WIDTH), BF16),
            jax.ShapeDtypeStruct((lay.t, 2 * KV_WIDTH), F32),
        ],
        compiler_params=_cparams(("arbitrary", "arbitrary")),
    )(x, mod_l, mod_l, g1.reshape(1, d), w_in16, rope_cos, rope_sin)


def _rope_tables(lay, tm):
    t = jnp.arange(lay.ls)
    row = (t // GRID_W).astype(F32)
    col = (t % GRID_W).astype(F32)
    inv = ROPE_BASE ** (-jnp.arange(ROT_F, dtype=F32) / ROT_F)
    ar, ac = row[:, None] * inv, col[:, None] * inv
    cos = jnp.concatenate([jnp.cos(ar), jnp.cos(ar), jnp.cos(ac), jnp.cos(ac)], axis=1)
    sin = jnp.concatenate([-jnp.sin(ar), jnp.sin(ar), -jnp.sin(ac), jnp.sin(ac)], axis=1)
    cos = jnp.concatenate([jnp.ones((tm, HEAD_DIM), F32), cos], axis=0)
    sin = jnp.concatenate([jnp.zeros((tm, HEAD_DIM), F32), sin], axis=0)
    return cos, sin


def _attend(q, sink_ref, j, parts):
    nq = q.shape[0]
    q4 = jnp.concatenate([q[:, g * HEAD_DIM:(g + 1) * HEAD_DIM] for g in range(GROUP)], axis=0)
    sink = jnp.concatenate([jnp.full((nq, 1), sink_ref[j * GROUP + g] * LOG2_E, F32) for g in range(GROUP)], axis=0)
    scores = []
    m = sink
    for k, _, mask in parts:
        s = lax.dot_general(q4, k, (((1,), (1,)), ((), ())), preferred_element_type=F32)
        if mask is not None:
            s = jnp.where(mask, s, NEG_INF)
        scores.append(s)
        m = jnp.maximum(m, jnp.max(s, axis=-1, keepdims=True))
    den = jnp.exp2(sink - m)
    out = jnp.zeros((GROUP * nq, HEAD_DIM), F32)
    for s, (_, v, _) in zip(scores, parts):
        p = jnp.exp2(s - m)
        den = den + jnp.sum(p, axis=-1, keepdims=True)
        out = out + jnp.dot(p.astype(BF16), v, preferred_element_type=F32)
    out = out / den
    return jnp.concatenate([out[g * nq:(g + 1) * nq] for g in range(GROUP)], axis=1)


def _head(x, j, width=HEAD_DIM):
    return x[:, j * width:(j + 1) * width]


def _attn_kernel(sink_ref, q_ref, kp_ref, kc_ref, kn_ref, vp_ref, vc_ref, vn_ref, ck_ref, cv_ref, o_ref, *,
                 n_ctx_blocks, blocks_per_seq, seq_len):
    i = pl.program_id(0)
    q = q_ref[...]

    @pl.when(i < n_ctx_blocks)
    def _():
        k, v = kc_ref[...], vc_ref[...]
        outs = [_attend(_head(q, j, GROUP * HEAD_DIM), sink_ref, j, [(_head(k, j), _head(v, j), None)])
                for j in range(N_KV_HEADS)]
        o_ref[...] = jnp.concatenate(outs, axis=1).astype(o_ref.dtype)

    @pl.when(i >= n_ctx_blocks)
    def _():
        blk = (i - n_ctx_blocks) % blocks_per_seq
        kw = jnp.concatenate([kp_ref[...], kc_ref[...], kn_ref[...]], axis=0)
        vw = jnp.concatenate([vp_ref[...], vc_ref[...], vn_ref[...]], axis=0)
        ck = ck_ref[...].astype(BF16)
        cv = cv_ref[...].astype(BF16)
        shape = (GROUP * ATTN_Q_ROWS, ATTN_Q_ROWS + 2 * ATTN_BLOCK)
        qoff = lax.broadcasted_iota(I32, shape, 0) % ATTN_Q_ROWS
        koff = lax.broadcasted_iota(I32, shape, 1) - ATTN_BLOCK
        kabs = koff + blk * ATTN_Q_ROWS
        mask = (jnp.abs(qoff - koff) <= WINDOW) & (kabs >= 0) & (kabs < seq_len)
        outs = [_attend(_head(q, j, GROUP * HEAD_DIM), sink_ref, j,
                        [(_head(kw, j), _head(vw, j), mask), (_head(ck, j), _head(cv, j), None)])
                for j in range(N_KV_HEADS)]
        o_ref[...] = jnp.concatenate(outs, axis=1).astype(o_ref.dtype)


def _attention(lay, y16, sink, cache_k_l, cache_v_l):
    smem = pl.BlockSpec(memory_space=pltpu.SMEM)
    kcol, vcol = ATTN_WIDTH // KV_WIDTH, (ATTN_WIDTH + KV_WIDTH) // KV_WIDTH
    qr = ATTN_Q_ROWS
    assert lay.lc == qr and lay.ls % qr == 0 and qr % ATTN_BLOCK == 0
    nblk = lay.ls // qr
    nctb = lay.tc // qr
    per = qr // ATTN_BLOCK
    last = lay.t // ATTN_BLOCK - 1
    past = cache_k_l.shape[1]

    def edge(col, delta):
        return pl.BlockSpec((ATTN_BLOCK, KV_WIDTH), lambda i: (jnp.clip(i * per + delta, 0, last), col))

    def own(col):
        return pl.BlockSpec((qr, KV_WIDTH), lambda i: (i, col))

    cspec = pl.BlockSpec((None, past, KV_WIDTH), lambda i: (jnp.maximum(i - nctb, 0) // nblk, 0, 0))
    return pl.pallas_call(
        functools.partial(_attn_kernel, n_ctx_blocks=nctb, blocks_per_seq=nblk, seq_len=lay.ls),
        grid=(lay.t // qr,),
        in_specs=[
            smem,
            pl.BlockSpec((qr, ATTN_WIDTH), lambda i: (i, 0)),
            edge(kcol, -1), own(kcol), edge(kcol, per),
            edge(vcol, -1), own(vcol), edge(vcol, per),
            cspec, cspec,
        ],
        out_specs=pl.BlockSpec((qr, ATTN_WIDTH), lambda i: (i, 0)),
        out_shape=jax.ShapeDtypeStruct((lay.t, ATTN_WIDTH), BF16),
        compiler_params=_cparams(("arbitrary",)),
    )(sink, y16, y16, y16, y16, y16, y16, y16,
      cache_k_l.reshape(lay.bs, past, KV_WIDTH), cache_v_l.reshape(lay.bs, past, KV_WIDTH))


def _s5_matrices(lam_re, lam_im, log_dt, b_re, b_im, c_re, c_im, d_skip, n_steps):
    q, p, g, n = SSM_CHUNK, SSM_CH, SSM_GROUPS, SSM_STATE
    lam = lax.complex(lam_re.astype(F32), lam_im.astype(F32))
    dt = jnp.exp(log_dt.astype(F32))[..., None]
    lam_dt = lam * dt
    lam_bar = jnp.exp(lam_dt)
    b_bar = ((lam_bar - 1.0) / lam)[..., None] * lax.complex(b_re.astype(F32), b_im.astype(F32))
    c_mat = lax.complex(c_re.astype(F32), c_im.astype(F32))
    steps = jnp.arange(q + 1, dtype=F32)
    pw = jnp.exp(lam_dt[:, None] * steps[None, :, None, None])
    kern = jnp.real(jnp.einsum('dgpn,dkgn,dgnr->dkgpr', c_mat, pw[:, :q], b_bar))
    tau_in = jnp.arange(q)[:, None]
    tau_out = jnp.arange(q)[None, :]
    lag_f = tau_out - tau_in
    lag_b = tau_in - tau_out
    lags = jnp.arange(q)
    pick_f = (lag_f[:, :, None] == lags).astype(F32)
    pick_b = (lag_b[:, :, None] == lags).astype(F32)
    exact = lax.Precision.HIGHEST
    kf = jnp.einsum('abk,kgpr->garbp', pick_f, kern[0], precision=exact)
    kb = jnp.einsum('abk,kgpr->garbp', pick_b, kern[1], precision=exact)
    m = kf + kb
    eye_q = jnp.eye(q, dtype=F32)[None, :, None, :, None]
    eye_p = jnp.eye(p, dtype=F32)[None, None, :, None, :]
    m = m + eye_q * eye_p * d_skip.astype(F32).reshape(g, 1, p, 1, 1)
    m = m.reshape(g, q * p, q * p)
    ws_f = pw[0, :q][::-1][:, :, :, None] * b_bar[0][None]
    ws_b = pw[1, :q][:, :, :, None] * b_bar[1][None]

    def cols(w):
        return w.transpose(1, 0, 3, 2).reshape(g, q * p, n)

    w1 = jnp.concatenate([m, jnp.real(cols(ws_f)), jnp.real(cols(ws_b)),
                          jnp.imag(cols(ws_f)), jnp.imag(cols(ws_b))], axis=2)
    cy_f = c_mat[0][None] * pw[0, 1:][:, :, None, :]
    cy_b = c_mat[1][None] * pw[1, 1:][::-1][:, :, None, :]

    def rows(w):
        return w.transpose(1, 3, 0, 2).reshape(g, n, q * p)

    wy = jnp.concatenate([jnp.real(rows(cy_f)), jnp.real(rows(cy_b)),
                          -jnp.imag(rows(cy_f)), -jnp.imag(rows(cy_b))], axis=1)
    hops = (q * 2.0 ** jnp.arange(8, dtype=F32))[None, :, None, None]
    a = jnp.exp(lam_dt[:, None] * hops)
    a = jnp.concatenate([a[0], a[1]], axis=-1).transpose(1, 0, 2)
    assert n_steps <= 8
    gb, nbun = S5_BUNDLE, g // S5_BUNDLE
    ab = a.reshape(nbun, gb, 8, 2 * n).transpose(0, 2, 1, 3).reshape(nbun, 8, gb * 2 * n)
    return (w1.astype(BF16).reshape(nbun, gb, q * p, q * p + 4 * n),
            wy.astype(BF16).reshape(nbun, gb, 4 * n, q * p), jnp.real(ab), jnp.imag(ab))


def _s5_lane_permutation():
    q, p, gb = SSM_CHUNK, SSM_CH, S5_BUNDLE
    lane = jnp.arange(q * gb * p)
    target = ((lane % (gb * p)) // p) * (q * p) + (lane // (gb * p)) * p + lane % p
    return (target[:, None] == lane[None, :]).astype(BF16)


def _s5_kernel(u_ref, perm_ref, w1_ref, wy_ref, are_ref, aim_ref, h0re_ref, h0im_ref, y_ref, fre_ref, fim_ref, *,
               nb, nc):
    r_tot = nb * nc
    w = SSM_CHUNK * SSM_CH
    ns = 2 * SSM_STATE
    n2 = S5_BUNDLE * ns
    perm = perm_ref[...]
    x = jnp.dot(u_ref[0], perm, preferred_element_type=F32).astype(BF16)
    proj = [jnp.dot(x[:, g * w:(g + 1) * w], w1_ref[0, g], preferred_element_type=F32) for g in range(S5_BUNDLE)]
    d_re = jnp.concatenate([pg[:, w:w + ns] for pg in proj], axis=1)
    d_im = jnp.concatenate([pg[:, w + ns:] for pg in proj], axis=1)
    h0re_ref, h0im_ref, fre_ref, fim_ref = (r.at[0] for r in (h0re_ref, h0im_ref, fre_ref, fim_ref))
    row = lax.broadcasted_iota(I32, (r_tot, n2), 0)
    chunk = row % nc
    seq = row // nc
    fwd = (lax.broadcasted_iota(I32, (r_tot, n2), 1) % (2 * SSM_STATE)) < SSM_STATE

    def previous(x, dist):
        valid = (fwd & (chunk >= dist)) | (~fwd & (chunk < nc - dist))
        moved = jnp.where(fwd, pltpu.roll(x, dist, 0), pltpu.roll(x, r_tot - dist, 0))
        return jnp.where(valid, moved, 0.0)

    h0_re = jnp.zeros((r_tot, n2), F32)
    h0_im = jnp.zeros((r_tot, n2), F32)
    for b in range(nb):
        h0_re = jnp.where(seq == b, h0re_ref[0, b:b + 1, :], h0_re)
        h0_im = jnp.where(seq == b, h0im_ref[0, b:b + 1, :], h0_im)
    first = (fwd & (chunk == 0)) | (~fwd & (chunk == nc - 1))
    e_re = jnp.where(first, h0_re, previous(d_re, 1))
    e_im = jnp.where(first, h0_im, previous(d_im, 1))
    k = 0
    while (1 << k) < nc:
        a_re = are_ref[0, k:k + 1, :]
        a_im = aim_ref[0, k:k + 1, :]
        p_re = previous(e_re, 1 << k)
        p_im = previous(e_im, 1 << k)
        e_re, e_im = e_re + a_re * p_re - a_im * p_im, e_im + a_re * p_im + a_im * p_re
        k += 1
    ys = []
    for g in range(S5_BUNDLE):
        e_g = jnp.concatenate([e_re[:, g * ns:(g + 1) * ns], e_im[:, g * ns:(g + 1) * ns]], axis=1).astype(BF16)
        ys.append(proj[g][:, :w] + jnp.dot(e_g, wy_ref[0, g], preferred_element_type=F32))
    y = jnp.concatenate(ys, axis=1)
    y_hi = y.astype(BF16)
    y_lo = (y - y_hi.astype(F32)).astype(BF16)
    back = (((1,), (1,)), ((), ()))
    y_ref[0] = (lax.dot_general(y_hi, perm, back, preferred_element_type=F32)
                + lax.dot_general(y_lo, perm, back, preferred_element_type=F32))
    a_re = are_ref[0, 0:1, :]
    a_im = aim_ref[0, 0:1, :]
    f_re = a_re * e_re - a_im * e_im + d_re
    f_im = a_re * e_im + a_im * e_re + d_im
    fwd_row = fwd[0:1, :]
    for b in range(nb):
        lo, hi = b * nc, b * nc + nc - 1
        fre_ref[0, b:b + 1, :] = jnp.where(fwd_row, f_re[hi:hi + 1, :], f_re[lo:lo + 1, :])
        fim_ref[0, b:b + 1, :] = jnp.where(fwd_row, f_im[hi:hi + 1, :], f_im[lo:lo + 1, :])


def _s5_scan(ub, row0, nb, nc, mats, h0_re, h0_im):
    g, q, p, n = SSM_GROUPS, SSM_CHUNK, SSM_CH, SSM_STATE
    gb, nbun = S5_BUNDLE, SSM_GROUPS // S5_BUNDLE
    assert nc & (nc - 1) == 0 and nc % 8 == 0
    seqs = max(1, min(nb, S5_BLOCK_ROWS // nc))
    while nb % seqs or row0 % (seqs * nc):
        seqs -= 1
    n_rb, rows = nb // seqs, seqs * nc
    rb0 = row0 // rows
    kw, sw = q * gb * p, gb * 2 * n
    perm, w1, wy, a_re, a_im = mats

    def lanes(h):
        h = h.astype(F32).reshape(n_rb, seqs, 2, nbun, gb, n).transpose(3, 0, 1, 4, 2, 5)
        return h.reshape(nbun, n_rb, seqs, sw)

    weight = lambda r, c: pl.BlockSpec((1, gb, r, c), lambda o, i: (o, 0, 0, 0))
    coeff = pl.BlockSpec((1, 8, sw), lambda o, i: (o, 0, 0))
    state = pl.BlockSpec((1, 1, seqs, sw), lambda o, i: (o, i, 0, 0))
    y, f_re, f_im = pl.pallas_call(
        functools.partial(_s5_kernel, nb=seqs, nc=nc),
        grid=(nbun, n_rb),
        in_specs=[pl.BlockSpec((1, rows, kw), lambda o, i: (o, rb0 + i, 0)),
                  pl.BlockSpec((kw, kw), lambda o, i: (0, 0), pipeline_mode=pl.Buffered(1)),
                  weight(q * p, q * p + 4 * n), weight(4 * n, q * p), coeff, coeff, state, state],
        out_specs=[pl.BlockSpec((1, rows, kw), lambda o, i: (o, i, 0)), state, state],
        out_shape=[jax.ShapeDtypeStruct((nbun, nb * nc, kw), F32),
                   jax.ShapeDtypeStruct((nbun, n_rb, seqs, sw), F32),
                   jax.ShapeDtypeStruct((nbun, n_rb, seqs, sw), F32)],
        compiler_params=_cparams(("arbitrary", "arbitrary")),
    )(ub, perm, w1, wy, a_re, a_im, lanes(h0_re), lanes(h0_im))

    def unlanes(f):
        f = f.reshape(nbun, n_rb, seqs, gb, 2, n).transpose(1, 2, 4, 0, 3, 5)
        return f.reshape(nb, 2, g, n)

    return y, unlanes(f_re), unlanes(f_im)


def _s5_both(lay, u, mats, h0_re, h0_im):
    q, nbun, lanes = SSM_CHUNK, SSM_GROUPS // S5_BUNDLE, S5_BUNDLE * SSM_CH
    n_rows = lay.t // q
    ub = u.reshape(n_rows, q, nbun, lanes).transpose(2, 0, 1, 3).reshape(nbun, n_rows, q * lanes)
    zeros = jnp.zeros((lay.bc, 2, SSM_GROUPS, SSM_STATE), F32)
    y_c, f_re, f_im = _s5_scan(ub, 0, lay.bc, lay.lc // q, mats, zeros, zeros)
    y_s, _, _ = _s5_scan(ub, lay.tc // q, lay.bs, lay.ls // q, mats, h0_re, h0_im)
    y = jnp.concatenate([y_c, y_s], axis=1)
    y = y.reshape(nbun, n_rows, q, lanes).transpose(1, 2, 0, 3).reshape(lay.t, SSM_WIDTH)
    return y, f_re, f_im


def _merge_kernel(h_ref, ys_ref, at_ref, gb_ref, gc_ref, uc_ref, gcp_ref, ucp_ref, gcn_ref, ucn_ref, cw_ref,
                  wglu_ref, wg0_ref, wg1_ref, wg2_ref, bg0_ref, bg1_ref, bg2_ref, ws_ref, wa_ref, wc_ref,
                  o_ref, ssm_scr, conv_scr, *, lay, tm, halo):
    i = pl.program_id(0)
    n = pl.program_id(1)

    @pl.when(n == 0)
    def _():
        y = ys_ref[...]
        ge = 0.5 * y * (1.0 + jnp.tanh(math.sqrt(2.0 / math.pi) * (y + 0.044715 * (y * y * y))))
        glu = jnp.dot(ge.astype(BF16), wglu_ref[...], preferred_element_type=F32)
        ssm_scr[...] = (ge * _sigmoid(glu)).astype(BF16)

        z = gc_ref[...].astype(F32) * uc_ref[...].astype(F32)
        z_before = gcp_ref[halo - 1:halo, :].astype(F32) * ucp_ref[halo - 1:halo, :].astype(F32)
        z_after = gcn_ref[0:1, :].astype(F32) * ucn_ref[0:1, :].astype(F32)
        local = lax.broadcasted_iota(I32, z.shape, 0)
        pos, seq_len = lay.seq_pos(local + i * tm, i, tm)
        z_prev = jnp.where(local == 0, z_before, pltpu.roll(z, 1, 0))
        z_prev = jnp.where(pos == 0, 0.0, z_prev)
        z_next = jnp.where(local == tm - 1, z_after, pltpu.roll(z, tm - 1, 0))
        z_next = jnp.where(pos == seq_len - 1, 0.0, z_next)
        conv = cw_ref[0:1, :] * z_prev + cw_ref[1:2, :] * z + cw_ref[2:3, :] * z_next
        conv_scr[...] = (gb_ref[...].astype(F32) * conv).astype(BF16)

    h = h_ref[...]
    acc = None
    for act, wg_ref, bg_ref, wb_ref in ((ssm_scr[...], wg0_ref, bg0_ref, ws_ref),
                                        (at_ref[...], wg1_ref, bg1_ref, wa_ref),
                                        (conv_scr[...], wg2_ref, bg2_ref, wc_ref)):
        gate = _sigmoid(jnp.dot(h, wg_ref[...], preferred_element_type=F32) + bg_ref[...])
        term = gate * jnp.dot(act, wb_ref[...], preferred_element_type=F32)
        acc = term if acc is None else acc + term
    o_ref[...] = acc.astype(BF16)


def _merge(lay, h16, y_ssm, attn, y16, conv_w, wglu16, wgates16, b_gates, wbs16, wba16, wbc16):
    d = lay.d
    tm = lay.row_tile(512)
    tn = min(COL_TILE, d)
    nd = d // tn
    halo = 16
    hb = tm // halo
    last_h = lay.t // halo - 1
    c0 = (ATTN_WIDTH + 2 * KV_WIDTH + SSM_WIDTH) // CONV_WIDTH
    row = lambda cb: pl.BlockSpec((tm, CONV_WIDTH), lambda i, n: (i, cb))
    before = lambda cb: pl.BlockSpec((halo, CONV_WIDTH), lambda i, n: (jnp.maximum(i * hb - 1, 0), cb))
    after = lambda cb: pl.BlockSpec((halo, CONV_WIDTH), lambda i, n: (jnp.minimum((i + 1) * hb, last_h), cb))
    gate_w = lambda br: pl.BlockSpec((d, tn), lambda i, n: (0, br * nd + n))
    gate_b = lambda br: pl.BlockSpec((1, tn), lambda i, n: (0, br * nd + n))
    return pl.pallas_call(
        functools.partial(_merge_kernel, lay=lay, tm=tm, halo=halo),
        grid=(lay.t // tm, nd),
        in_specs=[
            pl.BlockSpec((tm, d), lambda i, n: (i, 0)),
            pl.BlockSpec((tm, SSM_WIDTH), lambda i, n: (i, 0)),
            pl.BlockSpec((tm, ATTN_WIDTH), lambda i, n: (i, 0)),
            row(c0), row(c0 + 1), row(c0 + 2),
            before(c0 + 1), before(c0 + 2), after(c0 + 1), after(c0 + 2),
            pl.BlockSpec((3, CONV_WIDTH), lambda i, n: (0, 0)),
            pl.BlockSpec((SSM_WIDTH, SSM_WIDTH), lambda i, n: (0, 0)),
            gate_w(0), gate_w(1), gate_w(2), gate_b(0), gate_b(1), gate_b(2),
            pl.BlockSpec((SSM_WIDTH, tn), lambda i, n: (0, n)),
            pl.BlockSpec((ATTN_WIDTH, tn), lambda i, n: (0, n)),
            pl.BlockSpec((CONV_WIDTH, tn), lambda i, n: (0, n)),
        ],
        out_specs=pl.BlockSpec((tm, tn), lambda i, n: (i, n)),
        out_shape=jax.ShapeDtypeStruct((lay.t, d), BF16),
        scratch_shapes=[pltpu.VMEM((tm, SSM_WIDTH), BF16), pltpu.VMEM((tm, CONV_WIDTH), BF16)],
        compiler_params=_cparams(("arbitrary", "arbitrary")),
    )(h16, y_ssm, attn, y16, y16, y16, y16, y16, y16, y16, conv_w, wglu16,
      wgates16, wgates16, wgates16, b_gates, b_gates, b_gates, wbs16, wba16, wbc16)


def _outproj_kernel(m_ref, w_ref, x_ref, gate_ref, g2_ref, shift_ref, scale_ref, wrh_ref, wrl_ref,
                    xo_ref, hp_ref, lg_ref):
    acc = jnp.dot(m_ref[...], w_ref[...], preferred_element_type=F32)
    xn = x_ref[...] + gate_ref[0] * acc
    xo_ref[...] = xn
    ms = jnp.mean(xn * xn, axis=-1, keepdims=True)
    h2 = xn * lax.rsqrt(ms + EPS) * g2_ref[...]
    h2 = h2 * (1.0 + scale_ref[0]) + shift_ref[0]
    hp_ref[...] = _pack_bf16_pairs(h2)
    h_hi = h2.astype(BF16)
    h_lo = (h2 - h_hi.astype(F32)).astype(BF16)
    logits = (jnp.dot(h_hi, wrh_ref[...], preferred_element_type=F32)
              + jnp.dot(h_hi, wrl_ref[...], preferred_element_type=F32)
              + jnp.dot(h_lo, wrh_ref[...], preferred_element_type=F32))
    lg_ref[...] = logits.T[:N_EXPERTS, :]


def _outproj(lay, merged, wout16, x, mod_l, g2, w_router):
    d = lay.d
    tm = lay.row_tile(256)
    lanes = 128
    wr = jnp.zeros((d, lanes), F32).at[:, :N_EXPERTS].set(w_router.astype(F32))
    wr_hi = wr.astype(BF16)
    wr_lo = (wr - wr_hi.astype(F32)).astype(BF16)
    return pl.pallas_call(
        _outproj_kernel,
        grid=(lay.t // tm,),
        in_specs=[
            pl.BlockSpec((tm, d), lambda i: (i, 0)),
            pl.BlockSpec((d, d), lambda i: (0, 0)),
            pl.BlockSpec((tm, d), lambda i: (i, 0)),
            _mod_spec(lay, tm, 2, d),
            pl.BlockSpec((1, d), lambda i: (0, 0)),
            _mod_spec(lay, tm, 3, d),
            _mod_spec(lay, tm, 4, d),
            pl.BlockSpec((d, lanes), lambda i: (0, 0)),
            pl.BlockSpec((d, lanes), lambda i: (0, 0)),
        ],
        out_specs=[
            pl.BlockSpec((tm, d), lambda i: (i, 0)),
            pl.BlockSpec((tm, d // 2), lambda i: (i, 0)),
            pl.BlockSpec((N_EXPERTS, tm), lambda i: (0, i)),
        ],
        out_shape=[
            jax.ShapeDtypeStruct((lay.t, d), F32),
            jax.ShapeDtypeStruct((lay.t, d // 2), U32),
            jax.ShapeDtypeStruct((N_EXPERTS, lay.t), F32),
        ],
        compiler_params=_cparams(("arbitrary",)),
    )(merged, wout16, x, mod_l, g2.reshape(1, d), mod_l, mod_l, wr_hi, wr_lo)


def _route_kernel(lg_ref, br_ref, idx_ref, w_ref, pos_ref, cnt_ref, carry):
    step = pl.program_id(0)
    tt = lg_ref.shape[1]
    per_group = N_EXPERTS // N_EXPERT_GROUPS

    @pl.when(step == 0)
    def _():
        carry[...] = jnp.zeros_like(carry)

    scores = _sigmoid(lg_ref[...])
    biased = scores + br_ref[...]
    sub = lax.broadcasted_iota(I32, (per_group, tt), 0).astype(F32)
    blocks, group_score = [], []
    for g in range(N_EXPERT_GROUPS):
        blk = biased[g * per_group:(g + 1) * per_group, :]
        m1 = jnp.max(blk, axis=0, keepdims=True)
        i1 = jnp.min(jnp.where(blk == m1, sub, float(per_group)), axis=0, keepdims=True)
        m2 = jnp.max(jnp.where(sub == i1, -jnp.inf, blk), axis=0, keepdims=True)
        blocks.append(blk)
        group_score.append(m1 + m2)
    masked = []
    for g in range(N_EXPERT_GROUPS):
        beaten_by = jnp.zeros((1, tt), F32)
        for o in range(N_EXPERT_GROUPS):
            if o == g:
                continue
            wins = (group_score[o] > group_score[g]) | ((group_score[o] == group_score[g]) & (o < g))
            beaten_by = beaten_by + wins.astype(F32)
        masked.append(jnp.where(beaten_by < TOPK_GROUPS, blocks[g], -jnp.inf))
    masked = jnp.concatenate(masked, axis=0)
    eid = lax.broadcasted_iota(I32, (N_EXPERTS, tt), 0).astype(F32)
    chosen, weights = [], []
    onehot = jnp.zeros((N_EXPERTS, tt), F32)
    for _ in range(TOP_K):
        m = jnp.max(masked, axis=0, keepdims=True)
        e = jnp.min(jnp.where(masked == m, eid, float(N_EXPERTS)), axis=0, keepdims=True)
        hit = eid == e
        chosen.append(e)
        weights.append(jnp.sum(jnp.where(hit, scores, 0.0), axis=0, keepdims=True))
        onehot = onehot + hit.astype(F32)
        masked = jnp.where(hit, -jnp.inf, masked)
    total = weights[0]
    for wk in weights[1:]:
        total = total + wk
    earlier = (lax.broadcasted_iota(I32, (tt, tt), 0) < lax.broadcasted_iota(I32, (tt, tt), 1)).astype(BF16)
    rank = carry[...][:, 0:1] + jnp.dot(onehot.astype(BF16), earlier, preferred_element_type=F32)
    for k in range(TOP_K):
        idx_ref[k:k + 1, :] = chosen[k].astype(I32)
        w_ref[k:k + 1, :] = weights[k] / total * ROUTED_SCALE
        pos_ref[k:k + 1, :] = jnp.sum(jnp.where(eid == chosen[k], rank, 0.0), axis=0, keepdims=True).astype(I32)
    for k in range(TOP_K, 8):
        idx_ref[k:k + 1, :] = jnp.zeros((1, tt), I32)
        w_ref[k:k + 1, :] = jnp.zeros((1, tt), F32)
        pos_ref[k:k + 1, :] = jnp.zeros((1, tt), I32)
    carry[...] = carry[...] + jnp.sum(onehot, axis=1, keepdims=True)
    cnt_ref[...] = carry[...]


def _route(logits_t, b_router):
    t = logits_t.shape[1]
    tt = math.gcd(t, 512)
    tok = pl.BlockSpec((8, tt), lambda i: (0, i))
    return pl.pallas_call(
        _route_kernel,
        grid=(t // tt,),
        in_specs=[pl.BlockSpec((N_EXPERTS, tt), lambda i: (0, i)),
                  pl.BlockSpec((N_EXPERTS, 1), lambda i: (0, 0))],
        out_specs=[tok, tok, tok, pl.BlockSpec((N_EXPERTS, 128), lambda i: (0, 0))],
        out_shape=[jax.ShapeDtypeStruct((8, t), I32), jax.ShapeDtypeStruct((8, t), F32),
                   jax.ShapeDtypeStruct((8, t), I32), jax.ShapeDtypeStruct((N_EXPERTS, 128), F32)],
        scratch_shapes=[pltpu.VMEM((N_EXPERTS, 128), F32)],
        compiler_params=_cparams(("arbitrary",)),
    )(logits_t, b_router.astype(F32).reshape(N_EXPERTS, 1))


def _dest_kernel(idx_ref, pos_ref, start_ref, o_ref, *, tt):
    eid = lax.broadcasted_iota(I32, (N_EXPERTS, tt), 0)
    start = start_ref[...]
    for k in range(TOP_K):
        first = jnp.sum(jnp.where(eid == idx_ref[k:k + 1, :], start, 0.0), axis=0, keepdims=True)
        o_ref[k:k + 1, :] = first.astype(I32) + pos_ref[k:k + 1, :]
    for k in range(TOP_K, 8):
        o_ref[k:k + 1, :] = jnp.zeros((1, tt), I32)


def _dest_rows(idx_t, pos_t, pad_start):
    t = idx_t.shape[1]
    tt = math.gcd(t, 512)
    tok = pl.BlockSpec((8, tt), lambda i: (0, i))
    return pl.pallas_call(
        functools.partial(_dest_kernel, tt=tt),
        grid=(t // tt,),
        in_specs=[tok, tok, pl.BlockSpec((N_EXPERTS, 1), lambda i: (0, 0))],
        out_specs=tok,
        out_shape=jax.ShapeDtypeStruct((8, t), I32),
        compiler_params=_cparams(("arbitrary",)),
    )(idx_t, pos_t, pad_start.astype(F32).reshape(N_EXPERTS, 1))


def _sc_layout(n_pairs, period):
    info = plsc.get_sparse_core_info()
    workers = info.num_cores * info.num_subcores
    per_worker = n_pairs // workers
    chunk = math.gcd(math.gcd(per_worker, period), SC_STREAM_ROWS)
    assert per_worker * workers == n_pairs and chunk % 8 == 0
    return info.num_cores, per_worker, chunk


def _sc_scatter_rows(rows, dest, n_out):
    t, width = rows.shape
    copies = dest.shape[0] // t
    n_cores, per_worker, chunk = _sc_layout(t, t)
    mesh = plsc.VectorSubcoreMesh(core_axis_name="core", subcore_axis_name="subcore")

    @functools.partial(
        pl.kernel, mesh=mesh, out_type=jax.ShapeDtypeStruct((n_out, width), rows.dtype),
        scratch_types=[pltpu.VMEM((chunk,), I32), pltpu.VMEM((chunk, width), rows.dtype)])
    def scatter(rows_hbm, dest_hbm, out_hbm, dest_v, rows_v):
        base = (lax.axis_index("subcore") * n_cores + lax.axis_index("core")) * per_worker

        @pl.loop(0, per_worker // chunk)
        def _(j):
            tok = base + j * chunk
            pltpu.sync_copy(rows_hbm.at[pl.ds(tok, chunk)], rows_v)
            for k in range(copies):
                pltpu.sync_copy(dest_hbm.at[pl.ds(k * t + tok, chunk)], dest_v)
                pltpu.sync_copy(rows_v, out_hbm.at[dest_v])

    return scatter(rows, dest)


def _sc_gather_rows(table, dest):
    width = table.shape[1]
    n_pairs = dest.shape[0]
    n_cores, per_worker, chunk = _sc_layout(n_pairs, n_pairs)
    mesh = plsc.VectorSubcoreMesh(core_axis_name="core", subcore_axis_name="subcore")

    @functools.partial(
        pl.kernel, mesh=mesh, out_type=jax.ShapeDtypeStruct((n_pairs, width), table.dtype),
        scratch_types=[pltpu.VMEM((chunk,), I32), pltpu.VMEM((chunk, width), table.dtype)])
    def gather(table_hbm, dest_hbm, out_hbm, dest_v, rows_v):
        base = (lax.axis_index("subcore") * n_cores + lax.axis_index("core")) * per_worker

        @pl.loop(0, per_worker // chunk)
        def _(j):
            pair = base + j * chunk
            pltpu.sync_copy(dest_hbm.at[pl.ds(pair, chunk)], dest_v)
            pltpu.sync_copy(table_hbm.at[dest_v], rows_v)
            pltpu.sync_copy(rows_v, out_hbm.at[pl.ds(pair, chunk)])

    return gather(table, dest)


def _expert_kernel(be_ref, xs_ref, wg_ref, wu_ref, wd_ref, ys_ref, wg16, wu16, wd16):
    i = pl.program_id(0)
    changed = jnp.logical_or(i == 0, be_ref[i] != be_ref[jnp.maximum(i - 1, 0)])

    @pl.when(changed)
    def _():
        rows = 256
        d = wg16.shape[0]

        def cast_in(r, carry):
            sl = pl.ds(pl.multiple_of(r * rows, rows), rows)
            wg16[sl, :] = wg_ref[0, sl, :].astype(BF16)
            wu16[sl, :] = wu_ref[0, sl, :].astype(BF16)
            return carry

        lax.fori_loop(0, d // rows, cast_in, 0)

        def cast_down(r, carry):
            sl = pl.ds(pl.multiple_of(r * 128, 128), 128)
            wd16[sl, :] = wd_ref[0, sl, :].astype(BF16)
            return carry

        lax.fori_loop(0, D_EXPERT // 128, cast_down, 0)

    n_blocks = pl.num_programs(0)
    used = i < be_ref[n_blocks]

    @pl.when(used)
    def _():
        live = lax.broadcasted_iota(I32, (xs_ref.shape[0], 1), 0) < be_ref[n_blocks + 1 + i]
        hi, lo = _unpack_bf16_pairs(jnp.where(live, xs_ref[...], jnp.uint32(0)))
        x = jnp.concatenate([hi, lo], axis=1).astype(BF16)
        gate = jnp.dot(x, wg16[...], preferred_element_type=F32)
        up = jnp.dot(x, wu16[...], preferred_element_type=F32)
        act = (gate * _sigmoid(gate) * up).astype(BF16)
        ys_ref[...] = _pack_bf16_pairs(jnp.dot(act, wd16[...], preferred_element_type=F32))

    @pl.when(jnp.logical_not(used))
    def _():
        ys_ref[...] = jnp.zeros_like(ys_ref)


def _experts(xs, block_e, layer, w_gate, w_up, w_down):
    n_rows, half = xs.shape
    d = 2 * half
    br = EXPERT_ROWS
    grid_spec = pltpu.PrefetchScalarGridSpec(
        num_scalar_prefetch=1,
        grid=(n_rows // br,),
        in_specs=[
            pl.BlockSpec((br, half), lambda i, be: (i, 0)),
            pl.BlockSpec((None, 1, d, D_EXPERT), lambda i, be: (layer, be[i], 0, 0)),
            pl.BlockSpec((None, 1, d, D_EXPERT), lambda i, be: (layer, be[i], 0, 0)),
            pl.BlockSpec((None, 1, D_EXPERT, d), lambda i, be: (layer, be[i], 0, 0)),
        ],
        out_specs=pl.BlockSpec((br, half), lambda i, be: (i, 0)),
        scratch_shapes=[pltpu.VMEM((d, D_EXPERT), BF16), pltpu.VMEM((d, D_EXPERT), BF16),
                        pltpu.VMEM((D_EXPERT, d), BF16)],
    )
    return pl.pallas_call(
        _expert_kernel,
        grid_spec=grid_spec,
        out_shape=jax.ShapeDtypeStruct((n_rows, half), U32),
        compiler_params=_cparams(("arbitrary",)),
    )(block_e, xs, w_gate, w_up, w_down)


def _shared_kernel(hp_ref, wsg_ref, wsu_ref, wsd_ref, o_ref):
    hi, lo = _unpack_bf16_pairs(hp_ref[...])
    h2 = jnp.concatenate([hi, lo], axis=1).astype(BF16)
    sg = jnp.dot(h2, wsg_ref[...], preferred_element_type=F32)
    su = jnp.dot(h2, wsu_ref[...], preferred_element_type=F32)
    shared = jnp.dot((sg * _sigmoid(sg) * su).astype(BF16), wsd_ref[...], preferred_element_type=F32)
    o_ref[...] = shared.astype(o_ref.dtype)


def _shared_expert(lay, h2p, wsg16, wsu16, wsd16):
    d = lay.d
    tm = lay.row_tile(512)
    return pl.pallas_call(
        _shared_kernel,
        grid=(lay.t // tm,),
        in_specs=[pl.BlockSpec((tm, d // 2), lambda i: (i, 0)),
                  pl.BlockSpec((d, D_EXPERT), lambda i: (0, 0)),
                  pl.BlockSpec((d, D_EXPERT), lambda i: (0, 0)),
                  pl.BlockSpec((D_EXPERT, d), lambda i: (0, 0))],
        out_specs=pl.BlockSpec((tm, d), lambda i: (i, 0)),
        out_shape=jax.ShapeDtypeStruct((lay.t, d), BF16),
        compiler_params=_cparams(("arbitrary",)),
    )(h2p, wsg16, wsu16, wsd16)


def _combine_kernel(yk_ref, wsel_ref, sh_ref, x_ref, gate_ref, fg_ref, *outs, tt, final, n_ctx_tiles):
    shared = sh_ref[...].astype(F32)
    half = yk_ref.shape[2]
    r_hi = jnp.zeros((tt, half), F32)
    r_lo = jnp.zeros((tt, half), F32)
    for k in range(TOP_K):
        y_hi, y_lo = _unpack_bf16_pairs(yk_ref[k])
        wk = wsel_ref[:, k:k + 1]
        r_hi = r_hi + wk * y_hi
        r_lo = r_lo + wk * y_lo
    routed = jnp.concatenate([r_hi, r_lo], axis=1)
    out = x_ref[...] + gate_ref[0] * (routed + shared)
    if not final:
        outs[0][...] = out
        return
    ms = jnp.mean(out * out, axis=-1, keepdims=True)
    out = out * lax.rsqrt(ms + EPS) * fg_ref[...]
    ctx_ref, lat_ref = outs
    is_ctx = pl.program_id(0) < n_ctx_tiles

    @pl.when(is_ctx)
    def _():
        ctx_ref[...] = out

    @pl.when(jnp.logical_not(is_ctx))
    def _():
        lat_ref[...] = out


def _combine(lay, y_by_k, wsel, shared, x, mod_l, final_g, tt, final):
    d = lay.d
    half = d // 2
    nct = lay.tc // tt
    if final:
        out_specs = [pl.BlockSpec((tt, d), lambda i: (jnp.minimum(i, nct - 1), 0)),
                     pl.BlockSpec((tt, d), lambda i: (jnp.maximum(i - nct, 0), 0))]
        out_shape = [jax.ShapeDtypeStruct((lay.tc, d), F32), jax.ShapeDtypeStruct((lay.ts, d), F32)]
    else:
        out_specs = pl.BlockSpec((tt, d), lambda i: (i, 0))
        out_shape = jax.ShapeDtypeStruct((lay.t, d), F32)
    return pl.pallas_call(
        functools.partial(_combine_kernel, tt=tt, final=final, n_ctx_tiles=nct),
        grid=(lay.t // tt,),
        in_specs=[
            pl.BlockSpec((TOP_K, tt, half), lambda i: (0, i, 0)),
            pl.BlockSpec((tt, 8), lambda i: (i, 0)),
            pl.BlockSpec((tt, d), lambda i: (i, 0)),
            pl.BlockSpec((tt, d), lambda i: (i, 0)),
            _mod_spec(lay, tt, 5, d),
            pl.BlockSpec((1, d), lambda i: (0, 0)),
        ],
        out_specs=out_specs,
        out_shape=out_shape,
        compiler_params=_cparams(("arbitrary",)),
    )(y_by_k, wsel, shared, x, mod_l, final_g.reshape(1, d))


def _moe(lay, layer, x, h2p, logits_t, mod_l, b_router, w_e_gate, w_e_up, w_e_down, wsg16, wsu16, wsd16, final_g,
         final):
    t = lay.t
    idx_t, w_t, pos_t, counts = _route(logits_t, b_router)
    br = EXPERT_ROWS
    counts = counts[:, 0].astype(I32)
    padded = (counts + br - 1) // br * br
    pad_end = jnp.cumsum(padded)
    pad_start = pad_end - padded
    n_blocks = -(-(t * TOP_K + N_EXPERTS * (br - 1)) // br)
    n_rows = n_blocks * br
    first_row = jnp.arange(n_blocks, dtype=I32) * br
    block_e = jnp.minimum(jnp.sum((pad_end[None, :] <= first_row[:, None]).astype(I32), axis=1), N_EXPERTS - 1)
    own = block_e[:, None] == jnp.arange(N_EXPERTS, dtype=I32)[None, :]
    rows_end = jnp.sum(jnp.where(own, (pad_start + counts)[None, :], 0), axis=1)
    held = jnp.clip(rows_end - first_row, 0, br)
    block_meta = jnp.concatenate([block_e, pad_end[-1:] // br, held]).astype(I32)
    dest = _dest_rows(idx_t, pos_t, pad_start)[:TOP_K].reshape(TOP_K * t)
    xs = _sc_scatter_rows(h2p, dest, n_rows)
    shared = _shared_expert(lay, h2p, wsg16, wsu16, wsd16)
    ys = _experts(xs, block_meta, layer, w_e_gate, w_e_up, w_e_down)
    y_by_k = _sc_gather_rows(ys, dest).reshape(TOP_K, t, h2p.shape[1])
    return _combine(lay, y_by_k, w_t.T, shared, x, mod_l, final_g, lay.row_tile(256), final)


def kernel(x_prompt, x_sample, c, cache_k, cache_v, state_ssm_re, state_ssm_im, c_ctx, w_ada, b_ada, norm1_g, norm2_g, w_in, w_gates, b_gates, ssm_lam_re, ssm_lam_im, ssm_log_dt, ssm_b_re, ssm_b_im, ssm_c_re, ssm_c_im, ssm_d, ssm_w_glu, conv_w, attn_sink, w_br_ssm, w_br_attn, w_br_conv, w_out, w_router, b_router, w_e_gate, w_e_up, w_e_down, w_s_gate, w_s_up, w_s_down, final_g):
    bc, lc, d = x_prompt.shape
    bs, ls, _ = x_sample.shape
    depth = w_in.shape[0]
    lay = _Layout(bc, lc, bs, ls, d)
    assert 1 + bs <= MOD_ROWS

    x = jnp.concatenate([x_prompt.reshape(lay.tc, d), x_sample.reshape(lay.ts, d)], axis=0)
    cvec = jnp.zeros((MOD_ROWS, d), F32).at[0].set(c_ctx).at[1:1 + bs].set(c)
    mod = _adaln(cvec, w_ada, b_ada).reshape(depth, MOD_ROWS * 6, 1, d)
    rope_cos, rope_sin = _rope_tables(lay, lay.row_tile(1024))
    s5_perm = _s5_lane_permutation()
    s5_mats = jax.vmap(functools.partial(_s5_matrices, n_steps=8))(
        ssm_lam_re, ssm_lam_im, ssm_log_dt, ssm_b_re, ssm_b_im, ssm_c_re, ssm_c_im, ssm_d)

    ks, vs, s_re, s_im = [], [], [], []
    for l in range(depth):
        mod_l = mod[l]
        h16, y16, kv32 = _inproj(lay, x, mod_l, norm1_g[l], w_in[l].astype(BF16), rope_cos, rope_sin)
        ks.append(kv32[:lay.tc, :KV_WIDTH].reshape(bc, lc, N_KV_HEADS, HEAD_DIM))
        vs.append(kv32[:lay.tc, KV_WIDTH:].reshape(bc, lc, N_KV_HEADS, HEAD_DIM))

        attn = _attention(lay, y16, attn_sink[l].astype(F32), cache_k[:, l], cache_v[:, l])

        u0 = ATTN_WIDTH + 2 * KV_WIDTH
        u = y16[:, u0:u0 + SSM_WIDTH]
        mats = (s5_perm,) + tuple(m[l] for m in s5_mats)
        y_ssm, f_re, f_im = _s5_both(lay, u, mats, state_ssm_re[:, l], state_ssm_im[:, l])
        s_re.append(f_re)
        s_im.append(f_im)

        merged = _merge(lay, h16, y_ssm, attn, y16, conv_w[l], ssm_w_glu[l].astype(BF16),
                        w_gates[l].astype(BF16), b_gates[l].reshape(1, -1), w_br_ssm[l].astype(BF16),
                        w_br_attn[l].astype(BF16), w_br_conv[l].astype(BF16))
        x, h2p, logits_t = _outproj(lay, merged, w_out[l].astype(BF16), x, mod_l, norm2_g[l],
                                    w_router[l])
        x = _moe(lay, l, x, h2p, logits_t, mod_l, b_router[l], w_e_gate, w_e_up, w_e_down,
                 w_s_gate[l].astype(BF16), w_s_up[l].astype(BF16), w_s_down[l].astype(BF16),
                 final_g, l == depth - 1)

    y_prompt = x[0].reshape(bc, lc, d)
    y_sample = x[1].reshape(bs, ls, d)
    return (y_prompt, y_sample, jnp.stack(ks, axis=1), jnp.stack(vs, axis=1),
            jnp.stack(s_re, axis=1), jnp.stack(s_im, axis=1))
```

```python
import functools
import math

import jax
import jax.numpy as jnp
from jax import lax
from jax.experimental import pallas as pl
from jax.experimental.pallas import tpu as pltpu
from jax.experimental.pallas import tpu_sc as plsc

HEAD_DIM = 128
N_HEADS = 8
N_KV_HEADS = 2
GROUP = N_HEADS // N_KV_HEADS
ATTN_WIDTH = N_HEADS * HEAD_DIM
KV_WIDTH = N_KV_HEADS * HEAD_DIM
WINDOW = 128
ATTN_BLOCK = 128
ATTN_Q_ROWS = 256
ATTN_SCALE = HEAD_DIM ** -0.5
LOG2_E = 1.4426950408889634
ROPE_BASE = 10000.0
ROT_F = HEAD_DIM // 4
GRID_W = 64
SSM_WIDTH = 512
SSM_CH = 16
SSM_GROUPS = SSM_WIDTH // SSM_CH
SSM_STATE = 64
SSM_CHUNK = 16
S5_BUNDLE = 8
S5_BLOCK_ROWS = 256
CONV_WIDTH = 512
N_BRANCHES = 3
IN_WIDTH = ATTN_WIDTH + 2 * KV_WIDTH + SSM_WIDTH + 3 * CONV_WIDTH
N_EXPERTS = 64
TOP_K = 6
N_EXPERT_GROUPS = 8
TOPK_GROUPS = 4
D_EXPERT = 512
ROUTED_SCALE = 2.5
EPS = 1e-6
NEG_INF = -1e30

COL_TILE = 512
MOD_ROWS = 16
EXPERT_ROWS = 512
SC_STREAM_ROWS = 80
VMEM_LIMIT_V7X = 56 * 1024 * 1024

F32 = jnp.float32
BF16 = jnp.bfloat16
I32 = jnp.int32
U32 = jnp.uint32


def _cparams(sem, vmem=VMEM_LIMIT_V7X):
    return pltpu.CompilerParams(dimension_semantics=sem, vmem_limit_bytes=vmem)


def _sigmoid(x):
    return 1.0 / (1.0 + jnp.exp(-x))


def _pack_bf16_pairs(v):
    n = v.shape[1] // 2
    hi = lax.bitcast_convert_type(v[:, :n].astype(BF16).astype(F32), U32)
    lo = lax.bitcast_convert_type(v[:, n:].astype(BF16).astype(F32), U32)
    return hi | (lo >> 16)


def _unpack_bf16_pairs(p):
    hi = lax.bitcast_convert_type(p & jnp.uint32(0xFFFF0000), F32)
    lo = lax.bitcast_convert_type(p << 16, F32)
    return hi, lo


def _adaln_kernel(c_ref, w_ref, b_ref, o_ref):
    c = c_ref[...]
    s = (c * _sigmoid(c)).astype(BF16)
    o_ref[0] = jnp.dot(s, w_ref[0].astype(BF16), preferred_element_type=F32) + b_ref[0]


def _adaln(cvec, w_ada, b_ada):
    depth, d, n6 = w_ada.shape
    tn = math.gcd(1024, n6)
    return pl.pallas_call(
        _adaln_kernel,
        grid=(depth, n6 // tn),
        in_specs=[
            pl.BlockSpec((MOD_ROWS, d), lambda l, n: (0, 0)),
            pl.BlockSpec((1, d, tn), lambda l, n: (l, 0, n)),
            pl.BlockSpec((1, 1, tn), lambda l, n: (l, 0, n)),
        ],
        out_specs=pl.BlockSpec((1, MOD_ROWS, tn), lambda l, n: (l, 0, n)),
        out_shape=jax.ShapeDtypeStruct((depth, MOD_ROWS, n6), F32),
        compiler_params=_cparams(("arbitrary", "arbitrary")),
    )(cvec, w_ada, b_ada.reshape(depth, 1, n6))


class _Layout:
    def __init__(self, n_ctx_seq, len_ctx, n_lat_seq, len_lat, d_model):
        self.bc, self.lc, self.bs, self.ls, self.d = n_ctx_seq, len_ctx, n_lat_seq, len_lat, d_model
        self.tc = n_ctx_seq * len_ctx
        self.ts = n_lat_seq * len_lat
        self.t = self.tc + self.ts

    def row_tile(self, want):
        tm = math.gcd(math.gcd(self.tc, self.ls), want)
        assert tm % 16 == 0
        return tm

    def mod_index(self, i, tm):
        nct, tps = self.tc // tm, self.ls // tm
        return jnp.where(i < nct, 0, 1 + (i - nct) // tps)

    def seq_pos(self, rows, i, tm):
        is_lat = i >= self.tc // tm
        return jnp.where(is_lat, (rows - self.tc) % self.ls, rows % self.lc), jnp.where(is_lat, self.ls, self.lc)


def _mod_spec(lay, tm, slot, d):
    return pl.BlockSpec((1, 1, d), lambda i, *_: (lay.mod_index(i, tm) * 6 + slot, 0, 0))


def _rope(z, cos, sin_signed, first_half):
    swapped = jnp.where(first_half, pltpu.roll(z, HEAD_DIM - ROT_F, 1), pltpu.roll(z, ROT_F, 1))
    return z * cos + swapped * sin_signed


def _inproj_kernel(x_ref, shift_ref, scale_ref, g_ref, w_ref, cos_ref, sin_ref, h_ref, y_ref, kv_ref):
    n = pl.program_id(1)
    n_q = ATTN_WIDTH // COL_TILE

    @pl.when(n == 0)
    def _():
        x = x_ref[...]
        ms = jnp.mean(x * x, axis=-1, keepdims=True)
        y = x * lax.rsqrt(ms + EPS) * g_ref[...]
        h_ref[...] = (y * (1.0 + scale_ref[0]) + shift_ref[0]).astype(BF16)

    acc = jnp.dot(h_ref[...], w_ref[...], preferred_element_type=F32)

    def rotated(n_heads):
        cos, sin = cos_ref[...], sin_ref[...]
        first_half = (lax.broadcasted_iota(I32, cos.shape, 1) % (2 * ROT_F)) < ROT_F
        parts = [_rope(acc[:, s * HEAD_DIM:(s + 1) * HEAD_DIM], cos, sin, first_half) for s in range(n_heads)]
        parts.append(acc[:, n_heads * HEAD_DIM:])
        return jnp.concatenate(parts, axis=1) if n_heads * HEAD_DIM < COL_TILE else jnp.concatenate(parts[:-1], axis=1)

    @pl.when(n < n_q)
    def _():
        y_ref[...] = (rotated(COL_TILE // HEAD_DIM) * (ATTN_SCALE * LOG2_E)).astype(BF16)

    @pl.when(n == n_q)
    def _():
        kv_ref[...] = acc
        y_ref[...] = rotated(N_KV_HEADS).astype(BF16)

    @pl.when(n > n_q)
    def _():
        y_ref[...] = acc.astype(BF16)


def _inproj(lay, x, mod_l, g1, w_in16, rope_cos, rope_sin):
    d = lay.d
    tm = lay.row_tile(1024)
    nct, tps = lay.tc // tm, lay.ls // tm

    def rope_idx(i, n):
        return (jnp.where(i < nct, 0, 1 + (i - nct) % tps), 0)

    return pl.pallas_call(
        _inproj_kernel,
        grid=(lay.t // tm, IN_WIDTH // COL_TILE),
        in_specs=[
            pl.BlockSpec((tm, d), lambda i, n: (i, 0)),
            _mod_spec(lay, tm, 0, d),
            _mod_spec(lay, tm, 1, d),
            pl.BlockSpec((1, d), lambda i, n: (0, 0)),
            pl.BlockSpec((d, COL_TILE), lambda i, n: (0, n)),
            pl.BlockSpec((tm, HEAD_DIM), rope_idx),
            pl.BlockSpec((tm, HEAD_DIM), rope_idx),
        ],
        out_specs=[
            pl.BlockSpec((tm, d), lambda i, n: (i, 0)),
            pl.BlockSpec((tm, COL_TILE), lambda i, n: (i, n)),
            pl.BlockSpec((tm, 2 * KV_WIDTH), lambda i, n: (i, 0)),
        ],
        out_shape=[
            jax.ShapeDtypeStruct((lay.t, d), BF16),
            jax.ShapeDtypeStruct((lay.t, IN_WIDTH), BF16),
            jax.ShapeDtypeStruct((lay.t, 2 * KV_WIDTH), F32),
        ],
        compiler_params=_cparams(("arbitrary", "arbitrary")),
    )(x, mod_l, mod_l, g1.reshape(1, d), w_in16, rope_cos, rope_sin)


def _rope_tables(lay, tm):
    t = jnp.arange(lay.ls)
    row = (t // GRID_W).astype(F32)
    col = (t % GRID_W).astype(F32)
    inv = ROPE_BASE ** (-jnp.arange(ROT_F, dtype=F32) / ROT_F)
    ar, ac = row[:, None] * inv, col[:, None] * inv
    cos = jnp.concatenate([jnp.cos(ar), jnp.cos(ar), jnp.cos(ac), jnp.cos(ac)], axis=1)
    sin = jnp.concatenate([-jnp.sin(ar), jnp.sin(ar), -jnp.sin(ac), jnp.sin(ac)], axis=1)
    cos = jnp.concatenate([jnp.ones((tm, HEAD_DIM), F32), cos], axis=0)
    sin = jnp.concatenate([jnp.zeros((tm, HEAD_DIM), F32), sin], axis=0)
    return cos, sin


def _attend(q, sink_ref, j, parts):
    nq = q.shape[0]
    q4 = jnp.concatenate([q[:, g * HEAD_DIM:(g + 1) * HEAD_DIM] for g in range(GROUP)], axis=0)
    sink = jnp.concatenate([jnp.full((nq, 1), sink_ref[j * GROUP + g] * LOG2_E, F32) for g in range(GROUP)], axis=0)
    scores = []
    m = sink
    for k, _, mask in parts:
        s = lax.dot_general(q4, k, (((1,), (1,)), ((), ())), preferred_element_type=F32)
        if mask is not None:
            s = jnp.where(mask, s, NEG_INF)
        scores.append(s)
        m = jnp.maximum(m, jnp.max(s, axis=-1, keepdims=True))
    den = jnp.exp2(sink - m)
    out = jnp.zeros((GROUP * nq, HEAD_DIM), F32)
    for s, (_, v, _) in zip(scores, parts):
        p = jnp.exp2(s - m)
        den = den + jnp.sum(p, axis=-1, keepdims=True)
        out = out + jnp.dot(p.astype(BF16), v, preferred_element_type=F32)
    out = out / den
    return jnp.concatenate([out[g * nq:(g + 1) * nq] for g in range(GROUP)], axis=1)


def _head(x, j, width=HEAD_DIM):
    return x[:, j * width:(j + 1) * width]


def _attn_kernel(sink_ref, q_ref, kp_ref, kc_ref, kn_ref, vp_ref, vc_ref, vn_ref, ck_ref, cv_ref, o_ref, *,
                 n_ctx_blocks, blocks_per_seq, seq_len):
    i = pl.program_id(0)
    q = q_ref[...]

    @pl.when(i < n_ctx_blocks)
    def _():
        k, v = kc_ref[...], vc_ref[...]
        outs = [_attend(_head(q, j, GROUP * HEAD_DIM), sink_ref, j, [(_head(k, j), _head(v, j), None)])
                for j in range(N_KV_HEADS)]
        o_ref[...] = jnp.concatenate(outs, axis=1).astype(o_ref.dtype)

    @pl.when(i >= n_ctx_blocks)
    def _():
        blk = (i - n_ctx_blocks) % blocks_per_seq
        kw = jnp.concatenate([kp_ref[...], kc_ref[...], kn_ref[...]], axis=0)
        vw = jnp.concatenate([vp_ref[...], vc_ref[...], vn_ref[...]], axis=0)
        ck = ck_ref[...].astype(BF16)
        cv = cv_ref[...].astype(BF16)
        shape = (GROUP * ATTN_Q_ROWS, ATTN_Q_ROWS + 2 * ATTN_BLOCK)
        qoff = lax.broadcasted_iota(I32, shape, 0) % ATTN_Q_ROWS
        koff = lax.broadcasted_iota(I32, shape, 1) - ATTN_BLOCK
        kabs = koff + blk * ATTN_Q_ROWS
        mask = (jnp.abs(qoff - koff) <= WINDOW) & (kabs >= 0) & (kabs < seq_len)
        outs = [_attend(_head(q, j, GROUP * HEAD_DIM), sink_ref, j,
                        [(_head(kw, j), _head(vw, j), mask), (_head(ck, j), _head(cv, j), None)])
                for j in range(N_KV_HEADS)]
        o_ref[...] = jnp.concatenate(outs, axis=1).astype(o_ref.dtype)


def _attention(lay, y16, sink, cache_k_l, cache_v_l):
    smem = pl.BlockSpec(memory_space=pltpu.SMEM)
    kcol, vcol = ATTN_WIDTH // KV_WIDTH, (ATTN_WIDTH + KV_WIDTH) // KV_WIDTH
    qr = ATTN_Q_ROWS
    assert lay.lc == qr and lay.ls % qr == 0 and qr % ATTN_BLOCK == 0
    nblk = lay.ls // qr
    nctb = lay.tc // qr
    per = qr // ATTN_BLOCK
    last = lay.t // ATTN_BLOCK - 1
    past = cache_k_l.shape[1]

    def edge(col, delta):
        return pl.BlockSpec((ATTN_BLOCK, KV_WIDTH), lambda i: (jnp.clip(i * per + delta, 0, last), col))

    def own(col):
        return pl.BlockSpec((qr, KV_WIDTH), lambda i: (i, col))

    cspec = pl.BlockSpec((None, past, KV_WIDTH), lambda i: (jnp.maximum(i - nctb, 0) // nblk, 0, 0))
    return pl.pallas_call(
        functools.partial(_attn_kernel, n_ctx_blocks=nctb, blocks_per_seq=nblk, seq_len=lay.ls),
        grid=(lay.t // qr,),
        in_specs=[
            smem,
            pl.BlockSpec((qr, ATTN_WIDTH), lambda i: (i, 0)),
            edge(kcol, -1), own(kcol), edge(kcol, per),
            edge(vcol, -1), own(vcol), edge(vcol, per),
            cspec, cspec,
        ],
        out_specs=pl.BlockSpec((qr, ATTN_WIDTH), lambda i: (i, 0)),
        out_shape=jax.ShapeDtypeStruct((lay.t, ATTN_WIDTH), BF16),
        compiler_params=_cparams(("arbitrary",)),
    )(sink, y16, y16, y16, y16, y16, y16, y16,
      cache_k_l.reshape(lay.bs, past, KV_WIDTH), cache_v_l.reshape(lay.bs, past, KV_WIDTH))


def _s5_matrices(lam_re, lam_im, log_dt, b_re, b_im, c_re, c_im, d_skip, n_steps):
    q, p, g, n = SSM_CHUNK, SSM_CH, SSM_GROUPS, SSM_STATE
    lam = lax.complex(lam_re.astype(F32), lam_im.astype(F32))
    dt = jnp.exp(log_dt.astype(F32))[..., None]
    lam_dt = lam * dt
    lam_bar = jnp.exp(lam_dt)
    b_bar = ((lam_bar - 1.0) / lam)[..., None] * lax.complex(b_re.astype(F32), b_im.astype(F32))
    c_mat = lax.complex(c_re.astype(F32), c_im.astype(F32))
    steps = jnp.arange(q + 1, dtype=F32)
    pw = jnp.exp(lam_dt[:, None] * steps[None, :, None, None])
    kern = jnp.real(jnp.einsum('dgpn,dkgn,dgnr->dkgpr', c_mat, pw[:, :q], b_bar))
    tau_in = jnp.arange(q)[:, None]
    tau_out = jnp.arange(q)[None, :]
    lag_f = tau_out - tau_in
    lag_b = tau_in - tau_out
    lags = jnp.arange(q)
    pick_f = (lag_f[:, :, None] == lags).astype(F32)
    pick_b = (lag_b[:, :, None] == lags).astype(F32)
    exact = lax.Precision.HIGHEST
    kf = jnp.einsum('abk,kgpr->garbp', pick_f, kern[0], precision=exact)
    kb = jnp.einsum('abk,kgpr->garbp', pick_b, kern[1], precision=exact)
    m = kf + kb
    eye_q = jnp.eye(q, dtype=F32)[None, :, None, :, None]
    eye_p = jnp.eye(p, dtype=F32)[None, None, :, None, :]
    m = m + eye_q * eye_p * d_skip.astype(F32).reshape(g, 1, p, 1, 1)
    m = m.reshape(g, q * p, q * p)
    ws_f = pw[0, :q][::-1][:, :, :, None] * b_bar[0][None]
    ws_b = pw[1, :q][:, :, :, None] * b_bar[1][None]

    def cols(w):
        return w.transpose(1, 0, 3, 2).reshape(g, q * p, n)

    w1 = jnp.concatenate([m, jnp.real(cols(ws_f)), jnp.real(cols(ws_b)),
                          jnp.imag(cols(ws_f)), jnp.imag(cols(ws_b))], axis=2)
    cy_f = c_mat[0][None] * pw[0, 1:][:, :, None, :]
    cy_b = c_mat[1][None] * pw[1, 1:][::-1][:, :, None, :]

    def rows(w):
        return w.transpose(1, 3, 0, 2).reshape(g, n, q * p)

    wy = jnp.concatenate([jnp.real(rows(cy_f)), jnp.real(rows(cy_b)),
                          -jnp.imag(rows(cy_f)), -jnp.imag(rows(cy_b))], axis=1)
    hops = (q * 2.0 ** jnp.arange(8, dtype=F32))[None, :, None, None]
    a = jnp.exp(lam_dt[:, None] * hops)
    a = jnp.concatenate([a[0], a[1]], axis=-1).transpose(1, 0, 2)
    assert n_steps <= 8
    gb, nbun = S5_BUNDLE, g // S5_BUNDLE
    ab = a.reshape(nbun, gb, 8, 2 * n).transpose(0, 2, 1, 3).reshape(nbun, 8, gb * 2 * n)
    return (w1.astype(BF16).reshape(nbun, gb, q * p, q * p + 4 * n),
            wy.astype(BF16).reshape(nbun, gb, 4 * n, q * p), jnp.real(ab), jnp.imag(ab))


def _s5_lane_permutation():
    q, p, gb = SSM_CHUNK, SSM_CH, S5_BUNDLE
    lane = jnp.arange(q * gb * p)
    target = ((lane % (gb * p)) // p) * (q * p) + (lane // (gb * p)) * p + lane % p
    return (target[:, None] == lane[None, :]).astype(BF16)


def _s5_kernel(u_ref, perm_ref, w1_ref, wy_ref, are_ref, aim_ref, h0re_ref, h0im_ref, y_ref, fre_ref, fim_ref, *,
               nb, nc):
    r_tot = nb * nc
    w = SSM_CHUNK * SSM_CH
    ns = 2 * SSM_STATE
    n2 = S5_BUNDLE * ns
    perm = perm_ref[...]
    x = jnp.dot(u_ref[0], perm, preferred_element_type=F32).astype(BF16)
    proj = [jnp.dot(x[:, g * w:(g + 1) * w], w1_ref[0, g], preferred_element_type=F32) for g in range(S5_BUNDLE)]
    d_re = jnp.concatenate([pg[:, w:w + ns] for pg in proj], axis=1)
    d_im = jnp.concatenate([pg[:, w + ns:] for pg in proj], axis=1)
    h0re_ref, h0im_ref, fre_ref, fim_ref = (r.at[0] for r in (h0re_ref, h0im_ref, fre_ref, fim_ref))
    row = lax.broadcasted_iota(I32, (r_tot, n2), 0)
    chunk = row % nc
    seq = row // nc
    fwd = (lax.broadcasted_iota(I32, (r_tot, n2), 1) % (2 * SSM_STATE)) < SSM_STATE

    def previous(x, dist):
        valid = (fwd & (chunk >= dist)) | (~fwd & (chunk < nc - dist))
        moved = jnp.where(fwd, pltpu.roll(x, dist, 0), pltpu.roll(x, r_tot - dist, 0))
        return jnp.where(valid, moved, 0.0)

    h0_re = jnp.zeros((r_tot, n2), F32)
    h0_im = jnp.zeros((r_tot, n2), F32)
    for b in range(nb):
        h0_re = jnp.where(seq == b, h0re_ref[0, b:b + 1, :], h0_re)
        h0_im = jnp.where(seq == b, h0im_ref[0, b:b + 1, :], h0_im)
    first = (fwd & (chunk == 0)) | (~fwd & (chunk == nc - 1))
    e_re = jnp.where(first, h0_re, previous(d_re, 1))
    e_im = jnp.where(first, h0_im, previous(d_im, 1))
    k = 0
    while (1 << k) < nc:
        a_re = are_ref[0, k:k + 1, :]
        a_im = aim_ref[0, k:k + 1, :]
        p_re = previous(e_re, 1 << k)
        p_im = previous(e_im, 1 << k)
        e_re, e_im = e_re + a_re * p_re - a_im * p_im, e_im + a_re * p_im + a_im * p_re
        k += 1
    ys = []
    for g in range(S5_BUNDLE):
        e_g = jnp.concatenate([e_re[:, g * ns:(g + 1) * ns], e_im[:, g * ns:(g + 1) * ns]], axis=1).astype(BF16)
        ys.append(proj[g][:, :w] + jnp.dot(e_g, wy_ref[0, g], preferred_element_type=F32))
    y = jnp.concatenate(ys, axis=1)
    y_hi = y.astype(BF16)
    y_lo = (y - y_hi.astype(F32)).astype(BF16)
    back = (((1,), (1,)), ((), ()))
    y_ref[0] = (lax.dot_general(y_hi, perm, back, preferred_element_type=F32)
                + lax.dot_general(y_lo, perm, back, preferred_element_type=F32))
    a_re = are_ref[0, 0:1, :]
    a_im = aim_ref[0, 0:1, :]
    f_re = a_re * e_re - a_im * e_im + d_re
    f_im = a_re * e_im + a_im * e_re + d_im
    fwd_row = fwd[0:1, :]
    for b in range(nb):
        lo, hi = b * nc, b * nc + nc - 1
        fre_ref[0, b:b + 1, :] = jnp.where(fwd_row, f_re[hi:hi + 1, :], f_re[lo:lo + 1, :])
        fim_ref[0, b:b + 1, :] = jnp.where(fwd_row, f_im[hi:hi + 1, :], f_im[lo:lo + 1, :])


def _s5_scan(ub, row0, nb, nc, mats, h0_re, h0_im):
    g, q, p, n = SSM_GROUPS, SSM_CHUNK, SSM_CH, SSM_STATE
    gb, nbun = S5_BUNDLE, SSM_GROUPS // S5_BUNDLE
    assert nc & (nc - 1) == 0 and nc % 8 == 0
    seqs = max(1, min(nb, S5_BLOCK_ROWS // nc))
    while nb % seqs or row0 % (seqs * nc):
        seqs -= 1
    n_rb, rows = nb // seqs, seqs * nc
    rb0 = row0 // rows
    kw, sw = q * gb * p, gb * 2 * n
    perm, w1, wy, a_re, a_im = mats

    def lanes(h):
        h = h.astype(F32).reshape(n_rb, seqs, 2, nbun, gb, n).transpose(3, 0, 1, 4, 2, 5)
        return h.reshape(nbun, n_rb, seqs, sw)

    weight = lambda r, c: pl.BlockSpec((1, gb, r, c), lambda o, i: (o, 0, 0, 0))
    coeff = pl.BlockSpec((1, 8, sw), lambda o, i: (o, 0, 0))
    state = pl.BlockSpec((1, 1, seqs, sw), lambda o, i: (o, i, 0, 0))
    y, f_re, f_im = pl.pallas_call(
        functools.partial(_s5_kernel, nb=seqs, nc=nc),
        grid=(nbun, n_rb),
        in_specs=[pl.BlockSpec((1, rows, kw), lambda o, i: (o, rb0 + i, 0)),
                  pl.BlockSpec((kw, kw), lambda o, i: (0, 0), pipeline_mode=pl.Buffered(1)),
                  weight(q * p, q * p + 4 * n), weight(4 * n, q * p), coeff, coeff, state, state],
        out_specs=[pl.BlockSpec((1, rows, kw), lambda o, i: (o, i, 0)), state, state],
        out_shape=[jax.ShapeDtypeStruct((nbun, nb * nc, kw), F32),
                   jax.ShapeDtypeStruct((nbun, n_rb, seqs, sw), F32),
                   jax.ShapeDtypeStruct((nbun, n_rb, seqs, sw), F32)],
        compiler_params=_cparams(("arbitrary", "arbitrary")),
    )(ub, perm, w1, wy, a_re, a_im, lanes(h0_re), lanes(h0_im))

    def unlanes(f):
        f = f.reshape(nbun, n_rb, seqs, gb, 2, n).transpose(1, 2, 4, 0, 3, 5)
        return f.reshape(nb, 2, g, n)

    return y, unlanes(f_re), unlanes(f_im)


def _s5_both(lay, u, mats, h0_re, h0_im):
    q, nbun, lanes = SSM_CHUNK, SSM_GROUPS // S5_BUNDLE, S5_BUNDLE * SSM_CH
    n_rows = lay.t // q
    ub = u.reshape(n_rows, q, nbun, lanes).transpose(2, 0, 1, 3).reshape(nbun, n_rows, q * lanes)
    zeros = jnp.zeros((lay.bc, 2, SSM_GROUPS, SSM_STATE), F32)
    y_c, f_re, f_im = _s5_scan(ub, 0, lay.bc, lay.lc // q, mats, zeros, zeros)
    y_s, _, _ = _s5_scan(ub, lay.tc // q, lay.bs, lay.ls // q, mats, h0_re, h0_im)
    y = jnp.concatenate([y_c, y_s], axis=1)
    y = y.reshape(nbun, n_rows, q, lanes).transpose(1, 2, 0, 3).reshape(lay.t, SSM_WIDTH)
    return y, f_re, f_im


def _merge_kernel(h_ref, ys_ref, at_ref, gb_ref, gc_ref, uc_ref, gcp_ref, ucp_ref, gcn_ref, ucn_ref, cw_ref,
                  wglu_ref, wg0_ref, wg1_ref, wg2_ref, bg0_ref, bg1_ref, bg2_ref, ws_ref, wa_ref, wc_ref,
                  o_ref, ssm_scr, conv_scr, *, lay, tm, halo):
    i = pl.program_id(0)
    n = pl.program_id(1)

    @pl.when(n == 0)
    def _():
        y = ys_ref[...]
        ge = 0.5 * y * (1.0 + jnp.tanh(math.sqrt(2.0 / math.pi) * (y + 0.044715 * (y * y * y))))
        glu = jnp.dot(ge.astype(BF16), wglu_ref[...], preferred_element_type=F32)
        ssm_scr[...] = (ge * _sigmoid(glu)).astype(BF16)

        z = gc_ref[...].astype(F32) * uc_ref[...].astype(F32)
        z_before = gcp_ref[halo - 1:halo, :].astype(F32) * ucp_ref[halo - 1:halo, :].astype(F32)
        z_after = gcn_ref[0:1, :].astype(F32) * ucn_ref[0:1, :].astype(F32)
        local = lax.broadcasted_iota(I32, z.shape, 0)
        pos, seq_len = lay.seq_pos(local + i * tm, i, tm)
        z_prev = jnp.where(local == 0, z_before, pltpu.roll(z, 1, 0))
        z_prev = jnp.where(pos == 0, 0.0, z_prev)
        z_next = jnp.where(local == tm - 1, z_after, pltpu.roll(z, tm - 1, 0))
        z_next = jnp.where(pos == seq_len - 1, 0.0, z_next)
        conv = cw_ref[0:1, :] * z_prev + cw_ref[1:2, :] * z + cw_ref[2:3, :] * z_next
        conv_scr[...] = (gb_ref[...].astype(F32) * conv).astype(BF16)

    h = h_ref[...]
    acc = None
    for act, wg_ref, bg_ref, wb_ref in ((ssm_scr[...], wg0_ref, bg0_ref, ws_ref),
                                        (at_ref[...], wg1_ref, bg1_ref, wa_ref),
                                        (conv_scr[...], wg2_ref, bg2_ref, wc_ref)):
        gate = _sigmoid(jnp.dot(h, wg_ref[...], preferred_element_type=F32) + bg_ref[...])
        term = gate * jnp.dot(act, wb_ref[...], preferred_element_type=F32)
        acc = term if acc is None else acc + term
    o_ref[...] = acc.astype(BF16)


def _merge(lay, h16, y_ssm, attn, y16, conv_w, wglu16, wgates16, b_gates, wbs16, wba16, wbc16):
    d = lay.d
    tm = lay.row_tile(512)
    tn = min(COL_TILE, d)
    nd = d // tn
    halo = 16
    hb = tm // halo
    last_h = lay.t // halo - 1
    c0 = (ATTN_WIDTH + 2 * KV_WIDTH + SSM_WIDTH) // CONV_WIDTH
    row = lambda cb: pl.BlockSpec((tm, CONV_WIDTH), lambda i, n: (i, cb))
    before = lambda cb: pl.BlockSpec((halo, CONV_WIDTH), lambda i, n: (jnp.maximum(i * hb - 1, 0), cb))
    after = lambda cb: pl.BlockSpec((halo, CONV_WIDTH), lambda i, n: (jnp.minimum((i + 1) * hb, last_h), cb))
    gate_w = lambda br: pl.BlockSpec((d, tn), lambda i, n: (0, br * nd + n))
    gate_b = lambda br: pl.BlockSpec((1, tn), lambda i, n: (0, br * nd + n))
    return pl.pallas_call(
        functools.partial(_merge_kernel, lay=lay, tm=tm, halo=halo),
        grid=(lay.t // tm, nd),
        in_specs=[
            pl.BlockSpec((tm, d), lambda i, n: (i, 0)),
            pl.BlockSpec((tm, SSM_WIDTH), lambda i, n: (i, 0)),
            pl.BlockSpec((tm, ATTN_WIDTH), lambda i, n: (i, 0)),
            row(c0), row(c0 + 1), row(c0 + 2),
            before(c0 + 1), before(c0 + 2), after(c0 + 1), after(c0 + 2),
            pl.BlockSpec((3, CONV_WIDTH), lambda i, n: (0, 0)),
            pl.BlockSpec((SSM_WIDTH, SSM_WIDTH), lambda i, n: (0, 0)),
            gate_w(0), gate_w(1), gate_w(2), gate_b(0), gate_b(1), gate_b(2),
            pl.BlockSpec((SSM_WIDTH, tn), lambda i, n: (0, n)),
            pl.BlockSpec((ATTN_WIDTH, tn), lambda i, n: (0, n)),
            pl.BlockSpec((CONV_WIDTH, tn), lambda i, n: (0, n)),
        ],
        out_specs=pl.BlockSpec((tm, tn), lambda i, n: (i, n)),
        out_shape=jax.ShapeDtypeStruct((lay.t, d), BF16),
        scratch_shapes=[pltpu.VMEM((tm, SSM_WIDTH), BF16), pltpu.VMEM((tm, CONV_WIDTH), BF16)],
        compiler_params=_cparams(("arbitrary", "arbitrary")),
    )(h16, y_ssm, attn, y16, y16, y16, y16, y16, y16, y16, conv_w, wglu16,
      wgates16, wgates16, wgates16, b_gates, b_gates, b_gates, wbs16, wba16, wbc16)


def _outproj_kernel(m_ref, w_ref, x_ref, gate_ref, g2_ref, shift_ref, scale_ref, wrh_ref, wrl_ref,
                    xo_ref, hp_ref, lg_ref):
    acc = jnp.dot(m_ref[...], w_ref[...], preferred_element_type=F32)
    xn = x_ref[...] + gate_ref[0] * acc
    xo_ref[...] = xn
    ms = jnp.mean(xn * xn, axis=-1, keepdims=True)
    h2 = xn * lax.rsqrt(ms + EPS) * g2_ref[...]
    h2 = h2 * (1.0 + scale_ref[0]) + shift_ref[0]
    hp_ref[...] = _pack_bf16_pairs(h2)
    h_hi = h2.astype(BF16)
    h_lo = (h2 - h_hi.astype(F32)).astype(BF16)
    logits = (jnp.dot(h_hi, wrh_ref[...], preferred_element_type=F32)
              + jnp.dot(h_hi, wrl_ref[...], preferred_element_type=F32)
              + jnp.dot(h_lo, wrh_ref[...], preferred_element_type=F32))
    lg_ref[...] = logits.T[:N_EXPERTS, :]


def _outproj(lay, merged, wout16, x, mod_l, g2, w_router):
    d = lay.d
    tm = lay.row_tile(256)
    lanes = 128
    wr = jnp.zeros((d, lanes), F32).at[:, :N_EXPERTS].set(w_router.astype(F32))
    wr_hi = wr.astype(BF16)
    wr_lo = (wr - wr_hi.astype(F32)).astype(BF16)
    return pl.pallas_call(
        _outproj_kernel,
        grid=(lay.t // tm,),
        in_specs=[
            pl.BlockSpec((tm, d), lambda i: (i, 0)),
            pl.BlockSpec((d, d), lambda i: (0, 0)),
            pl.BlockSpec((tm, d), lambda i: (i, 0)),
            _mod_spec(lay, tm, 2, d),
            pl.BlockSpec((1, d), lambda i: (0, 0)),
            _mod_spec(lay, tm, 3, d),
            _mod_spec(lay, tm, 4, d),
            pl.BlockSpec((d, lanes), lambda i: (0, 0)),
            pl.BlockSpec((d, lanes), lambda i: (0, 0)),
        ],
        out_specs=[
            pl.BlockSpec((tm, d), lambda i: (i, 0)),
            pl.BlockSpec((tm, d // 2), lambda i: (i, 0)),
            pl.BlockSpec((N_EXPERTS, tm), lambda i: (0, i)),
        ],
        out_shape=[
            jax.ShapeDtypeStruct((lay.t, d), F32),
            jax.ShapeDtypeStruct((lay.t, d // 2), U32),
            jax.ShapeDtypeStruct((N_EXPERTS, lay.t), F32),
        ],
        compiler_params=_cparams(("arbitrary",)),
    )(merged, wout16, x, mod_l, g2.reshape(1, d), mod_l, mod_l, wr_hi, wr_lo)


def _route_kernel(lg_ref, br_ref, idx_ref, w_ref, pos_ref, cnt_ref, carry):
    step = pl.program_id(0)
    tt = lg_ref.shape[1]
    per_group = N_EXPERTS // N_EXPERT_GROUPS

    @pl.when(step == 0)
    def _():
        carry[...] = jnp.zeros_like(carry)

    scores = _sigmoid(lg_ref[...])
    biased = scores + br_ref[...]
    sub = lax.broadcasted_iota(I32, (per_group, tt), 0).astype(F32)
    blocks, group_score = [], []
    for g in range(N_EXPERT_GROUPS):
        blk = biased[g * per_group:(g + 1) * per_group, :]
        m1 = jnp.max(blk, axis=0, keepdims=True)
        i1 = jnp.min(jnp.where(blk == m1, sub, float(per_group)), axis=0, keepdims=True)
        m2 = jnp.max(jnp.where(sub == i1, -jnp.inf, blk), axis=0, keepdims=True)
        blocks.append(blk)
        group_score.append(m1 + m2)
    masked = []
    for g in range(N_EXPERT_GROUPS):
        beaten_by = jnp.zeros((1, tt), F32)
        for o in range(N_EXPERT_GROUPS):
            if o == g:
                continue
            wins = (group_score[o] > group_score[g]) | ((group_score[o] == group_score[g]) & (o < g))
            beaten_by = beaten_by + wins.astype(F32)
        masked.append(jnp.where(beaten_by < TOPK_GROUPS, blocks[g], -jnp.inf))
    masked = jnp.concatenate(masked, axis=0)
    eid = lax.broadcasted_iota(I32, (N_EXPERTS, tt), 0).astype(F32)
    chosen, weights = [], []
    onehot = jnp.zeros((N_EXPERTS, tt), F32)
    for _ in range(TOP_K):
        m = jnp.max(masked, axis=0, keepdims=True)
        e = jnp.min(jnp.where(masked == m, eid, float(N_EXPERTS)), axis=0, keepdims=True)
        hit = eid == e
        chosen.append(e)
        weights.append(jnp.sum(jnp.where(hit, scores, 0.0), axis=0, keepdims=True))
        onehot = onehot + hit.astype(F32)
        masked = jnp.where(hit, -jnp.inf, masked)
    total = weights[0]
    for wk in weights[1:]:
        total = total + wk
    earlier = (lax.broadcasted_iota(I32, (tt, tt), 0) < lax.broadcasted_iota(I32, (tt, tt), 1)).astype(BF16)
    rank = carry[...][:, 0:1] + jnp.dot(onehot.astype(BF16), earlier, preferred_element_type=F32)
    for k in range(TOP_K):
        idx_ref[k:k + 1, :] = chosen[k].astype(I32)
        w_ref[k:k + 1, :] = weights[k] / total * ROUTED_SCALE
        pos_ref[k:k + 1, :] = jnp.sum(jnp.where(eid == chosen[k], rank, 0.0), axis=0, keepdims=True).astype(I32)
    for k in range(TOP_K, 8):
        idx_ref[k:k + 1, :] = jnp.zeros((1, tt), I32)
        w_ref[k:k + 1, :] = jnp.zeros((1, tt), F32)
        pos_ref[k:k + 1, :] = jnp.zeros((1, tt), I32)
    carry[...] = carry[...] + jnp.sum(onehot, axis=1, keepdims=True)
    cnt_ref[...] = carry[...]


def _route(logits_t, b_router):
    t = logits_t.shape[1]
    tt = math.gcd(t, 512)
    tok = pl.BlockSpec((8, tt), lambda i: (0, i))
    return pl.pallas_call(
        _route_kernel,
        grid=(t // tt,),
        in_specs=[pl.BlockSpec((N_EXPERTS, tt), lambda i: (0, i)),
                  pl.BlockSpec((N_EXPERTS, 1), lambda i: (0, 0))],
        out_specs=[tok, tok, tok, pl.BlockSpec((N_EXPERTS, 128), lambda i: (0, 0))],
        out_shape=[jax.ShapeDtypeStruct((8, t), I32), jax.ShapeDtypeStruct((8, t), F32),
                   jax.ShapeDtypeStruct((8, t), I32), jax.ShapeDtypeStruct((N_EXPERTS, 128), F32)],
        scratch_shapes=[pltpu.VMEM((N_EXPERTS, 128), F32)],
        compiler_params=_cparams(("arbitrary",)),
    )(logits_t, b_router.astype(F32).reshape(N_EXPERTS, 1))


def _dest_kernel(idx_ref, pos_ref, start_ref, o_ref, *, tt):
    eid = lax.broadcasted_iota(I32, (N_EXPERTS, tt), 0)
    start = start_ref[...]
    for k in range(TOP_K):
        first = jnp.sum(jnp.where(eid == idx_ref[k:k + 1, :], start, 0.0), axis=0, keepdims=True)
        o_ref[k:k + 1, :] = first.astype(I32) + pos_ref[k:k + 1, :]
    for k in range(TOP_K, 8):
        o_ref[k:k + 1, :] = jnp.zeros((1, tt), I32)


def _dest_rows(idx_t, pos_t, pad_start):
    t = idx_t.shape[1]
    tt = math.gcd(t, 512)
    tok = pl.BlockSpec((8, tt), lambda i: (0, i))
    return pl.pallas_call(
        functools.partial(_dest_kernel, tt=tt),
        grid=(t // tt,),
        in_specs=[tok, tok, pl.BlockSpec((N_EXPERTS, 1), lambda i: (0, 0))],
        out_specs=tok,
        out_shape=jax.ShapeDtypeStruct((8, t), I32),
        compiler_params=_cparams(("arbitrary",)),
    )(idx_t, pos_t, pad_start.astype(F32).reshape(N_EXPERTS, 1))


def _sc_layout(n_pairs, period):
    info = plsc.get_sparse_core_info()
    workers = info.num_cores * info.num_subcores
    per_worker = n_pairs // workers
    chunk = math.gcd(math.gcd(per_worker, period), SC_STREAM_ROWS)
    assert per_worker * workers == n_pairs and chunk % 8 == 0
    return info.num_cores, per_worker, chunk


def _sc_scatter_rows(rows, dest, n_out):
    t, width = rows.shape
    copies = dest.shape[0] // t
    n_cores, per_worker, chunk = _sc_layout(t, t)
    mesh = plsc.VectorSubcoreMesh(core_axis_name="core", subcore_axis_name="subcore")

    @functools.partial(
        pl.kernel, mesh=mesh, out_type=jax.ShapeDtypeStruct((n_out, width), rows.dtype),
        scratch_types=[pltpu.VMEM((chunk,), I32), pltpu.VMEM((chunk, width), rows.dtype)])
    def scatter(rows_hbm, dest_hbm, out_hbm, dest_v, rows_v):
        base = (lax.axis_index("subcore") * n_cores + lax.axis_index("core")) * per_worker

        @pl.loop(0, per_worker // chunk)
        def _(j):
            tok = base + j * chunk
            pltpu.sync_copy(rows_hbm.at[pl.ds(tok, chunk)], rows_v)
            for k in range(copies):
                pltpu.sync_copy(dest_hbm.at[pl.ds(k * t + tok, chunk)], dest_v)
                pltpu.sync_copy(rows_v, out_hbm.at[dest_v])

    return scatter(rows, dest)


def _sc_gather_rows(table, dest):
    width = table.shape[1]
    n_pairs = dest.shape[0]
    n_cores, per_worker, chunk = _sc_layout(n_pairs, n_pairs)
    mesh = plsc.VectorSubcoreMesh(core_axis_name="core", subcore_axis_name="subcore")

    @functools.partial(
        pl.kernel, mesh=mesh, out_type=jax.ShapeDtypeStruct((n_pairs, width), table.dtype),
        scratch_types=[pltpu.VMEM((chunk,), I32), pltpu.VMEM((chunk, width), table.dtype)])
    def gather(table_hbm, dest_hbm, out_hbm, dest_v, rows_v):
        base = (lax.axis_index("subcore") * n_cores + lax.axis_index("core")) * per_worker

        @pl.loop(0, per_worker // chunk)
        def _(j):
            pair = base + j * chunk
            pltpu.sync_copy(dest_hbm.at[pl.ds(pair, chunk)], dest_v)
            pltpu.sync_copy(table_hbm.at[dest_v], rows_v)
            pltpu.sync_copy(rows_v, out_hbm.at[pl.ds(pair, chunk)])

    return gather(table, dest)


def _expert_kernel(be_ref, xs_ref, wg_ref, wu_ref, wd_ref, ys_ref, wg16, wu16, wd16):
    i = pl.program_id(0)
    changed = jnp.logical_or(i == 0, be_ref[i] != be_ref[jnp.maximum(i - 1, 0)])

    @pl.when(changed)
    def _():
        rows = 256
        d = wg16.shape[0]

        def cast_in(r, carry):
            sl = pl.ds(pl.multiple_of(r * rows, rows), rows)
            wg16[sl, :] = wg_ref[0, sl, :].astype(BF16)
            wu16[sl, :] = wu_ref[0, sl, :].astype(BF16)
            return carry

        lax.fori_loop(0, d // rows, cast_in, 0)

        def cast_down(r, carry):
            sl = pl.ds(pl.multiple_of(r * 128, 128), 128)
            wd16[sl, :] = wd_ref[0, sl, :].astype(BF16)
            return carry

        lax.fori_loop(0, D_EXPERT // 128, cast_down, 0)

    n_blocks = pl.num_programs(0)
    used = i < be_ref[n_blocks]

    @pl.when(used)
    def _():
        live = lax.broadcasted_iota(I32, (xs_ref.shape[0], 1), 0) < be_ref[n_blocks + 1 + i]
        hi, lo = _unpack_bf16_pairs(jnp.where(live, xs_ref[...], jnp.uint32(0)))
        x = jnp.concatenate([hi, lo], axis=1).astype(BF16)
        gate = jnp.dot(x, wg16[...], preferred_element_type=F32)
        up = jnp.dot(x, wu16[...], preferred_element_type=F32)
        act = (gate * _sigmoid(gate) * up).astype(BF16)
        ys_ref[...] = _pack_bf16_pairs(jnp.dot(act, wd16[...], preferred_element_type=F32))

    @pl.when(jnp.logical_not(used))
    def _():
        ys_ref[...] = jnp.zeros_like(ys_ref)


def _experts(xs, block_e, layer, w_gate, w_up, w_down):
    n_rows, half = xs.shape
    d = 2 * half
    br = EXPERT_ROWS
    grid_spec = pltpu.PrefetchScalarGridSpec(
        num_scalar_prefetch=1,
        grid=(n_rows // br,),
        in_specs=[
            pl.BlockSpec((br, half), lambda i, be: (i, 0)),
            pl.BlockSpec((None, 1, d, D_EXPERT), lambda i, be: (layer, be[i], 0, 0)),
            pl.BlockSpec((None, 1, d, D_EXPERT), lambda i, be: (layer, be[i], 0, 0)),
            pl.BlockSpec((None, 1, D_EXPERT, d), lambda i, be: (layer, be[i], 0, 0)),
        ],
        out_specs=pl.BlockSpec((br, half), lambda i, be: (i, 0)),
        scratch_shapes=[pltpu.VMEM((d, D_EXPERT), BF16), pltpu.VMEM((d, D_EXPERT), BF16),
                        pltpu.VMEM((D_EXPERT, d), BF16)],
    )
    return pl.pallas_call(
        _expert_kernel,
        grid_spec=grid_spec,
        out_shape=jax.ShapeDtypeStruct((n_rows, half), U32),
        compiler_params=_cparams(("arbitrary",)),
    )(block_e, xs, w_gate, w_up, w_down)


def _shared_kernel(hp_ref, wsg_ref, wsu_ref, wsd_ref, o_ref):
    hi, lo = _unpack_bf16_pairs(hp_ref[...])
    h2 = jnp.concatenate([hi, lo], axis=1).astype(BF16)
    sg = jnp.dot(h2, wsg_ref[...], preferred_element_type=F32)
    su = jnp.dot(h2, wsu_ref[...], preferred_element_type=F32)
    shared = jnp.dot((sg * _sigmoid(sg) * su).astype(BF16), wsd_ref[...], preferred_element_type=F32)
    o_ref[...] = shared.astype(o_ref.dtype)


def _shared_expert(lay, h2p, wsg16, wsu16, wsd16):
    d = lay.d
    tm = lay.row_tile(512)
    return pl.pallas_call(
        _shared_kernel,
        grid=(lay.t // tm,),
        in_specs=[pl.BlockSpec((tm, d // 2), lambda i: (i, 0)),
                  pl.BlockSpec((d, D_EXPERT), lambda i: (0, 0)),
                  pl.BlockSpec((d, D_EXPERT), lambda i: (0, 0)),
                  pl.BlockSpec((D_EXPERT, d), lambda i: (0, 0))],
        out_specs=pl.BlockSpec((tm, d), lambda i: (i, 0)),
        out_shape=jax.ShapeDtypeStruct((lay.t, d), BF16),
        compiler_params=_cparams(("arbitrary",)),
    )(h2p, wsg16, wsu16, wsd16)


def _combine_kernel(yk_ref, wsel_ref, sh_ref, x_ref, gate_ref, fg_ref, *outs, tt, final, n_ctx_tiles):
    shared = sh_ref[...].astype(F32)
    half = yk_ref.shape[2]
    r_hi = jnp.zeros((tt, half), F32)
    r_lo = jnp.zeros((tt, half), F32)
    for k in range(TOP_K):
        y_hi, y_lo = _unpack_bf16_pairs(yk_ref[k])
        wk = wsel_ref[:, k:k + 1]
        r_hi = r_hi + wk * y_hi
        r_lo = r_lo + wk * y_lo
    routed = jnp.concatenate([r_hi, r_lo], axis=1)
    out = x_ref[...] + gate_ref[0] * (routed + shared)
    if not final:
        outs[0][...] = out
        return
    ms = jnp.mean(out * out, axis=-1, keepdims=True)
    out = out * lax.rsqrt(ms + EPS) * fg_ref[...]
    ctx_ref, lat_ref = outs
    is_ctx = pl.program_id(0) < n_ctx_tiles

    @pl.when(is_ctx)
    def _():
        ctx_ref[...] = out

    @pl.when(jnp.logical_not(is_ctx))
    def _():
        lat_ref[...] = out


def _combine(lay, y_by_k, wsel, shared, x, mod_l, final_g, tt, final):
    d = lay.d
    half = d // 2
    nct = lay.tc // tt
    if final:
        out_specs = [pl.BlockSpec((tt, d), lambda i: (jnp.minimum(i, nct - 1), 0)),
                     pl.BlockSpec((tt, d), lambda i: (jnp.maximum(i - nct, 0), 0))]
        out_shape = [jax.ShapeDtypeStruct((lay.tc, d), F32), jax.ShapeDtypeStruct((lay.ts, d), F32)]
    else:
        out_specs = pl.BlockSpec((tt, d), lambda i: (i, 0))
        out_shape = jax.ShapeDtypeStruct((lay.t, d), F32)
    return pl.pallas_call(
        functools.partial(_combine_kernel, tt=tt, final=final, n_ctx_tiles=nct),
        grid=(lay.t // tt,),
        in_specs=[
            pl.BlockSpec((TOP_K, tt, half), lambda i: (0, i, 0)),
            pl.BlockSpec((tt, 8), lambda i: (i, 0)),
            pl.BlockSpec((tt, d), lambda i: (i, 0)),
            pl.BlockSpec((tt, d), lambda i: (i, 0)),
            _mod_spec(lay, tt, 5, d),
            pl.BlockSpec((1, d), lambda i: (0, 0)),
        ],
        out_specs=out_specs,
        out_shape=out_shape,
        compiler_params=_cparams(("arbitrary",)),
    )(y_by_k, wsel, shared, x, mod_l, final_g.reshape(1, d))


def _moe(lay, layer, x, h2p, logits_t, mod_l, b_router, w_e_gate, w_e_up, w_e_down, wsg16, wsu16, wsd16, final_g,
         final):
    t = lay.t
    idx_t, w_t, pos_t, counts = _route(logits_t, b_router)
    br = EXPERT_ROWS
    counts = counts[:, 0].astype(I32)
    padded = (counts + br - 1) // br * br
    pad_end = jnp.cumsum(padded)
    pad_start = pad_end - padded
    n_blocks = -(-(t * TOP_K + N_EXPERTS * (br - 1)) // br)
    n_rows = n_blocks * br
    first_row = jnp.arange(n_blocks, dtype=I32) * br
    block_e = jnp.minimum(jnp.sum((pad_end[None, :] <= first_row[:, None]).astype(I32), axis=1), N_EXPERTS - 1)
    own = block_e[:, None] == jnp.arange(N_EXPERTS, dtype=I32)[None, :]
    rows_end = jnp.sum(jnp.where(own, (pad_start + counts)[None, :], 0), axis=1)
    held = jnp.clip(rows_end - first_row, 0, br)
    block_meta = jnp.concatenate([block_e, pad_end[-1:] // br, held]).astype(I32)
    dest = _dest_rows(idx_t, pos_t, pad_start)[:TOP_K].reshape(TOP_K * t)
    xs = _sc_scatter_rows(h2p, dest, n_rows)
    shared = _shared_expert(lay, h2p, wsg16, wsu16, wsd16)
    ys = _experts(xs, block_meta, layer, w_e_gate, w_e_up, w_e_down)
    y_by_k = _sc_gather_rows(ys, dest).reshape(TOP_K, t, h2p.shape[1])
    return _combine(lay, y_by_k, w_t.T, shared, x, mod_l, final_g, lay.row_tile(256), final)


def kernel(x_prompt, x_sample, c, cache_k, cache_v, state_ssm_re, state_ssm_im, c_ctx, w_ada, b_ada, norm1_g, norm2_g, w_in, w_gates, b_gates, ssm_lam_re, ssm_lam_im, ssm_log_dt, ssm_b_re, ssm_b_im, ssm_c_re, ssm_c_im, ssm_d, ssm_w_glu, conv_w, attn_sink, w_br_ssm, w_br_attn, w_br_conv, w_out, w_router, b_router, w_e_gate, w_e_up, w_e_down, w_s_gate, w_s_up, w_s_down, final_g):
    bc, lc, d = x_prompt.shape
    bs, ls, _ = x_sample.shape
    depth = w_in.shape[0]
    lay = _Layout(bc, lc, bs, ls, d)
    assert 1 + bs <= MOD_ROWS

    x = jnp.concatenate([x_prompt.reshape(lay.tc, d), x_sample.reshape(lay.ts, d)], axis=0)
    cvec = jnp.zeros((MOD_ROWS, d), F32).at[0].set(c_ctx).at[1:1 + bs].set(c)
    mod = _adaln(cvec, w_ada, b_ada).reshape(depth, MOD_ROWS * 6, 1, d)
    rope_cos, rope_sin = _rope_tables(lay, lay.row_tile(1024))
    s5_perm = _s5_lane_permutation()
    s5_mats = jax.vmap(functools.partial(_s5_matrices, n_steps=8))(
        ssm_lam_re, ssm_lam_im, ssm_log_dt, ssm_b_re, ssm_b_im, ssm_c_re, ssm_c_im, ssm_d)

    ks, vs, s_re, s_im = [], [], [], []
    for l in range(depth):
        mod_l = mod[l]
        h16, y16, kv32 = _inproj(lay, x, mod_l, norm1_g[l], w_in[l].astype(BF16), rope_cos, rope_sin)
        ks.append(kv32[:lay.tc, :KV_WIDTH].reshape(bc, lc, N_KV_HEADS, HEAD_DIM))
        vs.append(kv32[:lay.tc, KV_WIDTH:].reshape(bc, lc, N_KV_HEADS, HEAD_DIM))

        attn = _attention(lay, y16, attn_sink[l].astype(F32), cache_k[:, l], cache_v[:, l])

        u0 = ATTN_WIDTH + 2 * KV_WIDTH
        u = y16[:, u0:u0 + SSM_WIDTH]
        mats = (s5_perm,) + tuple(m[l] for m in s5_mats)
        y_ssm, f_re, f_im = _s5_both(lay, u, mats, state_ssm_re[:, l], state_ssm_im[:, l])
        s_re.append(f_re)
        s_im.append(f_im)

        merged = _merge(lay, h16, y_ssm, attn, y16, conv_w[l], ssm_w_glu[l].astype(BF16),
                        w_gates[l].astype(BF16), b_gates[l].reshape(1, -1), w_br_ssm[l].astype(BF16),
                        w_br_attn[l].astype(BF16), w_br_conv[l].astype(BF16))
        x, h2p, logits_t = _outproj(lay, merged, w_out[l].astype(BF16), x, mod_l, norm2_g[l],
                                    w_router[l])
        x = _moe(lay, l, x, h2p, logits_t, mod_l, b_router[l], w_e_gate, w_e_up, w_e_down,
                 w_s_gate[l].astype(BF16), w_s_up[l].astype(BF16), w_s_down[l].astype(BF16),
                 final_g, l == depth - 1)

    y_prompt = x[0].reshape(bc, lc, d)
    y_sample = x[1].reshape(bs, ls, d)
    return (y_prompt, y_sample, jnp.stack(ks, axis=1), jnp.stack(vs, axis=1),
            jnp.stack(s_re, axis=1), jnp.stack(s_im, axis=1))
```

```python
import functools
import math

import jax
import jax.numpy as jnp
from jax import lax
from jax.experimental import pallas as pl
from jax.experimental.pallas import tpu as pltpu
from jax.experimental.pallas import tpu_sc as plsc

HEAD_DIM = 128
N_HEADS = 8
N_KV_HEADS = 2
GROUP = N_HEADS // N_KV_HEADS
ATTN_WIDTH = N_HEADS * HEAD_DIM
KV_WIDTH = N_KV_HEADS * HEAD_DIM
WINDOW = 128
ATTN_BLOCK = 128
ATTN_Q_ROWS = 256
ATTN_SCALE = HEAD_DIM ** -0.5
LOG2_E = 1.4426950408889634
ROPE_BASE = 10000.0
ROT_F = HEAD_DIM // 4
GRID_W = 64
SSM_WIDTH = 512
SSM_CH = 16
SSM_GROUPS = SSM_WIDTH // SSM_CH
SSM_STATE = 64
SSM_CHUNK = 16
S5_BUNDLE = 8
S5_BLOCK_ROWS = 256
CONV_WIDTH = 512
N_BRANCHES = 3
IN_WIDTH = ATTN_WIDTH + 2 * KV_WIDTH + SSM_WIDTH + 3 * CONV_WIDTH
N_EXPERTS = 64
TOP_K = 6
N_EXPERT_GROUPS = 8
TOPK_GROUPS = 4
D_EXPERT = 512
ROUTED_SCALE = 2.5
EPS = 1e-6
NEG_INF = -1e30

COL_TILE = 512
MOD_ROWS = 16
EXPERT_ROWS = 512
SC_STREAM_ROWS = 64
VMEM_LIMIT_V7X = 56 * 1024 * 1024

F32 = jnp.float32
BF16 = jnp.bfloat16
I32 = jnp.int32
U32 = jnp.uint32


def _cparams(sem, vmem=VMEM_LIMIT_V7X):
    return pltpu.CompilerParams(dimension_semantics=sem, vmem_limit_bytes=vmem)


def _sigmoid(x):
    return 1.0 / (1.0 + jnp.exp(-x))


def _pack_bf16_pairs(v):
    n = v.shape[1] // 2
    hi = lax.bitcast_convert_type(v[:, :n].astype(BF16).astype(F32), U32)
    lo = lax.bitcast_convert_type(v[:, n:].astype(BF16).astype(F32), U32)
    return hi | (lo >> 16)


def _unpack_bf16_pairs(p):
    hi = lax.bitcast_convert_type(p & jnp.uint32(0xFFFF0000), F32)
    lo = lax.bitcast_convert_type(p << 16, F32)
    return hi, lo


def _adaln_kernel(c_ref, w_ref, b_ref, o_ref):
    c = c_ref[...]
    s = (c * _sigmoid(c)).astype(BF16)
    o_ref[0] = jnp.dot(s, w_ref[0].astype(BF16), preferred_element_type=F32) + b_ref[0]


def _adaln(cvec, w_ada, b_ada):
    depth, d, n6 = w_ada.shape
    tn = math.gcd(1024, n6)
    return pl.pallas_call(
        _adaln_kernel,
        grid=(depth, n6 // tn),
        in_specs=[
            pl.BlockSpec((MOD_ROWS, d), lambda l, n: (0, 0)),
            pl.BlockSpec((1, d, tn), lambda l, n: (l, 0, n)),
            pl.BlockSpec((1, 1, tn), lambda l, n: (l, 0, n)),
        ],
        out_specs=pl.BlockSpec((1, MOD_ROWS, tn), lambda l, n: (l, 0, n)),
        out_shape=jax.ShapeDtypeStruct((depth, MOD_ROWS, n6), F32),
        compiler_params=_cparams(("arbitrary", "arbitrary")),
    )(cvec, w_ada, b_ada.reshape(depth, 1, n6))


class _Layout:
    def __init__(self, n_ctx_seq, len_ctx, n_lat_seq, len_lat, d_model):
        self.bc, self.lc, self.bs, self.ls, self.d = n_ctx_seq, len_ctx, n_lat_seq, len_lat, d_model
        self.tc = n_ctx_seq * len_ctx
        self.ts = n_lat_seq * len_lat
        self.t = self.tc + self.ts

    def row_tile(self, want):
        tm = math.gcd(math.gcd(self.tc, self.ls), want)
        assert tm % 16 == 0
        return tm

    def mod_index(self, i, tm):
        nct, tps = self.tc // tm, self.ls // tm
        return jnp.where(i < nct, 0, 1 + (i - nct) // tps)

    def seq_pos(self, rows, i, tm):
        is_lat = i >= self.tc // tm
        return jnp.where(is_lat, (rows - self.tc) % self.ls, rows % self.lc), jnp.where(is_lat, self.ls, self.lc)


def _mod_spec(lay, tm, slot, d):
    return pl.BlockSpec((1, 1, d), lambda i, *_: (lay.mod_index(i, tm) * 6 + slot, 0, 0))


def _rope(z, cos, sin_signed, first_half):
    swapped = jnp.where(first_half, pltpu.roll(z, HEAD_DIM - ROT_F, 1), pltpu.roll(z, ROT_F, 1))
    return z * cos + swapped * sin_signed


def _inproj_kernel(x_ref, shift_ref, scale_ref, g_ref, w_ref, cos_ref, sin_ref, h_ref, y_ref, kv_ref):
    n = pl.program_id(1)
    n_q = ATTN_WIDTH // COL_TILE

    @pl.when(n == 0)
    def _():
        x = x_ref[...]
        ms = jnp.mean(x * x, axis=-1, keepdims=True)
        y = x * lax.rsqrt(ms + EPS) * g_ref[...]
        h_ref[...] = (y * (1.0 + scale_ref[0]) + shift_ref[0]).astype(BF16)

    def project():
        return jnp.dot(h_ref[...], w_ref[...], preferred_element_type=F32)

    def rotated(acc, n_heads):
        cos, sin = cos_ref[...], sin_ref[...]
        first_half = (lax.broadcasted_iota(I32, cos.shape, 1) % (2 * ROT_F)) < ROT_F
        parts = [_rope(acc[:, s * HEAD_DIM:(s + 1) * HEAD_DIM], cos, sin, first_half) for s in range(n_heads)]
        if n_heads * HEAD_DIM < COL_TILE:
            parts.append(acc[:, n_heads * HEAD_DIM:])
        return jnp.concatenate(parts, axis=1)

    @pl.when(n < n_q)
    def _():
        y_ref[...] = (rotated(project(), COL_TILE // HEAD_DIM) * (ATTN_SCALE * LOG2_E)).astype(BF16)

    @pl.when(n == n_q)
    def _():
        acc = project()
        kv_ref[...] = acc
        y_ref[...] = rotated(acc, N_KV_HEADS).astype(BF16)

    @pl.when(n > n_q)
    def _():
        y_ref[...] = project().astype(BF16)


def _inproj(lay, x, mod_l, g1, w_in16, rope_cos, rope_sin):
    d = lay.d
    tm = lay.row_tile(1024)
    nct, tps = lay.tc // tm, lay.ls // tm

    def rope_idx(i, n):
        return (jnp.where(i < nct, 0, 1 + (i - nct) % tps), 0)

    return pl.pallas_call(
        _inproj_kernel,
        grid=(lay.t // tm, IN_WIDTH // COL_TILE),
        in_specs=[
            pl.BlockSpec((tm, d), lambda i, n: (i, 0)),
            _mod_spec(lay, tm, 0, d),
            _mod_spec(lay, tm, 1, d),
            pl.BlockSpec((1, d), lambda i, n: (0, 0)),
            pl.BlockSpec((d, COL_TILE), lambda i, n: (0, n)),
            pl.BlockSpec((tm, HEAD_DIM), rope_idx),
            pl.BlockSpec((tm, HEAD_DIM), rope_idx),
        ],
        out_specs=[
            pl.BlockSpec((tm, d), lambda i, n: (i, 0)),
            pl.BlockSpec((tm, COL_TILE), lambda i, n: (i, n)),
            pl.BlockSpec((tm, 2 * KV_WIDTH), lambda i, n: (i, 0)),
        ],
        out_shape=[
            jax.ShapeDtypeStruct((lay.t, d), BF16),
            jax.ShapeDtypeStruct((lay.t, IN_WIDTH), BF16),
            jax.ShapeDtypeStruct((lay.t, 2 * KV_WIDTH), F32),
        ],
        compiler_params=_cparams(("arbitrary", "arbitrary")),
    )(x, mod_l, mod_l, g1.reshape(1, d), w_in16, rope_cos, rope_sin)


def _rope_tables(lay, tm):
    t = jnp.arange(lay.ls)
    row = (t // GRID_W).astype(F32)
    col = (t % GRID_W).astype(F32)
    inv = ROPE_BASE ** (-jnp.arange(ROT_F, dtype=F32) / ROT_F)
    ar, ac = row[:, None] * inv, col[:, None] * inv
    cos = jnp.concatenate([jnp.cos(ar), jnp.cos(ar), jnp.cos(ac), jnp.cos(ac)], axis=1)
    sin = jnp.concatenate([-jnp.sin(ar), jnp.sin(ar), -jnp.sin(ac), jnp.sin(ac)], axis=1)
    cos = jnp.concatenate([jnp.ones((tm, HEAD_DIM), F32), cos], axis=0)
    sin = jnp.concatenate([jnp.zeros((tm, HEAD_DIM), F32), sin], axis=0)
    return cos, sin


def _attend(q, sink_ref, j, parts):
    nq = q.shape[0]
    q4 = jnp.concatenate([q[:, g * HEAD_DIM:(g + 1) * HEAD_DIM] for g in range(GROUP)], axis=0)
    sink = jnp.concatenate([jnp.full((nq, 1), sink_ref[j * GROUP + g] * LOG2_E, F32) for g in range(GROUP)], axis=0)
    scores = []
    m = sink
    for k, _, mask in parts:
        s = lax.dot_general(q4, k, (((1,), (1,)), ((), ())), preferred_element_type=F32)
        if mask is not None:
            s = jnp.where(mask, s, NEG_INF)
        scores.append(s)
        m = jnp.maximum(m, jnp.max(s, axis=-1, keepdims=True))
    den = jnp.exp2(sink - m)
    out = jnp.zeros((GROUP * nq, HEAD_DIM), F32)
    for s, (_, v, _) in zip(scores, parts):
        p = jnp.exp2(s - m)
        den = den + jnp.sum(p, axis=-1, keepdims=True)
        out = out + jnp.dot(p.astype(BF16), v, preferred_element_type=F32)
    out = out / den
    return jnp.concatenate([out[g * nq:(g + 1) * nq] for g in range(GROUP)], axis=1)


def _head(x, j, width=HEAD_DIM):
    return x[:, j * width:(j + 1) * width]


def _attn_kernel(sink_ref, q_ref, kp_ref, kc_ref, kn_ref, vp_ref, vc_ref, vn_ref, ck_ref, cv_ref, o_ref, *,
                 n_ctx_blocks, blocks_per_seq, seq_len):
    i = pl.program_id(0)
    q = q_ref[...]

    @pl.when(i < n_ctx_blocks)
    def _():
        k, v = kc_ref[...], vc_ref[...]
        outs = [_attend(_head(q, j, GROUP * HEAD_DIM), sink_ref, j, [(_head(k, j), _head(v, j), None)])
                for j in range(N_KV_HEADS)]
        o_ref[...] = jnp.concatenate(outs, axis=1).astype(o_ref.dtype)

    @pl.when(i >= n_ctx_blocks)
    def _():
        blk = (i - n_ctx_blocks) % blocks_per_seq
        kw = jnp.concatenate([kp_ref[...], kc_ref[...], kn_ref[...]], axis=0)
        vw = jnp.concatenate([vp_ref[...], vc_ref[...], vn_ref[...]], axis=0)
        ck = ck_ref[...].astype(BF16)
        cv = cv_ref[...].astype(BF16)
        shape = (GROUP * ATTN_Q_ROWS, ATTN_Q_ROWS + 2 * ATTN_BLOCK)
        qoff = lax.broadcasted_iota(I32, shape, 0) % ATTN_Q_ROWS
        koff = lax.broadcasted_iota(I32, shape, 1) - ATTN_BLOCK
        kabs = koff + blk * ATTN_Q_ROWS
        mask = (jnp.abs(qoff - koff) <= WINDOW) & (kabs >= 0) & (kabs < seq_len)
        outs = [_attend(_head(q, j, GROUP * HEAD_DIM), sink_ref, j,
                        [(_head(kw, j), _head(vw, j), mask), (_head(ck, j), _head(cv, j), None)])
                for j in range(N_KV_HEADS)]
        o_ref[...] = jnp.concatenate(outs, axis=1).astype(o_ref.dtype)


def _attention(lay, y16, sink, cache_k_l, cache_v_l):
    smem = pl.BlockSpec(memory_space=pltpu.SMEM)
    kcol, vcol = ATTN_WIDTH // KV_WIDTH, (ATTN_WIDTH + KV_WIDTH) // KV_WIDTH
    qr = ATTN_Q_ROWS
    assert lay.lc == qr and lay.ls % qr == 0 and qr % ATTN_BLOCK == 0
    nblk = lay.ls // qr
    nctb = lay.tc // qr
    per = qr // ATTN_BLOCK
    last = lay.t // ATTN_BLOCK - 1
    past = cache_k_l.shape[1]

    def edge(col, delta):
        return pl.BlockSpec((ATTN_BLOCK, KV_WIDTH), lambda i: (jnp.clip(i * per + delta, 0, last), col))

    def own(col):
        return pl.BlockSpec((qr, KV_WIDTH), lambda i: (i, col))

    cspec = pl.BlockSpec((None, past, KV_WIDTH), lambda i: (jnp.maximum(i - nctb, 0) // nblk, 0, 0))
    return pl.pallas_call(
        functools.partial(_attn_kernel, n_ctx_blocks=nctb, blocks_per_seq=nblk, seq_len=lay.ls),
        grid=(lay.t // qr,),
        in_specs=[
            smem,
            pl.BlockSpec((qr, ATTN_WIDTH), lambda i: (i, 0)),
            edge(kcol, -1), own(kcol), edge(kcol, per),
            edge(vcol, -1), own(vcol), edge(vcol, per),
            cspec, cspec,
        ],
        out_specs=pl.BlockSpec((qr, ATTN_WIDTH), lambda i: (i, 0)),
        out_shape=jax.ShapeDtypeStruct((lay.t, ATTN_WIDTH), BF16),
        compiler_params=_cparams(("arbitrary",)),
    )(sink, y16, y16, y16, y16, y16, y16, y16,
      cache_k_l.reshape(lay.bs, past, KV_WIDTH), cache_v_l.reshape(lay.bs, past, KV_WIDTH))


def _s5_matrices(lam_re, lam_im, log_dt, b_re, b_im, c_re, c_im, d_skip, n_steps):
    q, p, g, n = SSM_CHUNK, SSM_CH, SSM_GROUPS, SSM_STATE
    lam = lax.complex(lam_re.astype(F32), lam_im.astype(F32))
    dt = jnp.exp(log_dt.astype(F32))[..., None]
    lam_dt = lam * dt
    lam_bar = jnp.exp(lam_dt)
    b_bar = ((lam_bar - 1.0) / lam)[..., None] * lax.complex(b_re.astype(F32), b_im.astype(F32))
    c_mat = lax.complex(c_re.astype(F32), c_im.astype(F32))
    steps = jnp.arange(q + 1, dtype=F32)
    pw = jnp.exp(lam_dt[:, None] * steps[None, :, None, None])
    kern = jnp.real(jnp.einsum('dgpn,dkgn,dgnr->dkgpr', c_mat, pw[:, :q], b_bar))
    tau_in = jnp.arange(q)[:, None]
    tau_out = jnp.arange(q)[None, :]
    lag_f = tau_out - tau_in
    lag_b = tau_in - tau_out
    lags = jnp.arange(q)
    pick_f = (lag_f[:, :, None] == lags).astype(F32)
    pick_b = (lag_b[:, :, None] == lags).astype(F32)
    exact = lax.Precision.HIGHEST
    kf = jnp.einsum('abk,kgpr->garbp', pick_f, kern[0], precision=exact)
    kb = jnp.einsum('abk,kgpr->garbp', pick_b, kern[1], precision=exact)
    m = kf + kb
    eye_q = jnp.eye(q, dtype=F32)[None, :, None, :, None]
    eye_p = jnp.eye(p, dtype=F32)[None, None, :, None, :]
    m = m + eye_q * eye_p * d_skip.astype(F32).reshape(g, 1, p, 1, 1)
    m = m.reshape(g, q * p, q * p)
    ws_f = pw[0, :q][::-1][:, :, :, None] * b_bar[0][None]
    ws_b = pw[1, :q][:, :, :, None] * b_bar[1][None]

    def cols(w):
        return w.transpose(1, 0, 3, 2).reshape(g, q * p, n)

    w1 = jnp.concatenate([m, jnp.real(cols(ws_f)), jnp.real(cols(ws_b)),
                          jnp.imag(cols(ws_f)), jnp.imag(cols(ws_b))], axis=2)
    cy_f = c_mat[0][None] * pw[0, 1:][:, :, None, :]
    cy_b = c_mat[1][None] * pw[1, 1:][::-1][:, :, None, :]

    def rows(w):
        return w.transpose(1, 3, 0, 2).reshape(g, n, q * p)

    wy = jnp.concatenate([jnp.real(rows(cy_f)), jnp.real(rows(cy_b)),
                          -jnp.imag(rows(cy_f)), -jnp.imag(rows(cy_b))], axis=1)
    hops = (q * 2.0 ** jnp.arange(8, dtype=F32))[None, :, None, None]
    a = jnp.exp(lam_dt[:, None] * hops)
    a = jnp.concatenate([a[0], a[1]], axis=-1).transpose(1, 0, 2)
    assert n_steps <= 8
    gb, nbun = S5_BUNDLE, g // S5_BUNDLE
    ab = a.reshape(nbun, gb, 8, 2 * n).transpose(0, 2, 1, 3).reshape(nbun, 8, gb * 2 * n)
    return (w1.astype(BF16).reshape(nbun, gb, q * p, q * p + 4 * n),
            wy.astype(BF16).reshape(nbun, gb, 4 * n, q * p), jnp.real(ab), jnp.imag(ab))


def _s5_lane_permutation():
    q, p, gb = SSM_CHUNK, SSM_CH, S5_BUNDLE
    lane = jnp.arange(q * gb * p)
    target = ((lane % (gb * p)) // p) * (q * p) + (lane // (gb * p)) * p + lane % p
    return (target[:, None] == lane[None, :]).astype(BF16)


def _s5_kernel(u_ref, perm_ref, w1_ref, wy_ref, are_ref, aim_ref, h0re_ref, h0im_ref, y_ref, fre_ref, fim_ref, *,
               nb, nc):
    r_tot = nb * nc
    w = SSM_CHUNK * SSM_CH
    ns = 2 * SSM_STATE
    n2 = S5_BUNDLE * ns
    perm = perm_ref[...]
    x = jnp.dot(u_ref[0], perm, preferred_element_type=F32).astype(BF16)
    proj = [jnp.dot(x[:, g * w:(g + 1) * w], w1_ref[0, g], preferred_element_type=F32) for g in range(S5_BUNDLE)]
    d_re = jnp.concatenate([pg[:, w:w + ns] for pg in proj], axis=1)
    d_im = jnp.concatenate([pg[:, w + ns:] for pg in proj], axis=1)
    h0re_ref, h0im_ref, fre_ref, fim_ref = (r.at[0] for r in (h0re_ref, h0im_ref, fre_ref, fim_ref))
    row = lax.broadcasted_iota(I32, (r_tot, n2), 0)
    chunk = row % nc
    seq = row // nc
    fwd = (lax.broadcasted_iota(I32, (r_tot, n2), 1) % (2 * SSM_STATE)) < SSM_STATE

    def previous(x, dist):
        valid = (fwd & (chunk >= dist)) | (~fwd & (chunk < nc - dist))
        moved = jnp.where(fwd, pltpu.roll(x, dist, 0), pltpu.roll(x, r_tot - dist, 0))
        return jnp.where(valid, moved, 0.0)

    h0_re = jnp.zeros((r_tot, n2), F32)
    h0_im = jnp.zeros((r_tot, n2), F32)
    for b in range(nb):
        h0_re = jnp.where(seq == b, h0re_ref[0, b:b + 1, :], h0_re)
        h0_im = jnp.where(seq == b, h0im_ref[0, b:b + 1, :], h0_im)
    first = (fwd & (chunk == 0)) | (~fwd & (chunk == nc - 1))
    e_re = jnp.where(first, h0_re, previous(d_re, 1))
    e_im = jnp.where(first, h0_im, previous(d_im, 1))
    k = 0
    while (1 << k) < nc:
        a_re = are_ref[0, k:k + 1, :]
        a_im = aim_ref[0, k:k + 1, :]
        p_re = previous(e_re, 1 << k)
        p_im = previous(e_im, 1 << k)
        e_re, e_im = e_re + a_re * p_re - a_im * p_im, e_im + a_re * p_im + a_im * p_re
        k += 1
    ys = []
    for g in range(S5_BUNDLE):
        e_g = jnp.concatenate([e_re[:, g * ns:(g + 1) * ns], e_im[:, g * ns:(g + 1) * ns]], axis=1).astype(BF16)
        ys.append(proj[g][:, :w] + jnp.dot(e_g, wy_ref[0, g], preferred_element_type=F32))
    y = jnp.concatenate(ys, axis=1)
    y_hi = y.astype(BF16)
    y_lo = (y - y_hi.astype(F32)).astype(BF16)
    back = (((1,), (1,)), ((), ()))
    y_ref[0] = (lax.dot_general(y_hi, perm, back, preferred_element_type=F32)
                + lax.dot_general(y_lo, perm, back, preferred_element_type=F32))
    a_re = are_ref[0, 0:1, :]
    a_im = aim_ref[0, 0:1, :]
    f_re = a_re * e_re - a_im * e_im + d_re
    f_im = a_re * e_im + a_im * e_re + d_im
    fwd_row = fwd[0:1, :]
    for b in range(nb):
        lo, hi = b * nc, b * nc + nc - 1
        fre_ref[0, b:b + 1, :] = jnp.where(fwd_row, f_re[hi:hi + 1, :], f_re[lo:lo + 1, :])
        fim_ref[0, b:b + 1, :] = jnp.where(fwd_row, f_im[hi:hi + 1, :], f_im[lo:lo + 1, :])


def _s5_scan(ub, row0, nb, nc, mats, h0_re, h0_im):
    g, q, p, n = SSM_GROUPS, SSM_CHUNK, SSM_CH, SSM_STATE
    gb, nbun = S5_BUNDLE, SSM_GROUPS // S5_BUNDLE
    assert nc & (nc - 1) == 0 and nc % 8 == 0
    seqs = max(1, min(nb, S5_BLOCK_ROWS // nc))
    while nb % seqs or row0 % (seqs * nc):
        seqs -= 1
    n_rb, rows = nb // seqs, seqs * nc
    rb0 = row0 // rows
    kw, sw = q * gb * p, gb * 2 * n
    perm, w1, wy, a_re, a_im = mats

    def lanes(h):
        h = h.astype(F32).reshape(n_rb, seqs, 2, nbun, gb, n).transpose(3, 0, 1, 4, 2, 5)
        return h.reshape(nbun, n_rb, seqs, sw)

    weight = lambda r, c: pl.BlockSpec((1, gb, r, c), lambda o, i: (o, 0, 0, 0))
    coeff = pl.BlockSpec((1, 8, sw), lambda o, i: (o, 0, 0))
    state = pl.BlockSpec((1, 1, seqs, sw), lambda o, i: (o, i, 0, 0))
    y, f_re, f_im = pl.pallas_call(
        functools.partial(_s5_kernel, nb=seqs, nc=nc),
        grid=(nbun, n_rb),
        in_specs=[pl.BlockSpec((1, rows, kw), lambda o, i: (o, rb0 + i, 0)),
                  pl.BlockSpec((kw, kw), lambda o, i: (0, 0), pipeline_mode=pl.Buffered(1)),
                  weight(q * p, q * p + 4 * n), weight(4 * n, q * p), coeff, coeff, state, state],
        out_specs=[pl.BlockSpec((1, rows, kw), lambda o, i: (o, i, 0)), state, state],
        out_shape=[jax.ShapeDtypeStruct((nbun, nb * nc, kw), F32),
                   jax.ShapeDtypeStruct((nbun, n_rb, seqs, sw), F32),
                   jax.ShapeDtypeStruct((nbun, n_rb, seqs, sw), F32)],
        compiler_params=_cparams(("arbitrary", "arbitrary")),
    )(ub, perm, w1, wy, a_re, a_im, lanes(h0_re), lanes(h0_im))

    def unlanes(f):
        f = f.reshape(nbun, n_rb, seqs, gb, 2, n).transpose(1, 2, 4, 0, 3, 5)
        return f.reshape(nb, 2, g, n)

    return y, unlanes(f_re), unlanes(f_im)


def _s5_both(lay, u, mats, h0_re, h0_im):
    q, nbun, lanes = SSM_CHUNK, SSM_GROUPS // S5_BUNDLE, S5_BUNDLE * SSM_CH
    n_rows = lay.t // q
    ub = u.reshape(n_rows, q, nbun, lanes).transpose(2, 0, 1, 3).reshape(nbun, n_rows, q * lanes)
    zeros = jnp.zeros((lay.bc, 2, SSM_GROUPS, SSM_STATE), F32)
    y_c, f_re, f_im = _s5_scan(ub, 0, lay.bc, lay.lc // q, mats, zeros, zeros)
    y_s, _, _ = _s5_scan(ub, lay.tc // q, lay.bs, lay.ls // q, mats, h0_re, h0_im)
    y = jnp.concatenate([y_c, y_s], axis=1)
    y = y.reshape(nbun, n_rows, q, lanes).transpose(1, 2, 0, 3).reshape(lay.t, SSM_WIDTH)
    return y, f_re, f_im


def _merge_kernel(h_ref, ys_ref, at_ref, gb_ref, gc_ref, uc_ref, gcp_ref, ucp_ref, gcn_ref, ucn_ref, cw_ref,
                  wglu_ref, wg0_ref, wg1_ref, wg2_ref, bg0_ref, bg1_ref, bg2_ref, ws_ref, wa_ref, wc_ref,
                  o_ref, ssm_scr, conv_scr, *, lay, tm, halo):
    i = pl.program_id(0)
    n = pl.program_id(1)

    @pl.when(n == 0)
    def _():
        y = ys_ref[...]
        ge = 0.5 * y * (1.0 + jnp.tanh(math.sqrt(2.0 / math.pi) * (y + 0.044715 * (y * y * y))))
        glu = jnp.dot(ge.astype(BF16), wglu_ref[...], preferred_element_type=F32)
        ssm_scr[...] = (ge * _sigmoid(glu)).astype(BF16)

        z = gc_ref[...].astype(F32) * uc_ref[...].astype(F32)
        z_before = gcp_ref[halo - 1:halo, :].astype(F32) * ucp_ref[halo - 1:halo, :].astype(F32)
        z_after = gcn_ref[0:1, :].astype(F32) * ucn_ref[0:1, :].astype(F32)
        local = lax.broadcasted_iota(I32, z.shape, 0)
        pos, seq_len = lay.seq_pos(local + i * tm, i, tm)
        z_prev = jnp.where(local == 0, z_before, pltpu.roll(z, 1, 0))
        z_prev = jnp.where(pos == 0, 0.0, z_prev)
        z_next = jnp.where(local == tm - 1, z_after, pltpu.roll(z, tm - 1, 0))
        z_next = jnp.where(pos == seq_len - 1, 0.0, z_next)
        conv = cw_ref[0:1, :] * z_prev + cw_ref[1:2, :] * z + cw_ref[2:3, :] * z_next
        conv_scr[...] = (gb_ref[...].astype(F32) * conv).astype(BF16)

    h = h_ref[...]
    acc = None
    for act, wg_ref, bg_ref, wb_ref in ((ssm_scr[...], wg0_ref, bg0_ref, ws_ref),
                                        (at_ref[...], wg1_ref, bg1_ref, wa_ref),
                                        (conv_scr[...], wg2_ref, bg2_ref, wc_ref)):
        gate = _sigmoid(jnp.dot(h, wg_ref[...], preferred_element_type=F32) + bg_ref[...])
        term = gate * jnp.dot(act, wb_ref[...], preferred_element_type=F32)
        acc = term if acc is None else acc + term
    o_ref[...] = acc.astype(BF16)


def _merge(lay, h16, y_ssm, attn, y16, conv_w, wglu16, wgates16, b_gates, wbs16, wba16, wbc16):
    d = lay.d
    tm = lay.row_tile(512)
    tn = min(COL_TILE, d)
    nd = d // tn
    halo = 16
    hb = tm // halo
    last_h = lay.t // halo - 1
    c0 = (ATTN_WIDTH + 2 * KV_WIDTH + SSM_WIDTH) // CONV_WIDTH
    row = lambda cb: pl.BlockSpec((tm, CONV_WIDTH), lambda i, n: (i, cb))
    before = lambda cb: pl.BlockSpec((halo, CONV_WIDTH), lambda i, n: (jnp.maximum(i * hb - 1, 0), cb))
    after = lambda cb: pl.BlockSpec((halo, CONV_WIDTH), lambda i, n: (jnp.minimum((i + 1) * hb, last_h), cb))
    gate_w = lambda br: pl.BlockSpec((d, tn), lambda i, n: (0, br * nd + n))
    gate_b = lambda br: pl.BlockSpec((1, tn), lambda i, n: (0, br * nd + n))
    return pl.pallas_call(
        functools.partial(_merge_kernel, lay=lay, tm=tm, halo=halo),
        grid=(lay.t // tm, nd),
        in_specs=[
            pl.BlockSpec((tm, d), lambda i, n: (i, 0)),
            pl.BlockSpec((tm, SSM_WIDTH), lambda i, n: (i, 0)),
            pl.BlockSpec((tm, ATTN_WIDTH), lambda i, n: (i, 0)),
            row(c0), row(c0 + 1), row(c0 + 2),
            before(c0 + 1), before(c0 + 2), after(c0 + 1), after(c0 + 2),
            pl.BlockSpec((3, CONV_WIDTH), lambda i, n: (0, 0)),
            pl.BlockSpec((SSM_WIDTH, SSM_WIDTH), lambda i, n: (0, 0)),
            gate_w(0), gate_w(1), gate_w(2), gate_b(0), gate_b(1), gate_b(2),
            pl.BlockSpec((SSM_WIDTH, tn), lambda i, n: (0, n)),
            pl.BlockSpec((ATTN_WIDTH, tn), lambda i, n: (0, n)),
            pl.BlockSpec((CONV_WIDTH, tn), lambda i, n: (0, n)),
        ],
        out_specs=pl.BlockSpec((tm, tn), lambda i, n: (i, n)),
        out_shape=jax.ShapeDtypeStruct((lay.t, d), BF16),
        scratch_shapes=[pltpu.VMEM((tm, SSM_WIDTH), BF16), pltpu.VMEM((tm, CONV_WIDTH), BF16)],
        compiler_params=_cparams(("arbitrary", "arbitrary")),
    )(h16, y_ssm, attn, y16, y16, y16, y16, y16, y16, y16, conv_w, wglu16,
      wgates16, wgates16, wgates16, b_gates, b_gates, b_gates, wbs16, wba16, wbc16)


def _outproj_kernel(m_ref, w_ref, x_ref, gate_ref, g2_ref, shift_ref, scale_ref, wrh_ref, wrl_ref,
                    xo_ref, hp_ref, lg_ref):
    acc = jnp.dot(m_ref[...], w_ref[...], preferred_element_type=F32)
    xn = x_ref[...] + gate_ref[0] * acc
    xo_ref[...] = xn
    ms = jnp.mean(xn * xn, axis=-1, keepdims=True)
    h2 = xn * lax.rsqrt(ms + EPS) * g2_ref[...]
    h2 = h2 * (1.0 + scale_ref[0]) + shift_ref[0]
    hp_ref[...] = _pack_bf16_pairs(h2)
    h_hi = h2.astype(BF16)
    h_lo = (h2 - h_hi.astype(F32)).astype(BF16)
    logits = (jnp.dot(h_hi, wrh_ref[...], preferred_element_type=F32)
              + jnp.dot(h_hi, wrl_ref[...], preferred_element_type=F32)
              + jnp.dot(h_lo, wrh_ref[...], preferred_element_type=F32))
    lg_ref[...] = logits.T[:N_EXPERTS, :]


def _outproj(lay, merged, wout16, x, mod_l, g2, w_router):
    d = lay.d
    tm = lay.row_tile(256)
    lanes = 128
    wr = jnp.zeros((d, lanes), F32).at[:, :N_EXPERTS].set(w_router.astype(F32))
    wr_hi = wr.astype(BF16)
    wr_lo = (wr - wr_hi.astype(F32)).astype(BF16)
    return pl.pallas_call(
        _outproj_kernel,
        grid=(lay.t // tm,),
        in_specs=[
            pl.BlockSpec((tm, d), lambda i: (i, 0)),
            pl.BlockSpec((d, d), lambda i: (0, 0)),
            pl.BlockSpec((tm, d), lambda i: (i, 0)),
            _mod_spec(lay, tm, 2, d),
            pl.BlockSpec((1, d), lambda i: (0, 0)),
            _mod_spec(lay, tm, 3, d),
            _mod_spec(lay, tm, 4, d),
            pl.BlockSpec((d, lanes), lambda i: (0, 0)),
            pl.BlockSpec((d, lanes), lambda i: (0, 0)),
        ],
        out_specs=[
            pl.BlockSpec((tm, d), lambda i: (i, 0)),
            pl.BlockSpec((tm, d // 2), lambda i: (i, 0)),
            pl.BlockSpec((N_EXPERTS, tm), lambda i: (0, i)),
        ],
        out_shape=[
            jax.ShapeDtypeStruct((lay.t, d), F32),
            jax.ShapeDtypeStruct((lay.t, d // 2), U32),
            jax.ShapeDtypeStruct((N_EXPERTS, lay.t), F32),
        ],
        compiler_params=_cparams(("arbitrary",)),
    )(merged, wout16, x, mod_l, g2.reshape(1, d), mod_l, mod_l, wr_hi, wr_lo)


def _route_kernel(lg_ref, br_ref, idx_ref, w_ref, pos_ref, cnt_ref, carry):
    step = pl.program_id(0)
    tt = lg_ref.shape[1]
    per_group = N_EXPERTS // N_EXPERT_GROUPS

    @pl.when(step == 0)
    def _():
        carry[...] = jnp.zeros_like(carry)

    scores = _sigmoid(lg_ref[...])
    biased = scores + br_ref[...]
    sub = lax.broadcasted_iota(I32, (per_group, tt), 0).astype(F32)
    blocks, group_score = [], []
    for g in range(N_EXPERT_GROUPS):
        blk = biased[g * per_group:(g + 1) * per_group, :]
        m1 = jnp.max(blk, axis=0, keepdims=True)
        i1 = jnp.min(jnp.where(blk == m1, sub, float(per_group)), axis=0, keepdims=True)
        m2 = jnp.max(jnp.where(sub == i1, -jnp.inf, blk), axis=0, keepdims=True)
        blocks.append(blk)
        group_score.append(m1 + m2)
    masked = []
    for g in range(N_EXPERT_GROUPS):
        beaten_by = jnp.zeros((1, tt), F32)
        for o in range(N_EXPERT_GROUPS):
            if o == g:
                continue
            wins = (group_score[o] > group_score[g]) | ((group_score[o] == group_score[g]) & (o < g))
            beaten_by = beaten_by + wins.astype(F32)
        masked.append(jnp.where(beaten_by < TOPK_GROUPS, blocks[g], -jnp.inf))
    masked = jnp.concatenate(masked, axis=0)
    eid = lax.broadcasted_iota(I32, (N_EXPERTS, tt), 0).astype(F32)
    chosen, weights = [], []
    onehot = jnp.zeros((N_EXPERTS, tt), F32)
    for _ in range(TOP_K):
        m = jnp.max(masked, axis=0, keepdims=True)
        e = jnp.min(jnp.where(masked == m, eid, float(N_EXPERTS)), axis=0, keepdims=True)
        hit = eid == e
        chosen.append(e)
        weights.append(jnp.sum(jnp.where(hit, scores, 0.0), axis=0, keepdims=True))
        onehot = onehot + hit.astype(F32)
        masked = jnp.where(hit, -jnp.inf, masked)
    total = weights[0]
    for wk in weights[1:]:
        total = total + wk
    earlier = (lax.broadcasted_iota(I32, (tt, tt), 0) < lax.broadcasted_iota(I32, (tt, tt), 1)).astype(BF16)
    rank = carry[...][:, 0:1] + jnp.dot(onehot.astype(BF16), earlier, preferred_element_type=F32)
    for k in range(TOP_K):
        idx_ref[k:k + 1, :] = chosen[k].astype(I32)
        w_ref[k:k + 1, :] = weights[k] / total * ROUTED_SCALE
        pos_ref[k:k + 1, :] = jnp.sum(jnp.where(eid == chosen[k], rank, 0.0), axis=0, keepdims=True).astype(I32)
    for k in range(TOP_K, 8):
        idx_ref[k:k + 1, :] = jnp.zeros((1, tt), I32)
        w_ref[k:k + 1, :] = jnp.zeros((1, tt), F32)
        pos_ref[k:k + 1, :] = jnp.zeros((1, tt), I32)
    carry[...] = carry[...] + jnp.sum(onehot, axis=1, keepdims=True)
    cnt_ref[...] = carry[...]


def _route(logits_t, b_router):
    t = logits_t.shape[1]
    tt = math.gcd(t, 512)
    tok = pl.BlockSpec((8, tt), lambda i: (0, i))
    return pl.pallas_call(
        _route_kernel,
        grid=(t // tt,),
        in_specs=[pl.BlockSpec((N_EXPERTS, tt), lambda i: (0, i)),
                  pl.BlockSpec((N_EXPERTS, 1), lambda i: (0, 0))],
        out_specs=[tok, tok, tok, pl.BlockSpec((N_EXPERTS, 128), lambda i: (0, 0))],
        out_shape=[jax.ShapeDtypeStruct((8, t), I32), jax.ShapeDtypeStruct((8, t), F32),
                   jax.ShapeDtypeStruct((8, t), I32), jax.ShapeDtypeStruct((N_EXPERTS, 128), F32)],
        scratch_shapes=[pltpu.VMEM((N_EXPERTS, 128), F32)],
        compiler_params=_cparams(("arbitrary",)),
    )(logits_t, b_router.astype(F32).reshape(N_EXPERTS, 1))


def _dest_kernel(idx_ref, pos_ref, start_ref, o_ref, *, tt):
    eid = lax.broadcasted_iota(I32, (N_EXPERTS, tt), 0)
    start = start_ref[...]
    for k in range(TOP_K):
        first = jnp.sum(jnp.where(eid == idx_ref[k:k + 1, :], start, 0.0), axis=0, keepdims=True)
        o_ref[k:k + 1, :] = first.astype(I32) + pos_ref[k:k + 1, :]
    for k in range(TOP_K, 8):
        o_ref[k:k + 1, :] = jnp.zeros((1, tt), I32)


def _dest_rows(idx_t, pos_t, pad_start):
    t = idx_t.shape[1]
    tt = math.gcd(t, 512)
    tok = pl.BlockSpec((8, tt), lambda i: (0, i))
    return pl.pallas_call(
        functools.partial(_dest_kernel, tt=tt),
        grid=(t // tt,),
        in_specs=[tok, tok, pl.BlockSpec((N_EXPERTS, 1), lambda i: (0, 0))],
        out_specs=tok,
        out_shape=jax.ShapeDtypeStruct((8, t), I32),
        compiler_params=_cparams(("arbitrary",)),
    )(idx_t, pos_t, pad_start.astype(F32).reshape(N_EXPERTS, 1))


def _sc_layout(n_pairs, period):
    info = plsc.get_sparse_core_info()
    workers = info.num_cores * info.num_subcores
    per_worker = n_pairs // workers
    chunk = math.gcd(math.gcd(per_worker, period), SC_STREAM_ROWS)
    assert per_worker * workers == n_pairs and chunk % 8 == 0
    return info.num_cores, per_worker, chunk


def _sc_scatter_rows(rows, dest, n_out):
    t, width = rows.shape
    copies = dest.shape[0] // t
    n_cores, per_worker, chunk = _sc_layout(t, t)
    mesh = plsc.VectorSubcoreMesh(core_axis_name="core", subcore_axis_name="subcore")

    @functools.partial(
        pl.kernel, mesh=mesh, out_type=jax.ShapeDtypeStruct((n_out, width), rows.dtype),
        scratch_types=[pltpu.VMEM((chunk,), I32), pltpu.VMEM((chunk, width), rows.dtype)])
    def scatter(rows_hbm, dest_hbm, out_hbm, dest_v, rows_v):
        base = (lax.axis_index("subcore") * n_cores + lax.axis_index("core")) * per_worker

        @pl.loop(0, per_worker // chunk)
        def _(j):
            tok = base + j * chunk
            pltpu.sync_copy(rows_hbm.at[pl.ds(tok, chunk)], rows_v)
            for k in range(copies):
                pltpu.sync_copy(dest_hbm.at[pl.ds(k * t + tok, chunk)], dest_v)
                pltpu.sync_copy(rows_v, out_hbm.at[dest_v])

    return scatter(rows, dest)


def _sc_gather_rows(table, dest):
    width = table.shape[1]
    n_pairs = dest.shape[0]
    n_cores, per_worker, chunk = _sc_layout(n_pairs, n_pairs)
    mesh = plsc.VectorSubcoreMesh(core_axis_name="core", subcore_axis_name="subcore")

    @functools.partial(
        pl.kernel, mesh=mesh, out_type=jax.ShapeDtypeStruct((n_pairs, width), table.dtype),
        scratch_types=[pltpu.VMEM((chunk,), I32), pltpu.VMEM((chunk, width), table.dtype)])
    def gather(table_hbm, dest_hbm, out_hbm, dest_v, rows_v):
        base = (lax.axis_index("subcore") * n_cores + lax.axis_index("core")) * per_worker

        @pl.loop(0, per_worker // chunk)
        def _(j):
            pair = base + j * chunk
            pltpu.sync_copy(dest_hbm.at[pl.ds(pair, chunk)], dest_v)
            pltpu.sync_copy(table_hbm.at[dest_v], rows_v)
            pltpu.sync_copy(rows_v, out_hbm.at[pl.ds(pair, chunk)])

    return gather(table, dest)


def _expert_kernel(be_ref, xs_ref, wg_ref, wu_ref, wd_ref, ys_ref, wg16, wu16, wd16):
    i = pl.program_id(0)
    changed = jnp.logical_or(i == 0, be_ref[i] != be_ref[jnp.maximum(i - 1, 0)])

    @pl.when(changed)
    def _():
        rows = 256
        d = wg16.shape[0]

        def cast_in(r, carry):
            sl = pl.ds(pl.multiple_of(r * rows, rows), rows)
            wg16[sl, :] = wg_ref[0, sl, :].astype(BF16)
            wu16[sl, :] = wu_ref[0, sl, :].astype(BF16)
            return carry

        lax.fori_loop(0, d // rows, cast_in, 0)

        def cast_down(r, carry):
            sl = pl.ds(pl.multiple_of(r * 128, 128), 128)
            wd16[sl, :] = wd_ref[0, sl, :].astype(BF16)
            return carry

        lax.fori_loop(0, D_EXPERT // 128, cast_down, 0)

    n_blocks = pl.num_programs(0)
    used = i < be_ref[n_blocks]

    @pl.when(used)
    def _():
        live = lax.broadcasted_iota(I32, (xs_ref.shape[0], 1), 0) < be_ref[n_blocks + 1 + i]
        hi, lo = _unpack_bf16_pairs(jnp.where(live, xs_ref[...], jnp.uint32(0)))
        x = jnp.concatenate([hi, lo], axis=1).astype(BF16)
        gate = jnp.dot(x, wg16[...], preferred_element_type=F32)
        up = jnp.dot(x, wu16[...], preferred_element_type=F32)
        act = (gate * _sigmoid(gate) * up).astype(BF16)
        ys_ref[...] = _pack_bf16_pairs(jnp.dot(act, wd16[...], preferred_element_type=F32))

    @pl.when(jnp.logical_not(used))
    def _():
        ys_ref[...] = jnp.zeros_like(ys_ref)


def _experts(xs, block_e, layer, w_gate, w_up, w_down):
    n_rows, half = xs.shape
    d = 2 * half
    br = EXPERT_ROWS
    grid_spec = pltpu.PrefetchScalarGridSpec(
        num_scalar_prefetch=1,
        grid=(n_rows // br,),
        in_specs=[
            pl.BlockSpec((br, half), lambda i, be: (i, 0)),
            pl.BlockSpec((None, 1, d, D_EXPERT), lambda i, be: (layer, be[i], 0, 0)),
            pl.BlockSpec((None, 1, d, D_EXPERT), lambda i, be: (layer, be[i], 0, 0)),
            pl.BlockSpec((None, 1, D_EXPERT, d), lambda i, be: (layer, be[i], 0, 0)),
        ],
        out_specs=pl.BlockSpec((br, half), lambda i, be: (i, 0)),
        scratch_shapes=[pltpu.VMEM((d, D_EXPERT), BF16), pltpu.VMEM((d, D_EXPERT), BF16),
                        pltpu.VMEM((D_EXPERT, d), BF16)],
    )
    return pl.pallas_call(
        _expert_kernel,
        grid_spec=grid_spec,
        out_shape=jax.ShapeDtypeStruct((n_rows, half), U32),
        compiler_params=_cparams(("arbitrary",)),
    )(block_e, xs, w_gate, w_up, w_down)


def _shared_kernel(hp_ref, wsg_ref, wsu_ref, wsd_ref, o_ref):
    hi, lo = _unpack_bf16_pairs(hp_ref[...])
    h2 = jnp.concatenate([hi, lo], axis=1).astype(BF16)
    sg = jnp.dot(h2, wsg_ref[...], preferred_element_type=F32)
    su = jnp.dot(h2, wsu_ref[...], preferred_element_type=F32)
    shared = jnp.dot((sg * _sigmoid(sg) * su).astype(BF16), wsd_ref[...], preferred_element_type=F32)
    o_ref[...] = shared.astype(o_ref.dtype)


def _shared_expert(lay, h2p, wsg16, wsu16, wsd16):
    d = lay.d
    tm = lay.row_tile(512)
    return pl.pallas_call(
        _shared_kernel,
        grid=(lay.t // tm,),
        in_specs=[pl.BlockSpec((tm, d // 2), lambda i: (i, 0)),
                  pl.BlockSpec((d, D_EXPERT), lambda i: (0, 0)),
                  pl.BlockSpec((d, D_EXPERT), lambda i: (0, 0)),
                  pl.BlockSpec((D_EXPERT, d), lambda i: (0, 0))],
        out_specs=pl.BlockSpec((tm, d), lambda i: (i, 0)),
        out_shape=jax.ShapeDtypeStruct((lay.t, d), BF16),
        compiler_params=_cparams(("arbitrary",)),
    )(h2p, wsg16, wsu16, wsd16)


def _combine_kernel(yk_ref, wsel_ref, sh_ref, x_ref, gate_ref, fg_ref, *outs, tt, final, n_ctx_tiles):
    shared = sh_ref[...].astype(F32)
    half = yk_ref.shape[2]
    r_hi = jnp.zeros((tt, half), F32)
    r_lo = jnp.zeros((tt, half), F32)
    for k in range(TOP_K):
        y_hi, y_lo = _unpack_bf16_pairs(yk_ref[k])
        wk = wsel_ref[:, k:k + 1]
        r_hi = r_hi + wk * y_hi
        r_lo = r_lo + wk * y_lo
    routed = jnp.concatenate([r_hi, r_lo], axis=1)
    out = x_ref[...] + gate_ref[0] * (routed + shared)
    if not final:
        outs[0][...] = out
        return
    ms = jnp.mean(out * out, axis=-1, keepdims=True)
    out = out * lax.rsqrt(ms + EPS) * fg_ref[...]
    ctx_ref, lat_ref = outs
    is_ctx = pl.program_id(0) < n_ctx_tiles

    @pl.when(is_ctx)
    def _():
        ctx_ref[...] = out

    @pl.when(jnp.logical_not(is_ctx))
    def _():
        lat_ref[...] = out


def _combine(lay, y_by_k, wsel, shared, x, mod_l, final_g, tt, final):
    d = lay.d
    half = d // 2
    nct = lay.tc // tt
    if final:
        out_specs = [pl.BlockSpec((tt, d), lambda i: (jnp.minimum(i, nct - 1), 0)),
                     pl.BlockSpec((tt, d), lambda i: (jnp.maximum(i - nct, 0), 0))]
        out_shape = [jax.ShapeDtypeStruct((lay.tc, d), F32), jax.ShapeDtypeStruct((lay.ts, d), F32)]
    else:
        out_specs = pl.BlockSpec((tt, d), lambda i: (i, 0))
        out_shape = jax.ShapeDtypeStruct((lay.t, d), F32)
    return pl.pallas_call(
        functools.partial(_combine_kernel, tt=tt, final=final, n_ctx_tiles=nct),
        grid=(lay.t // tt,),
        in_specs=[
            pl.BlockSpec((TOP_K, tt, half), lambda i: (0, i, 0)),
            pl.BlockSpec((tt, 8), lambda i: (i, 0)),
            pl.BlockSpec((tt, d), lambda i: (i, 0)),
            pl.BlockSpec((tt, d), lambda i: (i, 0)),
            _mod_spec(lay, tt, 5, d),
            pl.BlockSpec((1, d), lambda i: (0, 0)),
        ],
        out_specs=out_specs,
        out_shape=out_shape,
        compiler_params=_cparams(("arbitrary",)),
    )(y_by_k, wsel, shared, x, mod_l, final_g.reshape(1, d))


def _moe(lay, layer, x, h2p, logits_t, mod_l, b_router, w_e_gate, w_e_up, w_e_down, wsg16, wsu16, wsd16, final_g,
         final):
    t = lay.t
    idx_t, w_t, pos_t, counts = _route(logits_t, b_router)
    br = EXPERT_ROWS
    counts = counts[:, 0].astype(I32)
    padded = (counts + br - 1) // br * br
    pad_end = jnp.cumsum(padded)
    pad_start = pad_end - padded
    n_blocks = -(-(t * TOP_K + N_EXPERTS * (br - 1)) // br)
    n_rows = n_blocks * br
    first_row = jnp.arange(n_blocks, dtype=I32) * br
    block_e = jnp.minimum(jnp.sum((pad_end[None, :] <= first_row[:, None]).astype(I32), axis=1), N_EXPERTS - 1)
    own = block_e[:, None] == jnp.arange(N_EXPERTS, dtype=I32)[None, :]
    rows_end = jnp.sum(jnp.where(own, (pad_start + counts)[None, :], 0), axis=1)
    held = jnp.clip(rows_end - first_row, 0, br)
    block_meta = jnp.concatenate([block_e, pad_end[-1:] // br, held]).astype(I32)
    dest = _dest_rows(idx_t, pos_t, pad_start)[:TOP_K].reshape(TOP_K * t)
    xs = _sc_scatter_rows(h2p, dest, n_rows)
    shared = _shared_expert(lay, h2p, wsg16, wsu16, wsd16)
    ys = _experts(xs, block_meta, layer, w_e_gate, w_e_up, w_e_down)
    y_by_k = _sc_gather_rows(ys, dest).reshape(TOP_K, t, h2p.shape[1])
    return _combine(lay, y_by_k, w_t.T, shared, x, mod_l, final_g, lay.row_tile(256), final)


def kernel(x_prompt, x_sample, c, cache_k, cache_v, state_ssm_re, state_ssm_im, c_ctx, w_ada, b_ada, norm1_g, norm2_g, w_in, w_gates, b_gates, ssm_lam_re, ssm_lam_im, ssm_log_dt, ssm_b_re, ssm_b_im, ssm_c_re, ssm_c_im, ssm_d, ssm_w_glu, conv_w, attn_sink, w_br_ssm, w_br_attn, w_br_conv, w_out, w_router, b_router, w_e_gate, w_e_up, w_e_down, w_s_gate, w_s_up, w_s_down, final_g):
    bc, lc, d = x_prompt.shape
    bs, ls, _ = x_sample.shape
    depth = w_in.shape[0]
    lay = _Layout(bc, lc, bs, ls, d)
    assert 1 + bs <= MOD_ROWS

    x = jnp.concatenate([x_prompt.reshape(lay.tc, d), x_sample.reshape(lay.ts, d)], axis=0)
    cvec = jnp.zeros((MOD_ROWS, d), F32).at[0].set(c_ctx).at[1:1 + bs].set(c)
    mod = _adaln(cvec, w_ada, b_ada).reshape(depth, MOD_ROWS * 6, 1, d)
    rope_cos, rope_sin = _rope_tables(lay, lay.row_tile(1024))
    s5_perm = _s5_lane_permutation()
    s5_mats = jax.vmap(functools.partial(_s5_matrices, n_steps=8))(
        ssm_lam_re, ssm_lam_im, ssm_log_dt, ssm_b_re, ssm_b_im, ssm_c_re, ssm_c_im, ssm_d)

    ks, vs, s_re, s_im = [], [], [], []
    for l in range(depth):
        mod_l = mod[l]
        h16, y16, kv32 = _inproj(lay, x, mod_l, norm1_g[l], w_in[l].astype(BF16), rope_cos, rope_sin)
        ks.append(kv32[:lay.tc, :KV_WIDTH].reshape(bc, lc, N_KV_HEADS, HEAD_DIM))
        vs.append(kv32[:lay.tc, KV_WIDTH:].reshape(bc, lc, N_KV_HEADS, HEAD_DIM))

        attn = _attention(lay, y16, attn_sink[l].astype(F32), cache_k[:, l], cache_v[:, l])

        u0 = ATTN_WIDTH + 2 * KV_WIDTH
        u = y16[:, u0:u0 + SSM_WIDTH]
        mats = (s5_perm,) + tuple(m[l] for m in s5_mats)
        y_ssm, f_re, f_im = _s5_both(lay, u, mats, state_ssm_re[:, l], state_ssm_im[:, l])
        s_re.append(f_re)
        s_im.append(f_im)

        merged = _merge(lay, h16, y_ssm, attn, y16, conv_w[l], ssm_w_glu[l].astype(BF16),
                        w_gates[l].astype(BF16), b_gates[l].reshape(1, -1), w_br_ssm[l].astype(BF16),
                        w_br_attn[l].astype(BF16), w_br_conv[l].astype(BF16))
        x, h2p, logits_t = _outproj(lay, merged, w_out[l].astype(BF16), x, mod_l, norm2_g[l],
                                    w_router[l])
        x = _moe(lay, l, x, h2p, logits_t, mod_l, b_router[l], w_e_gate, w_e_up, w_e_down,
                 w_s_gate[l].astype(BF16), w_s_up[l].astype(BF16), w_s_down[l].astype(BF16),
                 final_g, l == depth - 1)

    y_prompt = x[0].reshape(bc, lc, d)
    y_sample = x[1].reshape(bs, ls, d)
    return (y_prompt, y_sample, jnp.stack(ks, axis=1), jnp.stack(vs, axis=1),
            jnp.stack(s_re, axis=1), jnp.stack(s_im, axis=1))
```
